```python
import jax, jax.numpy as jnp
from jax import lax
import numpy as np

D_MODEL = 1024
BATCH = 8
SEQ = 2048
DEPTH = 1

NORM_EPS = 1e-5
GLA_HEADS = 4
GLA_DK = 128
GLA_DV = 256
GLA_GATE_RANK = 16
GLA_TAU = 16.0
GLA_CHUNK = 64
SWA_Q_HEADS = 16
SWA_KV_HEADS = 2
SWA_GROUP = SWA_Q_HEADS // SWA_KV_HEADS
SWA_HEAD_DIM = 64
SWA_WINDOW = 128
ROPE_THETA = 10000.0
N_EXPERTS = 32
TOP_K = 4
D_FF = D_MODEL
SWIGLU_LIMIT = 7.0
SWIGLU_ALPHA = 1.702
MOE_BLOCK = 256

SPLIT_SIZES = (
    GLA_HEADS * GLA_DK,
    GLA_HEADS * GLA_DK,
    GLA_HEADS * GLA_DV,
    GLA_HEADS * GLA_DV,
    GLA_GATE_RANK,
    SWA_Q_HEADS * SWA_HEAD_DIM,
    SWA_KV_HEADS * SWA_HEAD_DIM,
    SWA_KV_HEADS * SWA_HEAD_DIM,
    D_MODEL,
    D_MODEL,
)
D_IN = sum(SPLIT_SIZES)

kernel_name = "hybrid_gla_swa_sink_moe"


def rms_norm(x, g):
    xf = x.astype(jnp.float32)
    y = xf * lax.rsqrt(jnp.mean(xf * xf, axis=-1, keepdims=True) + NORM_EPS)
    return (y * g.astype(jnp.float32)).astype(x.dtype)


def rope(x, pos):
    half = x.shape[-1] // 2
    inv_freq = ROPE_THETA ** (-jnp.arange(half, dtype=jnp.float32) / half)
    ang = pos.astype(jnp.float32)[:, None] * inv_freq[None, :]
    cos = jnp.cos(ang)[None, :, None, :]
    sin = jnp.sin(ang)[None, :, None, :]
    xf = x.astype(jnp.float32)
    x1, x2 = xf[..., :half], xf[..., half:]
    return jnp.concatenate([x1 * cos - x2 * sin, x2 * cos + x1 * sin], axis=-1).astype(x.dtype)


def gla_chunked(q, k, v, log_g):
    bsz, s, h, dk = q.shape
    dv = v.shape[-1]
    c = GLA_CHUNK
    n = s // c

    def to_chunks(t):
        return t.astype(jnp.float32).reshape(bsz, n, c, h, t.shape[-1]).transpose(1, 0, 3, 2, 4)

    qc = to_chunks(q) * (GLA_DK ** -0.5)
    kc, vc, gc = to_chunks(k), to_chunks(v), to_chunks(log_g)
    causal = jnp.tril(jnp.ones((c, c), dtype=bool))

    def step(state, inp):
        q_c, k_c, v_c, g_c = inp
        b = jnp.cumsum(g_c, axis=-2)
        diff = b[..., :, None, :] - b[..., None, :, :]
        decay = jnp.exp(jnp.where(causal[:, :, None], diff, -jnp.inf))
        scores = jnp.einsum('bhtsd,bhsd->bhts', q_c[..., :, None, :] * decay, k_c)
        o = (jnp.einsum('bhts,bhsv->bhtv', scores, v_c)
             + jnp.einsum('bhtd,bhdv->bhtv', q_c * jnp.exp(b), state))
        b_last = b[..., -1:, :]
        state = (jnp.exp(b_last[..., 0, :])[..., None] * state
                 + jnp.einsum('bhsd,bhsv->bhdv', k_c * jnp.exp(b_last - b), v_c))
        return state, o

    state0 = jnp.zeros((bsz, h, dk, dv), jnp.float32)
    _, o = lax.scan(step, state0, (qc, kc, vc, gc))
    return o.transpose(1, 0, 3, 2, 4).reshape(bsz, s, h, dv)


def swa_with_sinks(q, k, v, sinks):
    bsz, s, _, d = q.shape
    w = SWA_WINDOW
    nb = s // w
    qb = q.reshape(bsz, nb, w, SWA_KV_HEADS, SWA_GROUP, d)

    def band(t):
        tb = t.reshape(bsz, nb, w, SWA_KV_HEADS, d)
        prev = jnp.pad(tb[:, :-1], ((0, 0), (1, 0), (0, 0), (0, 0), (0, 0)))
        return jnp.concatenate([prev, tb], axis=2)

    kband, vband = band(k), band(v)
    scores = jnp.einsum('bnqhgd,bnkhd->bnhgqk', qb, kband).astype(jnp.float32) * (d ** -0.5)
    blk = jnp.arange(nb)[:, None, None]
    q_pos = blk * w + jnp.arange(w)[None, :, None]
    k_pos = (blk - 1) * w + jnp.arange(2 * w)[None, None, :]
    valid = (k_pos >= 0) & (k_pos <= q_pos) & (q_pos - k_pos < w)
    scores = jnp.where(valid[None, :, None, None], scores, -jnp.inf)
    sink = sinks.astype(jnp.float32).reshape(SWA_KV_HEADS, SWA_GROUP)[None, None, :, :, None, None]
    m = jnp.maximum(jnp.max(scores, axis=-1, keepdims=True), sink)
    p = jnp.exp(scores - m)
    denom = jnp.sum(p, axis=-1, keepdims=True) + jnp.exp(sink - m)
    probs = (p / denom).astype(v.dtype)
    out = jnp.einsum('bnhgqk,bnkhd->bnqhgd', probs, vband)
    return out.reshape(bsz, s, SWA_Q_HEADS * d)


def moe_ffn(h, w_router, b_router, w_e1, b_e1, w_e2, b_e2):
    bsz, s, d = h.shape
    t = bsz * s
    hf = h.reshape(t, d)
    logits = (hf @ w_router + b_router).astype(jnp.float32)
    top_v, top_i = lax.top_k(logits, TOP_K)
    gates = jax.nn.softmax(top_v, axis=-1)

    a = t * TOP_K
    flat_e = top_i.reshape(a)
    flat_tok = jnp.repeat(jnp.arange(t, dtype=jnp.int32), TOP_K)
    flat_w = gates.reshape(a)
    order = jnp.argsort(flat_e)
    se, stok, sw = flat_e[order], flat_tok[order], flat_w[order]
    counts = jnp.bincount(flat_e, length=N_EXPERTS)
    padded = ((counts + MOE_BLOCK - 1) // MOE_BLOCK) * MOE_BLOCK
    start = jnp.cumsum(counts) - counts
    pend = jnp.cumsum(padded)
    pstart = pend - padded
    dest = pstart[se] + (jnp.arange(a, dtype=jnp.int32) - start[se])

    n_blocks = -(-a // MOE_BLOCK) + N_EXPERTS
    rows = n_blocks * MOE_BLOCK
    ptok = jnp.zeros((rows,), jnp.int32).at[dest].set(stok)
    pw = jnp.zeros((rows,), jnp.float32).at[dest].set(sw)
    block_e = jnp.minimum(
        jnp.searchsorted(pend, jnp.arange(n_blocks) * MOE_BLOCK, side='right'), N_EXPERTS - 1)

    def expert_block(args):
        tok, e = args
        xb = hf[tok]
        u = xb @ w_e1[e] + b_e1[e]
        x_glu = jnp.minimum(u[..., ::2], SWIGLU_LIMIT)
        x_lin = jnp.clip(u[..., 1::2], -SWIGLU_LIMIT, SWIGLU_LIMIT)
        act = x_glu * jax.nn.sigmoid(SWIGLU_ALPHA * x_glu) * (x_lin + 1.0)
        return act @ w_e2[e] + b_e2[e]

    ys = lax.map(expert_block, (ptok.reshape(n_blocks, MOE_BLOCK), block_e))
    ys = ys.reshape(rows, d) * pw[:, None].astype(ys.dtype)
    y = jax.ops.segment_sum(ys, ptok, num_segments=t)
    return y.reshape(bsz, s, d)


def setup_inputs(seed: int = 0) -> dict:
    key = jax.random.key(seed)
    ks = jax.random.split(key, 20)
    f32 = jnp.float32
    L = DEPTH

    def nrm(k, shape, scale):
        return jax.random.normal(k, shape, f32) * scale

    return {
        "x": nrm(ks[0], (BATCH, SEQ, D_MODEL), 1.0),
        "g_mix": 1.0 + nrm(ks[1], (L, D_MODEL), 0.02),
        "w_in": nrm(ks[2], (L, D_MODEL, D_IN), D_MODEL ** -0.5),
        "b_in": nrm(ks[3], (L, D_IN), 0.02),
        "w_gla_gate": nrm(ks[4], (L, GLA_GATE_RANK, GLA_HEADS * GLA_DK), GLA_GATE_RANK ** -0.5),
        "b_gla_gate": nrm(ks[5], (L, GLA_HEADS * GLA_DK), 0.5),
        "g_gla_head": 1.0 + nrm(ks[6], (L, GLA_DV), 0.02),
        "w_gla_out": nrm(ks[7], (L, GLA_HEADS * GLA_DV, D_MODEL), (GLA_HEADS * GLA_DV) ** -0.5),
        "sinks": nrm(ks[8], (L, SWA_Q_HEADS), 1.0),
        "w_swa_out": nrm(ks[9], (L, SWA_Q_HEADS * SWA_HEAD_DIM, D_MODEL), (SWA_Q_HEADS * SWA_HEAD_DIM) ** -0.5),
        "w_out": nrm(ks[10], (L, D_MODEL, D_MODEL), D_MODEL ** -0.5),
        "g_ffn": 1.0 + nrm(ks[11], (L, D_MODEL), 0.02),
        "w_router": nrm(ks[12], (L, D_MODEL, N_EXPERTS), D_MODEL ** -0.5),
        "b_router": nrm(ks[13], (L, N_EXPERTS), 0.01),
        "w_e1": nrm(ks[14], (L, N_EXPERTS, D_MODEL, 2 * D_FF), D_MODEL ** -0.5),
        "b_e1": nrm(ks[15], (L, N_EXPERTS, 2 * D_FF), 0.01),
        "w_e2": nrm(ks[16], (L, N_EXPERTS, D_FF, D_MODEL), D_FF ** -0.5),
        "b_e2": nrm(ks[17], (L, N_EXPERTS, D_MODEL), 0.01),
        "g_final": 1.0 + nrm(ks[18], (D_MODEL,), 0.02),
    }


def reference(x, g_mix, w_in, b_in, w_gla_gate, b_gla_gate, g_gla_head, w_gla_out, sinks,
              w_swa_out, w_out, g_ffn, w_router, b_router, w_e1, b_e1, w_e2, b_e2, g_final):
    bsz, s, _ = x.shape
    pos = jnp.arange(s, dtype=jnp.int32)
    for l in range(DEPTH):
        h = rms_norm(x, g_mix[l])
        proj = h @ w_in[l] + b_in[l]
        parts = []
        off = 0
        for size in SPLIT_SIZES:
            parts.append(proj[..., off:off + size])
            off += size
        gq, gk, gv, gr, g_lr, sq, sk, sv, gate_a, gate_b = parts

        log_g = jax.nn.log_sigmoid((g_lr @ w_gla_gate[l] + b_gla_gate[l]).astype(jnp.float32)) / GLA_TAU
        o_a = gla_chunked(gq.reshape(bsz, s, GLA_HEADS, GLA_DK),
                          gk.reshape(bsz, s, GLA_HEADS, GLA_DK),
                          gv.reshape(bsz, s, GLA_HEADS, GLA_DV),
                          log_g.reshape(bsz, s, GLA_HEADS, GLA_DK))
        o_a = o_a * lax.rsqrt(jnp.mean(o_a * o_a, axis=-1, keepdims=True) + NORM_EPS) \
            * g_gla_head[l].astype(jnp.float32)
        o_a = o_a.reshape(bsz, s, GLA_HEADS * GLA_DV).astype(x.dtype) * jax.nn.silu(gr)
        y_a = o_a @ w_gla_out[l]

        q = rope(sq.reshape(bsz, s, SWA_Q_HEADS, SWA_HEAD_DIM), pos)
        k = rope(sk.reshape(bsz, s, SWA_KV_HEADS, SWA_HEAD_DIM), pos)
        v = sv.reshape(bsz, s, SWA_KV_HEADS, SWA_HEAD_DIM)
        y_b = swa_with_sinks(q, k, v, sinks[l]) @ w_swa_out[l]

        mixed = jax.nn.sigmoid(gate_a) * y_a + jax.nn.sigmoid(gate_b) * y_b
        x = x + mixed @ w_out[l]

        x = x + moe_ffn(rms_norm(x, g_ffn[l]), w_router[l], b_router[l],
                        w_e1[l], b_e1[l], w_e2[l], b_e2[l])
    return rms_norm(x, g_final)
```

```python
import functools

import jax
import jax.numpy as jnp
from jax import lax
from jax.experimental import pallas as pl
from jax.experimental.pallas import tpu as pltpu

F32 = jnp.float32
BF16 = jnp.bfloat16
I32 = jnp.int32

NORM_EPS = 1e-5
GLA_HEADS = 4
GLA_DK = 128
GLA_DV = 256
GLA_GATE_RANK = 16
GLA_TAU = 16.0
SWA_Q_HEADS = 16
SWA_KV_HEADS = 2
SWA_GROUP = SWA_Q_HEADS // SWA_KV_HEADS
SWA_HEAD_DIM = 64
SWA_WINDOW = 128
ROPE_THETA = 10000.0
N_EXPERTS = 32
TOP_K = 4
SWIGLU_LIMIT = 7.0
SWIGLU_ALPHA = 1.702

LANES = 128
NEG = -1e30
VMEM_LIMIT = 56 * 1024 * 1024

PROJ_TM = 256
GLA_CHUNK = 64
GLA_SUB = 16
MERGE_TM = 512
ROUTE_TT = 512
MOE_BLOCK = 256
COPY_ROWS = 2048
FINAL_TM = 256

_D = 1024
_C_GQ = 0
_C_GK = _C_GQ + GLA_HEADS * GLA_DK
_C_GV = _C_GK + GLA_HEADS * GLA_DK
_C_GR = _C_GV + GLA_HEADS * GLA_DV
_C_SQ = _C_GR + GLA_HEADS * GLA_DV
_C_GA = _C_SQ + SWA_Q_HEADS * SWA_HEAD_DIM
_C_GB = _C_GA + _D
_C_SK = _C_GB + _D
_C_SV = _C_SK + SWA_KV_HEADS * LANES
_C_LR = _C_SV + SWA_KV_HEADS * LANES
_C_END = _C_LR + LANES


def _cparams(sem):
    return pltpu.CompilerParams(dimension_semantics=sem, vmem_limit_bytes=VMEM_LIMIT)


def _rms(x, g):
    return x * lax.rsqrt(jnp.mean(x * x, axis=-1, keepdims=True) + NORM_EPS) * g


def _sigmoid(x):
    return 1.0 / (1.0 + jnp.exp(-x))


def _dot_nt(a, b):
    return lax.dot_general(a, b, (((1,), (1,)), ((), ())), preferred_element_type=F32)


def _rope_slabs(acc, cos, sin, first_half):
    outs = []
    for i in range(acc.shape[1] // LANES):
        xs = acc[:, LANES * i:LANES * (i + 1)]
        partner = jnp.where(first_half, pltpu.roll(xs, LANES - 32, 1), pltpu.roll(xs, 32, 1))
        outs.append(xs * cos + partner * sin)
    return jnp.concatenate(outs, axis=1)


def _in_proj_kernel(x_ref, g_ref, w_ref, b_ref, cos_ref, sin_ref, wg_ref, bg_ref,
                    gq_ref, gk_ref, gv_ref, gr_ref, sq_ref, ga_ref, gb_ref, sk_ref, sv_ref, lg_ref):
    h = _rms(x_ref[...], g_ref[...]).astype(BF16)

    def proj(lo, hi):
        return jnp.dot(h, w_ref[:, lo:hi], preferred_element_type=F32) + b_ref[:, lo:hi]

    cos = cos_ref[...]
    sin = sin_ref[...]
    lane = lax.broadcasted_iota(I32, cos.shape, 1)
    first_half = (lane % SWA_HEAD_DIM) < (SWA_HEAD_DIM // 2)

    gq_ref[...] = (proj(_C_GQ, _C_GK) * (GLA_DK ** -0.5)).astype(BF16)
    gk_ref[...] = proj(_C_GK, _C_GV).astype(BF16)
    gv_ref[...] = proj(_C_GV, _C_GR).astype(BF16)
    gr = proj(_C_GR, _C_SQ)
    gr_ref[...] = (gr * _sigmoid(gr)).astype(BF16)
    sq = proj(_C_SQ, _C_GA) * (SWA_HEAD_DIM ** -0.5)
    sq_ref[...] = _rope_slabs(sq, cos, sin, first_half).astype(BF16)
    ga_ref[...] = _sigmoid(proj(_C_GA, _C_GB)).astype(BF16)
    gb_ref[...] = _sigmoid(proj(_C_GB, _C_SK)).astype(BF16)
    sk_ref[...] = _rope_slabs(proj(_C_SK, _C_SV), cos, sin, first_half).astype(BF16)
    sv_ref[...] = proj(_C_SV, _C_LR).astype(BF16)
    z = jnp.dot(proj(_C_LR, _C_END), wg_ref[...], precision=lax.Precision.HIGHEST,
                preferred_element_type=F32) + bg_ref[...]
    log_sig = jnp.minimum(z, 0.0) - jnp.log(1.0 + jnp.exp(-jnp.abs(z)))
    lg_ref[...] = log_sig * (1.0 / GLA_TAU)


def _in_proj(x2, g_mix, w_all, b_all, cos_t, sin_t, wg, bg, seq):
    t = x2.shape[0]
    tm = PROJ_TM
    pos_blocks = seq // tm
    const = lambda i: (0, 0)
    row = lambda i: (i, 0)
    widths = [(_C_GK - _C_GQ, BF16), (_C_GV - _C_GK, BF16), (_C_GR - _C_GV, BF16), (_C_SQ - _C_GR, BF16),
              (_C_GA - _C_SQ, BF16), (_D, BF16), (_D, BF16), (_C_SV - _C_SK, BF16), (_C_LR - _C_SV, BF16),
              (GLA_HEADS * GLA_DK, F32)]
    return pl.pallas_call(
        _in_proj_kernel,
        grid=(t // tm,),
        in_specs=[
            pl.BlockSpec((tm, _D), row),
            pl.BlockSpec((1, _D), const),
            pl.BlockSpec((_D, _C_END), const, pipeline_mode=pl.Buffered(1)),
            pl.BlockSpec((1, _C_END), const),
            pl.BlockSpec((tm, LANES), lambda i: (i % pos_blocks, 0)),
            pl.BlockSpec((tm, LANES), lambda i: (i % pos_blocks, 0)),
            pl.BlockSpec((LANES, GLA_HEADS * GLA_DK), const),
            pl.BlockSpec((1, GLA_HEADS * GLA_DK), const),
        ],
        out_specs=[pl.BlockSpec((tm, w), row) for w, _ in widths],
        out_shape=[jax.ShapeDtypeStruct((t, w), dt) for w, dt in widths],
        compiler_params=_cparams(("parallel",)),
        name="in_proj",
    )(x2, g_mix, w_all, b_all, cos_t, sin_t, wg, bg)


def _gla_kernel(q_ref, k_ref, v_ref, lg_ref, gr_ref, gh_ref, o_ref, st_ref):
    c_len, sub = GLA_CHUNK, GLA_SUB
    n_sub = c_len // sub
    seq = q_ref.shape[0]
    st_ref[...] = jnp.zeros_like(st_ref)

    r_i = lax.broadcasted_iota(I32, (c_len, c_len), 0)
    c_i = lax.broadcasted_iota(I32, (c_len, c_len), 1)
    tri = (r_i >= c_i).astype(F32)
    row_c = lax.broadcasted_iota(I32, (c_len, 1), 0)
    row_s = lax.broadcasted_iota(I32, (sub, 1), 0)
    col_s = lax.broadcasted_iota(I32, (sub, c_len), 1)

    def chunk(c, carry):
        r0 = pl.multiple_of(c * c_len, c_len)
        q = q_ref[pl.ds(r0, c_len), :].astype(F32)
        k = k_ref[pl.ds(r0, c_len), :].astype(F32)
        v = v_ref[pl.ds(r0, c_len), :]
        b = jnp.dot(tri, lg_ref[pl.ds(r0, c_len), :], precision=lax.Precision.HIGHEST,
                    preferred_element_type=F32)
        b_last = b[c_len - 1:c_len, :]
        k_bf = k.astype(BF16)

        st = st_ref[...]
        o = _dot_nt((q * jnp.exp(b)).astype(BF16), st.astype(BF16))

        g_rows = []
        for i in range(n_sub):
            qi = q[sub * i:sub * (i + 1), :]
            bi = b[sub * i:sub * (i + 1), :]
            for j in range(sub):
                e = jnp.exp(jnp.where(row_s >= j, bi - bi[j:j + 1, :], NEG))
                g_rows.append(qi * e)
        r_all = _dot_nt(jnp.concatenate(g_rows, axis=0).astype(BF16), k_bf)

        a_rows = []
        for i in range(n_sub):
            a_i = jnp.zeros((sub, c_len), F32)
            for j in range(sub):
                base = (i * sub + j) * sub
                a_i = a_i + jnp.where(col_s == i * sub + j, r_all[base:base + sub, :], 0.0)
            if i > 0:
                beta = b[sub * i - 1:sub * i, :]
                qh = (q[sub * i:sub * (i + 1), :] * jnp.exp(b[sub * i:sub * (i + 1), :] - beta)).astype(BF16)
                kh = (k * jnp.exp(jnp.where(row_c < sub * i, beta - b, NEG))).astype(BF16)
                a_i = a_i + _dot_nt(qh, kh)
            a_rows.append(a_i)
        a = jnp.concatenate(a_rows, axis=0)
        o = o + jnp.dot(a.astype(BF16), v, preferred_element_type=F32)

        k_end = (k * jnp.exp(b_last - b)).astype(BF16)
        upd = lax.dot_general(v, k_end, (((0,), (0,)), ((), ())), preferred_element_type=F32)
        st_ref[...] = st * jnp.exp(b_last) + upd

        on = _rms(o, gh_ref[...])
        o_ref[pl.ds(r0, c_len), :] = (on * gr_ref[pl.ds(r0, c_len), :].astype(F32)).astype(BF16)
        return carry

    lax.fori_loop(0, seq // c_len, chunk, 0)


def _gla(gq, gk, gv, lg, gr, g_head, bsz, seq):
    t = gq.shape[0]
    return pl.pallas_call(
        _gla_kernel,
        grid=(bsz, GLA_HEADS),
        in_specs=[
            pl.BlockSpec((seq, GLA_DK), lambda b, h: (b, h)),
            pl.BlockSpec((seq, GLA_DK), lambda b, h: (b, h)),
            pl.BlockSpec((seq, GLA_DV), lambda b, h: (b, h)),
            pl.BlockSpec((seq, GLA_DK), lambda b, h: (b, h)),
            pl.BlockSpec((seq, GLA_DV), lambda b, h: (b, h)),
            pl.BlockSpec((1, GLA_DV), lambda b, h: (0, 0)),
        ],
        out_specs=pl.BlockSpec((seq, GLA_DV), lambda b, h: (b, h)),
        out_shape=jax.ShapeDtypeStruct((t, GLA_HEADS * GLA_DV), BF16),
        scratch_shapes=[pltpu.VMEM((GLA_DV, GLA_DK), F32)],
        compiler_params=_cparams(("parallel", "parallel")),
        name="gla",
    )(gq, gk, gv, lg, gr, g_head)


def _swa_kernel(sink_ref, q_ref, k_ref, v_ref, o_ref):
    w = SWA_WINDOW
    seq = q_ref.shape[0]
    hk = pl.program_id(1)
    lane_q = lax.broadcasted_iota(I32, (w, LANES), 1)
    low_q = lane_q < SWA_HEAD_DIM
    lane_b = lax.broadcasted_iota(I32, (2 * w, LANES), 1)
    low_b = lane_b < SWA_HEAD_DIM
    qi = lax.broadcasted_iota(I32, (w, 2 * w), 0)
    kj = lax.broadcasted_iota(I32, (w, 2 * w), 1)
    in_window = (kj > qi) & (kj <= qi + w)
    zero_q = jnp.zeros((w, LANES), BF16)
    zero_b = jnp.zeros((2 * w, LANES), BF16)

    def block(n, carry):
        r0 = pl.multiple_of(n * w, w)
        p0 = pl.multiple_of(jnp.maximum(n - 1, 0) * w, w)
        kb = jnp.concatenate([k_ref[pl.ds(p0, w), :], k_ref[pl.ds(r0, w), :]], axis=0)
        vb = jnp.concatenate([v_ref[pl.ds(p0, w), :], v_ref[pl.ds(r0, w), :]], axis=0)
        valid = in_window & ((kj >= w) | (n > 0))
        v_lo = jnp.where(low_b, vb, zero_b)
        v_hi = jnp.where(low_b, zero_b, vb)
        for m in range(SWA_GROUP // 2):
            qp = q_ref[pl.ds(r0, w), LANES * m:LANES * (m + 1)]
            acc = jnp.zeros((w, LANES), F32)
            for par in range(2):
                qm = jnp.where(low_q, qp, zero_q) if par == 0 else jnp.where(low_q, zero_q, qp)
                s = jnp.where(valid, _dot_nt(qm, kb), NEG)
                sink = sink_ref[hk * SWA_GROUP + 2 * m + par]
                mx = jnp.maximum(jnp.max(s, axis=-1, keepdims=True), sink)
                p = jnp.exp(s - mx)
                den = jnp.sum(p, axis=-1, keepdims=True) + jnp.exp(sink - mx)
                pv = jnp.dot(p.astype(BF16), v_lo if par == 0 else v_hi, preferred_element_type=F32)
                acc = acc + pv / den
            o_ref[pl.ds(r0, w), LANES * m:LANES * (m + 1)] = acc.astype(BF16)
        return carry

    lax.fori_loop(0, seq // w, block, 0)


def _swa(sinks, sq, sk, sv, bsz, seq):
    t = sq.shape[0]
    gw = SWA_GROUP * SWA_HEAD_DIM
    return pl.pallas_call(
        _swa_kernel,
        grid_spec=pltpu.PrefetchScalarGridSpec(
            num_scalar_prefetch=1,
            grid=(bsz, SWA_KV_HEADS),
            in_specs=[
                pl.BlockSpec((seq, gw), lambda b, h, s: (b, h)),
                pl.BlockSpec((seq, LANES), lambda b, h, s: (b, h)),
                pl.BlockSpec((seq, LANES), lambda b, h, s: (b, h)),
            ],
            out_specs=pl.BlockSpec((seq, gw), lambda b, h, s: (b, h)),
        ),
        out_shape=jax.ShapeDtypeStruct((t, SWA_Q_HEADS * SWA_HEAD_DIM), BF16),
        compiler_params=_cparams(("parallel", "parallel")),
        name="swa",
    )(sinks, sq, sk, sv)


def _merge_kernel(x_ref, oa_ref, ob_ref, ga_ref, gb_ref, wa_ref, wb_ref, wo_ref, gf_ref, wr_ref, br_ref,
                  x1_ref, hn_ref, lt_ref):
    ya = jnp.dot(oa_ref[...], wa_ref[...], preferred_element_type=F32)
    yb = jnp.dot(ob_ref[...], wb_ref[...], preferred_element_type=F32)
    mixed = ga_ref[...].astype(F32) * ya + gb_ref[...].astype(F32) * yb
    x1 = x_ref[...] + jnp.dot(mixed.astype(BF16), wo_ref[...], preferred_element_type=F32)
    x1_ref[...] = x1
    hn = _rms(x1, gf_ref[...])
    hn_ref[...] = hn
    lt_ref[...] = lax.dot_general(wr_ref[...], hn, (((1,), (1,)), ((), ())),
                                  precision=lax.Precision.HIGHEST,
                                  preferred_element_type=F32) + br_ref[...]


def _merge(x2, oa, ob, ga, gb, wa, wb, wo, g_ffn, wr_t, br_col):
    t = x2.shape[0]
    tm = MERGE_TM
    row = lambda i: (i, 0)
    const = lambda i: (0, 0)
    return pl.pallas_call(
        _merge_kernel,
        grid=(t // tm,),
        in_specs=[pl.BlockSpec((tm, _D), row)] * 5 + [pl.BlockSpec((_D, _D), const)] * 3 + [
            pl.BlockSpec((1, _D), const),
            pl.BlockSpec((N_EXPERTS, _D), const),
            pl.BlockSpec((N_EXPERTS, 1), const),
        ],
        out_specs=[pl.BlockSpec((tm, _D), row), pl.BlockSpec((tm, _D), row),
                   pl.BlockSpec((N_EXPERTS, tm), lambda i: (0, i))],
        out_shape=[jax.ShapeDtypeStruct((t, _D), F32), jax.ShapeDtypeStruct((t, _D), F32),
                   jax.ShapeDtypeStruct((N_EXPERTS, t), F32)],
        compiler_params=_cparams(("parallel",)),
        name="merge",
    )(x2, oa, ob, ga, gb, wa, wb, wo, g_ffn, wr_t, br_col)


def _route_kernel(lt_ref, e_ref, g_ref, r_ref, cnt_ref, carry_ref):
    tt = lt_ref.shape[1]

    @pl.when(pl.program_id(0) == 0)
    def _():
        carry_ref[...] = jnp.zeros_like(carry_ref)

    eid = lax.broadcasted_iota(I32, (N_EXPERTS, tt), 0)
    work = lt_ref[...]
    vals, idxs = [], []
    chosen = jnp.zeros((N_EXPERTS, tt), F32)
    for _ in range(TOP_K):
        m = jnp.max(work, axis=0, keepdims=True)
        idx = jnp.min(jnp.where(work == m, eid, N_EXPERTS), axis=0, keepdims=True)
        hit = eid == idx
        work = jnp.where(hit, -jnp.inf, work)
        chosen = jnp.where(hit, 1.0, chosen)
        vals.append(m)
        idxs.append(idx)
    ex = [jnp.exp(v - vals[0]) for v in vals]
    den = ex[0] + ex[1] + ex[2] + ex[3]

    t_r = lax.broadcasted_iota(I32, (tt, tt), 0)
    t_c = lax.broadcasted_iota(I32, (tt, tt), 1)
    before = (t_r < t_c).astype(BF16)
    pref = jnp.dot(chosen.astype(BF16), before, preferred_element_type=F32) + carry_ref[:, 0:1]
    for k in range(TOP_K):
        e_ref[k:k + 1, :] = idxs[k]
        g_ref[k:k + 1, :] = ex[k] / den
        r_ref[k:k + 1, :] = jnp.sum(jnp.where(eid == idxs[k], pref, 0.0), axis=0, keepdims=True).astype(I32)
    total = pref[:, tt - 1:tt] + chosen[:, tt - 1:tt]
    carry_ref[...] = jnp.broadcast_to(total, carry_ref.shape)
    cnt_ref[...] = jnp.broadcast_to(total, cnt_ref.shape)


def _route(logits_t):
    t = logits_t.shape[1]
    tt = ROUTE_TT
    blk = lambda i: (0, i)
    return pl.pallas_call(
        _route_kernel,
        grid=(t // tt,),
        in_specs=[pl.BlockSpec((N_EXPERTS, tt), blk)],
        out_specs=[pl.BlockSpec((TOP_K, tt), blk), pl.BlockSpec((TOP_K, tt), blk), pl.BlockSpec((TOP_K, tt), blk),
                   pl.BlockSpec((N_EXPERTS, LANES), lambda i: (0, 0))],
        out_shape=[jax.ShapeDtypeStruct((TOP_K, t), I32), jax.ShapeDtypeStruct((TOP_K, t), F32),
                   jax.ShapeDtypeStruct((TOP_K, t), I32), jax.ShapeDtypeStruct((N_EXPERTS, LANES), F32)],
        scratch_shapes=[pltpu.VMEM((N_EXPERTS, LANES), F32)],
        compiler_params=_cparams(("arbitrary",)),
        name="route",
    )(logits_t)


def _row_copy(src_hbm, dst_hbm, s, d, sem):
    return pltpu.make_async_copy(src_hbm.at[pl.ds(s, 1)], dst_hbm.at[pl.ds(d, 1)], sem)


def _dispatch_kernel(dest_ref, hn_hbm, init_hbm, xs_hbm, sem):
    del init_hbm
    n_tok = dest_ref.shape[2] // TOP_K
    t0 = pl.program_id(0) * n_tok

    def issue(i, carry):
        for k in range(TOP_K):
            _row_copy(hn_hbm, xs_hbm, t0 + i, dest_ref[0, 0, i * TOP_K + k], sem).start()
        return carry

    def drain(i, carry):
        for k in range(TOP_K):
            _row_copy(hn_hbm, xs_hbm, 0, 0, sem).wait()
        return carry

    lax.fori_loop(0, n_tok, issue, 0)
    lax.fori_loop(0, n_tok, drain, 0)


def _dispatch(dest_flat, hn, rows):
    t = hn.shape[0]
    n = COPY_ROWS
    steps = (t * TOP_K) // n
    init = jnp.zeros((rows, _D), F32)
    return pl.pallas_call(
        _dispatch_kernel,
        grid=(steps,),
        in_specs=[pl.BlockSpec((1, 1, n), lambda i: (i, 0, 0), memory_space=pltpu.SMEM),
                  pl.BlockSpec(memory_space=pl.ANY), pl.BlockSpec(memory_space=pl.ANY)],
        out_specs=pl.BlockSpec(memory_space=pl.ANY),
        out_shape=jax.ShapeDtypeStruct((rows, _D), F32),
        scratch_shapes=[pltpu.SemaphoreType.DMA],
        input_output_aliases={2: 0},
        compiler_params=_cparams(("arbitrary",)),
        name="dispatch",
    )(dest_flat.reshape(steps, 1, n), hn, init)


def _combine_kernel(dest_ref, ys_hbm, y4_hbm, sem):
    n = dest_ref.shape[2]
    j0 = pl.program_id(0) * n

    def issue(i, carry):
        _row_copy(ys_hbm, y4_hbm, dest_ref[0, 0, i], j0 + i, sem).start()
        return carry

    def drain(i, carry):
        _row_copy(ys_hbm, y4_hbm, 0, 0, sem).wait()
        return carry

    lax.fori_loop(0, n, issue, 0, unroll=8)
    lax.fori_loop(0, n, drain, 0, unroll=8)


def _combine(dest_flat, ys):
    n = COPY_ROWS
    total = dest_flat.shape[0]
    steps = total // n
    return pl.pallas_call(
        _combine_kernel,
        grid=(steps,),
        in_specs=[pl.BlockSpec((1, 1, n), lambda i: (i, 0, 0), memory_space=pltpu.SMEM),
                  pl.BlockSpec(memory_space=pl.ANY)],
        out_specs=pl.BlockSpec(memory_space=pl.ANY),
        out_shape=jax.ShapeDtypeStruct((total, _D), F32),
        scratch_shapes=[pltpu.SemaphoreType.DMA],
        compiler_params=_cparams(("arbitrary",)),
        name="combine",
    )(dest_flat.reshape(steps, 1, n), ys)


def _experts_kernel(be_ref, nv_ref, xs_ref, w1g_ref, w1l_ref, b1g_ref, b1l_ref, w2_ref, b2_ref, ys_ref):
    del be_ref

    @pl.when(pl.program_id(0) < nv_ref[0])
    def _():
        x = xs_ref[...].astype(BF16)
        ug = jnp.dot(x, w1g_ref[0], preferred_element_type=F32) + b1g_ref[0]
        ul = jnp.dot(x, w1l_ref[0], preferred_element_type=F32) + b1l_ref[0]
        g = jnp.minimum(ug, SWIGLU_LIMIT)
        lin = jnp.clip(ul, -SWIGLU_LIMIT, SWIGLU_LIMIT)
        act = g * _sigmoid(SWIGLU_ALPHA * g) * (lin + 1.0)
        ys_ref[...] = jnp.dot(act.astype(BF16), w2_ref[0], preferred_element_type=F32) + b2_ref[0]

    @pl.when(pl.program_id(0) >= nv_ref[0])
    def _():
        ys_ref[...] = jnp.zeros_like(ys_ref)


def _experts(block_e, n_valid, xs, w1g, w1l, b1g, b1l, w2, b2):
    rows = xs.shape[0]
    n_blocks = rows // MOE_BLOCK
    dff = w1g.shape[2]
    wmap = lambda i, be, nv: (be[i], 0, 0)
    rmap = lambda i, be, nv: (i, 0)
    return pl.pallas_call(
        _experts_kernel,
        grid_spec=pltpu.PrefetchScalarGridSpec(
            num_scalar_prefetch=2,
            grid=(n_blocks,),
            in_specs=[
                pl.BlockSpec((MOE_BLOCK, _D), rmap),
                pl.BlockSpec((1, _D, dff), wmap),
                pl.BlockSpec((1, _D, dff), wmap),
                pl.BlockSpec((1, 1, dff), wmap),
                pl.BlockSpec((1, 1, dff), wmap),
                pl.BlockSpec((1, dff, _D), wmap),
                pl.BlockSpec((1, 1, _D), wmap),
            ],
            out_specs=pl.BlockSpec((MOE_BLOCK, _D), rmap),
        ),
        out_shape=jax.ShapeDtypeStruct((rows, _D), F32),
        compiler_params=_cparams(("arbitrary",)),
        name="experts",
    )(block_e, n_valid, xs, w1g, w1l, b1g, b1l, w2, b2)


def _final_kernel(x1_ref, y4_ref, gt_ref, gf_ref, o_ref):
    gates = gt_ref[...]
    y = x1_ref[...]
    for k in range(TOP_K):
        y = y + gates[:, k:k + 1] * y4_ref[:, _D * k:_D * (k + 1)]
    o_ref[...] = _rms(y, gf_ref[...])


def _final(x1, y4, gates_tk, g_final):
    t = x1.shape[0]
    tm = FINAL_TM
    row = lambda i: (i, 0)
    return pl.pallas_call(
        _final_kernel,
        grid=(t // tm,),
        in_specs=[pl.BlockSpec((tm, _D), row), pl.BlockSpec((tm, TOP_K * _D), row),
                  pl.BlockSpec((tm, TOP_K), row), pl.BlockSpec((1, _D), lambda i: (0, 0))],
        out_specs=pl.BlockSpec((tm, _D), row),
        out_shape=jax.ShapeDtypeStruct((t, _D), F32),
        compiler_params=_cparams(("parallel",)),
        name="final",
    )(x1, y4, gates_tk, g_final)


def _prep_in_proj(w_in, b_in, w_gate, b_gate):
    sizes = (GLA_HEADS * GLA_DK, GLA_HEADS * GLA_DK, GLA_HEADS * GLA_DV, GLA_HEADS * GLA_DV, GLA_GATE_RANK,
             SWA_Q_HEADS * SWA_HEAD_DIM, SWA_KV_HEADS * SWA_HEAD_DIM, SWA_KV_HEADS * SWA_HEAD_DIM, _D, _D)
    offs = [0]
    for s in sizes:
        offs.append(offs[-1] + s)

    def rearrange(m):
        p = [m[..., offs[i]:offs[i + 1]] for i in range(len(sizes))]
        gq, gk, gv, gr, lr, sq, sk, sv, ga, gb = p

        def dup_heads(a):
            hs = [a[..., SWA_HEAD_DIM * h:SWA_HEAD_DIM * (h + 1)] for h in range(SWA_KV_HEADS)]
            return jnp.concatenate([hh for h in hs for hh in (h, h)], axis=-1)

        lr_pad = jnp.pad(lr, [(0, 0)] * (lr.ndim - 1) + [(0, LANES - GLA_GATE_RANK)])
        return jnp.concatenate([gq, gk, gv, gr, sq, ga, gb, dup_heads(sk), dup_heads(sv), lr_pad], axis=-1)

    w_all = rearrange(w_in).astype(BF16)
    b_all = rearrange(b_in[None, :])
    wg = jnp.pad(w_gate, ((0, LANES - GLA_GATE_RANK), (0, 0)))
    return w_all, b_all, wg, b_gate[None, :]


def _rope_tables(seq):
    half = SWA_HEAD_DIM // 2
    inv_freq = ROPE_THETA ** (-jnp.arange(half, dtype=F32) / half)
    ang = jnp.arange(seq, dtype=F32)[:, None] * inv_freq[None, :]
    cos, sin = jnp.cos(ang), jnp.sin(ang)
    cos_t = jnp.concatenate([cos, cos] * (LANES // SWA_HEAD_DIM), axis=1)
    sin_t = jnp.concatenate([-sin, sin] * (LANES // SWA_HEAD_DIM), axis=1)
    return cos_t, sin_t


def kernel(x, g_mix, w_in, b_in, w_gla_gate, b_gla_gate, g_gla_head, w_gla_out, sinks, w_swa_out, w_out,
           g_ffn, w_router, b_router, w_e1, b_e1, w_e2, b_e2, g_final):
    bsz, seq, d = x.shape
    assert d == _D and w_in.shape[0] == 1, "single-layer, d_model=1024 only"
    assert seq % max(PROJ_TM, SWA_WINDOW, GLA_CHUNK) == 0
    t = bsz * seq
    assert t % max(MERGE_TM, ROUTE_TT, FINAL_TM) == 0 and (t * TOP_K) % COPY_ROWS == 0
    x2 = x.reshape(t, d)

    w_all, b_all, wg, bg = _prep_in_proj(w_in[0], b_in[0], w_gla_gate[0], b_gla_gate[0])
    cos_t, sin_t = _rope_tables(seq)
    gq, gk, gv, gr, sq, ga, gb, sk, sv, lg = _in_proj(x2, g_mix, w_all, b_all, cos_t, sin_t, wg, bg, seq)
    oa = _gla(gq, gk, gv, lg, gr, g_gla_head, bsz, seq)
    ob = _swa(sinks[0], sq, sk, sv, bsz, seq)
    x1, hn, logits_t = _merge(x2, oa, ob, ga, gb, w_gla_out[0].astype(BF16), w_swa_out[0].astype(BF16),
                              w_out[0].astype(BF16), g_ffn, w_router[0].T, b_router[0][:, None])

    e_kt, g_kt, r_kt, cnt = _route(logits_t)
    counts = cnt[:, 0].astype(I32)
    blocks_e = (counts + MOE_BLOCK - 1) // MOE_BLOCK
    bend = jnp.cumsum(blocks_e)
    pstart = (bend - blocks_e) * MOE_BLOCK
    onehot = e_kt[:, :, None] == jnp.arange(N_EXPERTS, dtype=I32)[None, None, :]
    dest_kt = r_kt + jnp.sum(jnp.where(onehot, pstart[None, None, :], 0), axis=-1)
    dest_flat = dest_kt.T.reshape(t * TOP_K)
    n_blocks = (t * TOP_K) // MOE_BLOCK + N_EXPERTS
    block_e = jnp.minimum(jnp.searchsorted(bend, jnp.arange(n_blocks, dtype=I32), side="right"),
                          N_EXPERTS - 1).astype(I32)
    n_valid = bend[-1:].astype(I32)

    dff = w_e2.shape[2]
    w1 = w_e1[0]
    w1g = w1[:, :, 0::2].astype(BF16)
    w1l = w1[:, :, 1::2].astype(BF16)
    b1g = b_e1[0][:, None, 0::2]
    b1l = b_e1[0][:, None, 1::2]
    del dff
    xs = _dispatch(dest_flat, hn, n_blocks * MOE_BLOCK)
    ys = _experts(block_e, n_valid, xs, w1g, w1l, b1g, b1l, w_e2[0].astype(BF16), b_e2[0][:, None, :])
    y4 = _combine(dest_flat, ys).reshape(t, TOP_K * d)

    out = _final(x1, y4, g_kt.T, g_final[None, :])
    return out.reshape(bsz, seq, d)
```

```python
import functools

import jax
import jax.numpy as jnp
from jax import lax
from jax.experimental import pallas as pl
from jax.experimental.pallas import tpu as pltpu

F32 = jnp.float32
BF16 = jnp.bfloat16
I32 = jnp.int32

NORM_EPS = 1e-5
GLA_HEADS = 4
GLA_DK = 128
GLA_DV = 256
GLA_GATE_RANK = 16
GLA_TAU = 16.0
SWA_Q_HEADS = 16
SWA_KV_HEADS = 2
SWA_GROUP = SWA_Q_HEADS // SWA_KV_HEADS
SWA_HEAD_DIM = 64
SWA_WINDOW = 128
ROPE_THETA = 10000.0
N_EXPERTS = 32
TOP_K = 4
SWIGLU_LIMIT = 7.0
SWIGLU_ALPHA = 1.702

LANES = 128
NEG = -1e30
VMEM_LIMIT = 56 * 1024 * 1024

PROJ_TM = 256
GLA_CHUNK = 64
GLA_SUB = 16
MERGE_TM = 512
ROUTE_TT = 512
MOE_BLOCK = 256
DISPATCH_TM = 512
W1_PREP_COLS = 1024
FINAL_TM = 256

_D = 1024
_C_GQ = 0
_C_GK = _C_GQ + GLA_HEADS * GLA_DK
_C_GV = _C_GK + GLA_HEADS * GLA_DK
_C_GR = _C_GV + GLA_HEADS * GLA_DV
_C_SQ = _C_GR + GLA_HEADS * GLA_DV
_C_GA = _C_SQ + SWA_Q_HEADS * SWA_HEAD_DIM
_C_GB = _C_GA + _D
_C_SK = _C_GB + _D
_C_SV = _C_SK + SWA_KV_HEADS * LANES
_C_LR = _C_SV + SWA_KV_HEADS * LANES
_C_END = _C_LR + LANES


def _cparams(sem):
    return pltpu.CompilerParams(dimension_semantics=sem, vmem_limit_bytes=VMEM_LIMIT)


def _rms(x, g):
    return x * lax.rsqrt(jnp.mean(x * x, axis=-1, keepdims=True) + NORM_EPS) * g


def _sigmoid(x):
    return 1.0 / (1.0 + jnp.exp(-x))


def _dot_nt(a, b):
    return lax.dot_general(a, b, (((1,), (1,)), ((), ())), preferred_element_type=F32)


def _rope_slabs(acc, cos, sin, first_half):
    outs = []
    for i in range(acc.shape[1] // LANES):
        xs = acc[:, LANES * i:LANES * (i + 1)]
        partner = jnp.where(first_half, pltpu.roll(xs, LANES - 32, 1), pltpu.roll(xs, 32, 1))
        outs.append(xs * cos + partner * sin)
    return jnp.concatenate(outs, axis=1)


def _in_proj_kernel(x_ref, g_ref, w_ref, b_ref, cos_ref, sin_ref, wg_ref, bg_ref,
                    gq_ref, gk_ref, gv_ref, gr_ref, sq_ref, ga_ref, gb_ref, sk_ref, sv_ref, lg_ref):
    h = _rms(x_ref[...], g_ref[...]).astype(BF16)

    def proj(lo, hi):
        return jnp.dot(h, w_ref[:, lo:hi], preferred_element_type=F32) + b_ref[:, lo:hi]

    cos = cos_ref[...]
    sin = sin_ref[...]
    lane = lax.broadcasted_iota(I32, cos.shape, 1)
    first_half = (lane % SWA_HEAD_DIM) < (SWA_HEAD_DIM // 2)

    gq_ref[...] = (proj(_C_GQ, _C_GK) * (GLA_DK ** -0.5)).astype(BF16)
    gk_ref[...] = proj(_C_GK, _C_GV).astype(BF16)
    gv_ref[...] = proj(_C_GV, _C_GR).astype(BF16)
    gr = proj(_C_GR, _C_SQ)
    gr_ref[...] = (gr * _sigmoid(gr)).astype(BF16)
    sq = proj(_C_SQ, _C_GA) * (SWA_HEAD_DIM ** -0.5)
    sq_ref[...] = _rope_slabs(sq, cos, sin, first_half).astype(BF16)
    ga_ref[...] = _sigmoid(proj(_C_GA, _C_GB)).astype(BF16)
    gb_ref[...] = _sigmoid(proj(_C_GB, _C_SK)).astype(BF16)
    sk_ref[...] = _rope_slabs(proj(_C_SK, _C_SV), cos, sin, first_half).astype(BF16)
    sv_ref[...] = proj(_C_SV, _C_LR).astype(BF16)
    z = jnp.dot(proj(_C_LR, _C_END), wg_ref[...], precision=lax.Precision.HIGHEST,
                preferred_element_type=F32) + bg_ref[...]
    log_sig = jnp.minimum(z, 0.0) - jnp.log(1.0 + jnp.exp(-jnp.abs(z)))
    lg_ref[...] = log_sig * (1.0 / GLA_TAU)


def _in_proj(x2, g_mix, w_all, b_all, cos_t, sin_t, wg, bg, seq):
    t = x2.shape[0]
    tm = PROJ_TM
    pos_blocks = seq // tm
    const = lambda i: (0, 0)
    row = lambda i: (i, 0)
    widths = [(_C_GK - _C_GQ, BF16), (_C_GV - _C_GK, BF16), (_C_GR - _C_GV, BF16), (_C_SQ - _C_GR, BF16),
              (_C_GA - _C_SQ, BF16), (_D, BF16), (_D, BF16), (_C_SV - _C_SK, BF16), (_C_LR - _C_SV, BF16),
              (GLA_HEADS * GLA_DK, F32)]
    return pl.pallas_call(
        _in_proj_kernel,
        grid=(t // tm,),
        in_specs=[
            pl.BlockSpec((tm, _D), row),
            pl.BlockSpec((1, _D), const),
            pl.BlockSpec((_D, _C_END), const, pipeline_mode=pl.Buffered(1)),
            pl.BlockSpec((1, _C_END), const),
            pl.BlockSpec((tm, LANES), lambda i: (i % pos_blocks, 0)),
            pl.BlockSpec((tm, LANES), lambda i: (i % pos_blocks, 0)),
            pl.BlockSpec((LANES, GLA_HEADS * GLA_DK), const),
            pl.BlockSpec((1, GLA_HEADS * GLA_DK), const),
        ],
        out_specs=[pl.BlockSpec((tm, w), row) for w, _ in widths],
        out_shape=[jax.ShapeDtypeStruct((t, w), dt) for w, dt in widths],
        compiler_params=_cparams(("parallel",)),
        name="in_proj",
    )(x2, g_mix, w_all, b_all, cos_t, sin_t, wg, bg)


def _gla_kernel(q_ref, k_ref, v_ref, lg_ref, gr_ref, gh_ref, o_ref, st_ref):
    c_len, sub = GLA_CHUNK, GLA_SUB
    n_sub = c_len // sub
    seq = q_ref.shape[0]
    st_ref[...] = jnp.zeros_like(st_ref)

    r_i = lax.broadcasted_iota(I32, (c_len, c_len), 0)
    c_i = lax.broadcasted_iota(I32, (c_len, c_len), 1)
    tri = (r_i >= c_i).astype(F32)
    row_c = lax.broadcasted_iota(I32, (c_len, 1), 0)
    row_s = lax.broadcasted_iota(I32, (sub, 1), 0)
    col_s = lax.broadcasted_iota(I32, (sub, c_len), 1)

    def chunk(c, carry):
        r0 = pl.multiple_of(c * c_len, c_len)
        q = q_ref[pl.ds(r0, c_len), :].astype(F32)
        k = k_ref[pl.ds(r0, c_len), :].astype(F32)
        v = v_ref[pl.ds(r0, c_len), :]
        b = jnp.dot(tri, lg_ref[pl.ds(r0, c_len), :], precision=lax.Precision.HIGHEST,
                    preferred_element_type=F32)
        b_last = b[c_len - 1:c_len, :]
        k_bf = k.astype(BF16)

        st = st_ref[...]
        o = _dot_nt((q * jnp.exp(b)).astype(BF16), st.astype(BF16))

        g_rows = []
        for i in range(n_sub):
            qi = q[sub * i:sub * (i + 1), :]
            bi = b[sub * i:sub * (i + 1), :]
            for j in range(sub):
                e = jnp.exp(jnp.where(row_s >= j, bi - bi[j:j + 1, :], NEG))
                g_rows.append(qi * e)
        r_all = _dot_nt(jnp.concatenate(g_rows, axis=0).astype(BF16), k_bf)

        a_rows = []
        for i in range(n_sub):
            a_i = jnp.zeros((sub, c_len), F32)
            for j in range(sub):
                base = (i * sub + j) * sub
                a_i = a_i + jnp.where(col_s == i * sub + j, r_all[base:base + sub, :], 0.0)
            if i > 0:
                beta = b[sub * i - 1:sub * i, :]
                qh = (q[sub * i:sub * (i + 1), :] * jnp.exp(b[sub * i:sub * (i + 1), :] - beta)).astype(BF16)
                kh = (k * jnp.exp(jnp.where(row_c < sub * i, beta - b, NEG))).astype(BF16)
                a_i = a_i + _dot_nt(qh, kh)
            a_rows.append(a_i)
        a = jnp.concatenate(a_rows, axis=0)
        o = o + jnp.dot(a.astype(BF16), v, preferred_element_type=F32)

        k_end = (k * jnp.exp(b_last - b)).astype(BF16)
        upd = lax.dot_general(v, k_end, (((0,), (0,)), ((), ())), preferred_element_type=F32)
        st_ref[...] = st * jnp.exp(b_last) + upd

        on = _rms(o, gh_ref[...])
        o_ref[pl.ds(r0, c_len), :] = (on * gr_ref[pl.ds(r0, c_len), :].astype(F32)).astype(BF16)
        return carry

    lax.fori_loop(0, seq // c_len, chunk, 0)


def _gla(gq, gk, gv, lg, gr, g_head, bsz, seq):
    t = gq.shape[0]
    return pl.pallas_call(
        _gla_kernel,
        grid=(bsz, GLA_HEADS),
        in_specs=[
            pl.BlockSpec((seq, GLA_DK), lambda b, h: (b, h)),
            pl.BlockSpec((seq, GLA_DK), lambda b, h: (b, h)),
            pl.BlockSpec((seq, GLA_DV), lambda b, h: (b, h)),
            pl.BlockSpec((seq, GLA_DK), lambda b, h: (b, h)),
            pl.BlockSpec((seq, GLA_DV), lambda b, h: (b, h)),
            pl.BlockSpec((1, GLA_DV), lambda b, h: (0, 0)),
        ],
        out_specs=pl.BlockSpec((seq, GLA_DV), lambda b, h: (b, h)),
        out_shape=jax.ShapeDtypeStruct((t, GLA_HEADS * GLA_DV), BF16),
        scratch_shapes=[pltpu.VMEM((GLA_DV, GLA_DK), F32)],
        compiler_params=_cparams(("parallel", "parallel")),
        name="gla",
    )(gq, gk, gv, lg, gr, g_head)


def _swa_kernel(sink_ref, q_ref, k_ref, v_ref, o_ref):
    w = SWA_WINDOW
    seq = q_ref.shape[0]
    hk = pl.program_id(1)
    lane_q = lax.broadcasted_iota(I32, (w, LANES), 1)
    low_q = lane_q < SWA_HEAD_DIM
    lane_b = lax.broadcasted_iota(I32, (2 * w, LANES), 1)
    low_b = lane_b < SWA_HEAD_DIM
    qi = lax.broadcasted_iota(I32, (w, 2 * w), 0)
    kj = lax.broadcasted_iota(I32, (w, 2 * w), 1)
    in_window = (kj > qi) & (kj <= qi + w)
    zero_q = jnp.zeros((w, LANES), BF16)
    zero_b = jnp.zeros((2 * w, LANES), BF16)

    def block(n, carry):
        r0 = pl.multiple_of(n * w, w)
        p0 = pl.multiple_of(jnp.maximum(n - 1, 0) * w, w)
        kb = jnp.concatenate([k_ref[pl.ds(p0, w), :], k_ref[pl.ds(r0, w), :]], axis=0)
        vb = jnp.concatenate([v_ref[pl.ds(p0, w), :], v_ref[pl.ds(r0, w), :]], axis=0)
        valid = in_window & ((kj >= w) | (n > 0))
        v_lo = jnp.where(low_b, vb, zero_b)
        v_hi = jnp.where(low_b, zero_b, vb)
        for m in range(SWA_GROUP // 2):
            qp = q_ref[pl.ds(r0, w), LANES * m:LANES * (m + 1)]
            acc = jnp.zeros((w, LANES), F32)
            for par in range(2):
                qm = jnp.where(low_q, qp, zero_q) if par == 0 else jnp.where(low_q, zero_q, qp)
                s = jnp.where(valid, _dot_nt(qm, kb), NEG)
                sink = sink_ref[hk * SWA_GROUP + 2 * m + par]
                mx = jnp.maximum(jnp.max(s, axis=-1, keepdims=True), sink)
                p = jnp.exp(s - mx)
                den = jnp.sum(p, axis=-1, keepdims=True) + jnp.exp(sink - mx)
                pv = jnp.dot(p.astype(BF16), v_lo if par == 0 else v_hi, preferred_element_type=F32)
                acc = acc + pv / den
            o_ref[pl.ds(r0, w), LANES * m:LANES * (m + 1)] = acc.astype(BF16)
        return carry

    lax.fori_loop(0, seq // w, block, 0)


def _swa(sinks, sq, sk, sv, bsz, seq):
    t = sq.shape[0]
    gw = SWA_GROUP * SWA_HEAD_DIM
    return pl.pallas_call(
        _swa_kernel,
        grid_spec=pltpu.PrefetchScalarGridSpec(
            num_scalar_prefetch=1,
            grid=(bsz, SWA_KV_HEADS),
            in_specs=[
                pl.BlockSpec((seq, gw), lambda b, h, s: (b, h)),
                pl.BlockSpec((seq, LANES), lambda b, h, s: (b, h)),
                pl.BlockSpec((seq, LANES), lambda b, h, s: (b, h)),
            ],
            out_specs=pl.BlockSpec((seq, gw), lambda b, h, s: (b, h)),
        ),
        out_shape=jax.ShapeDtypeStruct((t, SWA_Q_HEADS * SWA_HEAD_DIM), BF16),
        compiler_params=_cparams(("parallel", "parallel")),
        name="swa",
    )(sinks, sq, sk, sv)


def _merge_kernel(x_ref, oa_ref, ob_ref, ga_ref, gb_ref, wa_ref, wb_ref, wo_ref, gf_ref, wr_ref, br_ref,
                  x1_ref, hn_ref, lt_ref):
    ya = jnp.dot(oa_ref[...], wa_ref[...], preferred_element_type=F32)
    yb = jnp.dot(ob_ref[...], wb_ref[...], preferred_element_type=F32)
    mixed = ga_ref[...].astype(F32) * ya + gb_ref[...].astype(F32) * yb
    x1 = x_ref[...] + jnp.dot(mixed.astype(BF16), wo_ref[...], preferred_element_type=F32)
    x1_ref[...] = x1
    hn = _rms(x1, gf_ref[...])
    hn_ref[...] = hn
    lt_ref[...] = lax.dot_general(wr_ref[...], hn, (((1,), (1,)), ((), ())),
                                  precision=lax.Precision.HIGHEST,
                                  preferred_element_type=F32) + br_ref[...]


def _merge(x2, oa, ob, ga, gb, wa, wb, wo, g_ffn, wr_t, br_col):
    t = x2.shape[0]
    tm = MERGE_TM
    row = lambda i: (i, 0)
    const = lambda i: (0, 0)
    return pl.pallas_call(
        _merge_kernel,
        grid=(t // tm,),
        in_specs=[pl.BlockSpec((tm, _D), row)] * 5 + [pl.BlockSpec((_D, _D), const)] * 3 + [
            pl.BlockSpec((1, _D), const),
            pl.BlockSpec((N_EXPERTS, _D), const),
            pl.BlockSpec((N_EXPERTS, 1), const),
        ],
        out_specs=[pl.BlockSpec((tm, _D), row), pl.BlockSpec((tm, _D), row),
                   pl.BlockSpec((N_EXPERTS, tm), lambda i: (0, i))],
        out_shape=[jax.ShapeDtypeStruct((t, _D), F32), jax.ShapeDtypeStruct((t, _D), F32),
                   jax.ShapeDtypeStruct((N_EXPERTS, t), F32)],
        compiler_params=_cparams(("parallel",)),
        name="merge",
    )(x2, oa, ob, ga, gb, wa, wb, wo, g_ffn, wr_t, br_col)


def _route_kernel(lt_ref, e_ref, g_ref, r_ref, cnt_ref, carry_ref):
    tt = lt_ref.shape[1]

    @pl.when(pl.program_id(0) == 0)
    def _():
        carry_ref[...] = jnp.zeros_like(carry_ref)

    eid = lax.broadcasted_iota(I32, (N_EXPERTS, tt), 0)
    work = lt_ref[...]
    vals, idxs = [], []
    chosen = jnp.zeros((N_EXPERTS, tt), F32)
    for _ in range(TOP_K):
        m = jnp.max(work, axis=0, keepdims=True)
        idx = jnp.min(jnp.where(work == m, eid, N_EXPERTS), axis=0, keepdims=True)
        hit = eid == idx
        work = jnp.where(hit, -jnp.inf, work)
        chosen = jnp.where(hit, 1.0, chosen)
        vals.append(m)
        idxs.append(idx)
    ex = [jnp.exp(v - vals[0]) for v in vals]
    den = ex[0] + ex[1] + ex[2] + ex[3]

    t_r = lax.broadcasted_iota(I32, (tt, tt), 0)
    t_c = lax.broadcasted_iota(I32, (tt, tt), 1)
    before = (t_r < t_c).astype(BF16)
    pref = jnp.dot(chosen.astype(BF16), before, preferred_element_type=F32) + carry_ref[:, 0:1]
    for k in range(TOP_K):
        e_ref[k:k + 1, :] = idxs[k]
        g_ref[k:k + 1, :] = ex[k] / den
        r_ref[k:k + 1, :] = jnp.sum(jnp.where(eid == idxs[k], pref, 0.0), axis=0, keepdims=True).astype(I32)
    total = pref[:, tt - 1:tt] + chosen[:, tt - 1:tt]
    carry_ref[...] = jnp.broadcast_to(total, carry_ref.shape)
    cnt_ref[...] = jnp.broadcast_to(total, cnt_ref.shape)


def _route(logits_t):
    t = logits_t.shape[1]
    tt = ROUTE_TT
    blk = lambda i: (0, i)
    return pl.pallas_call(
        _route_kernel,
        grid=(t // tt,),
        in_specs=[pl.BlockSpec((N_EXPERTS, tt), blk)],
        out_specs=[pl.BlockSpec((TOP_K, tt), blk), pl.BlockSpec((TOP_K, tt), blk), pl.BlockSpec((TOP_K, tt), blk),
                   pl.BlockSpec((N_EXPERTS, LANES), lambda i: (0, 0))],
        out_shape=[jax.ShapeDtypeStruct((TOP_K, t), I32), jax.ShapeDtypeStruct((TOP_K, t), F32),
                   jax.ShapeDtypeStruct((TOP_K, t), I32), jax.ShapeDtypeStruct((N_EXPERTS, LANES), F32)],
        scratch_shapes=[pltpu.VMEM((N_EXPERTS, LANES), F32)],
        compiler_params=_cparams(("arbitrary",)),
        name="route",
    )(logits_t)


def _dispatch_kernel(dest_ref, hn_ref, init_hbm, xs_hbm, sem):
    del init_hbm
    n_tok = hn_ref.shape[0]

    def issue(i, carry):
        for k in range(TOP_K):
            d = dest_ref[0, 0, i * TOP_K + k]
            pltpu.make_async_copy(hn_ref.at[pl.ds(i, 1)], xs_hbm.at[pl.ds(d, 1)], sem).start()
        return carry

    lax.fori_loop(0, n_tok, issue, 0, unroll=4)
    for k in range(TOP_K):
        pltpu.make_async_copy(hn_ref, xs_hbm.at[pl.ds(0, n_tok)], sem).wait()


def _dispatch(dest_flat, hn, rows):
    t = hn.shape[0]
    tm = DISPATCH_TM
    steps = t // tm
    init = jnp.zeros((rows, _D), F32)
    return pl.pallas_call(
        _dispatch_kernel,
        grid=(steps,),
        in_specs=[pl.BlockSpec((1, 1, tm * TOP_K), lambda i: (i, 0, 0), memory_space=pltpu.SMEM),
                  pl.BlockSpec((tm, _D), lambda i: (i, 0)), pl.BlockSpec(memory_space=pl.ANY)],
        out_specs=pl.BlockSpec(memory_space=pl.ANY),
        out_shape=jax.ShapeDtypeStruct((rows, _D), F32),
        scratch_shapes=[pltpu.SemaphoreType.DMA],
        input_output_aliases={2: 0},
        compiler_params=_cparams(("arbitrary",)),
        name="dispatch",
    )(dest_flat.reshape(steps, 1, tm * TOP_K), hn, init)


def _w1_prep_kernel(w_ref, p_ref, o_ref):
    gw = p_ref.shape[0]
    for gi in range(w_ref.shape[2] // gw):
        wb = w_ref[0, :, gw * gi:gw * (gi + 1)].astype(BF16)
        o_ref[0, :, gw * gi:gw * (gi + 1)] = jnp.dot(wb, p_ref[...], preferred_element_type=F32).astype(BF16)


def _w1_prep(w1, perm):
    n_e, d, n = w1.shape
    cw = W1_PREP_COLS
    return pl.pallas_call(
        _w1_prep_kernel,
        grid=(n_e, n // cw),
        in_specs=[pl.BlockSpec((1, d, cw), lambda e, j: (e, 0, j)),
                  pl.BlockSpec(perm.shape, lambda e, j: (0, 0))],
        out_specs=pl.BlockSpec((1, d, cw), lambda e, j: (e, 0, j)),
        out_shape=jax.ShapeDtypeStruct((n_e, d, n), BF16),
        compiler_params=_cparams(("parallel", "parallel")),
        name="w1_prep",
    )(w1, perm)


def _experts_kernel(be_ref, nv_ref, xs_ref, w1_ref, b1_ref, w2_ref, b2_ref, ys_ref):
    del be_ref

    @pl.when(pl.program_id(0) < nv_ref[0])
    def _():
        x = xs_ref[...].astype(BF16)
        u = jnp.dot(x, w1_ref[0], preferred_element_type=F32) + b1_ref[0]
        acts = []
        for gi in range(u.shape[1] // (2 * LANES)):
            g = jnp.minimum(u[:, 2 * LANES * gi:2 * LANES * gi + LANES], SWIGLU_LIMIT)
            lin = jnp.clip(u[:, 2 * LANES * gi + LANES:2 * LANES * (gi + 1)], -SWIGLU_LIMIT, SWIGLU_LIMIT)
            acts.append((g * _sigmoid(SWIGLU_ALPHA * g) * (lin + 1.0)).astype(BF16))
        act = jnp.concatenate(acts, axis=1)
        ys_ref[...] = jnp.dot(act, w2_ref[0], preferred_element_type=F32) + b2_ref[0]

    @pl.when(pl.program_id(0) >= nv_ref[0])
    def _():
        ys_ref[...] = jnp.zeros_like(ys_ref)


def _experts(block_e, n_valid, xs, w1, b1, w2, b2):
    rows = xs.shape[0]
    n_blocks = rows // MOE_BLOCK
    dff = w2.shape[1]
    wmap = lambda i, be, nv: (be[i], 0, 0)
    rmap = lambda i, be, nv: (i, 0)
    return pl.pallas_call(
        _experts_kernel,
        grid_spec=pltpu.PrefetchScalarGridSpec(
            num_scalar_prefetch=2,
            grid=(n_blocks,),
            in_specs=[
                pl.BlockSpec((MOE_BLOCK, _D), rmap),
                pl.BlockSpec((1, _D, 2 * dff), wmap),
                pl.BlockSpec((1, 1, 2 * dff), wmap),
                pl.BlockSpec((1, dff, _D), wmap),
                pl.BlockSpec((1, 1, _D), wmap),
            ],
            out_specs=pl.BlockSpec((MOE_BLOCK, _D), rmap),
        ),
        out_shape=jax.ShapeDtypeStruct((rows, _D), F32),
        compiler_params=_cparams(("arbitrary",)),
        name="experts",
    )(block_e, n_valid, xs, w1, b1, w2, b2)


def _final_kernel(dest_ref, x1_ref, gt_ref, gf_ref, ys_hbm, o_ref, buf_ref, sem):
    n_tok = x1_ref.shape[0]

    def issue(i, carry):
        for k in range(TOP_K):
            d = dest_ref[0, 0, i * TOP_K + k]
            pltpu.make_async_copy(ys_hbm.at[pl.ds(d, 1)], buf_ref.at[k, pl.ds(i, 1)], sem).start()
        return carry

    lax.fori_loop(0, n_tok, issue, 0, unroll=4)
    for k in range(TOP_K):
        pltpu.make_async_copy(ys_hbm.at[pl.ds(0, n_tok)], buf_ref.at[k], sem).wait()

    gates = gt_ref[...]
    y = x1_ref[...]
    for k in range(TOP_K):
        y = y + gates[:, k:k + 1] * buf_ref[k]
    o_ref[...] = _rms(y, gf_ref[...])


def _final(dest_flat, x1, gates_tk, g_final, ys):
    t = x1.shape[0]
    tm = FINAL_TM
    steps = t // tm
    row = lambda i: (i, 0)
    return pl.pallas_call(
        _final_kernel,
        grid=(steps,),
        in_specs=[pl.BlockSpec((1, 1, tm * TOP_K), lambda i: (i, 0, 0), memory_space=pltpu.SMEM),
                  pl.BlockSpec((tm, _D), row), pl.BlockSpec((tm, TOP_K), row),
                  pl.BlockSpec((1, _D), lambda i: (0, 0)), pl.BlockSpec(memory_space=pl.ANY)],
        out_specs=pl.BlockSpec((tm, _D), row),
        out_shape=jax.ShapeDtypeStruct((t, _D), F32),
        scratch_shapes=[pltpu.VMEM((TOP_K, tm, _D), F32), pltpu.SemaphoreType.DMA],
        compiler_params=_cparams(("arbitrary",)),
        name="final",
    )(dest_flat.reshape(steps, 1, tm * TOP_K), x1, gates_tk, g_final, ys)


def _prep_in_proj(w_in, b_in, w_gate, b_gate):
    sizes = (GLA_HEADS * GLA_DK, GLA_HEADS * GLA_DK, GLA_HEADS * GLA_DV, GLA_HEADS * GLA_DV, GLA_GATE_RANK,
             SWA_Q_HEADS * SWA_HEAD_DIM, SWA_KV_HEADS * SWA_HEAD_DIM, SWA_KV_HEADS * SWA_HEAD_DIM, _D, _D)
    offs = [0]
    for s in sizes:
        offs.append(offs[-1] + s)

    def rearrange(m):
        p = [m[..., offs[i]:offs[i + 1]] for i in range(len(sizes))]
        gq, gk, gv, gr, lr, sq, sk, sv, ga, gb = p

        def dup_heads(a):
            hs = [a[..., SWA_HEAD_DIM * h:SWA_HEAD_DIM * (h + 1)] for h in range(SWA_KV_HEADS)]
            return jnp.concatenate([hh for h in hs for hh in (h, h)], axis=-1)

        lr_pad = jnp.pad(lr, [(0, 0)] * (lr.ndim - 1) + [(0, LANES - GLA_GATE_RANK)])
        return jnp.concatenate([gq, gk, gv, gr, sq, ga, gb, dup_heads(sk), dup_heads(sv), lr_pad], axis=-1)

    w_all = rearrange(w_in).astype(BF16)
    b_all = rearrange(b_in[None, :])
    wg = jnp.pad(w_gate, ((0, LANES - GLA_GATE_RANK), (0, 0)))
    return w_all, b_all, wg, b_gate[None, :]


def _rope_tables(seq):
    half = SWA_HEAD_DIM // 2
    inv_freq = ROPE_THETA ** (-jnp.arange(half, dtype=F32) / half)
    ang = jnp.arange(seq, dtype=F32)[:, None] * inv_freq[None, :]
    cos, sin = jnp.cos(ang), jnp.sin(ang)
    cos_t = jnp.concatenate([cos, cos] * (LANES // SWA_HEAD_DIM), axis=1)
    sin_t = jnp.concatenate([-sin, sin] * (LANES // SWA_HEAD_DIM), axis=1)
    return cos_t, sin_t


def _pair_split_perm():
    src = jnp.arange(2 * LANES, dtype=I32)
    dst = jnp.where(src % 2 == 0, src // 2, LANES + src // 2)
    return (dst[:, None] == jnp.arange(2 * LANES, dtype=I32)[None, :]).astype(BF16)


def kernel(x, g_mix, w_in, b_in, w_gla_gate, b_gla_gate, g_gla_head, w_gla_out, sinks, w_swa_out, w_out,
           g_ffn, w_router, b_router, w_e1, b_e1, w_e2, b_e2, g_final):
    bsz, seq, d = x.shape
    assert d == _D and w_in.shape[0] == 1, "single-layer, d_model=1024 only"
    assert seq % max(PROJ_TM, SWA_WINDOW, GLA_CHUNK) == 0
    t = bsz * seq
    assert t % max(MERGE_TM, ROUTE_TT, FINAL_TM, DISPATCH_TM) == 0
    x2 = x.reshape(t, d)

    w_all, b_all, wg, bg = _prep_in_proj(w_in[0], b_in[0], w_gla_gate[0], b_gla_gate[0])
    cos_t, sin_t = _rope_tables(seq)
    gq, gk, gv, gr, sq, ga, gb, sk, sv, lg = _in_proj(x2, g_mix, w_all, b_all, cos_t, sin_t, wg, bg, seq)
    oa = _gla(gq, gk, gv, lg, gr, g_gla_head, bsz, seq)
    ob = _swa(sinks[0], sq, sk, sv, bsz, seq)
    x1, hn, logits_t = _merge(x2, oa, ob, ga, gb, w_gla_out[0].astype(BF16), w_swa_out[0].astype(BF16),
                              w_out[0].astype(BF16), g_ffn, w_router[0].T, b_router[0][:, None])

    e_kt, g_kt, r_kt, cnt = _route(logits_t)
    counts = cnt[:, 0].astype(I32)
    blocks_e = (counts + MOE_BLOCK - 1) // MOE_BLOCK
    bend = jnp.cumsum(blocks_e)
    pstart = (bend - blocks_e) * MOE_BLOCK
    dest_kt = r_kt
    for e in range(N_EXPERTS):
        dest_kt = dest_kt + jnp.where(e_kt == e, pstart[e], 0)
    dest_flat = dest_kt.T.reshape(t * TOP_K)
    n_blocks = (t * TOP_K) // MOE_BLOCK + N_EXPERTS
    block_e = jnp.minimum(jnp.sum(bend[None, :] <= jnp.arange(n_blocks, dtype=I32)[:, None], axis=1),
                          N_EXPERTS - 1).astype(I32)
    n_valid = bend[-1:].astype(I32)

    w1 = _w1_prep(w_e1[0], _pair_split_perm())
    b1 = b_e1[0].reshape(N_EXPERTS, -1, LANES, 2).transpose(0, 1, 3, 2).reshape(N_EXPERTS, 1, -1)
    xs = _dispatch(dest_flat, hn, n_blocks * MOE_BLOCK)
    ys = _experts(block_e, n_valid, xs, w1, b1, w_e2[0].astype(BF16), b_e2[0][:, None, :])
    out = _final(dest_flat, x1, g_kt.T, g_final[None, :], ys)
    return out.reshape(bsz, seq, d)
```

```python
import jax
import jax.numpy as jnp
import numpy as np
from jax import lax
from jax.experimental import pallas as pl
from jax.experimental.pallas import tpu as pltpu

F32 = jnp.float32
BF16 = jnp.bfloat16
I32 = jnp.int32

NORM_EPS = 1e-5
GLA_HEADS = 4
GLA_DK = 128
GLA_DV = 256
GLA_GATE_RANK = 16
GLA_TAU = 16.0
SWA_Q_HEADS = 16
SWA_KV_HEADS = 2
SWA_GROUP = SWA_Q_HEADS // SWA_KV_HEADS
SWA_HEAD_DIM = 64
SWA_WINDOW = 128
ROPE_THETA = 10000.0
N_EXPERTS = 32
TOP_K = 4
SWIGLU_LIMIT = 7.0
SWIGLU_ALPHA = 1.702

LANES = 128
NEG = -1e30
VMEM_LIMIT = 56 * 1024 * 1024

PROJ_TM = 256
GLA_CHUNK = 128
GLA_HEADS_PER_STEP = 4
MERGE_TM = 512
ROUTE_TT = 512
MOE_BLOCK = 256
DISPATCH_TM = 512
W1_PREP_COLS = 1024
FINAL_TM = 512

_D = 1024
_C_GQ = 0
_C_GK = _C_GQ + GLA_HEADS * GLA_DK
_C_GV = _C_GK + GLA_HEADS * GLA_DK
_C_GR = _C_GV + GLA_HEADS * GLA_DV
_C_SQ = _C_GR + GLA_HEADS * GLA_DV
_C_GA = _C_SQ + SWA_Q_HEADS * SWA_HEAD_DIM
_C_GB = _C_GA + _D
_C_SK = _C_GB + _D
_C_SV = _C_SK + SWA_KV_HEADS * LANES
_C_LR = _C_SV + SWA_KV_HEADS * LANES
_C_END = _C_LR + LANES


def _cparams(sem):
    return pltpu.CompilerParams(dimension_semantics=sem, vmem_limit_bytes=VMEM_LIMIT)


def _rms(x, g):
    return x * lax.rsqrt(jnp.mean(x * x, axis=-1, keepdims=True) + NORM_EPS) * g


def _sigmoid(x):
    return 1.0 / (1.0 + jnp.exp(-x))


def _dot_nt(a, b):
    return lax.dot_general(a, b, (((1,), (1,)), ((), ())), preferred_element_type=F32)


def _rope_slabs(acc, cos, sin, first_half):
    outs = []
    for i in range(acc.shape[1] // LANES):
        xs = acc[:, LANES * i:LANES * (i + 1)]
        partner = jnp.where(first_half, pltpu.roll(xs, LANES - 32, 1), pltpu.roll(xs, 32, 1))
        outs.append(xs * cos + partner * sin)
    return jnp.concatenate(outs, axis=1)


def _in_proj_kernel(x_ref, g_ref, w_ref, b_ref, cos_ref, sin_ref, wg_ref, bg_ref,
                    gq_ref, gk_ref, gv_ref, gr_ref, sq_ref, ga_ref, gb_ref, sk_ref, sv_ref, lg_ref):
    h = _rms(x_ref[...], g_ref[...]).astype(BF16)

    def proj(lo, hi):
        return jnp.dot(h, w_ref[:, lo:hi], preferred_element_type=F32) + b_ref[:, lo:hi]

    cos = cos_ref[...]
    sin = sin_ref[...]
    lane = lax.broadcasted_iota(I32, cos.shape, 1)
    first_half = (lane % SWA_HEAD_DIM) < (SWA_HEAD_DIM // 2)

    gq_ref[...] = (proj(_C_GQ, _C_GK) * (GLA_DK ** -0.5)).astype(BF16)
    gk_ref[...] = proj(_C_GK, _C_GV).astype(BF16)
    gv_ref[...] = proj(_C_GV, _C_GR).astype(BF16)
    gr = proj(_C_GR, _C_SQ)
    gr_ref[...] = (gr * _sigmoid(gr)).astype(BF16)
    sq = proj(_C_SQ, _C_GA) * (SWA_HEAD_DIM ** -0.5)
    sq_ref[...] = _rope_slabs(sq, cos, sin, first_half).astype(BF16)
    ga_ref[...] = _sigmoid(proj(_C_GA, _C_GB)).astype(BF16)
    gb_ref[...] = _sigmoid(proj(_C_GB, _C_SK)).astype(BF16)
    sk_ref[...] = _rope_slabs(proj(_C_SK, _C_SV), cos, sin, first_half).astype(BF16)
    sv_ref[...] = proj(_C_SV, _C_LR).astype(BF16)
    z = jnp.dot(proj(_C_LR, _C_END), wg_ref[...], precision=lax.Precision.HIGHEST,
                preferred_element_type=F32) + bg_ref[...]
    log_sig = jnp.minimum(z, 0.0) - jnp.log(1.0 + jnp.exp(-jnp.abs(z)))
    lg_ref[...] = log_sig * (1.0 / GLA_TAU)


def _in_proj(x2, g_mix, w_all, b_all, cos_t, sin_t, wg, bg, seq):
    t = x2.shape[0]
    tm = PROJ_TM
    pos_blocks = seq // tm
    const = lambda i: (0, 0)
    row = lambda i: (i, 0)
    widths = [(_C_GK - _C_GQ, BF16), (_C_GV - _C_GK, BF16), (_C_GR - _C_GV, BF16), (_C_SQ - _C_GR, BF16),
              (_C_GA - _C_SQ, BF16), (_D, BF16), (_D, BF16), (_C_SV - _C_SK, BF16), (_C_LR - _C_SV, BF16),
              (GLA_HEADS * GLA_DK, F32)]
    return pl.pallas_call(
        _in_proj_kernel,
        grid=(t // tm,),
        in_specs=[
            pl.BlockSpec((tm, _D), row),
            pl.BlockSpec((1, _D), const),
            pl.BlockSpec((_D, _C_END), const, pipeline_mode=pl.Buffered(1)),
            pl.BlockSpec((1, _C_END), const),
            pl.BlockSpec((tm, LANES), lambda i: (i % pos_blocks, 0)),
            pl.BlockSpec((tm, LANES), lambda i: (i % pos_blocks, 0)),
            pl.BlockSpec((LANES, GLA_HEADS * GLA_DK), const),
            pl.BlockSpec((1, GLA_HEADS * GLA_DK), const),
        ],
        out_specs=[pl.BlockSpec((tm, w), row) for w, _ in widths],
        out_shape=[jax.ShapeDtypeStruct((t, w), dt) for w, dt in widths],
        compiler_params=_cparams(("parallel",)),
        name="in_proj",
    )(x2, g_mix, w_all, b_all, cos_t, sin_t, wg, bg)


def _gla_cumsum_operator(c_len):
    t = np.arange(c_len)[:, None]
    r = np.arange(c_len)[None, :]
    return np.tile((r <= t).astype(np.float32), (1, 3))


def _gla_kernel(q_ref, k_ref, v_ref, lg_ref, gr_ref, gh_ref, dm_ref, o_ref, st_ref):
    c_len = GLA_CHUNK
    n_lev = c_len.bit_length() - 1
    seq = q_ref.shape[0]
    st_ref[...] = jnp.zeros_like(st_ref)

    t_i = lax.broadcasted_iota(I32, (c_len, c_len), 0)
    j_i = lax.broadcasted_iota(I32, (c_len, c_len), 1)
    row = lax.broadcasted_iota(I32, (c_len, 1), 0)
    diag = t_i == j_i
    upper, pair = [], []
    for lev in range(n_lev):
        s = c_len >> (lev + 1)
        upper.append((row & s) != 0)
        pair.append(((t_i // (2 * s)) == (j_i // (2 * s))) & ((t_i & s) != 0) & ((j_i & s) == 0))

    sub8 = lax.broadcasted_iota(I32, (c_len // 8, 8, GLA_DK), 1)

    def boundary_rows(b, s):
        if s >= 4:
            b3 = b.reshape(c_len // (2 * s), 2 * s, GLA_DK)
            return jnp.broadcast_to(b3[:, s - 1:s, :], b3.shape).reshape(c_len, GLA_DK)
        b3 = b.reshape(c_len // 8, 8, GLA_DK)
        lo = jnp.broadcast_to(b3[:, 1:2, :], b3.shape)
        hi = jnp.broadcast_to(b3[:, 5:6, :], b3.shape)
        return jnp.where(sub8 < 4, lo, hi).reshape(c_len, GLA_DK)

    def head_chunk(r0, hh):
        kcols = slice(GLA_DK * hh, GLA_DK * (hh + 1))
        vcols = slice(GLA_DV * hh, GLA_DV * (hh + 1))
        q_bf = q_ref[pl.ds(r0, c_len), kcols]
        k_bf = k_ref[pl.ds(r0, c_len), kcols]
        q = q_bf.astype(F32)
        k = k_bf.astype(F32)
        v = v_ref[pl.ds(r0, c_len), vcols]

        lg = lg_ref[pl.ds(r0, c_len), kcols]
        lg_hi = lg.astype(BF16)
        rem = lg - lg_hi.astype(F32)
        lg_mid = rem.astype(BF16)
        lg_lo = (rem - lg_mid.astype(F32)).astype(BF16)
        b = jnp.dot(dm_ref[...], jnp.concatenate([lg_hi, lg_mid, lg_lo], axis=0),
                    preferred_element_type=F32)
        w_cum = jnp.exp(b)

        st = st_ref[hh]
        o = _dot_nt((q * w_cum).astype(BF16), st.astype(BF16))

        a = jnp.where(diag, _dot_nt(q_bf, k_bf), 0.0)
        for lev in range(n_lev):
            s = c_len >> (lev + 1)
            if s == 1:
                w = jnp.where(upper[lev], jnp.exp(lg), 1.0)
            else:
                w = jnp.exp(-jnp.abs(b - boundary_rows(b, s)))
            z = (jnp.where(upper[lev], q, k) * w).astype(BF16)
            a = jnp.where(pair[lev], _dot_nt(z, z), a)
        o = o + jnp.dot(a.astype(BF16), v, preferred_element_type=F32)

        b_last = b[c_len - 1:c_len, :]
        upd = lax.dot_general(v, (k * jnp.exp(b_last - b)).astype(BF16), (((0,), (0,)), ((), ())),
                              preferred_element_type=F32)
        st_ref[hh] = st * w_cum[c_len - 1:c_len, :] + upd

        on = _rms(o, gh_ref[...])
        o_ref[pl.ds(r0, c_len), vcols] = (on * gr_ref[pl.ds(r0, c_len), vcols].astype(F32)).astype(BF16)

    def chunk(c, carry):
        r0 = pl.multiple_of(c * c_len, c_len)
        for hh in range(GLA_HEADS_PER_STEP):
            head_chunk(r0, hh)
        return carry

    lax.fori_loop(0, seq // c_len, chunk, 0)


def _gla(gq, gk, gv, lg, gr, g_head, bsz, seq):
    t = gq.shape[0]
    hs = GLA_HEADS_PER_STEP
    dmat = jnp.asarray(_gla_cumsum_operator(GLA_CHUNK), dtype=BF16)
    return pl.pallas_call(
        _gla_kernel,
        grid=(bsz, GLA_HEADS // hs),
        in_specs=[
            pl.BlockSpec((seq, hs * GLA_DK), lambda b, h: (b, h)),
            pl.BlockSpec((seq, hs * GLA_DK), lambda b, h: (b, h)),
            pl.BlockSpec((seq, hs * GLA_DV), lambda b, h: (b, h)),
            pl.BlockSpec((seq, hs * GLA_DK), lambda b, h: (b, h)),
            pl.BlockSpec((seq, hs * GLA_DV), lambda b, h: (b, h)),
            pl.BlockSpec((1, GLA_DV), lambda b, h: (0, 0)),
            pl.BlockSpec(dmat.shape, lambda b, h: (0, 0)),
        ],
        out_specs=pl.BlockSpec((seq, hs * GLA_DV), lambda b, h: (b, h)),
        out_shape=jax.ShapeDtypeStruct((t, GLA_HEADS * GLA_DV), BF16),
        scratch_shapes=[pltpu.VMEM((hs, GLA_DV, GLA_DK), F32)],
        compiler_params=_cparams(("parallel", "parallel")),
        name="gla",
    )(gq, gk, gv, lg, gr, g_head, dmat)


def _swa_kernel(sink_ref, q_ref, k_ref, v_ref, o_ref):
    w = SWA_WINDOW
    seq = q_ref.shape[0]
    hk = pl.program_id(1)
    lane_q = lax.broadcasted_iota(I32, (w, LANES), 1)
    low_q = lane_q < SWA_HEAD_DIM
    lane_b = lax.broadcasted_iota(I32, (2 * w, LANES), 1)
    low_b = lane_b < SWA_HEAD_DIM
    qi = lax.broadcasted_iota(I32, (w, 2 * w), 0)
    kj = lax.broadcasted_iota(I32, (w, 2 * w), 1)
    in_window = (kj > qi) & (kj <= qi + w)
    zero_q = jnp.zeros((w, LANES), BF16)
    zero_b = jnp.zeros((2 * w, LANES), BF16)

    def block(n, carry):
        r0 = pl.multiple_of(n * w, w)
        p0 = pl.multiple_of(jnp.maximum(n - 1, 0) * w, w)
        kb = jnp.concatenate([k_ref[pl.ds(p0, w), :], k_ref[pl.ds(r0, w), :]], axis=0)
        vb = jnp.concatenate([v_ref[pl.ds(p0, w), :], v_ref[pl.ds(r0, w), :]], axis=0)
        valid = in_window & ((kj >= w) | (n > 0))
        v_lo = jnp.where(low_b, vb, zero_b)
        v_hi = jnp.where(low_b, zero_b, vb)
        for m in range(SWA_GROUP // 2):
            qp = q_ref[pl.ds(r0, w), LANES * m:LANES * (m + 1)]
            acc = jnp.zeros((w, LANES), F32)
            for par in range(2):
                qm = jnp.where(low_q, qp, zero_q) if par == 0 else jnp.where(low_q, zero_q, qp)
                s = jnp.where(valid, _dot_nt(qm, kb), NEG)
                sink = sink_ref[hk * SWA_GROUP + 2 * m + par]
                mx = jnp.maximum(jnp.max(s, axis=-1, keepdims=True), sink)
                p = jnp.exp(s - mx)
                den = jnp.sum(p, axis=-1, keepdims=True) + jnp.exp(sink - mx)
                pv = jnp.dot(p.astype(BF16), v_lo if par == 0 else v_hi, preferred_element_type=F32)
                acc = acc + pv / den
            o_ref[pl.ds(r0, w), LANES * m:LANES * (m + 1)] = acc.astype(BF16)
        return carry

    lax.fori_loop(0, seq // w, block, 0)


def _swa(sinks, sq, sk, sv, bsz, seq):
    t = sq.shape[0]
    gw = SWA_GROUP * SWA_HEAD_DIM
    return pl.pallas_call(
        _swa_kernel,
        grid_spec=pltpu.PrefetchScalarGridSpec(
            num_scalar_prefetch=1,
            grid=(bsz, SWA_KV_HEADS),
            in_specs=[
                pl.BlockSpec((seq, gw), lambda b, h, s: (b, h)),
                pl.BlockSpec((seq, LANES), lambda b, h, s: (b, h)),
                pl.BlockSpec((seq, LANES), lambda b, h, s: (b, h)),
            ],
            out_specs=pl.BlockSpec((seq, gw), lambda b, h, s: (b, h)),
        ),
        out_shape=jax.ShapeDtypeStruct((t, SWA_Q_HEADS * SWA_HEAD_DIM), BF16),
        compiler_params=_cparams(("parallel", "parallel")),
        name="swa",
    )(sinks, sq, sk, sv)


def _merge_kernel(x_ref, oa_ref, ob_ref, ga_ref, gb_ref, wa_ref, wb_ref, wo_ref, gf_ref, wr_ref, br_ref,
                  x1_ref, hn_ref, lt_ref):
    ya = jnp.dot(oa_ref[...], wa_ref[...], preferred_element_type=F32)
    yb = jnp.dot(ob_ref[...], wb_ref[...], preferred_element_type=F32)
    mixed = ga_ref[...].astype(F32) * ya + gb_ref[...].astype(F32) * yb
    x1 = x_ref[...] + jnp.dot(mixed.astype(BF16), wo_ref[...], preferred_element_type=F32)
    x1_ref[...] = x1
    hn = _rms(x1, gf_ref[...])
    hn_ref[...] = hn
    lt_ref[...] = lax.dot_general(wr_ref[...], hn, (((1,), (1,)), ((), ())),
                                  precision=lax.Precision.HIGHEST,
                                  preferred_element_type=F32) + br_ref[...]


def _merge(x2, oa, ob, ga, gb, wa, wb, wo, g_ffn, wr_t, br_col):
    t = x2.shape[0]
    tm = MERGE_TM
    row = lambda i: (i, 0)
    const = lambda i: (0, 0)
    return pl.pallas_call(
        _merge_kernel,
        grid=(t // tm,),
        in_specs=[pl.BlockSpec((tm, _D), row)] * 5 + [pl.BlockSpec((_D, _D), const)] * 3 + [
            pl.BlockSpec((1, _D), const),
            pl.BlockSpec((N_EXPERTS, _D), const),
            pl.BlockSpec((N_EXPERTS, 1), const),
        ],
        out_specs=[pl.BlockSpec((tm, _D), row), pl.BlockSpec((tm, _D), row),
                   pl.BlockSpec((N_EXPERTS, tm), lambda i: (0, i))],
        out_shape=[jax.ShapeDtypeStruct((t, _D), F32), jax.ShapeDtypeStruct((t, _D), F32),
                   jax.ShapeDtypeStruct((N_EXPERTS, t), F32)],
        compiler_params=_cparams(("parallel",)),
        name="merge",
    )(x2, oa, ob, ga, gb, wa, wb, wo, g_ffn, wr_t, br_col)


def _route_kernel(lt_ref, e_ref, g_ref, r_ref, cnt_ref, carry_ref):
    tt = lt_ref.shape[1]

    @pl.when(pl.program_id(0) == 0)
    def _():
        carry_ref[...] = jnp.zeros_like(carry_ref)

    eid = lax.broadcasted_iota(I32, (N_EXPERTS, tt), 0)
    work = lt_ref[...]
    vals, idxs = [], []
    chosen = jnp.zeros((N_EXPERTS, tt), F32)
    for _ in range(TOP_K):
        m = jnp.max(work, axis=0, keepdims=True)
        idx = jnp.min(jnp.where(work == m, eid, N_EXPERTS), axis=0, keepdims=True)
        hit = eid == idx
        work = jnp.where(hit, -jnp.inf, work)
        chosen = jnp.where(hit, 1.0, chosen)
        vals.append(m)
        idxs.append(idx)
    ex = [jnp.exp(v - vals[0]) for v in vals]
    den = ex[0] + ex[1] + ex[2] + ex[3]

    t_r = lax.broadcasted_iota(I32, (tt, tt), 0)
    t_c = lax.broadcasted_iota(I32, (tt, tt), 1)
    before = (t_r < t_c).astype(BF16)
    pref = jnp.dot(chosen.astype(BF16), before, preferred_element_type=F32) + carry_ref[:, 0:1]
    for k in range(TOP_K):
        e_ref[k:k + 1, :] = idxs[k]
        g_ref[k:k + 1, :] = ex[k] / den
        r_ref[k:k + 1, :] = jnp.sum(jnp.where(eid == idxs[k], pref, 0.0), axis=0, keepdims=True).astype(I32)
    total = pref[:, tt - 1:tt] + chosen[:, tt - 1:tt]
    carry_ref[...] = jnp.broadcast_to(total, carry_ref.shape)
    cnt_ref[...] = jnp.broadcast_to(total, cnt_ref.shape)


def _route(logits_t):
    t = logits_t.shape[1]
    tt = ROUTE_TT
    blk = lambda i: (0, i)
    return pl.pallas_call(
        _route_kernel,
        grid=(t // tt,),
        in_specs=[pl.BlockSpec((N_EXPERTS, tt), blk)],
        out_specs=[pl.BlockSpec((TOP_K, tt), blk), pl.BlockSpec((TOP_K, tt), blk), pl.BlockSpec((TOP_K, tt), blk),
                   pl.BlockSpec((N_EXPERTS, LANES), lambda i: (0, 0))],
        out_shape=[jax.ShapeDtypeStruct((TOP_K, t), I32), jax.ShapeDtypeStruct((TOP_K, t), F32),
                   jax.ShapeDtypeStruct((TOP_K, t), I32), jax.ShapeDtypeStruct((N_EXPERTS, LANES), F32)],
        scratch_shapes=[pltpu.VMEM((N_EXPERTS, LANES), F32)],
        compiler_params=_cparams(("arbitrary",)),
        name="route",
    )(logits_t)


def _dispatch_kernel(dest_ref, hn_ref, init_hbm, xs_hbm, sem):
    del init_hbm
    n_tok = hn_ref.shape[0]

    def issue(i, carry):
        for k in range(TOP_K):
            d = dest_ref[0, 0, i * TOP_K + k]
            pltpu.make_async_copy(hn_ref.at[pl.ds(i, 1)], xs_hbm.at[pl.ds(d, 1)], sem).start()
        return carry

    lax.fori_loop(0, n_tok, issue, 0, unroll=4)
    for k in range(TOP_K):
        pltpu.make_async_copy(hn_ref, xs_hbm.at[pl.ds(0, n_tok)], sem).wait()


def _dispatch(dest_flat, hn, rows):
    t = hn.shape[0]
    tm = DISPATCH_TM
    steps = t // tm
    init = jnp.zeros((rows, _D), F32)
    return pl.pallas_call(
        _dispatch_kernel,
        grid=(steps,),
        in_specs=[pl.BlockSpec((1, 1, tm * TOP_K), lambda i: (i, 0, 0), memory_space=pltpu.SMEM),
                  pl.BlockSpec((tm, _D), lambda i: (i, 0)), pl.BlockSpec(memory_space=pl.ANY)],
        out_specs=pl.BlockSpec(memory_space=pl.ANY),
        out_shape=jax.ShapeDtypeStruct((rows, _D), F32),
        scratch_shapes=[pltpu.SemaphoreType.DMA],
        input_output_aliases={2: 0},
        compiler_params=_cparams(("arbitrary",)),
        name="dispatch",
    )(dest_flat.reshape(steps, 1, tm * TOP_K), hn, init)


def _w1_prep_kernel(w_ref, p_ref, o_ref):
    gw = p_ref.shape[0]
    for gi in range(w_ref.shape[2] // gw):
        wb = w_ref[0, :, gw * gi:gw * (gi + 1)].astype(BF16)
        o_ref[0, :, gw * gi:gw * (gi + 1)] = jnp.dot(wb, p_ref[...], preferred_element_type=F32).astype(BF16)


def _w1_prep(w1, perm):
    n_e, d, n = w1.shape
    cw = W1_PREP_COLS
    return pl.pallas_call(
        _w1_prep_kernel,
        grid=(n_e, n // cw),
        in_specs=[pl.BlockSpec((1, d, cw), lambda e, j: (e, 0, j)),
                  pl.BlockSpec(perm.shape, lambda e, j: (0, 0))],
        out_specs=pl.BlockSpec((1, d, cw), lambda e, j: (e, 0, j)),
        out_shape=jax.ShapeDtypeStruct((n_e, d, n), BF16),
        compiler_params=_cparams(("parallel", "parallel")),
        name="w1_prep",
    )(w1, perm)


def _experts_kernel(be_ref, nv_ref, xs_ref, w1_ref, b1_ref, w2_ref, b2_ref, ys_ref):
    del be_ref

    @pl.when(pl.program_id(0) < nv_ref[0])
    def _():
        x = xs_ref[...].astype(BF16)
        u = jnp.dot(x, w1_ref[0], preferred_element_type=F32) + b1_ref[0]
        acts = []
        for gi in range(u.shape[1] // (2 * LANES)):
            g = jnp.minimum(u[:, 2 * LANES * gi:2 * LANES * gi + LANES], SWIGLU_LIMIT)
            lin = jnp.clip(u[:, 2 * LANES * gi + LANES:2 * LANES * (gi + 1)], -SWIGLU_LIMIT, SWIGLU_LIMIT)
            acts.append((g * _sigmoid(SWIGLU_ALPHA * g) * (lin + 1.0)).astype(BF16))
        act = jnp.concatenate(acts, axis=1)
        ys_ref[...] = jnp.dot(act, w2_ref[0], preferred_element_type=F32) + b2_ref[0]

    @pl.when(pl.program_id(0) >= nv_ref[0])
    def _():
        ys_ref[...] = jnp.zeros_like(ys_ref)


def _experts(block_e, n_valid, xs, w1, b1, w2, b2):
    rows = xs.shape[0]
    n_blocks = rows // MOE_BLOCK
    dff = w2.shape[1]
    wmap = lambda i, be, nv: (be[i], 0, 0)
    rmap = lambda i, be, nv: (i, 0)
    return pl.pallas_call(
        _experts_kernel,
        grid_spec=pltpu.PrefetchScalarGridSpec(
            num_scalar_prefetch=2,
            grid=(n_blocks,),
            in_specs=[
                pl.BlockSpec((MOE_BLOCK, _D), rmap),
                pl.BlockSpec((1, _D, 2 * dff), wmap),
                pl.BlockSpec((1, 1, 2 * dff), wmap),
                pl.BlockSpec((1, dff, _D), wmap),
                pl.BlockSpec((1, 1, _D), wmap),
            ],
            out_specs=pl.BlockSpec((MOE_BLOCK, _D), rmap),
        ),
        out_shape=jax.ShapeDtypeStruct((rows, _D), F32),
        compiler_params=_cparams(("arbitrary",)),
        name="experts",
    )(block_e, n_valid, xs, w1, b1, w2, b2)


def _final_kernel(dest_ref, x1_ref, gt_ref, gf_ref, ys_hbm, o_ref, buf_ref, sem):
    n_tok = x1_ref.shape[0]

    def issue(i, carry):
        for k in range(TOP_K):
            d = dest_ref[0, 0, i * TOP_K + k]
            pltpu.make_async_copy(ys_hbm.at[pl.ds(d, 1)], buf_ref.at[k, pl.ds(i, 1)], sem).start()
        return carry

    lax.fori_loop(0, n_tok, issue, 0, unroll=4)
    for k in range(TOP_K):
        pltpu.make_async_copy(ys_hbm.at[pl.ds(0, n_tok)], buf_ref.at[k], sem).wait()

    gates = gt_ref[...]
    y = x1_ref[...]
    for k in range(TOP_K):
        y = y + gates[:, k:k + 1] * buf_ref[k]
    o_ref[...] = _rms(y, gf_ref[...])


def _final(dest_flat, x1, gates_tk, g_final, ys):
    t = x1.shape[0]
    tm = FINAL_TM
    steps = t // tm
    row = lambda i: (i, 0)
    return pl.pallas_call(
        _final_kernel,
        grid=(steps,),
        in_specs=[pl.BlockSpec((1, 1, tm * TOP_K), lambda i: (i, 0, 0), memory_space=pltpu.SMEM),
                  pl.BlockSpec((tm, _D), row), pl.BlockSpec((tm, TOP_K), row),
                  pl.BlockSpec((1, _D), lambda i: (0, 0)), pl.BlockSpec(memory_space=pl.ANY)],
        out_specs=pl.BlockSpec((tm, _D), row),
        out_shape=jax.ShapeDtypeStruct((t, _D), F32),
        scratch_shapes=[pltpu.VMEM((TOP_K, tm, _D), F32), pltpu.SemaphoreType.DMA],
        compiler_params=_cparams(("arbitrary",)),
        name="final",
    )(dest_flat.reshape(steps, 1, tm * TOP_K), x1, gates_tk, g_final, ys)


def _prep_in_proj(w_in, b_in, w_gate, b_gate):
    sizes = (GLA_HEADS * GLA_DK, GLA_HEADS * GLA_DK, GLA_HEADS * GLA_DV, GLA_HEADS * GLA_DV, GLA_GATE_RANK,
             SWA_Q_HEADS * SWA_HEAD_DIM, SWA_KV_HEADS * SWA_HEAD_DIM, SWA_KV_HEADS * SWA_HEAD_DIM, _D, _D)
    offs = [0]
    for s in sizes:
        offs.append(offs[-1] + s)

    def rearrange(m):
        p = [m[..., offs[i]:offs[i + 1]] for i in range(len(sizes))]
        gq, gk, gv, gr, lr, sq, sk, sv, ga, gb = p

        def dup_heads(a):
            hs = [a[..., SWA_HEAD_DIM * h:SWA_HEAD_DIM * (h + 1)] for h in range(SWA_KV_HEADS)]
            return jnp.concatenate([hh for h in hs for hh in (h, h)], axis=-1)

        lr_pad = jnp.pad(lr, [(0, 0)] * (lr.ndim - 1) + [(0, LANES - GLA_GATE_RANK)])
        return jnp.concatenate([gq, gk, gv, gr, sq, ga, gb, dup_heads(sk), dup_heads(sv), lr_pad], axis=-1)

    w_all = rearrange(w_in).astype(BF16)
    b_all = rearrange(b_in[None, :])
    wg = jnp.pad(w_gate, ((0, LANES - GLA_GATE_RANK), (0, 0)))
    return w_all, b_all, wg, b_gate[None, :]


def _rope_tables(seq):
    half = SWA_HEAD_DIM // 2
    inv_freq = ROPE_THETA ** (-jnp.arange(half, dtype=F32) / half)
    ang = jnp.arange(seq, dtype=F32)[:, None] * inv_freq[None, :]
    cos, sin = jnp.cos(ang), jnp.sin(ang)
    cos_t = jnp.concatenate([cos, cos] * (LANES // SWA_HEAD_DIM), axis=1)
    sin_t = jnp.concatenate([-sin, sin] * (LANES // SWA_HEAD_DIM), axis=1)
    return cos_t, sin_t


def _pair_split_perm():
    src = jnp.arange(2 * LANES, dtype=I32)
    dst = jnp.where(src % 2 == 0, src // 2, LANES + src // 2)
    return (dst[:, None] == jnp.arange(2 * LANES, dtype=I32)[None, :]).astype(BF16)


def kernel(x, g_mix, w_in, b_in, w_gla_gate, b_gla_gate, g_gla_head, w_gla_out, sinks, w_swa_out, w_out,
           g_ffn, w_router, b_router, w_e1, b_e1, w_e2, b_e2, g_final):
    bsz, seq, d = x.shape
    assert d == _D and w_in.shape[0] == 1, "single-layer, d_model=1024 only"
    assert seq % max(PROJ_TM, SWA_WINDOW, GLA_CHUNK) == 0
    t = bsz * seq
    assert t % max(MERGE_TM, ROUTE_TT, FINAL_TM, DISPATCH_TM) == 0
    x2 = x.reshape(t, d)

    w_all, b_all, wg, bg = _prep_in_proj(w_in[0], b_in[0], w_gla_gate[0], b_gla_gate[0])
    cos_t, sin_t = _rope_tables(seq)
    gq, gk, gv, gr, sq, ga, gb, sk, sv, lg = _in_proj(x2, g_mix, w_all, b_all, cos_t, sin_t, wg, bg, seq)
    oa = _gla(gq, gk, gv, lg, gr, g_gla_head, bsz, seq)
    ob = _swa(sinks[0], sq, sk, sv, bsz, seq)
    x1, hn, logits_t = _merge(x2, oa, ob, ga, gb, w_gla_out[0].astype(BF16), w_swa_out[0].astype(BF16),
                              w_out[0].astype(BF16), g_ffn, w_router[0].T, b_router[0][:, None])

    e_kt, g_kt, r_kt, cnt = _route(logits_t)
    counts = cnt[:, 0].astype(I32)
    blocks_e = (counts + MOE_BLOCK - 1) // MOE_BLOCK
    bend = jnp.cumsum(blocks_e)
    pstart = (bend - blocks_e) * MOE_BLOCK
    dest_kt = r_kt
    for e in range(N_EXPERTS):
        dest_kt = dest_kt + jnp.where(e_kt == e, pstart[e], 0)
    dest_flat = dest_kt.T.reshape(t * TOP_K)
    n_blocks = (t * TOP_K) // MOE_BLOCK + N_EXPERTS
    block_e = jnp.minimum(jnp.sum(bend[None, :] <= jnp.arange(n_blocks, dtype=I32)[:, None], axis=1),
                          N_EXPERTS - 1).astype(I32)
    n_valid = bend[-1:].astype(I32)

    w1 = _w1_prep(w_e1[0], _pair_split_perm())
    b1 = b_e1[0].reshape(N_EXPERTS, -1, LANES, 2).transpose(0, 1, 3, 2).reshape(N_EXPERTS, 1, -1)
    xs = _dispatch(dest_flat, hn, n_blocks * MOE_BLOCK)
    ys = _experts(block_e, n_valid, xs, w1, b1, w_e2[0].astype(BF16), b_e2[0][:, None, :])
    out = _final(dest_flat, x1, g_kt.T, g_final[None, :], ys)
    return out.reshape(bsz, seq, d)
```

```python
import jax
import jax.numpy as jnp
import numpy as np
from jax import lax
from jax.experimental import pallas as pl
from jax.experimental.pallas import tpu as pltpu

F32 = jnp.float32
BF16 = jnp.bfloat16
I32 = jnp.int32

NORM_EPS = 1e-5
GLA_HEADS = 4
GLA_DK = 128
GLA_DV = 256
GLA_GATE_RANK = 16
GLA_TAU = 16.0
SWA_Q_HEADS = 16
SWA_KV_HEADS = 2
SWA_GROUP = SWA_Q_HEADS // SWA_KV_HEADS
SWA_HEAD_DIM = 64
SWA_WINDOW = 128
ROPE_THETA = 10000.0
N_EXPERTS = 32
TOP_K = 4
SWIGLU_LIMIT = 7.0
SWIGLU_ALPHA = 1.702

LANES = 128
NEG = -1e30
VMEM_LIMIT = 56 * 1024 * 1024

PROJ_TM = 256
GLA_CHUNK = 128
GLA_HEADS_PER_STEP = 4
MERGE_TM = 512
ROUTE_TT = 512
MOE_BLOCK = 256
DISPATCH_GROUPS = 16
FINAL_TM = 512

_D = 1024
_C_GQ = 0
_C_GK = _C_GQ + GLA_HEADS * GLA_DK
_C_GV = _C_GK + GLA_HEADS * GLA_DK
_C_GR = _C_GV + GLA_HEADS * GLA_DV
_C_SQ = _C_GR + GLA_HEADS * GLA_DV
_C_GA = _C_SQ + SWA_Q_HEADS * SWA_HEAD_DIM
_C_GB = _C_GA + _D
_C_SK = _C_GB + _D
_C_SV = _C_SK + SWA_KV_HEADS * LANES
_C_LR = _C_SV + SWA_KV_HEADS * LANES
_C_END = _C_LR + LANES


def _cparams(sem):
    return pltpu.CompilerParams(dimension_semantics=sem, vmem_limit_bytes=VMEM_LIMIT)


def _rms(x, g):
    return x * lax.rsqrt(jnp.mean(x * x, axis=-1, keepdims=True) + NORM_EPS) * g


def _sigmoid(x):
    return 1.0 / (1.0 + jnp.exp(-x))


def _dot_nt(a, b):
    return lax.dot_general(a, b, (((1,), (1,)), ((), ())), preferred_element_type=F32)


def _rope_slabs(acc, cos, sin, first_half):
    outs = []
    for i in range(acc.shape[1] // LANES):
        xs = acc[:, LANES * i:LANES * (i + 1)]
        partner = jnp.where(first_half, pltpu.roll(xs, LANES - 32, 1), pltpu.roll(xs, 32, 1))
        outs.append(xs * cos + partner * sin)
    return jnp.concatenate(outs, axis=1)


def _in_proj_kernel(x_ref, g_ref, w_ref, b_ref, cos_ref, sin_ref, wg_ref, bg_ref,
                    gq_ref, gk_ref, gv_ref, gr_ref, sq_ref, ga_ref, gb_ref, sk_ref, sv_ref, lg_ref):
    h = _rms(x_ref[...], g_ref[...]).astype(BF16)

    def proj(lo, hi):
        return jnp.dot(h, w_ref[:, lo:hi], preferred_element_type=F32) + b_ref[:, lo:hi]

    cos = cos_ref[...]
    sin = sin_ref[...]
    lane = lax.broadcasted_iota(I32, cos.shape, 1)
    first_half = (lane % SWA_HEAD_DIM) < (SWA_HEAD_DIM // 2)

    gq_ref[...] = (proj(_C_GQ, _C_GK) * (GLA_DK ** -0.5)).astype(BF16)
    gk_ref[...] = proj(_C_GK, _C_GV).astype(BF16)
    gv_ref[...] = proj(_C_GV, _C_GR).astype(BF16)
    gr = proj(_C_GR, _C_SQ)
    gr_ref[...] = (gr * _sigmoid(gr)).astype(BF16)
    sq = proj(_C_SQ, _C_GA) * (SWA_HEAD_DIM ** -0.5)
    sq_ref[...] = _rope_slabs(sq, cos, sin, first_half).astype(BF16)
    ga_ref[...] = _sigmoid(proj(_C_GA, _C_GB)).astype(BF16)
    gb_ref[...] = _sigmoid(proj(_C_GB, _C_SK)).astype(BF16)
    sk_ref[...] = _rope_slabs(proj(_C_SK, _C_SV), cos, sin, first_half).astype(BF16)
    sv_ref[...] = proj(_C_SV, _C_LR).astype(BF16)
    z = jnp.dot(proj(_C_LR, _C_END), wg_ref[...], precision=lax.Precision.HIGHEST,
                preferred_element_type=F32) + bg_ref[...]
    log_sig = jnp.minimum(z, 0.0) - jnp.log(1.0 + jnp.exp(-jnp.abs(z)))
    lg_ref[...] = log_sig * (1.0 / GLA_TAU)


def _in_proj(x2, g_mix, w_all, b_all, cos_t, sin_t, wg, bg, seq):
    t = x2.shape[0]
    tm = PROJ_TM
    pos_blocks = seq // tm
    const = lambda i: (0, 0)
    row = lambda i: (i, 0)
    widths = [(_C_GK - _C_GQ, BF16), (_C_GV - _C_GK, BF16), (_C_GR - _C_GV, BF16), (_C_SQ - _C_GR, BF16),
              (_C_GA - _C_SQ, BF16), (_D, BF16), (_D, BF16), (_C_SV - _C_SK, BF16), (_C_LR - _C_SV, BF16),
              (GLA_HEADS * GLA_DK, F32)]
    return pl.pallas_call(
        _in_proj_kernel,
        grid=(t // tm,),
        in_specs=[
            pl.BlockSpec((tm, _D), row),
            pl.BlockSpec((1, _D), const),
            pl.BlockSpec((_D, _C_END), const, pipeline_mode=pl.Buffered(1)),
            pl.BlockSpec((1, _C_END), const),
            pl.BlockSpec((tm, LANES), lambda i: (i % pos_blocks, 0)),
            pl.BlockSpec((tm, LANES), lambda i: (i % pos_blocks, 0)),
            pl.BlockSpec((LANES, GLA_HEADS * GLA_DK), const),
            pl.BlockSpec((1, GLA_HEADS * GLA_DK), const),
        ],
        out_specs=[pl.BlockSpec((tm, w), row) for w, _ in widths],
        out_shape=[jax.ShapeDtypeStruct((t, w), dt) for w, dt in widths],
        compiler_params=_cparams(("parallel",)),
        name="in_proj",
    )(x2, g_mix, w_all, b_all, cos_t, sin_t, wg, bg)


def _gla_cumsum_operator(c_len):
    t = np.arange(c_len)[:, None]
    r = np.arange(c_len)[None, :]
    return np.tile((r <= t).astype(np.float32), (1, 3))


def _gla_kernel(q_ref, k_ref, v_ref, lg_ref, gr_ref, gh_ref, dm_ref, o_ref, st_ref):
    c_len = GLA_CHUNK
    n_lev = c_len.bit_length() - 1
    seq = q_ref.shape[0]
    st_ref[...] = jnp.zeros_like(st_ref)

    t_i = lax.broadcasted_iota(I32, (c_len, c_len), 0)
    j_i = lax.broadcasted_iota(I32, (c_len, c_len), 1)
    row = lax.broadcasted_iota(I32, (c_len, 1), 0)
    diag = t_i == j_i
    upper, pair = [], []
    for lev in range(n_lev):
        s = c_len >> (lev + 1)
        upper.append((row & s) != 0)
        pair.append(((t_i // (2 * s)) == (j_i // (2 * s))) & ((t_i & s) != 0) & ((j_i & s) == 0))

    sub8 = lax.broadcasted_iota(I32, (c_len // 8, 8, GLA_DK), 1)

    def boundary_rows(b, s):
        if s >= 4:
            b3 = b.reshape(c_len // (2 * s), 2 * s, GLA_DK)
            return jnp.broadcast_to(b3[:, s - 1:s, :], b3.shape).reshape(c_len, GLA_DK)
        b3 = b.reshape(c_len // 8, 8, GLA_DK)
        lo = jnp.broadcast_to(b3[:, 1:2, :], b3.shape)
        hi = jnp.broadcast_to(b3[:, 5:6, :], b3.shape)
        return jnp.where(sub8 < 4, lo, hi).reshape(c_len, GLA_DK)

    def head_chunk(r0, hh):
        kcols = slice(GLA_DK * hh, GLA_DK * (hh + 1))
        vcols = slice(GLA_DV * hh, GLA_DV * (hh + 1))
        q_bf = q_ref[pl.ds(r0, c_len), kcols]
        k_bf = k_ref[pl.ds(r0, c_len), kcols]
        q = q_bf.astype(F32)
        k = k_bf.astype(F32)
        v = v_ref[pl.ds(r0, c_len), vcols]

        lg = lg_ref[pl.ds(r0, c_len), kcols]
        lg_hi = lg.astype(BF16)
        rem = lg - lg_hi.astype(F32)
        lg_mid = rem.astype(BF16)
        lg_lo = (rem - lg_mid.astype(F32)).astype(BF16)
        b = jnp.dot(dm_ref[...], jnp.concatenate([lg_hi, lg_mid, lg_lo], axis=0),
                    preferred_element_type=F32)
        w_cum = jnp.exp(b)

        st = st_ref[hh]
        o = _dot_nt((q * w_cum).astype(BF16), st.astype(BF16))

        a = jnp.where(diag, _dot_nt(q_bf, k_bf), 0.0)
        for lev in range(n_lev):
            s = c_len >> (lev + 1)
            if s == 1:
                w = jnp.where(upper[lev], jnp.exp(lg), 1.0)
            else:
                w = jnp.exp(-jnp.abs(b - boundary_rows(b, s)))
            z = (jnp.where(upper[lev], q, k) * w).astype(BF16)
            a = jnp.where(pair[lev], _dot_nt(z, z), a)
        o = o + jnp.dot(a.astype(BF16), v, preferred_element_type=F32)

        b_last = b[c_len - 1:c_len, :]
        upd = lax.dot_general(v, (k * jnp.exp(b_last - b)).astype(BF16), (((0,), (0,)), ((), ())),
                              preferred_element_type=F32)
        st_ref[hh] = st * w_cum[c_len - 1:c_len, :] + upd

        on = _rms(o, gh_ref[...])
        o_ref[pl.ds(r0, c_len), vcols] = (on * gr_ref[pl.ds(r0, c_len), vcols].astype(F32)).astype(BF16)

    def chunk(c, carry):
        r0 = pl.multiple_of(c * c_len, c_len)
        for hh in range(GLA_HEADS_PER_STEP):
            head_chunk(r0, hh)
        return carry

    lax.fori_loop(0, seq // c_len, chunk, 0)


def _gla(gq, gk, gv, lg, gr, g_head, bsz, seq):
    t = gq.shape[0]
    hs = GLA_HEADS_PER_STEP
    dmat = jnp.asarray(_gla_cumsum_operator(GLA_CHUNK), dtype=BF16)
    return pl.pallas_call(
        _gla_kernel,
        grid=(bsz, GLA_HEADS // hs),
        in_specs=[
            pl.BlockSpec((seq, hs * GLA_DK), lambda b, h: (b, h)),
            pl.BlockSpec((seq, hs * GLA_DK), lambda b, h: (b, h)),
            pl.BlockSpec((seq, hs * GLA_DV), lambda b, h: (b, h)),
            pl.BlockSpec((seq, hs * GLA_DK), lambda b, h: (b, h)),
            pl.BlockSpec((seq, hs * GLA_DV), lambda b, h: (b, h)),
            pl.BlockSpec((1, GLA_DV), lambda b, h: (0, 0)),
            pl.BlockSpec(dmat.shape, lambda b, h: (0, 0)),
        ],
        out_specs=pl.BlockSpec((seq, hs * GLA_DV), lambda b, h: (b, h)),
        out_shape=jax.ShapeDtypeStruct((t, GLA_HEADS * GLA_DV), BF16),
        scratch_shapes=[pltpu.VMEM((hs, GLA_DV, GLA_DK), F32)],
        compiler_params=_cparams(("parallel", "parallel")),
        name="gla",
    )(gq, gk, gv, lg, gr, g_head, dmat)


def _swa_kernel(sink_ref, q_ref, k_ref, v_ref, o_ref):
    w = SWA_WINDOW
    seq = q_ref.shape[0]
    hk = pl.program_id(1)
    lane_q = lax.broadcasted_iota(I32, (w, LANES), 1)
    low_q = lane_q < SWA_HEAD_DIM
    lane_b = lax.broadcasted_iota(I32, (2 * w, LANES), 1)
    low_b = lane_b < SWA_HEAD_DIM
    qi = lax.broadcasted_iota(I32, (w, 2 * w), 0)
    kj = lax.broadcasted_iota(I32, (w, 2 * w), 1)
    in_window = (kj > qi) & (kj <= qi + w)
    zero_q = jnp.zeros((w, LANES), BF16)
    zero_b = jnp.zeros((2 * w, LANES), BF16)

    def block(n, carry):
        r0 = pl.multiple_of(n * w, w)
        p0 = pl.multiple_of(jnp.maximum(n - 1, 0) * w, w)
        kb = jnp.concatenate([k_ref[pl.ds(p0, w), :], k_ref[pl.ds(r0, w), :]], axis=0)
        vb = jnp.concatenate([v_ref[pl.ds(p0, w), :], v_ref[pl.ds(r0, w), :]], axis=0)
        valid = in_window & ((kj >= w) | (n > 0))
        v_lo = jnp.where(low_b, vb, zero_b)
        v_hi = jnp.where(low_b, zero_b, vb)
        for m in range(SWA_GROUP // 2):
            qp = q_ref[pl.ds(r0, w), LANES * m:LANES * (m + 1)]
            acc = jnp.zeros((w, LANES), F32)
            for par in range(2):
                qm = jnp.where(low_q, qp, zero_q) if par == 0 else jnp.where(low_q, zero_q, qp)
                s = jnp.where(valid, _dot_nt(qm, kb), NEG)
                sink = sink_ref[hk * SWA_GROUP + 2 * m + par]
                mx = jnp.maximum(jnp.max(s, axis=-1, keepdims=True), sink)
                p = jnp.exp(s - mx)
                den = jnp.sum(p, axis=-1, keepdims=True) + jnp.exp(sink - mx)
                pv = jnp.dot(p.astype(BF16), v_lo if par == 0 else v_hi, preferred_element_type=F32)
                acc = acc + pv / den
            o_ref[pl.ds(r0, w), LANES * m:LANES * (m + 1)] = acc.astype(BF16)
        return carry

    lax.fori_loop(0, seq // w, block, 0)


def _swa(sinks, sq, sk, sv, bsz, seq):
    t = sq.shape[0]
    gw = SWA_GROUP * SWA_HEAD_DIM
    return pl.pallas_call(
        _swa_kernel,
        grid_spec=pltpu.PrefetchScalarGridSpec(
            num_scalar_prefetch=1,
            grid=(bsz, SWA_KV_HEADS),
            in_specs=[
                pl.BlockSpec((seq, gw), lambda b, h, s: (b, h)),
                pl.BlockSpec((seq, LANES), lambda b, h, s: (b, h)),
                pl.BlockSpec((seq, LANES), lambda b, h, s: (b, h)),
            ],
            out_specs=pl.BlockSpec((seq, gw), lambda b, h, s: (b, h)),
        ),
        out_shape=jax.ShapeDtypeStruct((t, SWA_Q_HEADS * SWA_HEAD_DIM), BF16),
        compiler_params=_cparams(("parallel", "parallel")),
        name="swa",
    )(sinks, sq, sk, sv)


def _merge_kernel(x_ref, oa_ref, ob_ref, ga_ref, gb_ref, wa_ref, wb_ref, wo_ref, gf_ref, wr_ref, br_ref,
                  x1_ref, hn_ref, lt_ref):
    ya = jnp.dot(oa_ref[...], wa_ref[...], preferred_element_type=F32)
    yb = jnp.dot(ob_ref[...], wb_ref[...], preferred_element_type=F32)
    mixed = ga_ref[...].astype(F32) * ya + gb_ref[...].astype(F32) * yb
    x1 = x_ref[...] + jnp.dot(mixed.astype(BF16), wo_ref[...], preferred_element_type=F32)
    x1_ref[...] = x1
    hn = _rms(x1, gf_ref[...])
    hn_ref[...] = hn
    lt_ref[...] = lax.dot_general(wr_ref[...], hn, (((1,), (1,)), ((), ())),
                                  precision=lax.Precision.HIGHEST,
                                  preferred_element_type=F32) + br_ref[...]


def _merge(x2, oa, ob, ga, gb, wa, wb, wo, g_ffn, wr_t, br_col):
    t = x2.shape[0]
    tm = MERGE_TM
    row = lambda i: (i, 0)
    const = lambda i: (0, 0)
    return pl.pallas_call(
        _merge_kernel,
        grid=(t // tm,),
        in_specs=[pl.BlockSpec((tm, _D), row)] * 5 + [pl.BlockSpec((_D, _D), const)] * 3 + [
            pl.BlockSpec((1, _D), const),
            pl.BlockSpec((N_EXPERTS, _D), const),
            pl.BlockSpec((N_EXPERTS, 1), const),
        ],
        out_specs=[pl.BlockSpec((tm, _D), row), pl.BlockSpec((tm, _D), row),
                   pl.BlockSpec((N_EXPERTS, tm), lambda i: (0, i))],
        out_shape=[jax.ShapeDtypeStruct((t, _D), F32), jax.ShapeDtypeStruct((t, _D), F32),
                   jax.ShapeDtypeStruct((N_EXPERTS, t), F32)],
        compiler_params=_cparams(("parallel",)),
        name="merge",
    )(x2, oa, ob, ga, gb, wa, wb, wo, g_ffn, wr_t, br_col)


def _route_kernel(lt_ref, e_ref, g_ref, r_ref, cnt_ref, carry_ref):
    tt = lt_ref.shape[1]

    @pl.when(pl.program_id(0) == 0)
    def _():
        carry_ref[...] = jnp.zeros_like(carry_ref)

    eid = lax.broadcasted_iota(I32, (N_EXPERTS, tt), 0)
    work = lt_ref[...]
    vals, idxs = [], []
    chosen = jnp.zeros((N_EXPERTS, tt), F32)
    for _ in range(TOP_K):
        m = jnp.max(work, axis=0, keepdims=True)
        idx = jnp.min(jnp.where(work == m, eid, N_EXPERTS), axis=0, keepdims=True)
        hit = eid == idx
        work = jnp.where(hit, -jnp.inf, work)
        chosen = jnp.where(hit, 1.0, chosen)
        vals.append(m)
        idxs.append(idx)
    ex = [jnp.exp(v - vals[0]) for v in vals]
    den = ex[0] + ex[1] + ex[2] + ex[3]

    t_r = lax.broadcasted_iota(I32, (tt, tt), 0)
    t_c = lax.broadcasted_iota(I32, (tt, tt), 1)
    before = (t_r < t_c).astype(BF16)
    pref = jnp.dot(chosen.astype(BF16), before, preferred_element_type=F32) + carry_ref[:, 0:1]
    for k in range(TOP_K):
        e_ref[k:k + 1, :] = idxs[k]
        g_ref[k:k + 1, :] = ex[k] / den
        r_ref[k:k + 1, :] = jnp.sum(jnp.where(eid == idxs[k], pref, 0.0), axis=0, keepdims=True).astype(I32)
    total = pref[:, tt - 1:tt] + chosen[:, tt - 1:tt]
    carry_ref[...] = jnp.broadcast_to(total, carry_ref.shape)
    cnt_ref[...] = jnp.broadcast_to(total, cnt_ref.shape)


def _route(logits_t):
    t = logits_t.shape[1]
    tt = ROUTE_TT
    blk = lambda i: (0, i)
    return pl.pallas_call(
        _route_kernel,
        grid=(t // tt,),
        in_specs=[pl.BlockSpec((N_EXPERTS, tt), blk)],
        out_specs=[pl.BlockSpec((TOP_K, tt), blk), pl.BlockSpec((TOP_K, tt), blk), pl.BlockSpec((TOP_K, tt), blk),
                   pl.BlockSpec((N_EXPERTS, LANES), lambda i: (0, 0))],
        out_shape=[jax.ShapeDtypeStruct((TOP_K, t), I32), jax.ShapeDtypeStruct((TOP_K, t), F32),
                   jax.ShapeDtypeStruct((TOP_K, t), I32), jax.ShapeDtypeStruct((N_EXPERTS, LANES), F32)],
        scratch_shapes=[pltpu.VMEM((N_EXPERTS, LANES), F32)],
        compiler_params=_cparams(("arbitrary",)),
        name="route",
    )(logits_t)


def _dispatch_kernel(pend_ref, nblk_ref, nv_ref, dest_ref, hn_ref, w1_ref, w2_ref, perm_ref,
                     xs_hbm, w1o_ref, w2o_ref, zero_ref, sem, zsem):
    n_tok = hn_ref.shape[0]
    n_rows = w1_ref.shape[1]
    gw = perm_ref.shape[0]
    n_col_groups = w1_ref.shape[2] // gw
    slices = DISPATCH_GROUPS * n_col_groups
    tok_s = n_tok // slices
    rows_g = n_rows // DISPATCH_GROUPS
    w2_cols = w2_ref.shape[2] // n_col_groups
    n_blocks = xs_hbm.shape[0] // MOE_BLOCK

    def zero_block(row0):
        return pltpu.make_async_copy(zero_ref, xs_hbm.at[pl.ds(pl.multiple_of(row0, MOE_BLOCK), MOE_BLOCK)], zsem)

    @pl.when(pl.program_id(0) == 0)
    def _():
        zero_ref[...] = jnp.zeros_like(zero_ref)

        def expert_tail(e, carry, start):
            @pl.when(nblk_ref[e] > 0)
            def _():
                cp = zero_block(pend_ref[e] - MOE_BLOCK)
                cp.start() if start else cp.wait()
            return carry

        def unused_block(b, carry, start):
            cp = zero_block(b * MOE_BLOCK)
            cp.start() if start else cp.wait()
            return carry

        for start in (True, False):
            lax.fori_loop(0, N_EXPERTS, lambda e, c: expert_tail(e, c, start), 0)
            lax.fori_loop(nv_ref[0], n_blocks, lambda b, c: unused_block(b, c, start), 0)

    def group(g, carry):
        r0 = pl.multiple_of(g * rows_g, rows_g)
        for gi in range(n_col_groups):
            t0 = (g * n_col_groups + gi) * tok_s
            for i in range(tok_s):
                for k in range(TOP_K):
                    d = dest_ref[0, 0, (t0 + i) * TOP_K + k]
                    pltpu.make_async_copy(hn_ref.at[pl.ds(t0 + i, 1)], xs_hbm.at[pl.ds(d, 1)], sem).start()
            wb = w1_ref[0, pl.ds(r0, rows_g), gw * gi:gw * (gi + 1)].astype(BF16)
            w1o_ref[0, pl.ds(r0, rows_g), gw * gi:gw * (gi + 1)] = jnp.dot(
                wb, perm_ref[...], preferred_element_type=F32).astype(BF16)
            w2o_ref[0, pl.ds(r0, rows_g), w2_cols * gi:w2_cols * (gi + 1)] = w2_ref[
                0, pl.ds(r0, rows_g), w2_cols * gi:w2_cols * (gi + 1)].astype(BF16)
        return carry

    lax.fori_loop(0, DISPATCH_GROUPS, group, 0)
    for k in range(TOP_K):
        pltpu.make_async_copy(hn_ref, xs_hbm.at[pl.ds(0, n_tok)], sem).wait()


def _dispatch(pend_rows, blocks_e, n_valid, dest_flat, hn, w1, w2, perm, rows):
    t = hn.shape[0]
    n_e, d, n1 = w1.shape
    dff = w2.shape[1]
    tm = t // n_e
    assert dff == d and tm % (DISPATCH_GROUPS * (n1 // perm.shape[0])) == 0
    emap = lambda e, *_: (e, 0, 0)
    return pl.pallas_call(
        _dispatch_kernel,
        grid_spec=pltpu.PrefetchScalarGridSpec(
            num_scalar_prefetch=3,
            grid=(n_e,),
            in_specs=[pl.BlockSpec((1, 1, tm * TOP_K), emap, memory_space=pltpu.SMEM),
                      pl.BlockSpec((tm, _D), lambda e, *_: (e, 0)),
                      pl.BlockSpec((1, d, n1), emap),
                      pl.BlockSpec((1, dff, _D), emap),
                      pl.BlockSpec(perm.shape, lambda e, *_: (0, 0))],
            out_specs=[pl.BlockSpec(memory_space=pl.ANY),
                       pl.BlockSpec((1, d, n1), emap),
                       pl.BlockSpec((1, dff, _D), emap)],
            scratch_shapes=[pltpu.VMEM((MOE_BLOCK, _D), F32), pltpu.SemaphoreType.DMA, pltpu.SemaphoreType.DMA],
        ),
        out_shape=[jax.ShapeDtypeStruct((rows, _D), F32),
                   jax.ShapeDtypeStruct((n_e, d, n1), BF16),
                   jax.ShapeDtypeStruct((n_e, dff, _D), BF16)],
        compiler_params=_cparams(("arbitrary",)),
        name="dispatch",
    )(pend_rows, blocks_e, n_valid, dest_flat.reshape(n_e, 1, tm * TOP_K), hn, w1, w2, perm)


def _experts_kernel(be_ref, nv_ref, xs_ref, w1_ref, b1_ref, w2_ref, b2_ref, ys_ref):
    del be_ref

    @pl.when(pl.program_id(0) < nv_ref[0])
    def _():
        x = xs_ref[...].astype(BF16)
        u = jnp.dot(x, w1_ref[0], preferred_element_type=F32) + b1_ref[0]
        acts = []
        for gi in range(u.shape[1] // (2 * LANES)):
            g = jnp.minimum(u[:, 2 * LANES * gi:2 * LANES * gi + LANES], SWIGLU_LIMIT)
            lin = jnp.clip(u[:, 2 * LANES * gi + LANES:2 * LANES * (gi + 1)], -SWIGLU_LIMIT, SWIGLU_LIMIT)
            acts.append((g * _sigmoid(SWIGLU_ALPHA * g) * (lin + 1.0)).astype(BF16))
        act = jnp.concatenate(acts, axis=1)
        ys_ref[...] = jnp.dot(act, w2_ref[0], preferred_element_type=F32) + b2_ref[0]

    @pl.when(pl.program_id(0) >= nv_ref[0])
    def _():
        ys_ref[...] = jnp.zeros_like(ys_ref)


def _experts(block_e, n_valid, xs, w1, b1, w2, b2):
    rows = xs.shape[0]
    n_blocks = rows // MOE_BLOCK
    dff = w2.shape[1]
    wmap = lambda i, be, nv: (be[i], 0, 0)
    rmap = lambda i, be, nv: (i, 0)
    return pl.pallas_call(
        _experts_kernel,
        grid_spec=pltpu.PrefetchScalarGridSpec(
            num_scalar_prefetch=2,
            grid=(n_blocks,),
            in_specs=[
                pl.BlockSpec((MOE_BLOCK, _D), rmap),
                pl.BlockSpec((1, _D, 2 * dff), wmap),
                pl.BlockSpec((1, 1, 2 * dff), wmap),
                pl.BlockSpec((1, dff, _D), wmap),
                pl.BlockSpec((1, 1, _D), wmap),
            ],
            out_specs=pl.BlockSpec((MOE_BLOCK, _D), rmap),
        ),
        out_shape=jax.ShapeDtypeStruct((rows, _D), F32),
        compiler_params=_cparams(("arbitrary",)),
        name="experts",
    )(block_e, n_valid, xs, w1, b1, w2, b2)


def _final_kernel(dest_ref, x1_ref, gt_ref, gf_ref, ys_hbm, o_ref, buf_ref, sem):
    n_tok = x1_ref.shape[0]

    def issue(i, carry):
        for k in range(TOP_K):
            d = dest_ref[0, 0, i * TOP_K + k]
            pltpu.make_async_copy(ys_hbm.at[pl.ds(d, 1)], buf_ref.at[k, pl.ds(i, 1)], sem).start()
        return carry

    lax.fori_loop(0, n_tok, issue, 0, unroll=4)
    for k in range(TOP_K):
        pltpu.make_async_copy(ys_hbm.at[pl.ds(0, n_tok)], buf_ref.at[k], sem).wait()

    gates = gt_ref[...]
    y = x1_ref[...]
    for k in range(TOP_K):
        y = y + gates[:, k:k + 1] * buf_ref[k]
    o_ref[...] = _rms(y, gf_ref[...])


def _final(dest_flat, x1, gates_tk, g_final, ys):
    t = x1.shape[0]
    tm = FINAL_TM
    steps = t // tm
    row = lambda i: (i, 0)
    return pl.pallas_call(
        _final_kernel,
        grid=(steps,),
        in_specs=[pl.BlockSpec((1, 1, tm * TOP_K), lambda i: (i, 0, 0), memory_space=pltpu.SMEM),
                  pl.BlockSpec((tm, _D), row), pl.BlockSpec((tm, TOP_K), row),
                  pl.BlockSpec((1, _D), lambda i: (0, 0)), pl.BlockSpec(memory_space=pl.ANY)],
        out_specs=pl.BlockSpec((tm, _D), row),
        out_shape=jax.ShapeDtypeStruct((t, _D), F32),
        scratch_shapes=[pltpu.VMEM((TOP_K, tm, _D), F32), pltpu.SemaphoreType.DMA],
        compiler_params=_cparams(("arbitrary",)),
        name="final",
    )(dest_flat.reshape(steps, 1, tm * TOP_K), x1, gates_tk, g_final, ys)


def _prep_in_proj(w_in, b_in, w_gate, b_gate):
    sizes = (GLA_HEADS * GLA_DK, GLA_HEADS * GLA_DK, GLA_HEADS * GLA_DV, GLA_HEADS * GLA_DV, GLA_GATE_RANK,
             SWA_Q_HEADS * SWA_HEAD_DIM, SWA_KV_HEADS * SWA_HEAD_DIM, SWA_KV_HEADS * SWA_HEAD_DIM, _D, _D)
    offs = [0]
    for s in sizes:
        offs.append(offs[-1] + s)

    def rearrange(m):
        p = [m[..., offs[i]:offs[i + 1]] for i in range(len(sizes))]
        gq, gk, gv, gr, lr, sq, sk, sv, ga, gb = p

        def dup_heads(a):
            hs = [a[..., SWA_HEAD_DIM * h:SWA_HEAD_DIM * (h + 1)] for h in range(SWA_KV_HEADS)]
            return jnp.concatenate([hh for h in hs for hh in (h, h)], axis=-1)

        lr_pad = jnp.pad(lr, [(0, 0)] * (lr.ndim - 1) + [(0, LANES - GLA_GATE_RANK)])
        return jnp.concatenate([gq, gk, gv, gr, sq, ga, gb, dup_heads(sk), dup_heads(sv), lr_pad], axis=-1)

    w_all = rearrange(w_in).astype(BF16)
    b_all = rearrange(b_in[None, :])
    wg = jnp.pad(w_gate, ((0, LANES - GLA_GATE_RANK), (0, 0)))
    return w_all, b_all, wg, b_gate[None, :]


def _rope_tables(seq):
    half = SWA_HEAD_DIM // 2
    inv_freq = ROPE_THETA ** (-jnp.arange(half, dtype=F32) / half)
    ang = jnp.arange(seq, dtype=F32)[:, None] * inv_freq[None, :]
    cos, sin = jnp.cos(ang), jnp.sin(ang)
    cos_t = jnp.concatenate([cos, cos] * (LANES // SWA_HEAD_DIM), axis=1)
    sin_t = jnp.concatenate([-sin, sin] * (LANES // SWA_HEAD_DIM), axis=1)
    return cos_t, sin_t


def _pair_split_perm():
    src = jnp.arange(2 * LANES, dtype=I32)
    dst = jnp.where(src % 2 == 0, src // 2, LANES + src // 2)
    return (dst[:, None] == jnp.arange(2 * LANES, dtype=I32)[None, :]).astype(BF16)


def kernel(x, g_mix, w_in, b_in, w_gla_gate, b_gla_gate, g_gla_head, w_gla_out, sinks, w_swa_out, w_out,
           g_ffn, w_router, b_router, w_e1, b_e1, w_e2, b_e2, g_final):
    bsz, seq, d = x.shape
    assert d == _D and w_in.shape[0] == 1, "single-layer, d_model=1024 only"
    assert seq % max(PROJ_TM, SWA_WINDOW, GLA_CHUNK) == 0
    t = bsz * seq
    assert t % max(MERGE_TM, ROUTE_TT, FINAL_TM) == 0 and t % N_EXPERTS == 0
    x2 = x.reshape(t, d)

    w_all, b_all, wg, bg = _prep_in_proj(w_in[0], b_in[0], w_gla_gate[0], b_gla_gate[0])
    cos_t, sin_t = _rope_tables(seq)
    gq, gk, gv, gr, sq, ga, gb, sk, sv, lg = _in_proj(x2, g_mix, w_all, b_all, cos_t, sin_t, wg, bg, seq)
    oa = _gla(gq, gk, gv, lg, gr, g_gla_head, bsz, seq)
    ob = _swa(sinks[0], sq, sk, sv, bsz, seq)
    x1, hn, logits_t = _merge(x2, oa, ob, ga, gb, w_gla_out[0].astype(BF16), w_swa_out[0].astype(BF16),
                              w_out[0].astype(BF16), g_ffn, w_router[0].T, b_router[0][:, None])

    e_kt, g_kt, r_kt, cnt = _route(logits_t)
    counts = cnt[:, 0].astype(I32)
    blocks_e = (counts + MOE_BLOCK - 1) // MOE_BLOCK
    bend = jnp.cumsum(blocks_e)
    pstart = (bend - blocks_e) * MOE_BLOCK
    dest_kt = r_kt
    for e in range(N_EXPERTS):
        dest_kt = dest_kt + jnp.where(e_kt == e, pstart[e], 0)
    dest_flat = dest_kt.T.reshape(t * TOP_K)
    n_blocks = (t * TOP_K) // MOE_BLOCK + N_EXPERTS
    block_e = jnp.minimum(jnp.sum(bend[None, :] <= jnp.arange(n_blocks, dtype=I32)[:, None], axis=1),
                          N_EXPERTS - 1).astype(I32)
    n_valid = bend[-1:].astype(I32)

    b1 = b_e1[0].reshape(N_EXPERTS, -1, LANES, 2).transpose(0, 1, 3, 2).reshape(N_EXPERTS, 1, -1)
    xs, w1, w2 = _dispatch(bend * MOE_BLOCK, blocks_e, n_valid, dest_flat, hn, w_e1[0], w_e2[0],
                           _pair_split_perm(), n_blocks * MOE_BLOCK)
    ys = _experts(block_e, n_valid, xs, w1, b1, w2, b_e2[0][:, None, :])
    out = _final(dest_flat, x1, g_kt.T, g_final[None, :], ys)
    return out.reshape(bsz, seq, d)
```

```python
import jax
import jax.numpy as jnp
import numpy as np
from jax import lax
from jax.experimental import pallas as pl
from jax.experimental.pallas import tpu as pltpu

F32 = jnp.float32
BF16 = jnp.bfloat16
I32 = jnp.int32

NORM_EPS = 1e-5
GLA_HEADS = 4
GLA_DK = 128
GLA_DV = 256
GLA_GATE_RANK = 16
GLA_TAU = 16.0
SWA_Q_HEADS = 16
SWA_KV_HEADS = 2
SWA_GROUP = SWA_Q_HEADS // SWA_KV_HEADS
SWA_HEAD_DIM = 64
SWA_WINDOW = 128
ROPE_THETA = 10000.0
N_EXPERTS = 32
TOP_K = 4
SWIGLU_LIMIT = 7.0
SWIGLU_ALPHA = 1.702

LANES = 128
NEG = -1e30
VMEM_LIMIT = 56 * 1024 * 1024

PROJ_TM = 256
GLA_CHUNK = 128
GLA_HEADS_PER_STEP = 4
MERGE_TM = 512
ROUTE_TT = 512
MOE_BLOCK = 512
DISPATCH_GROUPS = 16
FINAL_TM = 512

_D = 1024
_C_GQ = 0
_C_GK = _C_GQ + GLA_HEADS * GLA_DK
_C_GV = _C_GK + GLA_HEADS * GLA_DK
_C_GR = _C_GV + GLA_HEADS * GLA_DV
_C_SQ = _C_GR + GLA_HEADS * GLA_DV
_C_GA = _C_SQ + SWA_Q_HEADS * SWA_HEAD_DIM
_C_GB = _C_GA + _D
_C_SK = _C_GB + _D
_C_SV = _C_SK + SWA_KV_HEADS * LANES
_C_LR = _C_SV + SWA_KV_HEADS * LANES
_C_END = _C_LR + LANES


def _cparams(sem):
    return pltpu.CompilerParams(dimension_semantics=sem, vmem_limit_bytes=VMEM_LIMIT)


def _rms(x, g):
    return x * lax.rsqrt(jnp.mean(x * x, axis=-1, keepdims=True) + NORM_EPS) * g


def _sigmoid(x):
    return 1.0 / (1.0 + jnp.exp(-x))


def _dot_nt(a, b):
    return lax.dot_general(a, b, (((1,), (1,)), ((), ())), preferred_element_type=F32)


def _rope_slabs(acc, cos, sin, first_half):
    outs = []
    for i in range(acc.shape[1] // LANES):
        xs = acc[:, LANES * i:LANES * (i + 1)]
        partner = jnp.where(first_half, pltpu.roll(xs, LANES - 32, 1), pltpu.roll(xs, 32, 1))
        outs.append(xs * cos + partner * sin)
    return jnp.concatenate(outs, axis=1)


def _in_proj_kernel(x_ref, g_ref, w_ref, b_ref, cos_ref, sin_ref, wg_ref, bg_ref,
                    gq_ref, gk_ref, gv_ref, gr_ref, sq_ref, ga_ref, gb_ref, sk_ref, sv_ref, lg_ref):
    h = _rms(x_ref[...], g_ref[...]).astype(BF16)

    def proj(lo, hi):
        return jnp.dot(h, w_ref[:, lo:hi], preferred_element_type=F32) + b_ref[:, lo:hi]

    cos = cos_ref[...]
    sin = sin_ref[...]
    lane = lax.broadcasted_iota(I32, cos.shape, 1)
    first_half = (lane % SWA_HEAD_DIM) < (SWA_HEAD_DIM // 2)

    gq_ref[...] = (proj(_C_GQ, _C_GK) * (GLA_DK ** -0.5)).astype(BF16)
    gk_ref[...] = proj(_C_GK, _C_GV).astype(BF16)
    gv_ref[...] = proj(_C_GV, _C_GR).astype(BF16)
    gr = proj(_C_GR, _C_SQ)
    gr_ref[...] = (gr * _sigmoid(gr)).astype(BF16)
    sq = proj(_C_SQ, _C_GA) * (SWA_HEAD_DIM ** -0.5)
    sq_ref[...] = _rope_slabs(sq, cos, sin, first_half).astype(BF16)
    ga_ref[...] = _sigmoid(proj(_C_GA, _C_GB)).astype(BF16)
    gb_ref[...] = _sigmoid(proj(_C_GB, _C_SK)).astype(BF16)
    sk_ref[...] = _rope_slabs(proj(_C_SK, _C_SV), cos, sin, first_half).astype(BF16)
    sv_ref[...] = proj(_C_SV, _C_LR).astype(BF16)
    z = jnp.dot(proj(_C_LR, _C_END), wg_ref[...], precision=lax.Precision.HIGHEST,
                preferred_element_type=F32) + bg_ref[...]
    log_sig = jnp.minimum(z, 0.0) - jnp.log(1.0 + jnp.exp(-jnp.abs(z)))
    lg_ref[...] = log_sig * (1.0 / GLA_TAU)


def _in_proj(x2, g_mix, w_all, b_all, cos_t, sin_t, wg, bg, seq):
    t = x2.shape[0]
    tm = PROJ_TM
    pos_blocks = seq // tm
    const = lambda i: (0, 0)
    row = lambda i: (i, 0)
    widths = [(_C_GK - _C_GQ, BF16), (_C_GV - _C_GK, BF16), (_C_GR - _C_GV, BF16), (_C_SQ - _C_GR, BF16),
              (_C_GA - _C_SQ, BF16), (_D, BF16), (_D, BF16), (_C_SV - _C_SK, BF16), (_C_LR - _C_SV, BF16),
              (GLA_HEADS * GLA_DK, F32)]
    return pl.pallas_call(
        _in_proj_kernel,
        grid=(t // tm,),
        in_specs=[
            pl.BlockSpec((tm, _D), row),
            pl.BlockSpec((1, _D), const),
            pl.BlockSpec((_D, _C_END), const, pipeline_mode=pl.Buffered(1)),
            pl.BlockSpec((1, _C_END), const),
            pl.BlockSpec((tm, LANES), lambda i: (i % pos_blocks, 0)),
            pl.BlockSpec((tm, LANES), lambda i: (i % pos_blocks, 0)),
            pl.BlockSpec((LANES, GLA_HEADS * GLA_DK), const),
            pl.BlockSpec((1, GLA_HEADS * GLA_DK), const),
        ],
        out_specs=[pl.BlockSpec((tm, w), row) for w, _ in widths],
        out_shape=[jax.ShapeDtypeStruct((t, w), dt) for w, dt in widths],
        compiler_params=_cparams(("parallel",)),
        name="in_proj",
    )(x2, g_mix, w_all, b_all, cos_t, sin_t, wg, bg)


def _gla_cumsum_operator(c_len):
    t = np.arange(c_len)[:, None]
    r = np.arange(c_len)[None, :]
    return np.tile((r <= t).astype(np.float32), (1, 3))


def _gla_kernel(q_ref, k_ref, v_ref, lg_ref, gr_ref, gh_ref, dm_ref, o_ref, st_ref):
    c_len = GLA_CHUNK
    n_lev = c_len.bit_length() - 1
    seq = q_ref.shape[0]
    st_ref[...] = jnp.zeros_like(st_ref)

    t_i = lax.broadcasted_iota(I32, (c_len, c_len), 0)
    j_i = lax.broadcasted_iota(I32, (c_len, c_len), 1)
    row = lax.broadcasted_iota(I32, (c_len, 1), 0)
    diag = t_i == j_i
    upper, pair = [], []
    for lev in range(n_lev):
        s = c_len >> (lev + 1)
        upper.append((row & s) != 0)
        pair.append(((t_i // (2 * s)) == (j_i // (2 * s))) & ((t_i & s) != 0) & ((j_i & s) == 0))

    sub8 = lax.broadcasted_iota(I32, (c_len // 8, 8, GLA_DK), 1)

    def boundary_rows(b, s):
        if s >= 4:
            b3 = b.reshape(c_len // (2 * s), 2 * s, GLA_DK)
            return jnp.broadcast_to(b3[:, s - 1:s, :], b3.shape).reshape(c_len, GLA_DK)
        b3 = b.reshape(c_len // 8, 8, GLA_DK)
        lo = jnp.broadcast_to(b3[:, 1:2, :], b3.shape)
        hi = jnp.broadcast_to(b3[:, 5:6, :], b3.shape)
        return jnp.where(sub8 < 4, lo, hi).reshape(c_len, GLA_DK)

    def head_chunk(r0, hh):
        kcols = slice(GLA_DK * hh, GLA_DK * (hh + 1))
        vcols = slice(GLA_DV * hh, GLA_DV * (hh + 1))
        q_bf = q_ref[pl.ds(r0, c_len), kcols]
        k_bf = k_ref[pl.ds(r0, c_len), kcols]
        q = q_bf.astype(F32)
        k = k_bf.astype(F32)
        v = v_ref[pl.ds(r0, c_len), vcols]

        lg = lg_ref[pl.ds(r0, c_len), kcols]
        lg_hi = lg.astype(BF16)
        rem = lg - lg_hi.astype(F32)
        lg_mid = rem.astype(BF16)
        lg_lo = (rem - lg_mid.astype(F32)).astype(BF16)
        b = jnp.dot(dm_ref[...], jnp.concatenate([lg_hi, lg_mid, lg_lo], axis=0),
                    preferred_element_type=F32)
        w_cum = jnp.exp(b)

        st = st_ref[hh]
        o = _dot_nt((q * w_cum).astype(BF16), st.astype(BF16))

        a = jnp.where(diag, _dot_nt(q_bf, k_bf), 0.0)
        for lev in range(n_lev):
            s = c_len >> (lev + 1)
            if s == 1:
                w = jnp.where(upper[lev], jnp.exp(lg), 1.0)
            else:
                w = jnp.exp(-jnp.abs(b - boundary_rows(b, s)))
            z = (jnp.where(upper[lev], q, k) * w).astype(BF16)
            a = jnp.where(pair[lev], _dot_nt(z, z), a)
        o = o + jnp.dot(a.astype(BF16), v, preferred_element_type=F32)

        b_last = b[c_len - 1:c_len, :]
        upd = lax.dot_general(v, (k * jnp.exp(b_last - b)).astype(BF16), (((0,), (0,)), ((), ())),
                              preferred_element_type=F32)
        st_ref[hh] = st * w_cum[c_len - 1:c_len, :] + upd

        on = _rms(o, gh_ref[...])
        o_ref[pl.ds(r0, c_len), vcols] = (on * gr_ref[pl.ds(r0, c_len), vcols].astype(F32)).astype(BF16)

    def chunk(c, carry):
        r0 = pl.multiple_of(c * c_len, c_len)
        for hh in range(GLA_HEADS_PER_STEP):
            head_chunk(r0, hh)
        return carry

    lax.fori_loop(0, seq // c_len, chunk, 0)


def _gla(gq, gk, gv, lg, gr, g_head, bsz, seq):
    t = gq.shape[0]
    hs = GLA_HEADS_PER_STEP
    dmat = jnp.asarray(_gla_cumsum_operator(GLA_CHUNK), dtype=BF16)
    return pl.pallas_call(
        _gla_kernel,
        grid=(bsz, GLA_HEADS // hs),
        in_specs=[
            pl.BlockSpec((seq, hs * GLA_DK), lambda b, h: (b, h)),
            pl.BlockSpec((seq, hs * GLA_DK), lambda b, h: (b, h)),
            pl.BlockSpec((seq, hs * GLA_DV), lambda b, h: (b, h)),
            pl.BlockSpec((seq, hs * GLA_DK), lambda b, h: (b, h)),
            pl.BlockSpec((seq, hs * GLA_DV), lambda b, h: (b, h)),
            pl.BlockSpec((1, GLA_DV), lambda b, h: (0, 0)),
            pl.BlockSpec(dmat.shape, lambda b, h: (0, 0)),
        ],
        out_specs=pl.BlockSpec((seq, hs * GLA_DV), lambda b, h: (b, h)),
        out_shape=jax.ShapeDtypeStruct((t, GLA_HEADS * GLA_DV), BF16),
        scratch_shapes=[pltpu.VMEM((hs, GLA_DV, GLA_DK), F32)],
        compiler_params=_cparams(("parallel", "parallel")),
        name="gla",
    )(gq, gk, gv, lg, gr, g_head, dmat)


def _swa_kernel(sink_ref, q_ref, k_ref, v_ref, o_ref):
    w = SWA_WINDOW
    seq = q_ref.shape[0]
    hk = pl.program_id(1)
    lane_q = lax.broadcasted_iota(I32, (w, LANES), 1)
    low_q = lane_q < SWA_HEAD_DIM
    lane_b = lax.broadcasted_iota(I32, (2 * w, LANES), 1)
    low_b = lane_b < SWA_HEAD_DIM
    qi = lax.broadcasted_iota(I32, (w, 2 * w), 0)
    kj = lax.broadcasted_iota(I32, (w, 2 * w), 1)
    in_window = (kj > qi) & (kj <= qi + w)
    zero_q = jnp.zeros((w, LANES), BF16)
    zero_b = jnp.zeros((2 * w, LANES), BF16)

    def block(n, carry):
        r0 = pl.multiple_of(n * w, w)
        p0 = pl.multiple_of(jnp.maximum(n - 1, 0) * w, w)
        kb = jnp.concatenate([k_ref[pl.ds(p0, w), :], k_ref[pl.ds(r0, w), :]], axis=0)
        vb = jnp.concatenate([v_ref[pl.ds(p0, w), :], v_ref[pl.ds(r0, w), :]], axis=0)
        valid = in_window & ((kj >= w) | (n > 0))
        v_lo = jnp.where(low_b, vb, zero_b)
        v_hi = jnp.where(low_b, zero_b, vb)
        for m in range(SWA_GROUP // 2):
            qp = q_ref[pl.ds(r0, w), LANES * m:LANES * (m + 1)]
            acc = jnp.zeros((w, LANES), F32)
            for par in range(2):
                qm = jnp.where(low_q, qp, zero_q) if par == 0 else jnp.where(low_q, zero_q, qp)
                s = jnp.where(valid, _dot_nt(qm, kb), NEG)
                sink = sink_ref[hk * SWA_GROUP + 2 * m + par]
                mx = jnp.maximum(jnp.max(s, axis=-1, keepdims=True), sink)
                p = jnp.exp(s - mx)
                den = jnp.sum(p, axis=-1, keepdims=True) + jnp.exp(sink - mx)
                pv = jnp.dot(p.astype(BF16), v_lo if par == 0 else v_hi, preferred_element_type=F32)
                acc = acc + pv / den
            o_ref[pl.ds(r0, w), LANES * m:LANES * (m + 1)] = acc.astype(BF16)
        return carry

    lax.fori_loop(0, seq // w, block, 0)


def _swa(sinks, sq, sk, sv, bsz, seq):
    t = sq.shape[0]
    gw = SWA_GROUP * SWA_HEAD_DIM
    return pl.pallas_call(
        _swa_kernel,
        grid_spec=pltpu.PrefetchScalarGridSpec(
            num_scalar_prefetch=1,
            grid=(bsz, SWA_KV_HEADS),
            in_specs=[
                pl.BlockSpec((seq, gw), lambda b, h, s: (b, h)),
                pl.BlockSpec((seq, LANES), lambda b, h, s: (b, h)),
                pl.BlockSpec((seq, LANES), lambda b, h, s: (b, h)),
            ],
            out_specs=pl.BlockSpec((seq, gw), lambda b, h, s: (b, h)),
        ),
        out_shape=jax.ShapeDtypeStruct((t, SWA_Q_HEADS * SWA_HEAD_DIM), BF16),
        compiler_params=_cparams(("parallel", "parallel")),
        name="swa",
    )(sinks, sq, sk, sv)


def _merge_kernel(x_ref, oa_ref, ob_ref, ga_ref, gb_ref, wa_ref, wb_ref, wo_ref, gf_ref, wr_ref, br_ref,
                  x1_ref, hn_ref, lt_ref):
    ya = jnp.dot(oa_ref[...], wa_ref[...], preferred_element_type=F32)
    yb = jnp.dot(ob_ref[...], wb_ref[...], preferred_element_type=F32)
    mixed = ga_ref[...].astype(F32) * ya + gb_ref[...].astype(F32) * yb
    x1 = x_ref[...] + jnp.dot(mixed.astype(BF16), wo_ref[...], preferred_element_type=F32)
    x1_ref[...] = x1
    hn = _rms(x1, gf_ref[...])
    hn_ref[...] = hn
    lt_ref[...] = lax.dot_general(wr_ref[...], hn, (((1,), (1,)), ((), ())),
                                  precision=lax.Precision.HIGHEST,
                                  preferred_element_type=F32) + br_ref[...]


def _merge(x2, oa, ob, ga, gb, wa, wb, wo, g_ffn, wr_t, br_col):
    t = x2.shape[0]
    tm = MERGE_TM
    row = lambda i: (i, 0)
    const = lambda i: (0, 0)
    return pl.pallas_call(
        _merge_kernel,
        grid=(t // tm,),
        in_specs=[pl.BlockSpec((tm, _D), row)] * 5 + [pl.BlockSpec((_D, _D), const)] * 3 + [
            pl.BlockSpec((1, _D), const),
            pl.BlockSpec((N_EXPERTS, _D), const),
            pl.BlockSpec((N_EXPERTS, 1), const),
        ],
        out_specs=[pl.BlockSpec((tm, _D), row), pl.BlockSpec((tm, _D), row),
                   pl.BlockSpec((N_EXPERTS, tm), lambda i: (0, i))],
        out_shape=[jax.ShapeDtypeStruct((t, _D), F32), jax.ShapeDtypeStruct((t, _D), F32),
                   jax.ShapeDtypeStruct((N_EXPERTS, t), F32)],
        compiler_params=_cparams(("parallel",)),
        name="merge",
    )(x2, oa, ob, ga, gb, wa, wb, wo, g_ffn, wr_t, br_col)


def _route_kernel(lt_ref, e_ref, g_ref, r_ref, cnt_ref, carry_ref):
    tt = lt_ref.shape[1]

    @pl.when(pl.program_id(0) == 0)
    def _():
        carry_ref[...] = jnp.zeros_like(carry_ref)

    eid = lax.broadcasted_iota(I32, (N_EXPERTS, tt), 0)
    work = lt_ref[...]
    vals, idxs = [], []
    chosen = jnp.zeros((N_EXPERTS, tt), F32)
    for _ in range(TOP_K):
        m = jnp.max(work, axis=0, keepdims=True)
        idx = jnp.min(jnp.where(work == m, eid, N_EXPERTS), axis=0, keepdims=True)
        hit = eid == idx
        work = jnp.where(hit, -jnp.inf, work)
        chosen = jnp.where(hit, 1.0, chosen)
        vals.append(m)
        idxs.append(idx)
    ex = [jnp.exp(v - vals[0]) for v in vals]
    den = ex[0] + ex[1] + ex[2] + ex[3]

    t_r = lax.broadcasted_iota(I32, (tt, tt), 0)
    t_c = lax.broadcasted_iota(I32, (tt, tt), 1)
    before = (t_r < t_c).astype(BF16)
    pref = jnp.dot(chosen.astype(BF16), before, preferred_element_type=F32) + carry_ref[:, 0:1]
    for k in range(TOP_K):
        e_ref[k:k + 1, :] = idxs[k]
        g_ref[k:k + 1, :] = ex[k] / den
        r_ref[k:k + 1, :] = jnp.sum(jnp.where(eid == idxs[k], pref, 0.0), axis=0, keepdims=True).astype(I32)
    total = pref[:, tt - 1:tt] + chosen[:, tt - 1:tt]
    carry_ref[...] = jnp.broadcast_to(total, carry_ref.shape)
    cnt_ref[...] = jnp.broadcast_to(total, cnt_ref.shape)


def _route(logits_t):
    t = logits_t.shape[1]
    tt = ROUTE_TT
    blk = lambda i: (0, i)
    return pl.pallas_call(
        _route_kernel,
        grid=(t // tt,),
        in_specs=[pl.BlockSpec((N_EXPERTS, tt), blk)],
        out_specs=[pl.BlockSpec((TOP_K, tt), blk), pl.BlockSpec((TOP_K, tt), blk), pl.BlockSpec((TOP_K, tt), blk),
                   pl.BlockSpec((N_EXPERTS, LANES), lambda i: (0, 0))],
        out_shape=[jax.ShapeDtypeStruct((TOP_K, t), I32), jax.ShapeDtypeStruct((TOP_K, t), F32),
                   jax.ShapeDtypeStruct((TOP_K, t), I32), jax.ShapeDtypeStruct((N_EXPERTS, LANES), F32)],
        scratch_shapes=[pltpu.VMEM((N_EXPERTS, LANES), F32)],
        compiler_params=_cparams(("arbitrary",)),
        name="route",
    )(logits_t)


def _dispatch_kernel(pend_ref, nblk_ref, nv_ref, dest_ref, hn_ref, w1_ref, w2_ref, perm_ref,
                     xs_hbm, w1o_ref, w2o_ref, zero_ref, sem, zsem):
    n_tok = hn_ref.shape[0]
    n_rows = w1_ref.shape[1]
    gw = perm_ref.shape[0]
    n_col_groups = w1_ref.shape[2] // gw
    slices = DISPATCH_GROUPS * n_col_groups
    tok_s = n_tok // slices
    rows_g = n_rows // DISPATCH_GROUPS
    w2_cols = w2_ref.shape[2] // n_col_groups
    n_blocks = xs_hbm.shape[0] // MOE_BLOCK

    def zero_block(row0):
        return pltpu.make_async_copy(zero_ref, xs_hbm.at[pl.ds(pl.multiple_of(row0, MOE_BLOCK), MOE_BLOCK)], zsem)

    @pl.when(pl.program_id(0) == 0)
    def _():
        zero_ref[...] = jnp.zeros_like(zero_ref)

        def expert_tail(e, carry, start):
            @pl.when(nblk_ref[e] > 0)
            def _():
                cp = zero_block(pend_ref[e] - MOE_BLOCK)
                cp.start() if start else cp.wait()
            return carry

        def unused_block(b, carry, start):
            cp = zero_block(b * MOE_BLOCK)
            cp.start() if start else cp.wait()
            return carry

        for start in (True, False):
            lax.fori_loop(0, N_EXPERTS, lambda e, c: expert_tail(e, c, start), 0)
            lax.fori_loop(nv_ref[0], n_blocks, lambda b, c: unused_block(b, c, start), 0)

    def group(g, carry):
        r0 = pl.multiple_of(g * rows_g, rows_g)
        for gi in range(n_col_groups):
            t0 = (g * n_col_groups + gi) * tok_s
            for i in range(tok_s):
                for k in range(TOP_K):
                    d = dest_ref[0, 0, (t0 + i) * TOP_K + k]
                    pltpu.make_async_copy(hn_ref.at[pl.ds(t0 + i, 1)], xs_hbm.at[pl.ds(d, 1)], sem).start()
            wb = w1_ref[0, pl.ds(r0, rows_g), gw * gi:gw * (gi + 1)].astype(BF16)
            w1o_ref[0, pl.ds(r0, rows_g), gw * gi:gw * (gi + 1)] = jnp.dot(
                wb, perm_ref[...], preferred_element_type=F32).astype(BF16)
            w2o_ref[0, pl.ds(r0, rows_g), w2_cols * gi:w2_cols * (gi + 1)] = w2_ref[
                0, pl.ds(r0, rows_g), w2_cols * gi:w2_cols * (gi + 1)].astype(BF16)
        return carry

    lax.fori_loop(0, DISPATCH_GROUPS, group, 0)
    for k in range(TOP_K):
        pltpu.make_async_copy(hn_ref, xs_hbm.at[pl.ds(0, n_tok)], sem).wait()


def _dispatch(pend_rows, blocks_e, n_valid, dest_flat, hn, w1, w2, perm, rows):
    t = hn.shape[0]
    n_e, d, n1 = w1.shape
    dff = w2.shape[1]
    tm = t // n_e
    assert dff == d and tm % (DISPATCH_GROUPS * (n1 // perm.shape[0])) == 0
    emap = lambda e, *_: (e, 0, 0)
    return pl.pallas_call(
        _dispatch_kernel,
        grid_spec=pltpu.PrefetchScalarGridSpec(
            num_scalar_prefetch=3,
            grid=(n_e,),
            in_specs=[pl.BlockSpec((1, 1, tm * TOP_K), emap, memory_space=pltpu.SMEM),
                      pl.BlockSpec((tm, _D), lambda e, *_: (e, 0)),
                      pl.BlockSpec((1, d, n1), emap),
                      pl.BlockSpec((1, dff, _D), emap),
                      pl.BlockSpec(perm.shape, lambda e, *_: (0, 0))],
            out_specs=[pl.BlockSpec(memory_space=pl.ANY),
                       pl.BlockSpec((1, d, n1), emap),
                       pl.BlockSpec((1, dff, _D), emap)],
            scratch_shapes=[pltpu.VMEM((MOE_BLOCK, _D), F32), pltpu.SemaphoreType.DMA, pltpu.SemaphoreType.DMA],
        ),
        out_shape=[jax.ShapeDtypeStruct((rows, _D), F32),
                   jax.ShapeDtypeStruct((n_e, d, n1), BF16),
                   jax.ShapeDtypeStruct((n_e, dff, _D), BF16)],
        compiler_params=_cparams(("arbitrary",)),
        name="dispatch",
    )(pend_rows, blocks_e, n_valid, dest_flat.reshape(n_e, 1, tm * TOP_K), hn, w1, w2, perm)


def _experts_kernel(be_ref, nv_ref, xs_ref, w1_ref, b1_ref, w2_ref, b2_ref, ys_ref):
    del be_ref

    @pl.when(pl.program_id(0) < nv_ref[0])
    def _():
        x = xs_ref[...].astype(BF16)
        u = jnp.dot(x, w1_ref[0], preferred_element_type=F32) + b1_ref[0]
        acts = []
        for gi in range(u.shape[1] // (2 * LANES)):
            g = jnp.minimum(u[:, 2 * LANES * gi:2 * LANES * gi + LANES], SWIGLU_LIMIT)
            lin = jnp.clip(u[:, 2 * LANES * gi + LANES:2 * LANES * (gi + 1)], -SWIGLU_LIMIT, SWIGLU_LIMIT)
            acts.append((g * _sigmoid(SWIGLU_ALPHA * g) * (lin + 1.0)).astype(BF16))
        act = jnp.concatenate(acts, axis=1)
        ys_ref[...] = jnp.dot(act, w2_ref[0], preferred_element_type=F32) + b2_ref[0]

    @pl.when(pl.program_id(0) >= nv_ref[0])
    def _():
        ys_ref[...] = jnp.zeros_like(ys_ref)


def _experts(block_e, n_valid, xs, w1, b1, w2, b2):
    rows = xs.shape[0]
    n_blocks = rows // MOE_BLOCK
    dff = w2.shape[1]
    wmap = lambda i, be, nv: (be[i], 0, 0)
    rmap = lambda i, be, nv: (i, 0)
    return pl.pallas_call(
        _experts_kernel,
        grid_spec=pltpu.PrefetchScalarGridSpec(
            num_scalar_prefetch=2,
            grid=(n_blocks,),
            in_specs=[
                pl.BlockSpec((MOE_BLOCK, _D), rmap),
                pl.BlockSpec((1, _D, 2 * dff), wmap),
                pl.BlockSpec((1, 1, 2 * dff), wmap),
                pl.BlockSpec((1, dff, _D), wmap),
                pl.BlockSpec((1, 1, _D), wmap),
            ],
            out_specs=pl.BlockSpec((MOE_BLOCK, _D), rmap),
        ),
        out_shape=jax.ShapeDtypeStruct((rows, _D), F32),
        compiler_params=_cparams(("arbitrary",)),
        name="experts",
    )(block_e, n_valid, xs, w1, b1, w2, b2)


def _final_kernel(dest_ref, x1_ref, gt_ref, gf_ref, ys_hbm, o_ref, buf_ref, sem):
    n_tok = x1_ref.shape[0]

    def issue(i, carry):
        for k in range(TOP_K):
            d = dest_ref[0, 0, i * TOP_K + k]
            pltpu.make_async_copy(ys_hbm.at[pl.ds(d, 1)], buf_ref.at[k, pl.ds(i, 1)], sem).start()
        return carry

    lax.fori_loop(0, n_tok, issue, 0, unroll=4)
    for k in range(TOP_K):
        pltpu.make_async_copy(ys_hbm.at[pl.ds(0, n_tok)], buf_ref.at[k], sem).wait()

    gates = gt_ref[...]
    y = x1_ref[...]
    for k in range(TOP_K):
        y = y + gates[:, k:k + 1] * buf_ref[k]
    o_ref[...] = _rms(y, gf_ref[...])


def _final(dest_flat, x1, gates_tk, g_final, ys):
    t = x1.shape[0]
    tm = FINAL_TM
    steps = t // tm
    row = lambda i: (i, 0)
    return pl.pallas_call(
        _final_kernel,
        grid=(steps,),
        in_specs=[pl.BlockSpec((1, 1, tm * TOP_K), lambda i: (i, 0, 0), memory_space=pltpu.SMEM),
                  pl.BlockSpec((tm, _D), row), pl.BlockSpec((tm, TOP_K), row),
                  pl.BlockSpec((1, _D), lambda i: (0, 0)), pl.BlockSpec(memory_space=pl.ANY)],
        out_specs=pl.BlockSpec((tm, _D), row),
        out_shape=jax.ShapeDtypeStruct((t, _D), F32),
        scratch_shapes=[pltpu.VMEM((TOP_K, tm, _D), F32), pltpu.SemaphoreType.DMA],
        compiler_params=_cparams(("arbitrary",)),
        name="final",
    )(dest_flat.reshape(steps, 1, tm * TOP_K), x1, gates_tk, g_final, ys)


def _prep_in_proj(w_in, b_in, w_gate, b_gate):
    sizes = (GLA_HEADS * GLA_DK, GLA_HEADS * GLA_DK, GLA_HEADS * GLA_DV, GLA_HEADS * GLA_DV, GLA_GATE_RANK,
             SWA_Q_HEADS * SWA_HEAD_DIM, SWA_KV_HEADS * SWA_HEAD_DIM, SWA_KV_HEADS * SWA_HEAD_DIM, _D, _D)
    offs = [0]
    for s in sizes:
        offs.append(offs[-1] + s)

    def rearrange(m):
        p = [m[..., offs[i]:offs[i + 1]] for i in range(len(sizes))]
        gq, gk, gv, gr, lr, sq, sk, sv, ga, gb = p

        def dup_heads(a):
            hs = [a[..., SWA_HEAD_DIM * h:SWA_HEAD_DIM * (h + 1)] for h in range(SWA_KV_HEADS)]
            return jnp.concatenate([hh for h in hs for hh in (h, h)], axis=-1)

        lr_pad = jnp.pad(lr, [(0, 0)] * (lr.ndim - 1) + [(0, LANES - GLA_GATE_RANK)])
        return jnp.concatenate([gq, gk, gv, gr, sq, ga, gb, dup_heads(sk), dup_heads(sv), lr_pad], axis=-1)

    w_all = rearrange(w_in).astype(BF16)
    b_all = rearrange(b_in[None, :])
    wg = jnp.pad(w_gate, ((0, LANES - GLA_GATE_RANK), (0, 0)))
    return w_all, b_all, wg, b_gate[None, :]


def _rope_tables(seq):
    half = SWA_HEAD_DIM // 2
    inv_freq = ROPE_THETA ** (-jnp.arange(half, dtype=F32) / half)
    ang = jnp.arange(seq, dtype=F32)[:, None] * inv_freq[None, :]
    cos, sin = jnp.cos(ang), jnp.sin(ang)
    cos_t = jnp.concatenate([cos, cos] * (LANES // SWA_HEAD_DIM), axis=1)
    sin_t = jnp.concatenate([-sin, sin] * (LANES // SWA_HEAD_DIM), axis=1)
    return cos_t, sin_t


def _pair_split_perm():
    src = jnp.arange(2 * LANES, dtype=I32)
    dst = jnp.where(src % 2 == 0, src // 2, LANES + src // 2)
    return (dst[:, None] == jnp.arange(2 * LANES, dtype=I32)[None, :]).astype(BF16)


def kernel(x, g_mix, w_in, b_in, w_gla_gate, b_gla_gate, g_gla_head, w_gla_out, sinks, w_swa_out, w_out,
           g_ffn, w_router, b_router, w_e1, b_e1, w_e2, b_e2, g_final):
    bsz, seq, d = x.shape
    assert d == _D and w_in.shape[0] == 1, "single-layer, d_model=1024 only"
    assert seq % max(PROJ_TM, SWA_WINDOW, GLA_CHUNK) == 0
    t = bsz * seq
    assert t % max(MERGE_TM, ROUTE_TT, FINAL_TM) == 0 and t % N_EXPERTS == 0
    x2 = x.reshape(t, d)

    w_all, b_all, wg, bg = _prep_in_proj(w_in[0], b_in[0], w_gla_gate[0], b_gla_gate[0])
    cos_t, sin_t = _rope_tables(seq)
    gq, gk, gv, gr, sq, ga, gb, sk, sv, lg = _in_proj(x2, g_mix, w_all, b_all, cos_t, sin_t, wg, bg, seq)
    oa = _gla(gq, gk, gv, lg, gr, g_gla_head, bsz, seq)
    ob = _swa(sinks[0], sq, sk, sv, bsz, seq)
    x1, hn, logits_t = _merge(x2, oa, ob, ga, gb, w_gla_out[0].astype(BF16), w_swa_out[0].astype(BF16),
                              w_out[0].astype(BF16), g_ffn, w_router[0].T, b_router[0][:, None])

    e_kt, g_kt, r_kt, cnt = _route(logits_t)
    counts = cnt[:, 0].astype(I32)
    blocks_e = (counts + MOE_BLOCK - 1) // MOE_BLOCK
    bend = jnp.cumsum(blocks_e)
    pstart = (bend - blocks_e) * MOE_BLOCK
    dest_kt = r_kt
    for e in range(N_EXPERTS):
        dest_kt = dest_kt + jnp.where(e_kt == e, pstart[e], 0)
    dest_flat = dest_kt.T.reshape(t * TOP_K)
    n_blocks = (t * TOP_K) // MOE_BLOCK + N_EXPERTS
    block_e = jnp.minimum(jnp.sum(bend[None, :] <= jnp.arange(n_blocks, dtype=I32)[:, None], axis=1),
                          N_EXPERTS - 1).astype(I32)
    n_valid = bend[-1:].astype(I32)

    b1 = b_e1[0].reshape(N_EXPERTS, -1, LANES, 2).transpose(0, 1, 3, 2).reshape(N_EXPERTS, 1, -1)
    xs, w1, w2 = _dispatch(bend * MOE_BLOCK, blocks_e, n_valid, dest_flat, hn, w_e1[0], w_e2[0],
                           _pair_split_perm(), n_blocks * MOE_BLOCK)
    ys = _experts(block_e, n_valid, xs, w1, b1, w2, b_e2[0][:, None, :])
    out = _final(dest_flat, x1, g_kt.T, g_final[None, :], ys)
    return out.reshape(bsz, seq, d)
```

```python
import jax
import jax.numpy as jnp
import numpy as np
from jax import lax
from jax.experimental import pallas as pl
from jax.experimental.pallas import tpu as pltpu

F32 = jnp.float32
BF16 = jnp.bfloat16
I32 = jnp.int32

NORM_EPS = 1e-5
GLA_HEADS = 4
GLA_DK = 128
GLA_DV = 256
GLA_GATE_RANK = 16
GLA_TAU = 16.0
SWA_Q_HEADS = 16
SWA_KV_HEADS = 2
SWA_GROUP = SWA_Q_HEADS // SWA_KV_HEADS
SWA_HEAD_DIM = 64
SWA_WINDOW = 128
ROPE_THETA = 10000.0
N_EXPERTS = 32
TOP_K = 4
SWIGLU_LIMIT = 7.0
SWIGLU_ALPHA = 1.702

LANES = 128
NEG = -1e30
VMEM_LIMIT = 56 * 1024 * 1024

PROJ_TM = 256
GLA_CHUNK = 128
GLA_HEADS_PER_STEP = 4
MERGE_TM = 512
MERGE_TN = 256
ROUTE_TT = 512
MOE_BLOCK = 512
DISPATCH_GROUPS = 16
FINAL_TM = 512

_D = 1024
_C_GQ = 0
_C_GK = _C_GQ + GLA_HEADS * GLA_DK
_C_GV = _C_GK + GLA_HEADS * GLA_DK
_C_GR = _C_GV + GLA_HEADS * GLA_DV
_C_SQ = _C_GR + GLA_HEADS * GLA_DV
_C_GA = _C_SQ + SWA_Q_HEADS * SWA_HEAD_DIM
_C_GB = _C_GA + _D
_C_SK = _C_GB + _D
_C_SV = _C_SK + SWA_KV_HEADS * LANES
_C_LR = _C_SV + SWA_KV_HEADS * LANES
_C_END = _C_LR + LANES


def _cparams(sem):
    return pltpu.CompilerParams(dimension_semantics=sem, vmem_limit_bytes=VMEM_LIMIT)


def _rms(x, g):
    return x * lax.rsqrt(jnp.mean(x * x, axis=-1, keepdims=True) + NORM_EPS) * g


def _sigmoid(x):
    return 1.0 / (1.0 + jnp.exp(-x))


def _dot_nt(a, b):
    return lax.dot_general(a, b, (((1,), (1,)), ((), ())), preferred_element_type=F32)


def _rope_slabs(acc, cos, sin, first_half):
    outs = []
    for i in range(acc.shape[1] // LANES):
        xs = acc[:, LANES * i:LANES * (i + 1)]
        partner = jnp.where(first_half, pltpu.roll(xs, LANES - 32, 1), pltpu.roll(xs, 32, 1))
        outs.append(xs * cos + partner * sin)
    return jnp.concatenate(outs, axis=1)


def _in_proj_kernel(x_ref, g_ref, w_ref, b_ref, cos_ref, sin_ref, wg_ref, bg_ref,
                    gq_ref, gk_ref, gv_ref, gr_ref, sq_ref, ga_ref, gb_ref, sk_ref, sv_ref, lg_ref):
    h = _rms(x_ref[...], g_ref[...]).astype(BF16)

    def proj(lo, hi):
        return jnp.dot(h, w_ref[:, lo:hi], preferred_element_type=F32) + b_ref[:, lo:hi]

    cos = cos_ref[...]
    sin = sin_ref[...]
    lane = lax.broadcasted_iota(I32, cos.shape, 1)
    first_half = (lane % SWA_HEAD_DIM) < (SWA_HEAD_DIM // 2)

    gq_ref[...] = (proj(_C_GQ, _C_GK) * (GLA_DK ** -0.5)).astype(BF16)
    gk_ref[...] = proj(_C_GK, _C_GV).astype(BF16)
    gv_ref[...] = proj(_C_GV, _C_GR).astype(BF16)
    gr = proj(_C_GR, _C_SQ)
    gr_ref[...] = (gr * _sigmoid(gr)).astype(BF16)
    sq = proj(_C_SQ, _C_GA) * (SWA_HEAD_DIM ** -0.5)
    sq_ref[...] = _rope_slabs(sq, cos, sin, first_half).astype(BF16)
    ga_ref[...] = _sigmoid(proj(_C_GA, _C_GB)).astype(BF16)
    gb_ref[...] = _sigmoid(proj(_C_GB, _C_SK)).astype(BF16)
    sk_ref[...] = _rope_slabs(proj(_C_SK, _C_SV), cos, sin, first_half).astype(BF16)
    sv_ref[...] = proj(_C_SV, _C_LR).astype(BF16)
    z = jnp.dot(proj(_C_LR, _C_END), wg_ref[...], precision=lax.Precision.HIGHEST,
                preferred_element_type=F32) + bg_ref[...]
    log_sig = jnp.minimum(z, 0.0) - jnp.log(1.0 + jnp.exp(-jnp.abs(z)))
    lg_ref[...] = log_sig * (1.0 / GLA_TAU)


def _in_proj(x2, g_mix, w_all, b_all, cos_t, sin_t, wg, bg, seq):
    t = x2.shape[0]
    tm = PROJ_TM
    pos_blocks = seq // tm
    const = lambda i: (0, 0)
    row = lambda i: (i, 0)
    widths = [(_C_GK - _C_GQ, BF16), (_C_GV - _C_GK, BF16), (_C_GR - _C_GV, BF16), (_C_SQ - _C_GR, BF16),
              (_C_GA - _C_SQ, BF16), (_D, BF16), (_D, BF16), (_C_SV - _C_SK, BF16), (_C_LR - _C_SV, BF16),
              (GLA_HEADS * GLA_DK, F32)]
    return pl.pallas_call(
        _in_proj_kernel,
        grid=(t // tm,),
        in_specs=[
            pl.BlockSpec((tm, _D), row),
            pl.BlockSpec((1, _D), const),
            pl.BlockSpec((_D, _C_END), const, pipeline_mode=pl.Buffered(1)),
            pl.BlockSpec((1, _C_END), const),
            pl.BlockSpec((tm, LANES), lambda i: (i % pos_blocks, 0)),
            pl.BlockSpec((tm, LANES), lambda i: (i % pos_blocks, 0)),
            pl.BlockSpec((LANES, GLA_HEADS * GLA_DK), const),
            pl.BlockSpec((1, GLA_HEADS * GLA_DK), const),
        ],
        out_specs=[pl.BlockSpec((tm, w), row) for w, _ in widths],
        out_shape=[jax.ShapeDtypeStruct((t, w), dt) for w, dt in widths],
        compiler_params=_cparams(("parallel",)),
        name="in_proj",
    )(x2, g_mix, w_all, b_all, cos_t, sin_t, wg, bg)


def _gla_cumsum_operator(c_len):
    t = np.arange(c_len)[:, None]
    r = np.arange(c_len)[None, :]
    return np.tile((r <= t).astype(np.float32), (1, 3))


def _gla_kernel(q_ref, k_ref, v_ref, lg_ref, gr_ref, gh_ref, dm_ref, o_ref, st_ref):
    c_len = GLA_CHUNK
    n_lev = c_len.bit_length() - 1
    seq = q_ref.shape[0]
    st_ref[...] = jnp.zeros_like(st_ref)

    t_i = lax.broadcasted_iota(I32, (c_len, c_len), 0)
    j_i = lax.broadcasted_iota(I32, (c_len, c_len), 1)
    row = lax.broadcasted_iota(I32, (c_len, 1), 0)
    diag = t_i == j_i
    upper, pair = [], []
    for lev in range(n_lev):
        s = c_len >> (lev + 1)
        upper.append((row & s) != 0)
        pair.append(((t_i // (2 * s)) == (j_i // (2 * s))) & ((t_i & s) != 0) & ((j_i & s) == 0))

    sub8 = lax.broadcasted_iota(I32, (c_len // 8, 8, GLA_DK), 1)

    def boundary_rows(b, s):
        if s >= 4:
            b3 = b.reshape(c_len // (2 * s), 2 * s, GLA_DK)
            return jnp.broadcast_to(b3[:, s - 1:s, :], b3.shape).reshape(c_len, GLA_DK)
        b3 = b.reshape(c_len // 8, 8, GLA_DK)
        lo = jnp.broadcast_to(b3[:, 1:2, :], b3.shape)
        hi = jnp.broadcast_to(b3[:, 5:6, :], b3.shape)
        return jnp.where(sub8 < 4, lo, hi).reshape(c_len, GLA_DK)

    def head_chunk(r0, hh):
        kcols = slice(GLA_DK * hh, GLA_DK * (hh + 1))
        vcols = slice(GLA_DV * hh, GLA_DV * (hh + 1))
        q_bf = q_ref[pl.ds(r0, c_len), kcols]
        k_bf = k_ref[pl.ds(r0, c_len), kcols]
        q = q_bf.astype(F32)
        k = k_bf.astype(F32)
        v = v_ref[pl.ds(r0, c_len), vcols]

        lg = lg_ref[pl.ds(r0, c_len), kcols]
        lg_hi = lg.astype(BF16)
        rem = lg - lg_hi.astype(F32)
        lg_mid = rem.astype(BF16)
        lg_lo = (rem - lg_mid.astype(F32)).astype(BF16)
        b = jnp.dot(dm_ref[...], jnp.concatenate([lg_hi, lg_mid, lg_lo], axis=0),
                    preferred_element_type=F32)
        w_cum = jnp.exp(b)

        st = st_ref[hh]
        o = _dot_nt((q * w_cum).astype(BF16), st.astype(BF16))

        a = jnp.where(diag, _dot_nt(q_bf, k_bf), 0.0)
        for lev in range(n_lev):
            s = c_len >> (lev + 1)
            if s == 1:
                w = jnp.where(upper[lev], jnp.exp(lg), 1.0)
            else:
                w = jnp.exp(-jnp.abs(b - boundary_rows(b, s)))
            z = (jnp.where(upper[lev], q, k) * w).astype(BF16)
            a = jnp.where(pair[lev], _dot_nt(z, z), a)
        o = o + jnp.dot(a.astype(BF16), v, preferred_element_type=F32)

        b_last = b[c_len - 1:c_len, :]
        upd = lax.dot_general(v, (k * jnp.exp(b_last - b)).astype(BF16), (((0,), (0,)), ((), ())),
                              preferred_element_type=F32)
        st_ref[hh] = st * w_cum[c_len - 1:c_len, :] + upd

        on = _rms(o, gh_ref[...])
        o_ref[pl.ds(r0, c_len), vcols] = (on * gr_ref[pl.ds(r0, c_len), vcols].astype(F32)).astype(BF16)

    def chunk(c, carry):
        r0 = pl.multiple_of(c * c_len, c_len)
        for hh in range(GLA_HEADS_PER_STEP):
            head_chunk(r0, hh)
        return carry

    lax.fori_loop(0, seq // c_len, chunk, 0, unroll=2)


def _gla(gq, gk, gv, lg, gr, g_head, bsz, seq):
    t = gq.shape[0]
    hs = GLA_HEADS_PER_STEP
    dmat = jnp.asarray(_gla_cumsum_operator(GLA_CHUNK), dtype=BF16)
    return pl.pallas_call(
        _gla_kernel,
        grid=(bsz, GLA_HEADS // hs),
        in_specs=[
            pl.BlockSpec((seq, hs * GLA_DK), lambda b, h: (b, h)),
            pl.BlockSpec((seq, hs * GLA_DK), lambda b, h: (b, h)),
            pl.BlockSpec((seq, hs * GLA_DV), lambda b, h: (b, h)),
            pl.BlockSpec((seq, hs * GLA_DK), lambda b, h: (b, h)),
            pl.BlockSpec((seq, hs * GLA_DV), lambda b, h: (b, h)),
            pl.BlockSpec((1, GLA_DV), lambda b, h: (0, 0)),
            pl.BlockSpec(dmat.shape, lambda b, h: (0, 0)),
        ],
        out_specs=pl.BlockSpec((seq, hs * GLA_DV), lambda b, h: (b, h)),
        out_shape=jax.ShapeDtypeStruct((t, GLA_HEADS * GLA_DV), BF16),
        scratch_shapes=[pltpu.VMEM((hs, GLA_DV, GLA_DK), F32)],
        compiler_params=_cparams(("parallel", "parallel")),
        name="gla",
    )(gq, gk, gv, lg, gr, g_head, dmat)


def _swa_kernel(sink_ref, q_ref, k_ref, v_ref, o_ref):
    w = SWA_WINDOW
    seq = q_ref.shape[0]
    hk = pl.program_id(1)
    lane_q = lax.broadcasted_iota(I32, (w, LANES), 1)
    low_q = lane_q < SWA_HEAD_DIM
    lane_b = lax.broadcasted_iota(I32, (2 * w, LANES), 1)
    low_b = lane_b < SWA_HEAD_DIM
    qi = lax.broadcasted_iota(I32, (w, 2 * w), 0)
    kj = lax.broadcasted_iota(I32, (w, 2 * w), 1)
    in_window = (kj > qi) & (kj <= qi + w)
    zero_q = jnp.zeros((w, LANES), BF16)
    zero_b = jnp.zeros((2 * w, LANES), BF16)

    def block(n, carry):
        r0 = pl.multiple_of(n * w, w)
        p0 = pl.multiple_of(jnp.maximum(n - 1, 0) * w, w)
        kb = jnp.concatenate([k_ref[pl.ds(p0, w), :], k_ref[pl.ds(r0, w), :]], axis=0)
        vb = jnp.concatenate([v_ref[pl.ds(p0, w), :], v_ref[pl.ds(r0, w), :]], axis=0)
        valid = in_window & ((kj >= w) | (n > 0))
        v_lo = jnp.where(low_b, vb, zero_b)
        v_hi = jnp.where(low_b, zero_b, vb)
        for m in range(SWA_GROUP // 2):
            qp = q_ref[pl.ds(r0, w), LANES * m:LANES * (m + 1)]
            acc = jnp.zeros((w, LANES), F32)
            for par in range(2):
                qm = jnp.where(low_q, qp, zero_q) if par == 0 else jnp.where(low_q, zero_q, qp)
                s = jnp.where(valid, _dot_nt(qm, kb), NEG)
                sink = sink_ref[hk * SWA_GROUP + 2 * m + par]
                mx = jnp.maximum(jnp.max(s, axis=-1, keepdims=True), sink)
                p = jnp.exp(s - mx)
                den = jnp.sum(p, axis=-1, keepdims=True) + jnp.exp(sink - mx)
                pv = jnp.dot(p.astype(BF16), v_lo if par == 0 else v_hi, preferred_element_type=F32)
                acc = acc + pv / den
            o_ref[pl.ds(r0, w), LANES * m:LANES * (m + 1)] = acc.astype(BF16)
        return carry

    lax.fori_loop(0, seq // w, block, 0, unroll=2)


def _swa(sinks, sq, sk, sv, bsz, seq):
    t = sq.shape[0]
    gw = SWA_GROUP * SWA_HEAD_DIM
    return pl.pallas_call(
        _swa_kernel,
        grid_spec=pltpu.PrefetchScalarGridSpec(
            num_scalar_prefetch=1,
            grid=(bsz, SWA_KV_HEADS),
            in_specs=[
                pl.BlockSpec((seq, gw), lambda b, h, s: (b, h)),
                pl.BlockSpec((seq, LANES), lambda b, h, s: (b, h)),
                pl.BlockSpec((seq, LANES), lambda b, h, s: (b, h)),
            ],
            out_specs=pl.BlockSpec((seq, gw), lambda b, h, s: (b, h)),
        ),
        out_shape=jax.ShapeDtypeStruct((t, SWA_Q_HEADS * SWA_HEAD_DIM), BF16),
        compiler_params=_cparams(("parallel", "parallel")),
        name="swa",
    )(sinks, sq, sk, sv)


def _merge_kernel(x_ref, oa_ref, ob_ref, ga_ref, gb_ref, wa_ref, wb_ref, wo_ref, gf_ref, wr_ref, br_ref,
                  x1_ref, hn_ref, lt_ref, mixed_ref, ya_ref):
    oa = oa_ref[...]
    ob = ob_ref[...]
    for n in range(_D // MERGE_TN):
        cols = slice(MERGE_TN * n, MERGE_TN * (n + 1))
        ya_ref[...] = ga_ref[:, cols].astype(F32) * jnp.dot(oa, wa_ref[:, cols], preferred_element_type=F32)
        yb = gb_ref[:, cols].astype(F32) * jnp.dot(ob, wb_ref[:, cols], preferred_element_type=F32)
        mixed_ref[:, cols] = (ya_ref[...] + yb).astype(BF16)
    mixed = mixed_ref[...]
    for n in range(_D // MERGE_TN):
        cols = slice(MERGE_TN * n, MERGE_TN * (n + 1))
        x1_ref[:, cols] = x_ref[:, cols] + jnp.dot(mixed, wo_ref[:, cols], preferred_element_type=F32)
    x1 = x1_ref[...]
    hn = _rms(x1, gf_ref[...])
    hn_ref[...] = hn
    lt_ref[...] = lax.dot_general(wr_ref[...], hn, (((1,), (1,)), ((), ())),
                                  precision=lax.Precision.HIGHEST,
                                  preferred_element_type=F32) + br_ref[...]


def _merge(x2, oa, ob, ga, gb, wa, wb, wo, g_ffn, wr_t, br_col):
    t = x2.shape[0]
    tm = MERGE_TM
    row = lambda i: (i, 0)
    const = lambda i: (0, 0)
    return pl.pallas_call(
        _merge_kernel,
        grid=(t // tm,),
        in_specs=[pl.BlockSpec((tm, _D), row)] * 5 + [pl.BlockSpec((_D, _D), const)] * 3 + [
            pl.BlockSpec((1, _D), const),
            pl.BlockSpec((N_EXPERTS, _D), const),
            pl.BlockSpec((N_EXPERTS, 1), const),
        ],
        out_specs=[pl.BlockSpec((tm, _D), row), pl.BlockSpec((tm, _D), row),
                   pl.BlockSpec((N_EXPERTS, tm), lambda i: (0, i))],
        out_shape=[jax.ShapeDtypeStruct((t, _D), F32), jax.ShapeDtypeStruct((t, _D), F32),
                   jax.ShapeDtypeStruct((N_EXPERTS, t), F32)],
        scratch_shapes=[pltpu.VMEM((tm, _D), BF16), pltpu.VMEM((tm, MERGE_TN), F32)],
        compiler_params=_cparams(("parallel",)),
        name="merge",
    )(x2, oa, ob, ga, gb, wa, wb, wo, g_ffn, wr_t, br_col)


def _route_kernel(lt_ref, e_ref, g_ref, r_ref, cnt_ref, carry_ref):
    tt = lt_ref.shape[1]

    @pl.when(pl.program_id(0) == 0)
    def _():
        carry_ref[...] = jnp.zeros_like(carry_ref)

    eid = lax.broadcasted_iota(I32, (N_EXPERTS, tt), 0)
    work = lt_ref[...]
    vals, idxs = [], []
    chosen = jnp.zeros((N_EXPERTS, tt), F32)
    for _ in range(TOP_K):
        m = jnp.max(work, axis=0, keepdims=True)
        idx = jnp.min(jnp.where(work == m, eid, N_EXPERTS), axis=0, keepdims=True)
        hit = eid == idx
        work = jnp.where(hit, -jnp.inf, work)
        chosen = jnp.where(hit, 1.0, chosen)
        vals.append(m)
        idxs.append(idx)
    ex = [jnp.exp(v - vals[0]) for v in vals]
    den = ex[0] + ex[1] + ex[2] + ex[3]

    t_r = lax.broadcasted_iota(I32, (tt, tt), 0)
    t_c = lax.broadcasted_iota(I32, (tt, tt), 1)
    before = (t_r < t_c).astype(BF16)
    pref = jnp.dot(chosen.astype(BF16), before, preferred_element_type=F32) + carry_ref[:, 0:1]
    for k in range(TOP_K):
        e_ref[k:k + 1, :] = idxs[k]
        g_ref[k:k + 1, :] = ex[k] / den
        r_ref[k:k + 1, :] = jnp.sum(jnp.where(eid == idxs[k], pref, 0.0), axis=0, keepdims=True).astype(I32)
    total = pref[:, tt - 1:tt] + chosen[:, tt - 1:tt]
    carry_ref[...] = jnp.broadcast_to(total, carry_ref.shape)
    cnt_ref[...] = jnp.broadcast_to(total, cnt_ref.shape)


def _route(logits_t):
    t = logits_t.shape[1]
    tt = ROUTE_TT
    blk = lambda i: (0, i)
    return pl.pallas_call(
        _route_kernel,
        grid=(t // tt,),
        in_specs=[pl.BlockSpec((N_EXPERTS, tt), blk)],
        out_specs=[pl.BlockSpec((TOP_K, tt), blk), pl.BlockSpec((TOP_K, tt), blk), pl.BlockSpec((TOP_K, tt), blk),
                   pl.BlockSpec((N_EXPERTS, LANES), lambda i: (0, 0))],
        out_shape=[jax.ShapeDtypeStruct((TOP_K, t), I32), jax.ShapeDtypeStruct((TOP_K, t), F32),
                   jax.ShapeDtypeStruct((TOP_K, t), I32), jax.ShapeDtypeStruct((N_EXPERTS, LANES), F32)],
        scratch_shapes=[pltpu.VMEM((N_EXPERTS, LANES), F32)],
        compiler_params=_cparams(("arbitrary",)),
        name="route",
    )(logits_t)


def _dispatch_kernel(pend_ref, nblk_ref, nv_ref, dest_ref, hn_ref, w1_ref, w2_ref, perm_ref,
                     xs_hbm, w1o_ref, w2o_ref, zero_ref, sem, zsem):
    n_tok = hn_ref.shape[0]
    n_rows = w1_ref.shape[1]
    gw = perm_ref.shape[0]
    n_col_groups = w1_ref.shape[2] // gw
    slices = DISPATCH_GROUPS * n_col_groups
    tok_s = n_tok // slices
    rows_g = n_rows // DISPATCH_GROUPS
    w2_cols = w2_ref.shape[2] // n_col_groups
    n_blocks = xs_hbm.shape[0] // MOE_BLOCK

    def zero_block(row0):
        return pltpu.make_async_copy(zero_ref, xs_hbm.at[pl.ds(pl.multiple_of(row0, MOE_BLOCK), MOE_BLOCK)], zsem)

    @pl.when(pl.program_id(0) == 0)
    def _():
        zero_ref[...] = jnp.zeros_like(zero_ref)

        def expert_tail(e, carry, start):
            @pl.when(nblk_ref[e] > 0)
            def _():
                cp = zero_block(pend_ref[e] - MOE_BLOCK)
                cp.start() if start else cp.wait()
            return carry

        def unused_block(b, carry, start):
            cp = zero_block(b * MOE_BLOCK)
            cp.start() if start else cp.wait()
            return carry

        for start in (True, False):
            lax.fori_loop(0, N_EXPERTS, lambda e, c: expert_tail(e, c, start), 0)
            lax.fori_loop(nv_ref[0], n_blocks, lambda b, c: unused_block(b, c, start), 0)

    def group(g, carry):
        r0 = pl.multiple_of(g * rows_g, rows_g)
        for gi in range(n_col_groups):
            t0 = (g * n_col_groups + gi) * tok_s
            for i in range(tok_s):
                for k in range(TOP_K):
                    d = dest_ref[0, 0, (t0 + i) * TOP_K + k]
                    pltpu.make_async_copy(hn_ref.at[pl.ds(t0 + i, 1)], xs_hbm.at[pl.ds(d, 1)], sem).start()
            wb = w1_ref[0, pl.ds(r0, rows_g), gw * gi:gw * (gi + 1)].astype(BF16)
            w1o_ref[0, pl.ds(r0, rows_g), gw * gi:gw * (gi + 1)] = jnp.dot(
                wb, perm_ref[...], preferred_element_type=F32).astype(BF16)
            w2o_ref[0, pl.ds(r0, rows_g), w2_cols * gi:w2_cols * (gi + 1)] = w2_ref[
                0, pl.ds(r0, rows_g), w2_cols * gi:w2_cols * (gi + 1)].astype(BF16)
        return carry

    lax.fori_loop(0, DISPATCH_GROUPS, group, 0)
    for k in range(TOP_K):
        pltpu.make_async_copy(hn_ref, xs_hbm.at[pl.ds(0, n_tok)], sem).wait()


def _dispatch(pend_rows, blocks_e, n_valid, dest_flat, hn, w1, w2, perm, rows):
    t = hn.shape[0]
    n_e, d, n1 = w1.shape
    dff = w2.shape[1]
    tm = t // n_e
    assert dff == d and tm % (DISPATCH_GROUPS * (n1 // perm.shape[0])) == 0
    emap = lambda e, *_: (e, 0, 0)
    return pl.pallas_call(
        _dispatch_kernel,
        grid_spec=pltpu.PrefetchScalarGridSpec(
            num_scalar_prefetch=3,
            grid=(n_e,),
            in_specs=[pl.BlockSpec((1, 1, tm * TOP_K), emap, memory_space=pltpu.SMEM),
                      pl.BlockSpec((tm, _D), lambda e, *_: (e, 0)),
                      pl.BlockSpec((1, d, n1), emap),
                      pl.BlockSpec((1, dff, _D), emap),
                      pl.BlockSpec(perm.shape, lambda e, *_: (0, 0))],
            out_specs=[pl.BlockSpec(memory_space=pl.ANY),
                       pl.BlockSpec((1, d, n1), emap),
                       pl.BlockSpec((1, dff, _D), emap)],
            scratch_shapes=[pltpu.VMEM((MOE_BLOCK, _D), F32), pltpu.SemaphoreType.DMA, pltpu.SemaphoreType.DMA],
        ),
        out_shape=[jax.ShapeDtypeStruct((rows, _D), F32),
                   jax.ShapeDtypeStruct((n_e, d, n1), BF16),
                   jax.ShapeDtypeStruct((n_e, dff, _D), BF16)],
        compiler_params=_cparams(("arbitrary",)),
        name="dispatch",
    )(pend_rows, blocks_e, n_valid, dest_flat.reshape(n_e, 1, tm * TOP_K), hn, w1, w2, perm)


def _experts_kernel(be_ref, nv_ref, xs_ref, w1_ref, b1_ref, w2_ref, b2_ref, ys_ref):
    del be_ref

    @pl.when(pl.program_id(0) < nv_ref[0])
    def _():
        x = xs_ref[...].astype(BF16)
        u = jnp.dot(x, w1_ref[0], preferred_element_type=F32) + b1_ref[0]
        acts = []
        for gi in range(u.shape[1] // (2 * LANES)):
            g = jnp.minimum(u[:, 2 * LANES * gi:2 * LANES * gi + LANES], SWIGLU_LIMIT)
            lin = jnp.clip(u[:, 2 * LANES * gi + LANES:2 * LANES * (gi + 1)], -SWIGLU_LIMIT, SWIGLU_LIMIT)
            acts.append((g * _sigmoid(SWIGLU_ALPHA * g) * (lin + 1.0)).astype(BF16))
        act = jnp.concatenate(acts, axis=1)
        ys_ref[...] = jnp.dot(act, w2_ref[0], preferred_element_type=F32) + b2_ref[0]

    @pl.when(pl.program_id(0) >= nv_ref[0])
    def _():
        ys_ref[...] = jnp.zeros_like(ys_ref)


def _experts(block_e, n_valid, xs, w1, b1, w2, b2):
    rows = xs.shape[0]
    n_blocks = rows // MOE_BLOCK
    dff = w2.shape[1]
    wmap = lambda i, be, nv: (be[i], 0, 0)
    rmap = lambda i, be, nv: (i, 0)
    return pl.pallas_call(
        _experts_kernel,
        grid_spec=pltpu.PrefetchScalarGridSpec(
            num_scalar_prefetch=2,
            grid=(n_blocks,),
            in_specs=[
                pl.BlockSpec((MOE_BLOCK, _D), rmap),
                pl.BlockSpec((1, _D, 2 * dff), wmap),
                pl.BlockSpec((1, 1, 2 * dff), wmap),
                pl.BlockSpec((1, dff, _D), wmap),
                pl.BlockSpec((1, 1, _D), wmap),
            ],
            out_specs=pl.BlockSpec((MOE_BLOCK, _D), rmap),
        ),
        out_shape=jax.ShapeDtypeStruct((rows, _D), F32),
        compiler_params=_cparams(("arbitrary",)),
        name="experts",
    )(block_e, n_valid, xs, w1, b1, w2, b2)


def _final_kernel(dest_ref, x1_ref, gt_ref, gf_ref, ys_hbm, o_ref, buf_ref, sem):
    n_tok = x1_ref.shape[0]

    def issue(i, carry):
        for k in range(TOP_K):
            d = dest_ref[0, 0, i * TOP_K + k]
            pltpu.make_async_copy(ys_hbm.at[pl.ds(d, 1)], buf_ref.at[k, pl.ds(i, 1)], sem).start()
        return carry

    lax.fori_loop(0, n_tok, issue, 0, unroll=4)
    for k in range(TOP_K):
        pltpu.make_async_copy(ys_hbm.at[pl.ds(0, n_tok)], buf_ref.at[k], sem).wait()

    gates = gt_ref[...]
    y = x1_ref[...]
    for k in range(TOP_K):
        y = y + gates[:, k:k + 1] * buf_ref[k]
    o_ref[...] = _rms(y, gf_ref[...])


def _final(dest_flat, x1, gates_tk, g_final, ys):
    t = x1.shape[0]
    tm = FINAL_TM
    steps = t // tm
    row = lambda i: (i, 0)
    return pl.pallas_call(
        _final_kernel,
        grid=(steps,),
        in_specs=[pl.BlockSpec((1, 1, tm * TOP_K), lambda i: (i, 0, 0), memory_space=pltpu.SMEM),
                  pl.BlockSpec((tm, _D), row), pl.BlockSpec((tm, TOP_K), row),
                  pl.BlockSpec((1, _D), lambda i: (0, 0)), pl.BlockSpec(memory_space=pl.ANY)],
        out_specs=pl.BlockSpec((tm, _D), row),
        out_shape=jax.ShapeDtypeStruct((t, _D), F32),
        scratch_shapes=[pltpu.VMEM((TOP_K, tm, _D), F32), pltpu.SemaphoreType.DMA],
        compiler_params=_cparams(("arbitrary",)),
        name="final",
    )(dest_flat.reshape(steps, 1, tm * TOP_K), x1, gates_tk, g_final, ys)


def _prep_in_proj(w_in, b_in, w_gate, b_gate):
    sizes = (GLA_HEADS * GLA_DK, GLA_HEADS * GLA_DK, GLA_HEADS * GLA_DV, GLA_HEADS * GLA_DV, GLA_GATE_RANK,
             SWA_Q_HEADS * SWA_HEAD_DIM, SWA_KV_HEADS * SWA_HEAD_DIM, SWA_KV_HEADS * SWA_HEAD_DIM, _D, _D)
    offs = [0]
    for s in sizes:
        offs.append(offs[-1] + s)

    def rearrange(m, dtype):
        p = [m[..., offs[i]:offs[i + 1]].astype(dtype) for i in range(len(sizes))]
        gq, gk, gv, gr, lr, sq, sk, sv, ga, gb = p

        def dup_heads(a):
            hs = [a[..., SWA_HEAD_DIM * h:SWA_HEAD_DIM * (h + 1)] for h in range(SWA_KV_HEADS)]
            return jnp.concatenate([hh for h in hs for hh in (h, h)], axis=-1)

        lr_pad = jnp.pad(lr, [(0, 0)] * (lr.ndim - 1) + [(0, LANES - GLA_GATE_RANK)])
        return jnp.concatenate([gq, gk, gv, gr, sq, ga, gb, dup_heads(sk), dup_heads(sv), lr_pad], axis=-1)

    w_all = rearrange(w_in, BF16)
    b_all = rearrange(b_in[None, :], F32)
    wg = jnp.pad(w_gate, ((0, LANES - GLA_GATE_RANK), (0, 0)))
    return w_all, b_all, wg, b_gate[None, :]


def _rope_tables(seq):
    half = SWA_HEAD_DIM // 2
    inv_freq = ROPE_THETA ** (-jnp.arange(half, dtype=F32) / half)
    ang = jnp.arange(seq, dtype=F32)[:, None] * inv_freq[None, :]
    cos, sin = jnp.cos(ang), jnp.sin(ang)
    cos_t = jnp.concatenate([cos, cos] * (LANES // SWA_HEAD_DIM), axis=1)
    sin_t = jnp.concatenate([-sin, sin] * (LANES // SWA_HEAD_DIM), axis=1)
    return cos_t, sin_t


def _pair_split_perm():
    src = jnp.arange(2 * LANES, dtype=I32)
    dst = jnp.where(src % 2 == 0, src // 2, LANES + src // 2)
    return (dst[:, None] == jnp.arange(2 * LANES, dtype=I32)[None, :]).astype(BF16)


def kernel(x, g_mix, w_in, b_in, w_gla_gate, b_gla_gate, g_gla_head, w_gla_out, sinks, w_swa_out, w_out,
           g_ffn, w_router, b_router, w_e1, b_e1, w_e2, b_e2, g_final):
    bsz, seq, d = x.shape
    assert d == _D and w_in.shape[0] == 1, "single-layer, d_model=1024 only"
    assert seq % max(PROJ_TM, SWA_WINDOW, GLA_CHUNK) == 0
    t = bsz * seq
    assert t % max(MERGE_TM, ROUTE_TT, FINAL_TM) == 0 and t % N_EXPERTS == 0
    x2 = x.reshape(t, d)

    w_all, b_all, wg, bg = _prep_in_proj(w_in[0], b_in[0], w_gla_gate[0], b_gla_gate[0])
    cos_t, sin_t = _rope_tables(seq)
    gq, gk, gv, gr, sq, ga, gb, sk, sv, lg = _in_proj(x2, g_mix, w_all, b_all, cos_t, sin_t, wg, bg, seq)
    oa = _gla(gq, gk, gv, lg, gr, g_gla_head, bsz, seq)
    ob = _swa(sinks[0], sq, sk, sv, bsz, seq)
    x1, hn, logits_t = _merge(x2, oa, ob, ga, gb, w_gla_out[0].astype(BF16), w_swa_out[0].astype(BF16),
                              w_out[0].astype(BF16), g_ffn, w_router[0].T, b_router[0][:, None])

    e_kt, g_kt, r_kt, cnt = _route(logits_t)
    counts = cnt[:, 0].astype(I32)
    blocks_e = (counts + MOE_BLOCK - 1) // MOE_BLOCK
    bend = jnp.cumsum(blocks_e)
    pstart = (bend - blocks_e) * MOE_BLOCK
    dest_kt = r_kt
    for e in range(N_EXPERTS):
        dest_kt = dest_kt + jnp.where(e_kt == e, pstart[e], 0)
    dest_flat = dest_kt.T.reshape(t * TOP_K)
    n_blocks = (t * TOP_K) // MOE_BLOCK + N_EXPERTS
    block_e = jnp.minimum(jnp.sum(bend[None, :] <= jnp.arange(n_blocks, dtype=I32)[:, None], axis=1),
                          N_EXPERTS - 1).astype(I32)
    n_valid = bend[-1:].astype(I32)

    b1 = b_e1[0].reshape(N_EXPERTS, -1, LANES, 2).transpose(0, 1, 3, 2).reshape(N_EXPERTS, 1, -1)
    xs, w1, w2 = _dispatch(bend * MOE_BLOCK, blocks_e, n_valid, dest_flat, hn, w_e1[0], w_e2[0],
                           _pair_split_perm(), n_blocks * MOE_BLOCK)
    ys = _experts(block_e, n_valid, xs, w1, b1, w2, b_e2[0][:, None, :])
    out = _final(dest_flat, x1, g_kt.T, g_final[None, :], ys)
    return out.reshape(bsz, seq, d)
```

```python
import jax
import jax.numpy as jnp
import numpy as np
from jax import lax
from jax.experimental import pallas as pl
from jax.experimental.pallas import tpu as pltpu

F32 = jnp.float32
BF16 = jnp.bfloat16
I32 = jnp.int32

NORM_EPS = 1e-5
GLA_HEADS = 4
GLA_DK = 128
GLA_DV = 256
GLA_GATE_RANK = 16
GLA_TAU = 16.0
SWA_Q_HEADS = 16
SWA_KV_HEADS = 2
SWA_GROUP = SWA_Q_HEADS // SWA_KV_HEADS
SWA_HEAD_DIM = 64
SWA_WINDOW = 128
ROPE_THETA = 10000.0
N_EXPERTS = 32
TOP_K = 4
SWIGLU_LIMIT = 7.0
SWIGLU_ALPHA = 1.702

LANES = 128
NEG = -1e30
VMEM_LIMIT = 56 * 1024 * 1024

PROJ_TM = 256
GLA_CHUNK = 128
GLA_HEADS_PER_STEP = 4
MERGE_TM = 512
MERGE_TN = 256
ROUTE_TT = 512
MOE_BLOCK = 512
DISPATCH_GROUPS = 16
FINAL_TM = 512
FINAL_SLICE = 32

_D = 1024
_C_GQ = 0
_C_GK = _C_GQ + GLA_HEADS * GLA_DK
_C_GV = _C_GK + GLA_HEADS * GLA_DK
_C_GR = _C_GV + GLA_HEADS * GLA_DV
_C_SQ = _C_GR + GLA_HEADS * GLA_DV
_C_GA = _C_SQ + SWA_Q_HEADS * SWA_HEAD_DIM
_C_GB = _C_GA + _D
_C_SK = _C_GB + _D
_C_SV = _C_SK + SWA_KV_HEADS * LANES
_C_LR = _C_SV + SWA_KV_HEADS * LANES
_C_END = _C_LR + LANES


def _cparams(sem):
    return pltpu.CompilerParams(dimension_semantics=sem, vmem_limit_bytes=VMEM_LIMIT)


def _rms(x, g):
    return x * lax.rsqrt(jnp.mean(x * x, axis=-1, keepdims=True) + NORM_EPS) * g


def _sigmoid(x):
    return 1.0 / (1.0 + jnp.exp(-x))


def _dot_nt(a, b):
    return lax.dot_general(a, b, (((1,), (1,)), ((), ())), preferred_element_type=F32)


def _rope_slabs(acc, cos, sin, first_half):
    outs = []
    for i in range(acc.shape[1] // LANES):
        xs = acc[:, LANES * i:LANES * (i + 1)]
        partner = jnp.where(first_half, pltpu.roll(xs, LANES - 32, 1), pltpu.roll(xs, 32, 1))
        outs.append(xs * cos + partner * sin)
    return jnp.concatenate(outs, axis=1)


def _in_proj_kernel(x_ref, g_ref, w_ref, b_ref, cos_ref, sin_ref, wg_ref, bg_ref,
                    gq_ref, gk_ref, gv_ref, gr_ref, sq_ref, ga_ref, gb_ref, sk_ref, sv_ref, lg_ref):
    h = _rms(x_ref[...], g_ref[...]).astype(BF16)

    def proj(lo, hi):
        return jnp.dot(h, w_ref[:, lo:hi], preferred_element_type=F32) + b_ref[:, lo:hi]

    cos = cos_ref[...]
    sin = sin_ref[...]
    lane = lax.broadcasted_iota(I32, cos.shape, 1)
    first_half = (lane % SWA_HEAD_DIM) < (SWA_HEAD_DIM // 2)

    gq_ref[...] = (proj(_C_GQ, _C_GK) * (GLA_DK ** -0.5)).astype(BF16)
    gk_ref[...] = proj(_C_GK, _C_GV).astype(BF16)
    gv_ref[...] = proj(_C_GV, _C_GR).astype(BF16)
    gr = proj(_C_GR, _C_SQ)
    gr_ref[...] = (gr * _sigmoid(gr)).astype(BF16)
    sq = proj(_C_SQ, _C_GA) * (SWA_HEAD_DIM ** -0.5)
    sq_ref[...] = _rope_slabs(sq, cos, sin, first_half).astype(BF16)
    ga_ref[...] = _sigmoid(proj(_C_GA, _C_GB)).astype(BF16)
    gb_ref[...] = _sigmoid(proj(_C_GB, _C_SK)).astype(BF16)
    sk_ref[...] = _rope_slabs(proj(_C_SK, _C_SV), cos, sin, first_half).astype(BF16)
    sv_ref[...] = proj(_C_SV, _C_LR).astype(BF16)
    z = jnp.dot(proj(_C_LR, _C_END), wg_ref[...], precision=lax.Precision.HIGHEST,
                preferred_element_type=F32) + bg_ref[...]
    log_sig = jnp.minimum(z, 0.0) - jnp.log(1.0 + jnp.exp(-jnp.abs(z)))
    lg_ref[...] = log_sig * (1.0 / GLA_TAU)


def _in_proj(x2, g_mix, w_all, b_all, cos_t, sin_t, wg, bg, seq):
    t = x2.shape[0]
    tm = PROJ_TM
    pos_blocks = seq // tm
    const = lambda i: (0, 0)
    row = lambda i: (i, 0)
    widths = [(_C_GK - _C_GQ, BF16), (_C_GV - _C_GK, BF16), (_C_GR - _C_GV, BF16), (_C_SQ - _C_GR, BF16),
              (_C_GA - _C_SQ, BF16), (_D, BF16), (_D, BF16), (_C_SV - _C_SK, BF16), (_C_LR - _C_SV, BF16),
              (GLA_HEADS * GLA_DK, F32)]
    return pl.pallas_call(
        _in_proj_kernel,
        grid=(t // tm,),
        in_specs=[
            pl.BlockSpec((tm, _D), row),
            pl.BlockSpec((1, _D), const),
            pl.BlockSpec((_D, _C_END), const, pipeline_mode=pl.Buffered(1)),
            pl.BlockSpec((1, _C_END), const),
            pl.BlockSpec((tm, LANES), lambda i: (i % pos_blocks, 0)),
            pl.BlockSpec((tm, LANES), lambda i: (i % pos_blocks, 0)),
            pl.BlockSpec((LANES, GLA_HEADS * GLA_DK), const),
            pl.BlockSpec((1, GLA_HEADS * GLA_DK), const),
        ],
        out_specs=[pl.BlockSpec((tm, w), row) for w, _ in widths],
        out_shape=[jax.ShapeDtypeStruct((t, w), dt) for w, dt in widths],
        compiler_params=_cparams(("parallel",)),
        name="in_proj",
    )(x2, g_mix, w_all, b_all, cos_t, sin_t, wg, bg)


def _gla_cumsum_operator(c_len):
    t = np.arange(c_len)[:, None]
    r = np.arange(c_len)[None, :]
    return np.tile((r <= t).astype(np.float32), (1, 3))


def _gla_kernel(q_ref, k_ref, v_ref, lg_ref, gr_ref, gh_ref, dm_ref, o_ref, st_ref):
    c_len = GLA_CHUNK
    n_lev = c_len.bit_length() - 1
    seq = q_ref.shape[0]
    st_ref[...] = jnp.zeros_like(st_ref)

    t_i = lax.broadcasted_iota(I32, (c_len, c_len), 0)
    j_i = lax.broadcasted_iota(I32, (c_len, c_len), 1)
    row = lax.broadcasted_iota(I32, (c_len, 1), 0)
    diag = t_i == j_i
    upper, pair = [], []
    for lev in range(n_lev):
        s = c_len >> (lev + 1)
        upper.append((row & s) != 0)
        pair.append(((t_i // (2 * s)) == (j_i // (2 * s))) & ((t_i & s) != 0) & ((j_i & s) == 0))

    sub8 = lax.broadcasted_iota(I32, (c_len // 8, 8, GLA_DK), 1)

    def boundary_rows(b, s):
        if s >= 4:
            b3 = b.reshape(c_len // (2 * s), 2 * s, GLA_DK)
            return jnp.broadcast_to(b3[:, s - 1:s, :], b3.shape).reshape(c_len, GLA_DK)
        b3 = b.reshape(c_len // 8, 8, GLA_DK)
        lo = jnp.broadcast_to(b3[:, 1:2, :], b3.shape)
        hi = jnp.broadcast_to(b3[:, 5:6, :], b3.shape)
        return jnp.where(sub8 < 4, lo, hi).reshape(c_len, GLA_DK)

    def head_chunk(r0, hh):
        kcols = slice(GLA_DK * hh, GLA_DK * (hh + 1))
        vcols = slice(GLA_DV * hh, GLA_DV * (hh + 1))
        q_bf = q_ref[pl.ds(r0, c_len), kcols]
        k_bf = k_ref[pl.ds(r0, c_len), kcols]
        q = q_bf.astype(F32)
        k = k_bf.astype(F32)
        v = v_ref[pl.ds(r0, c_len), vcols]

        lg = lg_ref[pl.ds(r0, c_len), kcols]
        lg_hi = lg.astype(BF16)
        rem = lg - lg_hi.astype(F32)
        lg_mid = rem.astype(BF16)
        lg_lo = (rem - lg_mid.astype(F32)).astype(BF16)
        b = jnp.dot(dm_ref[...], jnp.concatenate([lg_hi, lg_mid, lg_lo], axis=0),
                    preferred_element_type=F32)
        w_cum = jnp.exp(b)

        st = st_ref[hh]
        o = _dot_nt((q * w_cum).astype(BF16), st.astype(BF16))

        a = jnp.where(diag, _dot_nt(q_bf, k_bf), 0.0)
        for lev in range(n_lev):
            s = c_len >> (lev + 1)
            if s == 1:
                w = jnp.where(upper[lev], jnp.exp(lg), 1.0)
            else:
                w = jnp.exp(-jnp.abs(b - boundary_rows(b, s)))
            z = (jnp.where(upper[lev], q, k) * w).astype(BF16)
            a = jnp.where(pair[lev], _dot_nt(z, z), a)
        o = o + jnp.dot(a.astype(BF16), v, preferred_element_type=F32)

        b_last = b[c_len - 1:c_len, :]
        upd = lax.dot_general(v, (k * jnp.exp(b_last - b)).astype(BF16), (((0,), (0,)), ((), ())),
                              preferred_element_type=F32)
        st_ref[hh] = st * w_cum[c_len - 1:c_len, :] + upd

        on = _rms(o, gh_ref[...])
        o_ref[pl.ds(r0, c_len), vcols] = (on * gr_ref[pl.ds(r0, c_len), vcols].astype(F32)).astype(BF16)

    def chunk(c, carry):
        r0 = pl.multiple_of(c * c_len, c_len)
        for hh in range(GLA_HEADS_PER_STEP):
            head_chunk(r0, hh)
        return carry

    lax.fori_loop(0, seq // c_len, chunk, 0, unroll=2)


def _gla(gq, gk, gv, lg, gr, g_head, bsz, seq):
    t = gq.shape[0]
    hs = GLA_HEADS_PER_STEP
    dmat = jnp.asarray(_gla_cumsum_operator(GLA_CHUNK), dtype=BF16)
    return pl.pallas_call(
        _gla_kernel,
        grid=(bsz, GLA_HEADS // hs),
        in_specs=[
            pl.BlockSpec((seq, hs * GLA_DK), lambda b, h: (b, h)),
            pl.BlockSpec((seq, hs * GLA_DK), lambda b, h: (b, h)),
            pl.BlockSpec((seq, hs * GLA_DV), lambda b, h: (b, h)),
            pl.BlockSpec((seq, hs * GLA_DK), lambda b, h: (b, h)),
            pl.BlockSpec((seq, hs * GLA_DV), lambda b, h: (b, h)),
            pl.BlockSpec((1, GLA_DV), lambda b, h: (0, 0)),
            pl.BlockSpec(dmat.shape, lambda b, h: (0, 0)),
        ],
        out_specs=pl.BlockSpec((seq, hs * GLA_DV), lambda b, h: (b, h)),
        out_shape=jax.ShapeDtypeStruct((t, GLA_HEADS * GLA_DV), BF16),
        scratch_shapes=[pltpu.VMEM((hs, GLA_DV, GLA_DK), F32)],
        compiler_params=_cparams(("parallel", "parallel")),
        name="gla",
    )(gq, gk, gv, lg, gr, g_head, dmat)


def _swa_kernel(sink_ref, q_ref, k_ref, v_ref, o_ref):
    w = SWA_WINDOW
    seq = q_ref.shape[0]
    hk = pl.program_id(1)
    lane_q = lax.broadcasted_iota(I32, (w, LANES), 1)
    low_q = lane_q < SWA_HEAD_DIM
    lane_b = lax.broadcasted_iota(I32, (2 * w, LANES), 1)
    low_b = lane_b < SWA_HEAD_DIM
    qi = lax.broadcasted_iota(I32, (w, 2 * w), 0)
    kj = lax.broadcasted_iota(I32, (w, 2 * w), 1)
    in_window = (kj > qi) & (kj <= qi + w)
    zero_q = jnp.zeros((w, LANES), BF16)
    zero_b = jnp.zeros((2 * w, LANES), BF16)

    def block(n, carry):
        r0 = pl.multiple_of(n * w, w)
        p0 = pl.multiple_of(jnp.maximum(n - 1, 0) * w, w)
        kb = jnp.concatenate([k_ref[pl.ds(p0, w), :], k_ref[pl.ds(r0, w), :]], axis=0)
        vb = jnp.concatenate([v_ref[pl.ds(p0, w), :], v_ref[pl.ds(r0, w), :]], axis=0)
        valid = in_window & ((kj >= w) | (n > 0))
        v_lo = jnp.where(low_b, vb, zero_b)
        v_hi = jnp.where(low_b, zero_b, vb)
        for m in range(SWA_GROUP // 2):
            qp = q_ref[pl.ds(r0, w), LANES * m:LANES * (m + 1)]
            acc = jnp.zeros((w, LANES), F32)
            for par in range(2):
                qm = jnp.where(low_q, qp, zero_q) if par == 0 else jnp.where(low_q, zero_q, qp)
                s = jnp.where(valid, _dot_nt(qm, kb), NEG)
                sink = sink_ref[hk * SWA_GROUP + 2 * m + par]
                mx = jnp.maximum(jnp.max(s, axis=-1, keepdims=True), sink)
                p = jnp.exp(s - mx)
                den = jnp.sum(p, axis=-1, keepdims=True) + jnp.exp(sink - mx)
                pv = jnp.dot(p.astype(BF16), v_lo if par == 0 else v_hi, preferred_element_type=F32)
                acc = acc + pv / den
            o_ref[pl.ds(r0, w), LANES * m:LANES * (m + 1)] = acc.astype(BF16)
        return carry

    lax.fori_loop(0, seq // w, block, 0, unroll=2)


def _swa(sinks, sq, sk, sv, bsz, seq):
    t = sq.shape[0]
    gw = SWA_GROUP * SWA_HEAD_DIM
    return pl.pallas_call(
        _swa_kernel,
        grid_spec=pltpu.PrefetchScalarGridSpec(
            num_scalar_prefetch=1,
            grid=(bsz, SWA_KV_HEADS),
            in_specs=[
                pl.BlockSpec((seq, gw), lambda b, h, s: (b, h)),
                pl.BlockSpec((seq, LANES), lambda b, h, s: (b, h)),
                pl.BlockSpec((seq, LANES), lambda b, h, s: (b, h)),
            ],
            out_specs=pl.BlockSpec((seq, gw), lambda b, h, s: (b, h)),
        ),
        out_shape=jax.ShapeDtypeStruct((t, SWA_Q_HEADS * SWA_HEAD_DIM), BF16),
        compiler_params=_cparams(("parallel", "parallel")),
        name="swa",
    )(sinks, sq, sk, sv)


def _merge_kernel(x_ref, oa_ref, ob_ref, ga_ref, gb_ref, wa_ref, wb_ref, wo_ref, gf_ref, wr_ref, br_ref,
                  x1_ref, hn_ref, lt_ref, mixed_ref, ya_ref):
    oa = oa_ref[...]
    ob = ob_ref[...]
    for n in range(_D // MERGE_TN):
        cols = slice(MERGE_TN * n, MERGE_TN * (n + 1))
        ya_ref[...] = ga_ref[:, cols].astype(F32) * jnp.dot(oa, wa_ref[:, cols], preferred_element_type=F32)
        yb = gb_ref[:, cols].astype(F32) * jnp.dot(ob, wb_ref[:, cols], preferred_element_type=F32)
        mixed_ref[:, cols] = (ya_ref[...] + yb).astype(BF16)
    mixed = mixed_ref[...]
    for n in range(_D // MERGE_TN):
        cols = slice(MERGE_TN * n, MERGE_TN * (n + 1))
        x1_ref[:, cols] = x_ref[:, cols] + jnp.dot(mixed, wo_ref[:, cols], preferred_element_type=F32)
    x1 = x1_ref[...]
    hn = _rms(x1, gf_ref[...])
    hn_ref[...] = hn
    lt_ref[...] = lax.dot_general(wr_ref[...], hn, (((1,), (1,)), ((), ())),
                                  precision=lax.Precision.HIGHEST,
                                  preferred_element_type=F32) + br_ref[...]


def _merge(x2, oa, ob, ga, gb, wa, wb, wo, g_ffn, wr_t, br_col):
    t = x2.shape[0]
    tm = MERGE_TM
    row = lambda i: (i, 0)
    const = lambda i: (0, 0)
    return pl.pallas_call(
        _merge_kernel,
        grid=(t // tm,),
        in_specs=[pl.BlockSpec((tm, _D), row)] * 5 + [pl.BlockSpec((_D, _D), const)] * 3 + [
            pl.BlockSpec((1, _D), const),
            pl.BlockSpec((N_EXPERTS, _D), const),
            pl.BlockSpec((N_EXPERTS, 1), const),
        ],
        out_specs=[pl.BlockSpec((tm, _D), row), pl.BlockSpec((tm, _D), row),
                   pl.BlockSpec((N_EXPERTS, tm), lambda i: (0, i))],
        out_shape=[jax.ShapeDtypeStruct((t, _D), F32), jax.ShapeDtypeStruct((t, _D), F32),
                   jax.ShapeDtypeStruct((N_EXPERTS, t), F32)],
        scratch_shapes=[pltpu.VMEM((tm, _D), BF16), pltpu.VMEM((tm, MERGE_TN), F32)],
        compiler_params=_cparams(("parallel",)),
        name="merge",
    )(x2, oa, ob, ga, gb, wa, wb, wo, g_ffn, wr_t, br_col)


def _route_kernel(lt_ref, e_ref, g_ref, r_ref, cnt_ref, carry_ref):
    tt = lt_ref.shape[1]

    @pl.when(pl.program_id(0) == 0)
    def _():
        carry_ref[...] = jnp.zeros_like(carry_ref)

    eid = lax.broadcasted_iota(I32, (N_EXPERTS, tt), 0)
    work = lt_ref[...]
    vals, idxs = [], []
    chosen = jnp.zeros((N_EXPERTS, tt), F32)
    for _ in range(TOP_K):
        m = jnp.max(work, axis=0, keepdims=True)
        idx = jnp.min(jnp.where(work == m, eid, N_EXPERTS), axis=0, keepdims=True)
        hit = eid == idx
        work = jnp.where(hit, -jnp.inf, work)
        chosen = jnp.where(hit, 1.0, chosen)
        vals.append(m)
        idxs.append(idx)
    ex = [jnp.exp(v - vals[0]) for v in vals]
    den = ex[0] + ex[1] + ex[2] + ex[3]

    t_r = lax.broadcasted_iota(I32, (tt, tt), 0)
    t_c = lax.broadcasted_iota(I32, (tt, tt), 1)
    before = (t_r < t_c).astype(BF16)
    pref = jnp.dot(chosen.astype(BF16), before, preferred_element_type=F32) + carry_ref[:, 0:1]
    for k in range(TOP_K):
        e_ref[k:k + 1, :] = idxs[k]
        g_ref[k:k + 1, :] = ex[k] / den
        r_ref[k:k + 1, :] = jnp.sum(jnp.where(eid == idxs[k], pref, 0.0), axis=0, keepdims=True).astype(I32)
    total = pref[:, tt - 1:tt] + chosen[:, tt - 1:tt]
    carry_ref[...] = jnp.broadcast_to(total, carry_ref.shape)
    cnt_ref[...] = jnp.broadcast_to(total, cnt_ref.shape)


def _route(logits_t):
    t = logits_t.shape[1]
    tt = ROUTE_TT
    blk = lambda i: (0, i)
    return pl.pallas_call(
        _route_kernel,
        grid=(t // tt,),
        in_specs=[pl.BlockSpec((N_EXPERTS, tt), blk)],
        out_specs=[pl.BlockSpec((TOP_K, tt), blk), pl.BlockSpec((TOP_K, tt), blk), pl.BlockSpec((TOP_K, tt), blk),
                   pl.BlockSpec((N_EXPERTS, LANES), lambda i: (0, 0))],
        out_shape=[jax.ShapeDtypeStruct((TOP_K, t), I32), jax.ShapeDtypeStruct((TOP_K, t), F32),
                   jax.ShapeDtypeStruct((TOP_K, t), I32), jax.ShapeDtypeStruct((N_EXPERTS, LANES), F32)],
        scratch_shapes=[pltpu.VMEM((N_EXPERTS, LANES), F32)],
        compiler_params=_cparams(("arbitrary",)),
        name="route",
    )(logits_t)


def _dispatch_kernel(pend_ref, nblk_ref, nv_ref, dest_ref, hn_ref, w1_ref, w2_ref, perm_ref,
                     xs_hbm, w1o_ref, w2o_ref, zero_ref, sem, zsem):
    n_tok = hn_ref.shape[0]
    n_rows = w1_ref.shape[1]
    gw = perm_ref.shape[0]
    n_col_groups = w1_ref.shape[2] // gw
    slices = DISPATCH_GROUPS * n_col_groups
    tok_s = n_tok // slices
    rows_g = n_rows // DISPATCH_GROUPS
    w2_cols = w2_ref.shape[2] // n_col_groups
    n_blocks = xs_hbm.shape[0] // MOE_BLOCK

    def zero_block(row0):
        return pltpu.make_async_copy(zero_ref, xs_hbm.at[pl.ds(pl.multiple_of(row0, MOE_BLOCK), MOE_BLOCK)], zsem)

    @pl.when(pl.program_id(0) == 0)
    def _():
        zero_ref[...] = jnp.zeros_like(zero_ref)

        def expert_tail(e, carry, start):
            @pl.when(nblk_ref[e] > 0)
            def _():
                cp = zero_block(pend_ref[e] - MOE_BLOCK)
                cp.start() if start else cp.wait()
            return carry

        def unused_block(b, carry, start):
            cp = zero_block(b * MOE_BLOCK)
            cp.start() if start else cp.wait()
            return carry

        for start in (True, False):
            lax.fori_loop(0, N_EXPERTS, lambda e, c: expert_tail(e, c, start), 0)
            lax.fori_loop(nv_ref[0], n_blocks, lambda b, c: unused_block(b, c, start), 0)

    def group(g, carry):
        r0 = pl.multiple_of(g * rows_g, rows_g)
        for gi in range(n_col_groups):
            t0 = (g * n_col_groups + gi) * tok_s
            for i in range(tok_s):
                for k in range(TOP_K):
                    d = dest_ref[0, 0, (t0 + i) * TOP_K + k]
                    pltpu.make_async_copy(hn_ref.at[pl.ds(t0 + i, 1)], xs_hbm.at[pl.ds(d, 1)], sem).start()
            wb = w1_ref[0, pl.ds(r0, rows_g), gw * gi:gw * (gi + 1)].astype(BF16)
            w1o_ref[0, pl.ds(r0, rows_g), gw * gi:gw * (gi + 1)] = jnp.dot(
                wb, perm_ref[...], preferred_element_type=F32).astype(BF16)
            w2o_ref[0, pl.ds(r0, rows_g), w2_cols * gi:w2_cols * (gi + 1)] = w2_ref[
                0, pl.ds(r0, rows_g), w2_cols * gi:w2_cols * (gi + 1)].astype(BF16)
        return carry

    lax.fori_loop(0, DISPATCH_GROUPS, group, 0)
    for k in range(TOP_K):
        pltpu.make_async_copy(hn_ref, xs_hbm.at[pl.ds(0, n_tok)], sem).wait()


def _dispatch(pend_rows, blocks_e, n_valid, dest_flat, hn, w1, w2, perm, rows):
    t = hn.shape[0]
    n_e, d, n1 = w1.shape
    dff = w2.shape[1]
    tm = t // n_e
    assert dff == d and tm % (DISPATCH_GROUPS * (n1 // perm.shape[0])) == 0
    emap = lambda e, *_: (e, 0, 0)
    return pl.pallas_call(
        _dispatch_kernel,
        grid_spec=pltpu.PrefetchScalarGridSpec(
            num_scalar_prefetch=3,
            grid=(n_e,),
            in_specs=[pl.BlockSpec((1, 1, tm * TOP_K), emap, memory_space=pltpu.SMEM),
                      pl.BlockSpec((tm, _D), lambda e, *_: (e, 0)),
                      pl.BlockSpec((1, d, n1), emap),
                      pl.BlockSpec((1, dff, _D), emap),
                      pl.BlockSpec(perm.shape, lambda e, *_: (0, 0))],
            out_specs=[pl.BlockSpec(memory_space=pl.ANY),
                       pl.BlockSpec((1, d, n1), emap),
                       pl.BlockSpec((1, dff, _D), emap)],
            scratch_shapes=[pltpu.VMEM((MOE_BLOCK, _D), F32), pltpu.SemaphoreType.DMA, pltpu.SemaphoreType.DMA],
        ),
        out_shape=[jax.ShapeDtypeStruct((rows, _D), F32),
                   jax.ShapeDtypeStruct((n_e, d, n1), BF16),
                   jax.ShapeDtypeStruct((n_e, dff, _D), BF16)],
        compiler_params=_cparams(("arbitrary",)),
        name="dispatch",
    )(pend_rows, blocks_e, n_valid, dest_flat.reshape(n_e, 1, tm * TOP_K), hn, w1, w2, perm)


def _experts_kernel(be_ref, nr_ref, xs_ref, w1_ref, b1_ref, w2_ref, b2_ref, ys_ref):
    del be_ref
    n_rows = nr_ref[pl.program_id(0)]
    half = MOE_BLOCK // 2

    def ffn(rows):
        x = xs_ref[0:rows, :].astype(BF16)
        u = jnp.dot(x, w1_ref[0], preferred_element_type=F32) + b1_ref[0]
        acts = []
        for gi in range(u.shape[1] // (2 * LANES)):
            g = jnp.minimum(u[:, 2 * LANES * gi:2 * LANES * gi + LANES], SWIGLU_LIMIT)
            lin = jnp.clip(u[:, 2 * LANES * gi + LANES:2 * LANES * (gi + 1)], -SWIGLU_LIMIT, SWIGLU_LIMIT)
            acts.append((g * _sigmoid(SWIGLU_ALPHA * g) * (lin + 1.0)).astype(BF16))
        act = jnp.concatenate(acts, axis=1)
        ys_ref[0:rows, :] = jnp.dot(act, w2_ref[0], preferred_element_type=F32) + b2_ref[0]

    @pl.when(n_rows > half)
    def _():
        ffn(MOE_BLOCK)

    @pl.when((n_rows > 0) & (n_rows <= half))
    def _():
        ffn(half)
        ys_ref[half:, :] = jnp.zeros((MOE_BLOCK - half, ys_ref.shape[1]), ys_ref.dtype)

    @pl.when(n_rows == 0)
    def _():
        ys_ref[...] = jnp.zeros_like(ys_ref)


def _experts(block_e, block_rows, xs, w1, b1, w2, b2):
    rows = xs.shape[0]
    n_blocks = rows // MOE_BLOCK
    dff = w2.shape[1]
    wmap = lambda i, be, nv: (be[i], 0, 0)
    rmap = lambda i, be, nv: (i, 0)
    return pl.pallas_call(
        _experts_kernel,
        grid_spec=pltpu.PrefetchScalarGridSpec(
            num_scalar_prefetch=2,
            grid=(n_blocks,),
            in_specs=[
                pl.BlockSpec((MOE_BLOCK, _D), rmap),
                pl.BlockSpec((1, _D, 2 * dff), wmap),
                pl.BlockSpec((1, 1, 2 * dff), wmap),
                pl.BlockSpec((1, dff, _D), wmap),
                pl.BlockSpec((1, 1, _D), wmap),
            ],
            out_specs=pl.BlockSpec((MOE_BLOCK, _D), rmap),
        ),
        out_shape=jax.ShapeDtypeStruct((rows, _D), F32),
        compiler_params=_cparams(("arbitrary",)),
        name="experts",
    )(block_e, block_rows, xs, w1, b1, w2, b2)


def _final_kernel(dcur_ref, dnext_ref, x1_ref, gt_ref, gf_ref, ys_hbm, o_ref, buf0_ref, buf1_ref, sems):
    n_tok = x1_ref.shape[0]
    rs = FINAL_SLICE
    step = pl.program_id(0)
    last = pl.num_programs(0) - 1
    bufs = (buf0_ref, buf1_ref)

    def issue_rows(d_ref, slot, t0):
        for i in range(rs):
            for k in range(TOP_K):
                d = d_ref[0, 0, (t0 + i) * TOP_K + k]
                pltpu.make_async_copy(ys_hbm.at[pl.ds(d, 1)], bufs[slot].at[k, pl.ds(t0 + i, 1)],
                                      sems.at[slot]).start()

    def compute_rows(slot, r0):
        rows = pl.ds(r0, rs)
        gates = gt_ref[rows, :]
        y = x1_ref[rows, :]
        for k in range(TOP_K):
            y = y + gates[:, k:k + 1] * bufs[slot][k, rows, :]
        o_ref[rows, :] = _rms(y, gf_ref[...])

    def prologue(s, carry):
        issue_rows(dcur_ref, 0, pl.multiple_of(s * rs, rs))
        return carry

    @pl.when(step == 0)
    def _():
        lax.fori_loop(0, n_tok // rs, prologue, 0)

    for slot in (0, 1):
        def overlapped(s, carry, slot=slot):
            r0 = pl.multiple_of(s * rs, rs)
            issue_rows(dnext_ref, 1 - slot, r0)
            compute_rows(slot, r0)
            return carry

        def epilogue(s, carry, slot=slot):
            compute_rows(slot, pl.multiple_of(s * rs, rs))
            return carry

        @pl.when(step % 2 == slot)
        def _(slot=slot, overlapped=overlapped, epilogue=epilogue):
            for k in range(TOP_K):
                pltpu.make_async_copy(ys_hbm.at[pl.ds(0, n_tok)], bufs[slot].at[k], sems.at[slot]).wait()

            @pl.when(step < last)
            def _():
                lax.fori_loop(0, n_tok // rs, overlapped, 0)

            @pl.when(step == last)
            def _():
                lax.fori_loop(0, n_tok // rs, epilogue, 0)


def _final(dest_flat, x1, gates_tk, g_final, ys):
    t = x1.shape[0]
    tm = FINAL_TM
    steps = t // tm
    row = lambda i: (i, 0)
    dest3 = dest_flat.reshape(steps, 1, tm * TOP_K)
    return pl.pallas_call(
        _final_kernel,
        grid=(steps,),
        in_specs=[pl.BlockSpec((1, 1, tm * TOP_K), lambda i: (i, 0, 0), memory_space=pltpu.SMEM),
                  pl.BlockSpec((1, 1, tm * TOP_K), lambda i: (jnp.minimum(i + 1, steps - 1), 0, 0),
                               memory_space=pltpu.SMEM),
                  pl.BlockSpec((tm, _D), row), pl.BlockSpec((tm, TOP_K), row),
                  pl.BlockSpec((1, _D), lambda i: (0, 0)), pl.BlockSpec(memory_space=pl.ANY)],
        out_specs=pl.BlockSpec((tm, _D), row),
        out_shape=jax.ShapeDtypeStruct((t, _D), F32),
        scratch_shapes=[pltpu.VMEM((TOP_K, tm, _D), F32), pltpu.VMEM((TOP_K, tm, _D), F32),
                        pltpu.SemaphoreType.DMA((2,))],
        compiler_params=_cparams(("arbitrary",)),
        name="final",
    )(dest3, dest3, x1, gates_tk, g_final, ys)


def _prep_in_proj(w_in, b_in, w_gate, b_gate):
    sizes = (GLA_HEADS * GLA_DK, GLA_HEADS * GLA_DK, GLA_HEADS * GLA_DV, GLA_HEADS * GLA_DV, GLA_GATE_RANK,
             SWA_Q_HEADS * SWA_HEAD_DIM, SWA_KV_HEADS * SWA_HEAD_DIM, SWA_KV_HEADS * SWA_HEAD_DIM, _D, _D)
    offs = [0]
    for s in sizes:
        offs.append(offs[-1] + s)

    def rearrange(m, dtype):
        p = [m[..., offs[i]:offs[i + 1]].astype(dtype) for i in range(len(sizes))]
        gq, gk, gv, gr, lr, sq, sk, sv, ga, gb = p

        def dup_heads(a):
            hs = [a[..., SWA_HEAD_DIM * h:SWA_HEAD_DIM * (h + 1)] for h in range(SWA_KV_HEADS)]
            return jnp.concatenate([hh for h in hs for hh in (h, h)], axis=-1)

        lr_pad = jnp.pad(lr, [(0, 0)] * (lr.ndim - 1) + [(0, LANES - GLA_GATE_RANK)])
        return jnp.concatenate([gq, gk, gv, gr, sq, ga, gb, dup_heads(sk), dup_heads(sv), lr_pad], axis=-1)

    w_all = rearrange(w_in, BF16)
    b_all = rearrange(b_in[None, :], F32)
    wg = jnp.pad(w_gate, ((0, LANES - GLA_GATE_RANK), (0, 0)))
    return w_all, b_all, wg, b_gate[None, :]


def _rope_tables(seq):
    half = SWA_HEAD_DIM // 2
    inv_freq = ROPE_THETA ** (-jnp.arange(half, dtype=F32) / half)
    ang = jnp.arange(seq, dtype=F32)[:, None] * inv_freq[None, :]
    cos, sin = jnp.cos(ang), jnp.sin(ang)
    cos_t = jnp.concatenate([cos, cos] * (LANES // SWA_HEAD_DIM), axis=1)
    sin_t = jnp.concatenate([-sin, sin] * (LANES // SWA_HEAD_DIM), axis=1)
    return cos_t, sin_t


def _pair_split_perm():
    src = jnp.arange(2 * LANES, dtype=I32)
    dst = jnp.where(src % 2 == 0, src // 2, LANES + src // 2)
    return (dst[:, None] == jnp.arange(2 * LANES, dtype=I32)[None, :]).astype(BF16)


def kernel(x, g_mix, w_in, b_in, w_gla_gate, b_gla_gate, g_gla_head, w_gla_out, sinks, w_swa_out, w_out,
           g_ffn, w_router, b_router, w_e1, b_e1, w_e2, b_e2, g_final):
    bsz, seq, d = x.shape
    assert d == _D and w_in.shape[0] == 1, "single-layer, d_model=1024 only"
    assert seq % max(PROJ_TM, SWA_WINDOW, GLA_CHUNK) == 0
    t = bsz * seq
    assert t % max(MERGE_TM, ROUTE_TT, FINAL_TM) == 0 and t % N_EXPERTS == 0
    x2 = x.reshape(t, d)

    w_all, b_all, wg, bg = _prep_in_proj(w_in[0], b_in[0], w_gla_gate[0], b_gla_gate[0])
    cos_t, sin_t = _rope_tables(seq)
    gq, gk, gv, gr, sq, ga, gb, sk, sv, lg = _in_proj(x2, g_mix, w_all, b_all, cos_t, sin_t, wg, bg, seq)
    oa = _gla(gq, gk, gv, lg, gr, g_gla_head, bsz, seq)
    ob = _swa(sinks[0], sq, sk, sv, bsz, seq)
    x1, hn, logits_t = _merge(x2, oa, ob, ga, gb, w_gla_out[0].astype(BF16), w_swa_out[0].astype(BF16),
                              w_out[0].astype(BF16), g_ffn, w_router[0].T, b_router[0][:, None])

    e_kt, g_kt, r_kt, cnt = _route(logits_t)
    counts = cnt[:, 0].astype(I32)
    blocks_e = (counts + MOE_BLOCK - 1) // MOE_BLOCK
    bend = jnp.cumsum(blocks_e)
    pstart = (bend - blocks_e) * MOE_BLOCK
    dest_kt = r_kt
    for e in range(N_EXPERTS):
        dest_kt = dest_kt + jnp.where(e_kt == e, pstart[e], 0)
    dest_flat = dest_kt.T.reshape(t * TOP_K)
    n_blocks = (t * TOP_K) // MOE_BLOCK + N_EXPERTS
    block_e = jnp.minimum(jnp.sum(bend[None, :] <= jnp.arange(n_blocks, dtype=I32)[:, None], axis=1),
                          N_EXPERTS - 1).astype(I32)
    n_valid = bend[-1:].astype(I32)
    blk = jnp.arange(n_blocks, dtype=I32)
    of_expert = block_e[:, None] == jnp.arange(N_EXPERTS, dtype=I32)[None, :]
    left = (jnp.sum(jnp.where(of_expert, counts[None, :] + pstart[None, :], 0), axis=1) - blk * MOE_BLOCK)
    block_rows = jnp.where(blk < n_valid[0], jnp.clip(left, 0, MOE_BLOCK), 0).astype(I32)

    b1 = b_e1[0].reshape(N_EXPERTS, -1, LANES, 2).transpose(0, 1, 3, 2).reshape(N_EXPERTS, 1, -1)
    xs, w1, w2 = _dispatch(bend * MOE_BLOCK, blocks_e, n_valid, dest_flat, hn, w_e1[0], w_e2[0],
                           _pair_split_perm(), n_blocks * MOE_BLOCK)
    ys = _experts(block_e, block_rows, xs, w1, b1, w2, b_e2[0][:, None, :])
    out = _final(dest_flat, x1, g_kt.T, g_final[None, :], ys)
    return out.reshape(bsz, seq, d)
```

```python
import jax
import jax.numpy as jnp
import numpy as np
from jax import lax
from jax.experimental import pallas as pl
from jax.experimental.pallas import tpu as pltpu

F32 = jnp.float32
BF16 = jnp.bfloat16
I32 = jnp.int32

NORM_EPS = 1e-5
GLA_HEADS = 4
GLA_DK = 128
GLA_DV = 256
GLA_GATE_RANK = 16
GLA_TAU = 16.0
SWA_Q_HEADS = 16
SWA_KV_HEADS = 2
SWA_GROUP = SWA_Q_HEADS // SWA_KV_HEADS
SWA_HEAD_DIM = 64
SWA_WINDOW = 128
ROPE_THETA = 10000.0
N_EXPERTS = 32
TOP_K = 4
SWIGLU_LIMIT = 7.0
SWIGLU_ALPHA = 1.702

LANES = 128
NEG = -1e30
VMEM_LIMIT = 56 * 1024 * 1024

PROJ_TM = 256
GLA_CHUNK = 128
GLA_HEADS_PER_STEP = 4
MERGE_TM = 512
MERGE_TN = 256
MERGE_ROW_GROUPS = 1
ROUTE_TT = 512
MOE_BLOCK = 512
DISPATCH_GROUPS = 16
FINAL_TM = 512

_D = 1024
_C_GQ = 0
_C_GK = _C_GQ + GLA_HEADS * GLA_DK
_C_GV = _C_GK + GLA_HEADS * GLA_DK
_C_GR = _C_GV + GLA_HEADS * GLA_DV
_C_SQ = _C_GR + GLA_HEADS * GLA_DV
_C_GA = _C_SQ + SWA_Q_HEADS * SWA_HEAD_DIM
_C_GB = _C_GA + _D
_C_SK = _C_GB + _D
_C_SV = _C_SK + SWA_KV_HEADS * LANES
_C_LR = _C_SV + SWA_KV_HEADS * LANES
_C_END = _C_LR + LANES


def _cparams(sem):
    return pltpu.CompilerParams(dimension_semantics=sem, vmem_limit_bytes=VMEM_LIMIT)


def _rms(x, g):
    return x * lax.rsqrt(jnp.mean(x * x, axis=-1, keepdims=True) + NORM_EPS) * g


def _sigmoid(x):
    return 1.0 / (1.0 + jnp.exp(-x))


def _dot_nt(a, b):
    return lax.dot_general(a, b, (((1,), (1,)), ((), ())), preferred_element_type=F32)


def _rope_slabs(acc, cos, sin, first_half):
    outs = []
    for i in range(acc.shape[1] // LANES):
        xs = acc[:, LANES * i:LANES * (i + 1)]
        partner = jnp.where(first_half, pltpu.roll(xs, LANES - 32, 1), pltpu.roll(xs, 32, 1))
        outs.append(xs * cos + partner * sin)
    return jnp.concatenate(outs, axis=1)


def _in_proj_kernel(x_ref, g_ref, w_ref, b_ref, cos_ref, sin_ref, wg_ref, bg_ref,
                    gq_ref, gk_ref, gv_ref, gr_ref, sq_ref, ga_ref, gb_ref, sk_ref, sv_ref, lg_ref):
    h = _rms(x_ref[...], g_ref[...]).astype(BF16)

    def proj(lo, hi):
        return jnp.dot(h, w_ref[:, lo:hi], preferred_element_type=F32) + b_ref[:, lo:hi]

    cos = cos_ref[...]
    sin = sin_ref[...]
    lane = lax.broadcasted_iota(I32, cos.shape, 1)
    first_half = (lane % SWA_HEAD_DIM) < (SWA_HEAD_DIM // 2)

    gq_ref[...] = (proj(_C_GQ, _C_GK) * (GLA_DK ** -0.5)).astype(BF16)
    gk_ref[...] = proj(_C_GK, _C_GV).astype(BF16)
    gv_ref[...] = proj(_C_GV, _C_GR).astype(BF16)
    gr = proj(_C_GR, _C_SQ)
    gr_ref[...] = (gr * _sigmoid(gr)).astype(BF16)
    sq = proj(_C_SQ, _C_GA) * (SWA_HEAD_DIM ** -0.5)
    sq_ref[...] = _rope_slabs(sq, cos, sin, first_half).astype(BF16)
    ga_ref[...] = _sigmoid(proj(_C_GA, _C_GB)).astype(BF16)
    gb_ref[...] = _sigmoid(proj(_C_GB, _C_SK)).astype(BF16)
    sk_ref[...] = _rope_slabs(proj(_C_SK, _C_SV), cos, sin, first_half).astype(BF16)
    sv_ref[...] = proj(_C_SV, _C_LR).astype(BF16)
    z = jnp.dot(proj(_C_LR, _C_END), wg_ref[...], precision=lax.Precision.HIGHEST,
                preferred_element_type=F32) + bg_ref[...]
    log_sig = jnp.minimum(z, 0.0) - jnp.log(1.0 + jnp.exp(-jnp.abs(z)))
    lg_ref[...] = log_sig * (1.0 / GLA_TAU)


def _in_proj(x2, g_mix, w_all, b_all, cos_t, sin_t, wg, bg, seq):
    t = x2.shape[0]
    tm = PROJ_TM
    pos_blocks = seq // tm
    const = lambda i: (0, 0)
    row = lambda i: (i, 0)
    widths = [(_C_GK - _C_GQ, BF16), (_C_GV - _C_GK, BF16), (_C_GR - _C_GV, BF16), (_C_SQ - _C_GR, BF16),
              (_C_GA - _C_SQ, BF16), (_D, BF16), (_D, BF16), (_C_SV - _C_SK, BF16), (_C_LR - _C_SV, BF16),
              (GLA_HEADS * GLA_DK, F32)]
    return pl.pallas_call(
        _in_proj_kernel,
        grid=(t // tm,),
        in_specs=[
            pl.BlockSpec((tm, _D), row),
            pl.BlockSpec((1, _D), const),
            pl.BlockSpec((_D, _C_END), const, pipeline_mode=pl.Buffered(1)),
            pl.BlockSpec((1, _C_END), const),
            pl.BlockSpec((tm, LANES), lambda i: (i % pos_blocks, 0)),
            pl.BlockSpec((tm, LANES), lambda i: (i % pos_blocks, 0)),
            pl.BlockSpec((LANES, GLA_HEADS * GLA_DK), const),
            pl.BlockSpec((1, GLA_HEADS * GLA_DK), const),
        ],
        out_specs=[pl.BlockSpec((tm, w), row) for w, _ in widths],
        out_shape=[jax.ShapeDtypeStruct((t, w), dt) for w, dt in widths],
        compiler_params=_cparams(("parallel",)),
        name="in_proj",
    )(x2, g_mix, w_all, b_all, cos_t, sin_t, wg, bg)


def _gla_cumsum_operator(c_len):
    t = np.arange(c_len)[:, None]
    r = np.arange(c_len)[None, :]
    return np.tile((r <= t).astype(np.float32), (1, 3))


def _gla_kernel(q_ref, k_ref, v_ref, lg_ref, gr_ref, gh_ref, dm_ref, o_ref, st_ref):
    c_len = GLA_CHUNK
    n_lev = c_len.bit_length() - 1
    seq = q_ref.shape[0]
    st_ref[...] = jnp.zeros_like(st_ref)

    t_i = lax.broadcasted_iota(I32, (c_len, c_len), 0)
    j_i = lax.broadcasted_iota(I32, (c_len, c_len), 1)
    row = lax.broadcasted_iota(I32, (c_len, 1), 0)
    diag = t_i == j_i
    upper, pair = [], []
    for lev in range(n_lev):
        s = c_len >> (lev + 1)
        upper.append((row & s) != 0)
        pair.append(((t_i // (2 * s)) == (j_i // (2 * s))) & ((t_i & s) != 0) & ((j_i & s) == 0))

    sub8 = lax.broadcasted_iota(I32, (c_len // 8, 8, GLA_DK), 1)

    def boundary_rows(b, s):
        if s >= 4:
            b3 = b.reshape(c_len // (2 * s), 2 * s, GLA_DK)
            return jnp.broadcast_to(b3[:, s - 1:s, :], b3.shape).reshape(c_len, GLA_DK)
        b3 = b.reshape(c_len // 8, 8, GLA_DK)
        lo = jnp.broadcast_to(b3[:, 1:2, :], b3.shape)
        hi = jnp.broadcast_to(b3[:, 5:6, :], b3.shape)
        return jnp.where(sub8 < 4, lo, hi).reshape(c_len, GLA_DK)

    def head_chunk(r0, hh):
        kcols = slice(GLA_DK * hh, GLA_DK * (hh + 1))
        vcols = slice(GLA_DV * hh, GLA_DV * (hh + 1))
        q_bf = q_ref[pl.ds(r0, c_len), kcols]
        k_bf = k_ref[pl.ds(r0, c_len), kcols]
        q = q_bf.astype(F32)
        k = k_bf.astype(F32)
        v = v_ref[pl.ds(r0, c_len), vcols]

        lg = lg_ref[pl.ds(r0, c_len), kcols]
        lg_hi = lg.astype(BF16)
        rem = lg - lg_hi.astype(F32)
        lg_mid = rem.astype(BF16)
        lg_lo = (rem - lg_mid.astype(F32)).astype(BF16)
        b = jnp.dot(dm_ref[...], jnp.concatenate([lg_hi, lg_mid, lg_lo], axis=0),
                    preferred_element_type=F32)
        w_cum = jnp.exp(b)

        st = st_ref[hh]
        o = _dot_nt((q * w_cum).astype(BF16), st.astype(BF16))

        a = jnp.where(diag, _dot_nt(q_bf, k_bf), 0.0)
        for lev in range(n_lev):
            s = c_len >> (lev + 1)
            if s == 1:
                w = jnp.where(upper[lev], jnp.exp(lg), 1.0)
            else:
                w = jnp.exp(-jnp.abs(b - boundary_rows(b, s)))
            z = (jnp.where(upper[lev], q, k) * w).astype(BF16)
            a = jnp.where(pair[lev], _dot_nt(z, z), a)
        o = o + jnp.dot(a.astype(BF16), v, preferred_element_type=F32)

        b_last = b[c_len - 1:c_len, :]
        upd = lax.dot_general(v, (k * jnp.exp(b_last - b)).astype(BF16), (((0,), (0,)), ((), ())),
                              preferred_element_type=F32)
        st_ref[hh] = st * w_cum[c_len - 1:c_len, :] + upd

        on = _rms(o, gh_ref[...])
        o_ref[pl.ds(r0, c_len), vcols] = (on * gr_ref[pl.ds(r0, c_len), vcols].astype(F32)).astype(BF16)

    def chunk(c, carry):
        r0 = pl.multiple_of(c * c_len, c_len)
        for hh in range(GLA_HEADS_PER_STEP):
            head_chunk(r0, hh)
        return carry

    lax.fori_loop(0, seq // c_len, chunk, 0, unroll=2)


def _gla(gq, gk, gv, lg, gr, g_head, bsz, seq):
    t = gq.shape[0]
    hs = GLA_HEADS_PER_STEP
    dmat = jnp.asarray(_gla_cumsum_operator(GLA_CHUNK), dtype=BF16)
    return pl.pallas_call(
        _gla_kernel,
        grid=(bsz, GLA_HEADS // hs),
        in_specs=[
            pl.BlockSpec((seq, hs * GLA_DK), lambda b, h: (b, h)),
            pl.BlockSpec((seq, hs * GLA_DK), lambda b, h: (b, h)),
            pl.BlockSpec((seq, hs * GLA_DV), lambda b, h: (b, h)),
            pl.BlockSpec((seq, hs * GLA_DK), lambda b, h: (b, h)),
            pl.BlockSpec((seq, hs * GLA_DV), lambda b, h: (b, h)),
            pl.BlockSpec((1, GLA_DV), lambda b, h: (0, 0)),
            pl.BlockSpec(dmat.shape, lambda b, h: (0, 0)),
        ],
        out_specs=pl.BlockSpec((seq, hs * GLA_DV), lambda b, h: (b, h)),
        out_shape=jax.ShapeDtypeStruct((t, GLA_HEADS * GLA_DV), BF16),
        scratch_shapes=[pltpu.VMEM((hs, GLA_DV, GLA_DK), F32)],
        compiler_params=_cparams(("parallel", "parallel")),
        name="gla",
    )(gq, gk, gv, lg, gr, g_head, dmat)


def _swa_kernel(sink_ref, q_ref, k_ref, v_ref, o_ref):
    w = SWA_WINDOW
    seq = q_ref.shape[0]
    hk = pl.program_id(1)
    lane_q = lax.broadcasted_iota(I32, (w, LANES), 1)
    low_q = lane_q < SWA_HEAD_DIM
    lane_b = lax.broadcasted_iota(I32, (2 * w, LANES), 1)
    low_b = lane_b < SWA_HEAD_DIM
    qi = lax.broadcasted_iota(I32, (w, 2 * w), 0)
    kj = lax.broadcasted_iota(I32, (w, 2 * w), 1)
    in_window = (kj > qi) & (kj <= qi + w)
    zero_q = jnp.zeros((w, LANES), BF16)
    zero_b = jnp.zeros((2 * w, LANES), BF16)

    def block(n, carry):
        r0 = pl.multiple_of(n * w, w)
        p0 = pl.multiple_of(jnp.maximum(n - 1, 0) * w, w)
        kb = jnp.concatenate([k_ref[pl.ds(p0, w), :], k_ref[pl.ds(r0, w), :]], axis=0)
        vb = jnp.concatenate([v_ref[pl.ds(p0, w), :], v_ref[pl.ds(r0, w), :]], axis=0)
        valid = in_window & ((kj >= w) | (n > 0))
        v_lo = jnp.where(low_b, vb, zero_b)
        v_hi = jnp.where(low_b, zero_b, vb)
        for m in range(SWA_GROUP // 2):
            qp = q_ref[pl.ds(r0, w), LANES * m:LANES * (m + 1)]
            acc = jnp.zeros((w, LANES), F32)
            for par in range(2):
                qm = jnp.where(low_q, qp, zero_q) if par == 0 else jnp.where(low_q, zero_q, qp)
                s = jnp.where(valid, _dot_nt(qm, kb), NEG)
                sink = sink_ref[hk * SWA_GROUP + 2 * m + par]
                mx = jnp.maximum(jnp.max(s, axis=-1, keepdims=True), sink)
                p = jnp.exp(s - mx)
                den = jnp.sum(p, axis=-1, keepdims=True) + jnp.exp(sink - mx)
                pv = jnp.dot(p.astype(BF16), v_lo if par == 0 else v_hi, preferred_element_type=F32)
                acc = acc + pv / den
            o_ref[pl.ds(r0, w), LANES * m:LANES * (m + 1)] = acc.astype(BF16)
        return carry

    lax.fori_loop(0, seq // w, block, 0, unroll=2)


def _swa(sinks, sq, sk, sv, bsz, seq):
    t = sq.shape[0]
    gw = SWA_GROUP * SWA_HEAD_DIM
    return pl.pallas_call(
        _swa_kernel,
        grid_spec=pltpu.PrefetchScalarGridSpec(
            num_scalar_prefetch=1,
            grid=(bsz, SWA_KV_HEADS),
            in_specs=[
                pl.BlockSpec((seq, gw), lambda b, h, s: (b, h)),
                pl.BlockSpec((seq, LANES), lambda b, h, s: (b, h)),
                pl.BlockSpec((seq, LANES), lambda b, h, s: (b, h)),
            ],
            out_specs=pl.BlockSpec((seq, gw), lambda b, h, s: (b, h)),
        ),
        out_shape=jax.ShapeDtypeStruct((t, SWA_Q_HEADS * SWA_HEAD_DIM), BF16),
        compiler_params=_cparams(("parallel", "parallel")),
        name="swa",
    )(sinks, sq, sk, sv)


def _merge_kernel(x_ref, oa_ref, ob_ref, ga_ref, gb_ref, wa_ref, wb_ref, wo_ref, gf_ref, wr_ref, br_ref,
                  x1_ref, hn_ref, lt_ref, mixed_ref):
    rows_g = x_ref.shape[0] // MERGE_ROW_GROUPS
    for h in range(MERGE_ROW_GROUPS):
        rows = slice(rows_g * h, rows_g * (h + 1))
        oa = oa_ref[rows, :]
        ob = ob_ref[rows, :]
        for n in range(_D // MERGE_TN):
            cols = slice(MERGE_TN * n, MERGE_TN * (n + 1))
            ya = ga_ref[rows, cols].astype(F32) * jnp.dot(oa, wa_ref[:, cols], preferred_element_type=F32)
            yb = gb_ref[rows, cols].astype(F32) * jnp.dot(ob, wb_ref[:, cols], preferred_element_type=F32)
            mixed_ref[rows, cols] = (ya + yb).astype(BF16)
        mixed = mixed_ref[rows, :]
        for n in range(_D // MERGE_TN):
            cols = slice(MERGE_TN * n, MERGE_TN * (n + 1))
            x1_ref[rows, cols] = x_ref[rows, cols] + jnp.dot(mixed, wo_ref[:, cols], preferred_element_type=F32)
        hn = _rms(x1_ref[rows, :], gf_ref[...])
        hn_ref[rows, :] = hn
        lt_ref[:, rows] = lax.dot_general(wr_ref[...], hn, (((1,), (1,)), ((), ())),
                                          precision=lax.Precision.HIGHEST,
                                          preferred_element_type=F32) + br_ref[...]


def _merge(x2, oa, ob, ga, gb, wa, wb, wo, g_ffn, wr_t, br_col):
    t = x2.shape[0]
    tm = MERGE_TM
    row = lambda i: (i, 0)
    const = lambda i: (0, 0)
    return pl.pallas_call(
        _merge_kernel,
        grid=(t // tm,),
        in_specs=[pl.BlockSpec((tm, _D), row)] * 5 + [pl.BlockSpec((_D, _D), const)] * 3 + [
            pl.BlockSpec((1, _D), const),
            pl.BlockSpec((N_EXPERTS, _D), const),
            pl.BlockSpec((N_EXPERTS, 1), const),
        ],
        out_specs=[pl.BlockSpec((tm, _D), row), pl.BlockSpec((tm, _D), row),
                   pl.BlockSpec((N_EXPERTS, tm), lambda i: (0, i))],
        out_shape=[jax.ShapeDtypeStruct((t, _D), F32), jax.ShapeDtypeStruct((t, _D), F32),
                   jax.ShapeDtypeStruct((N_EXPERTS, t), F32)],
        scratch_shapes=[pltpu.VMEM((tm, _D), BF16)],
        compiler_params=_cparams(("parallel",)),
        name="merge",
    )(x2, oa, ob, ga, gb, wa, wb, wo, g_ffn, wr_t, br_col)


def _route_kernel(lt_ref, e_ref, g_ref, r_ref, cnt_ref, carry_ref):
    tt = lt_ref.shape[1]

    @pl.when(pl.program_id(0) == 0)
    def _():
        carry_ref[...] = jnp.zeros_like(carry_ref)

    eid = lax.broadcasted_iota(I32, (N_EXPERTS, tt), 0)
    work = lt_ref[...]
    vals, idxs = [], []
    chosen = jnp.zeros((N_EXPERTS, tt), F32)
    for _ in range(TOP_K):
        m = jnp.max(work, axis=0, keepdims=True)
        idx = jnp.min(jnp.where(work == m, eid, N_EXPERTS), axis=0, keepdims=True)
        hit = eid == idx
        work = jnp.where(hit, -jnp.inf, work)
        chosen = jnp.where(hit, 1.0, chosen)
        vals.append(m)
        idxs.append(idx)
    ex = [jnp.exp(v - vals[0]) for v in vals]
    den = ex[0] + ex[1] + ex[2] + ex[3]

    t_r = lax.broadcasted_iota(I32, (tt, tt), 0)
    t_c = lax.broadcasted_iota(I32, (tt, tt), 1)
    before = (t_r < t_c).astype(BF16)
    pref = jnp.dot(chosen.astype(BF16), before, preferred_element_type=F32) + carry_ref[:, 0:1]
    for k in range(TOP_K):
        e_ref[k:k + 1, :] = idxs[k]
        g_ref[k:k + 1, :] = ex[k] / den
        r_ref[k:k + 1, :] = jnp.sum(jnp.where(eid == idxs[k], pref, 0.0), axis=0, keepdims=True).astype(I32)
    total = pref[:, tt - 1:tt] + chosen[:, tt - 1:tt]
    carry_ref[...] = jnp.broadcast_to(total, carry_ref.shape)
    cnt_ref[...] = jnp.broadcast_to(total, cnt_ref.shape)


def _route(logits_t):
    t = logits_t.shape[1]
    tt = ROUTE_TT
    blk = lambda i: (0, i)
    return pl.pallas_call(
        _route_kernel,
        grid=(t // tt,),
        in_specs=[pl.BlockSpec((N_EXPERTS, tt), blk)],
        out_specs=[pl.BlockSpec((TOP_K, tt), blk), pl.BlockSpec((TOP_K, tt), blk), pl.BlockSpec((TOP_K, tt), blk),
                   pl.BlockSpec((N_EXPERTS, LANES), lambda i: (0, 0))],
        out_shape=[jax.ShapeDtypeStruct((TOP_K, t), I32), jax.ShapeDtypeStruct((TOP_K, t), F32),
                   jax.ShapeDtypeStruct((TOP_K, t), I32), jax.ShapeDtypeStruct((N_EXPERTS, LANES), F32)],
        scratch_shapes=[pltpu.VMEM((N_EXPERTS, LANES), F32)],
        compiler_params=_cparams(("arbitrary",)),
        name="route",
    )(logits_t)


def _dispatch_kernel(pend_ref, nblk_ref, nv_ref, dest_ref, hn_ref, w1_ref, w2_ref, perm_ref,
                     xs_hbm, w1o_ref, w2o_ref, inv_ref, zero_ref, sem, zsem):
    n_tok = hn_ref.shape[0]
    n_rows = w1_ref.shape[1]
    gw = perm_ref.shape[0]
    n_col_groups = w1_ref.shape[2] // gw
    slices = DISPATCH_GROUPS * n_col_groups
    tok_s = n_tok // slices
    rows_g = n_rows // DISPATCH_GROUPS
    w2_cols = w2_ref.shape[2] // n_col_groups
    n_blocks = xs_hbm.shape[0] // MOE_BLOCK
    n_tokens = n_tok * N_EXPERTS
    n_slots = TOP_K * n_tokens
    tok0 = pl.program_id(0) * n_tok

    def zero_block(row0):
        return pltpu.make_async_copy(zero_ref, xs_hbm.at[pl.ds(pl.multiple_of(row0, MOE_BLOCK), MOE_BLOCK)], zsem)

    @pl.when(pl.program_id(0) == 0)
    def _():
        zero_ref[...] = jnp.zeros_like(zero_ref)

        def spill_slot(b, carry):
            for r in range(MOE_BLOCK):
                inv_ref[b * MOE_BLOCK + r] = n_slots + r
            return carry

        lax.fori_loop(0, n_blocks, spill_slot, 0)

        def expert_tail(e, carry, start):
            @pl.when(nblk_ref[e] > 0)
            def _():
                cp = zero_block(pend_ref[e] - MOE_BLOCK)
                cp.start() if start else cp.wait()
            return carry

        def unused_block(b, carry, start):
            cp = zero_block(b * MOE_BLOCK)
            cp.start() if start else cp.wait()
            return carry

        for start in (True, False):
            lax.fori_loop(0, N_EXPERTS, lambda e, c: expert_tail(e, c, start), 0)
            lax.fori_loop(nv_ref[0], n_blocks, lambda b, c: unused_block(b, c, start), 0)

    def group(g, carry):
        r0 = pl.multiple_of(g * rows_g, rows_g)
        for gi in range(n_col_groups):
            t0 = (g * n_col_groups + gi) * tok_s
            for i in range(tok_s):
                for k in range(TOP_K):
                    d = dest_ref[0, 0, (t0 + i) * TOP_K + k]
                    pltpu.make_async_copy(hn_ref.at[pl.ds(t0 + i, 1)], xs_hbm.at[pl.ds(d, 1)], sem).start()
                    inv_ref[d] = k * n_tokens + tok0 + t0 + i
            wb = w1_ref[0, pl.ds(r0, rows_g), gw * gi:gw * (gi + 1)].astype(BF16)
            w1o_ref[0, pl.ds(r0, rows_g), gw * gi:gw * (gi + 1)] = jnp.dot(
                wb, perm_ref[...], preferred_element_type=F32).astype(BF16)
            w2o_ref[0, pl.ds(r0, rows_g), w2_cols * gi:w2_cols * (gi + 1)] = w2_ref[
                0, pl.ds(r0, rows_g), w2_cols * gi:w2_cols * (gi + 1)].astype(BF16)
        return carry

    lax.fori_loop(0, DISPATCH_GROUPS, group, 0)
    for k in range(TOP_K):
        pltpu.make_async_copy(hn_ref, xs_hbm.at[pl.ds(0, n_tok)], sem).wait()


def _dispatch(pend_rows, blocks_e, n_valid, dest_flat, hn, w1, w2, perm, rows):
    t = hn.shape[0]
    n_e, d, n1 = w1.shape
    dff = w2.shape[1]
    tm = t // n_e
    assert dff == d and n_e == N_EXPERTS and tm % (DISPATCH_GROUPS * (n1 // perm.shape[0])) == 0
    emap = lambda e, *_: (e, 0, 0)
    return pl.pallas_call(
        _dispatch_kernel,
        grid_spec=pltpu.PrefetchScalarGridSpec(
            num_scalar_prefetch=3,
            grid=(n_e,),
            in_specs=[pl.BlockSpec((1, 1, tm * TOP_K), emap, memory_space=pltpu.SMEM),
                      pl.BlockSpec((tm, _D), lambda e, *_: (e, 0)),
                      pl.BlockSpec((1, d, n1), emap),
                      pl.BlockSpec((1, dff, _D), emap),
                      pl.BlockSpec(perm.shape, lambda e, *_: (0, 0))],
            out_specs=[pl.BlockSpec(memory_space=pl.ANY),
                       pl.BlockSpec((1, d, n1), emap),
                       pl.BlockSpec((1, dff, _D), emap),
                       pl.BlockSpec(memory_space=pltpu.SMEM)],
            scratch_shapes=[pltpu.VMEM((MOE_BLOCK, _D), F32), pltpu.SemaphoreType.DMA, pltpu.SemaphoreType.DMA],
        ),
        out_shape=[jax.ShapeDtypeStruct((rows, _D), F32),
                   jax.ShapeDtypeStruct((n_e, d, n1), BF16),
                   jax.ShapeDtypeStruct((n_e, dff, _D), BF16),
                   jax.ShapeDtypeStruct((rows,), I32)],
        compiler_params=_cparams(("arbitrary",)),
        name="dispatch",
    )(pend_rows, blocks_e, n_valid, dest_flat.reshape(n_e, 1, tm * TOP_K), hn, w1, w2, perm)


def _experts_kernel(be_ref, nv_ref, invp_ref, xs_ref, w1_ref, b1_ref, w2_ref, b2_ref,
                    y4_hbm, buf0_ref, buf1_ref, xb_ref, act_ref, sems):
    del be_ref
    i = pl.program_id(0)
    n_valid = nv_ref[0]
    bufs = (buf0_ref, buf1_ref)
    cw = 2 * LANES
    n_c1 = w1_ref.shape[2] // cw
    n_c2 = w2_ref.shape[2] // cw
    rows_c = MOE_BLOCK // n_c1

    def row_copy(src_ref, r, sem):
        return pltpu.make_async_copy(src_ref.at[pl.ds(r, 1)], y4_hbm.at[pl.ds(invp_ref[0, 0, r], 1)], sem)

    def drain(src_ref, sem):
        pltpu.make_async_copy(src_ref, y4_hbm.at[pl.ds(0, MOE_BLOCK)], sem).wait()

    for par in (0, 1):
        cur, prev = bufs[par], bufs[1 - par]

        @pl.when((i < n_valid) & (i % 2 == par))
        def _(par=par, cur=cur, prev=prev):
            @pl.when(i == 0)
            def _():
                prev[...] = jnp.zeros_like(prev)

            @pl.when(i > 0)
            def _():
                drain(cur, sems.at[par])

            xb_ref[...] = xs_ref[...].astype(BF16)
            for c in range(n_c1):
                u = jnp.dot(xb_ref[...], w1_ref[0, :, cw * c:cw * (c + 1)],
                            preferred_element_type=F32) + b1_ref[0, :, cw * c:cw * (c + 1)]
                g = jnp.minimum(u[:, :LANES], SWIGLU_LIMIT)
                lin = jnp.clip(u[:, LANES:], -SWIGLU_LIMIT, SWIGLU_LIMIT)
                act_ref[:, LANES * c:LANES * (c + 1)] = (g * _sigmoid(SWIGLU_ALPHA * g) * (lin + 1.0)).astype(BF16)
                for r in range(rows_c * c, rows_c * (c + 1)):
                    row_copy(prev, r, sems.at[1 - par]).start()
            for c in range(n_c2):
                cur[:, cw * c:cw * (c + 1)] = jnp.dot(act_ref[...], w2_ref[0, :, cw * c:cw * (c + 1)],
                                                      preferred_element_type=F32) + b2_ref[0, :, cw * c:cw * (c + 1)]

        @pl.when((i == n_valid) & ((i + 1) % 2 == par))
        def _(par=par, cur=cur, prev=prev):
            drain(prev, sems.at[1 - par])

            def issue(r, carry):
                row_copy(cur, r, sems.at[par]).start()
                return carry

            lax.fori_loop(0, MOE_BLOCK, issue, 0, unroll=8)
            drain(cur, sems.at[par])


def _experts(block_e, n_valid, inv_prev, xs, w1, b1, w2, b2, n_slots):
    rows = xs.shape[0]
    n_blocks = rows // MOE_BLOCK
    dff = w2.shape[1]
    wmap = lambda i, be, nv: (be[i], 0, 0)
    return pl.pallas_call(
        _experts_kernel,
        grid_spec=pltpu.PrefetchScalarGridSpec(
            num_scalar_prefetch=2,
            grid=(n_blocks,),
            in_specs=[
                pl.BlockSpec((1, 1, MOE_BLOCK), lambda i, be, nv: (i, 0, 0), memory_space=pltpu.SMEM),
                pl.BlockSpec((MOE_BLOCK, _D), lambda i, be, nv: (i, 0)),
                pl.BlockSpec((1, _D, 2 * dff), wmap),
                pl.BlockSpec((1, 1, 2 * dff), wmap),
                pl.BlockSpec((1, dff, _D), wmap),
                pl.BlockSpec((1, 1, _D), wmap),
            ],
            out_specs=pl.BlockSpec(memory_space=pl.ANY),
            scratch_shapes=[pltpu.VMEM((MOE_BLOCK, _D), F32), pltpu.VMEM((MOE_BLOCK, _D), F32),
                            pltpu.VMEM((MOE_BLOCK, _D), BF16), pltpu.VMEM((MOE_BLOCK, dff), BF16),
                            pltpu.SemaphoreType.DMA((2,))],
        ),
        out_shape=jax.ShapeDtypeStruct((n_slots + MOE_BLOCK, _D), F32),
        compiler_params=_cparams(("arbitrary",)),
        name="experts",
    )(block_e, n_valid, inv_prev, xs, w1, b1, w2, b2)


def _final_kernel(x1_ref, gt_ref, gf_ref, y0_ref, y1_ref, y2_ref, y3_ref, o_ref):
    gates = gt_ref[...]
    y = x1_ref[...]
    for k, yk_ref in enumerate((y0_ref, y1_ref, y2_ref, y3_ref)):
        y = y + gates[:, k:k + 1] * yk_ref[...]
    o_ref[...] = _rms(y, gf_ref[...])


def _final(x1, gates_tk, g_final, y4):
    t = x1.shape[0]
    tm = FINAL_TM
    steps = t // tm
    row = lambda i: (i, 0)
    assert TOP_K == 4
    return pl.pallas_call(
        _final_kernel,
        grid=(steps,),
        in_specs=[pl.BlockSpec((tm, _D), row), pl.BlockSpec((tm, TOP_K), row),
                  pl.BlockSpec((1, _D), lambda i: (0, 0))]
                 + [pl.BlockSpec((tm, _D), lambda i, k=k: (k * steps + i, 0)) for k in range(TOP_K)],
        out_specs=pl.BlockSpec((tm, _D), row),
        out_shape=jax.ShapeDtypeStruct((t, _D), F32),
        compiler_params=_cparams(("parallel",)),
        name="final",
    )(x1, gates_tk, g_final, y4, y4, y4, y4)


def _prep_in_proj(w_in, b_in, w_gate, b_gate):
    sizes = (GLA_HEADS * GLA_DK, GLA_HEADS * GLA_DK, GLA_HEADS * GLA_DV, GLA_HEADS * GLA_DV, GLA_GATE_RANK,
             SWA_Q_HEADS * SWA_HEAD_DIM, SWA_KV_HEADS * SWA_HEAD_DIM, SWA_KV_HEADS * SWA_HEAD_DIM, _D, _D)
    offs = [0]
    for s in sizes:
        offs.append(offs[-1] + s)

    def rearrange(m, dtype):
        p = [m[..., offs[i]:offs[i + 1]].astype(dtype) for i in range(len(sizes))]
        gq, gk, gv, gr, lr, sq, sk, sv, ga, gb = p

        def dup_heads(a):
            hs = [a[..., SWA_HEAD_DIM * h:SWA_HEAD_DIM * (h + 1)] for h in range(SWA_KV_HEADS)]
            return jnp.concatenate([hh for h in hs for hh in (h, h)], axis=-1)

        lr_pad = jnp.pad(lr, [(0, 0)] * (lr.ndim - 1) + [(0, LANES - GLA_GATE_RANK)])
        return jnp.concatenate([gq, gk, gv, gr, sq, ga, gb, dup_heads(sk), dup_heads(sv), lr_pad], axis=-1)

    w_all = rearrange(w_in, BF16)
    b_all = rearrange(b_in[None, :], F32)
    wg = jnp.pad(w_gate, ((0, LANES - GLA_GATE_RANK), (0, 0)))
    return w_all, b_all, wg, b_gate[None, :]


def _rope_tables(seq):
    half = SWA_HEAD_DIM // 2
    inv_freq = ROPE_THETA ** (-jnp.arange(half, dtype=F32) / half)
    ang = jnp.arange(seq, dtype=F32)[:, None] * inv_freq[None, :]
    cos, sin = jnp.cos(ang), jnp.sin(ang)
    cos_t = jnp.concatenate([cos, cos] * (LANES // SWA_HEAD_DIM), axis=1)
    sin_t = jnp.concatenate([-sin, sin] * (LANES // SWA_HEAD_DIM), axis=1)
    return cos_t, sin_t


def _pair_split_perm():
    src = jnp.arange(2 * LANES, dtype=I32)
    dst = jnp.where(src % 2 == 0, src // 2, LANES + src // 2)
    return (dst[:, None] == jnp.arange(2 * LANES, dtype=I32)[None, :]).astype(BF16)


def kernel(x, g_mix, w_in, b_in, w_gla_gate, b_gla_gate, g_gla_head, w_gla_out, sinks, w_swa_out, w_out,
           g_ffn, w_router, b_router, w_e1, b_e1, w_e2, b_e2, g_final):
    bsz, seq, d = x.shape
    assert d == _D and w_in.shape[0] == 1, "single-layer, d_model=1024 only"
    assert seq % max(PROJ_TM, SWA_WINDOW, GLA_CHUNK) == 0
    t = bsz * seq
    assert t % max(MERGE_TM, ROUTE_TT, FINAL_TM) == 0 and t % N_EXPERTS == 0
    x2 = x.reshape(t, d)

    w_all, b_all, wg, bg = _prep_in_proj(w_in[0], b_in[0], w_gla_gate[0], b_gla_gate[0])
    cos_t, sin_t = _rope_tables(seq)
    gq, gk, gv, gr, sq, ga, gb, sk, sv, lg = _in_proj(x2, g_mix, w_all, b_all, cos_t, sin_t, wg, bg, seq)
    oa = _gla(gq, gk, gv, lg, gr, g_gla_head, bsz, seq)
    ob = _swa(sinks[0], sq, sk, sv, bsz, seq)
    x1, hn, logits_t = _merge(x2, oa, ob, ga, gb, w_gla_out[0].astype(BF16), w_swa_out[0].astype(BF16),
                              w_out[0].astype(BF16), g_ffn, w_router[0].T, b_router[0][:, None])

    e_kt, g_kt, r_kt, cnt = _route(logits_t)
    counts = cnt[:, 0].astype(I32)
    blocks_e = (counts + MOE_BLOCK - 1) // MOE_BLOCK
    bend = jnp.cumsum(blocks_e)
    pstart = (bend - blocks_e) * MOE_BLOCK
    dest_kt = r_kt
    for e in range(N_EXPERTS):
        dest_kt = dest_kt + jnp.where(e_kt == e, pstart[e], 0)
    dest_flat = dest_kt.T.reshape(t * TOP_K)
    n_blocks = (t * TOP_K) // MOE_BLOCK + N_EXPERTS
    block_e = jnp.minimum(jnp.sum(bend[None, :] <= jnp.arange(n_blocks, dtype=I32)[:, None], axis=1),
                          N_EXPERTS - 1).astype(I32)
    n_valid = bend[-1:].astype(I32)

    b1 = b_e1[0].reshape(N_EXPERTS, -1, LANES, 2).transpose(0, 1, 3, 2).reshape(N_EXPERTS, 1, -1)
    xs, w1, w2, inv = _dispatch(bend * MOE_BLOCK, blocks_e, n_valid, dest_flat, hn, w_e1[0], w_e2[0],
                                _pair_split_perm(), n_blocks * MOE_BLOCK)
    n_slots = t * TOP_K
    inv_prev = jnp.concatenate([n_slots + jnp.arange(MOE_BLOCK, dtype=I32), inv[:-MOE_BLOCK]])
    y4 = _experts(block_e, n_valid, inv_prev.reshape(n_blocks, 1, MOE_BLOCK), xs, w1, b1, w2,
                  b_e2[0][:, None, :], n_slots)
    out = _final(x1, g_kt.T, g_final[None, :], y4)
    return out.reshape(bsz, seq, d)
```

```python
import jax
import jax.numpy as jnp
import numpy as np
from jax import lax
from jax.experimental import pallas as pl
from jax.experimental.pallas import tpu as pltpu

F32 = jnp.float32
BF16 = jnp.bfloat16
I32 = jnp.int32

NORM_EPS = 1e-5
GLA_HEADS = 4
GLA_DK = 128
GLA_DV = 256
GLA_GATE_RANK = 16
GLA_TAU = 16.0
SWA_Q_HEADS = 16
SWA_KV_HEADS = 2
SWA_GROUP = SWA_Q_HEADS // SWA_KV_HEADS
SWA_HEAD_DIM = 64
SWA_WINDOW = 128
ROPE_THETA = 10000.0
N_EXPERTS = 32
TOP_K = 4
SWIGLU_LIMIT = 7.0
SWIGLU_ALPHA = 1.702

LANES = 128
NEG = -1e30
VMEM_LIMIT = 56 * 1024 * 1024

PROJ_TM = 256
GLA_CHUNK = 128
GLA_HEADS_PER_STEP = 4
MERGE_TM = 512
MERGE_TN = 256
MERGE_ROW_GROUPS = 1
ROUTE_TT = 512
MOE_BLOCK = 512
DISPATCH_GROUPS = 16
FINAL_TM = 512

_D = 1024
_C_GQ = 0
_C_GK = _C_GQ + GLA_HEADS * GLA_DK
_C_GV = _C_GK + GLA_HEADS * GLA_DK
_C_GR = _C_GV + GLA_HEADS * GLA_DV
_C_SQ = _C_GR + GLA_HEADS * GLA_DV
_C_GA = _C_SQ + SWA_Q_HEADS * SWA_HEAD_DIM
_C_GB = _C_GA + _D
_C_SK = _C_GB + _D
_C_SV = _C_SK + SWA_KV_HEADS * LANES
_C_LR = _C_SV + SWA_KV_HEADS * LANES
_C_END = _C_LR + LANES


def _cparams(sem):
    return pltpu.CompilerParams(dimension_semantics=sem, vmem_limit_bytes=VMEM_LIMIT)


def _rms(x, g):
    return x * lax.rsqrt(jnp.mean(x * x, axis=-1, keepdims=True) + NORM_EPS) * g


def _sigmoid(x):
    return 1.0 / (1.0 + jnp.exp(-x))


def _dot_nt(a, b):
    return lax.dot_general(a, b, (((1,), (1,)), ((), ())), preferred_element_type=F32)


def _rope_slabs(acc, cos, sin, first_half):
    outs = []
    for i in range(acc.shape[1] // LANES):
        xs = acc[:, LANES * i:LANES * (i + 1)]
        partner = jnp.where(first_half, pltpu.roll(xs, LANES - 32, 1), pltpu.roll(xs, 32, 1))
        outs.append(xs * cos + partner * sin)
    return jnp.concatenate(outs, axis=1)


def _in_proj_kernel(x_ref, g_ref, w_ref, b_ref, cos_ref, sin_ref, wg_ref, bg_ref,
                    gq_ref, gk_ref, gv_ref, gr_ref, sq_ref, ga_ref, gb_ref, sk_ref, sv_ref, lg_ref):
    h = _rms(x_ref[...], g_ref[...]).astype(BF16)

    def proj(lo, hi):
        return jnp.dot(h, w_ref[:, lo:hi], preferred_element_type=F32) + b_ref[:, lo:hi]

    cos = cos_ref[...]
    sin = sin_ref[...]
    lane = lax.broadcasted_iota(I32, cos.shape, 1)
    first_half = (lane % SWA_HEAD_DIM) < (SWA_HEAD_DIM // 2)

    gq_ref[...] = (proj(_C_GQ, _C_GK) * (GLA_DK ** -0.5)).astype(BF16)
    gk_ref[...] = proj(_C_GK, _C_GV).astype(BF16)
    gv_ref[...] = proj(_C_GV, _C_GR).astype(BF16)
    gr = proj(_C_GR, _C_SQ)
    gr_ref[...] = (gr * _sigmoid(gr)).astype(BF16)
    sq = proj(_C_SQ, _C_GA) * (SWA_HEAD_DIM ** -0.5)
    sq_ref[...] = _rope_slabs(sq, cos, sin, first_half).astype(BF16)
    ga_ref[...] = _sigmoid(proj(_C_GA, _C_GB)).astype(BF16)
    gb_ref[...] = _sigmoid(proj(_C_GB, _C_SK)).astype(BF16)
    sk_ref[...] = _rope_slabs(proj(_C_SK, _C_SV), cos, sin, first_half).astype(BF16)
    sv_ref[...] = proj(_C_SV, _C_LR).astype(BF16)
    z = jnp.dot(proj(_C_LR, _C_END), wg_ref[...], precision=lax.Precision.HIGHEST,
                preferred_element_type=F32) + bg_ref[...]
    log_sig = jnp.minimum(z, 0.0) - jnp.log(1.0 + jnp.exp(-jnp.abs(z)))
    lg_ref[...] = log_sig * (1.0 / GLA_TAU)


def _in_proj(x2, g_mix, w_all, b_all, cos_t, sin_t, wg, bg, seq):
    t = x2.shape[0]
    tm = PROJ_TM
    pos_blocks = seq // tm
    const = lambda i: (0, 0)
    row = lambda i: (i, 0)
    widths = [(_C_GK - _C_GQ, BF16), (_C_GV - _C_GK, BF16), (_C_GR - _C_GV, BF16), (_C_SQ - _C_GR, BF16),
              (_C_GA - _C_SQ, BF16), (_D, BF16), (_D, BF16), (_C_SV - _C_SK, BF16), (_C_LR - _C_SV, BF16),
              (GLA_HEADS * GLA_DK, F32)]
    return pl.pallas_call(
        _in_proj_kernel,
        grid=(t // tm,),
        in_specs=[
            pl.BlockSpec((tm, _D), row),
            pl.BlockSpec((1, _D), const),
            pl.BlockSpec((_D, _C_END), const, pipeline_mode=pl.Buffered(1)),
            pl.BlockSpec((1, _C_END), const),
            pl.BlockSpec((tm, LANES), lambda i: (i % pos_blocks, 0)),
            pl.BlockSpec((tm, LANES), lambda i: (i % pos_blocks, 0)),
            pl.BlockSpec((LANES, GLA_HEADS * GLA_DK), const),
            pl.BlockSpec((1, GLA_HEADS * GLA_DK), const),
        ],
        out_specs=[pl.BlockSpec((tm, w), row) for w, _ in widths],
        out_shape=[jax.ShapeDtypeStruct((t, w), dt) for w, dt in widths],
        compiler_params=_cparams(("parallel",)),
        name="in_proj",
    )(x2, g_mix, w_all, b_all, cos_t, sin_t, wg, bg)


def _gla_cumsum_operator(c_len):
    t = np.arange(c_len)[:, None]
    r = np.arange(c_len)[None, :]
    return np.tile((r <= t).astype(np.float32), (1, 3))


def _gla_kernel(q_ref, k_ref, v_ref, lg_ref, gr_ref, gh_ref, dm_ref, o_ref, st_ref):
    c_len = GLA_CHUNK
    n_lev = c_len.bit_length() - 1
    seq = q_ref.shape[0]
    st_ref[...] = jnp.zeros_like(st_ref)

    t_i = lax.broadcasted_iota(I32, (c_len, c_len), 0)
    j_i = lax.broadcasted_iota(I32, (c_len, c_len), 1)
    row = lax.broadcasted_iota(I32, (c_len, 1), 0)
    diag = t_i == j_i
    upper, pair = [], []
    for lev in range(n_lev):
        s = c_len >> (lev + 1)
        upper.append((row & s) != 0)
        pair.append(((t_i // (2 * s)) == (j_i // (2 * s))) & ((t_i & s) != 0) & ((j_i & s) == 0))

    sub8 = lax.broadcasted_iota(I32, (c_len // 8, 8, GLA_DK), 1)

    def boundary_rows(b, s):
        if s >= 4:
            b3 = b.reshape(c_len // (2 * s), 2 * s, GLA_DK)
            return jnp.broadcast_to(b3[:, s - 1:s, :], b3.shape).reshape(c_len, GLA_DK)
        b3 = b.reshape(c_len // 8, 8, GLA_DK)
        lo = jnp.broadcast_to(b3[:, 1:2, :], b3.shape)
        hi = jnp.broadcast_to(b3[:, 5:6, :], b3.shape)
        return jnp.where(sub8 < 4, lo, hi).reshape(c_len, GLA_DK)

    def head_chunk(r0, hh):
        kcols = slice(GLA_DK * hh, GLA_DK * (hh + 1))
        vcols = slice(GLA_DV * hh, GLA_DV * (hh + 1))
        q_bf = q_ref[pl.ds(r0, c_len), kcols]
        k_bf = k_ref[pl.ds(r0, c_len), kcols]
        q = q_bf.astype(F32)
        k = k_bf.astype(F32)
        v = v_ref[pl.ds(r0, c_len), vcols]

        lg = lg_ref[pl.ds(r0, c_len), kcols]
        lg_hi = lg.astype(BF16)
        rem = lg - lg_hi.astype(F32)
        lg_mid = rem.astype(BF16)
        lg_lo = (rem - lg_mid.astype(F32)).astype(BF16)
        b = jnp.dot(dm_ref[...], jnp.concatenate([lg_hi, lg_mid, lg_lo], axis=0),
                    preferred_element_type=F32)
        w_cum = jnp.exp(b)

        st = st_ref[hh]
        o = _dot_nt((q * w_cum).astype(BF16), st.astype(BF16))

        a = jnp.where(diag, _dot_nt(q_bf, k_bf), 0.0)
        for lev in range(n_lev):
            s = c_len >> (lev + 1)
            if s == 1:
                w = jnp.where(upper[lev], jnp.exp(lg), 1.0)
            else:
                w = jnp.exp(-jnp.abs(b - boundary_rows(b, s)))
            z = (jnp.where(upper[lev], q, k) * w).astype(BF16)
            a = jnp.where(pair[lev], _dot_nt(z, z), a)
        o = o + jnp.dot(a.astype(BF16), v, preferred_element_type=F32)

        b_last = b[c_len - 1:c_len, :]
        upd = lax.dot_general(v, (k * jnp.exp(b_last - b)).astype(BF16), (((0,), (0,)), ((), ())),
                              preferred_element_type=F32)
        st_ref[hh] = st * w_cum[c_len - 1:c_len, :] + upd

        on = _rms(o, gh_ref[...])
        o_ref[pl.ds(r0, c_len), vcols] = (on * gr_ref[pl.ds(r0, c_len), vcols].astype(F32)).astype(BF16)

    def chunk(c, carry):
        r0 = pl.multiple_of(c * c_len, c_len)
        for hh in range(GLA_HEADS_PER_STEP):
            head_chunk(r0, hh)
        return carry

    lax.fori_loop(0, seq // c_len, chunk, 0, unroll=2)


def _gla(gq, gk, gv, lg, gr, g_head, bsz, seq):
    t = gq.shape[0]
    hs = GLA_HEADS_PER_STEP
    dmat = jnp.asarray(_gla_cumsum_operator(GLA_CHUNK), dtype=BF16)
    return pl.pallas_call(
        _gla_kernel,
        grid=(bsz, GLA_HEADS // hs),
        in_specs=[
            pl.BlockSpec((seq, hs * GLA_DK), lambda b, h: (b, h)),
            pl.BlockSpec((seq, hs * GLA_DK), lambda b, h: (b, h)),
            pl.BlockSpec((seq, hs * GLA_DV), lambda b, h: (b, h)),
            pl.BlockSpec((seq, hs * GLA_DK), lambda b, h: (b, h)),
            pl.BlockSpec((seq, hs * GLA_DV), lambda b, h: (b, h)),
            pl.BlockSpec((1, GLA_DV), lambda b, h: (0, 0)),
            pl.BlockSpec(dmat.shape, lambda b, h: (0, 0)),
        ],
        out_specs=pl.BlockSpec((seq, hs * GLA_DV), lambda b, h: (b, h)),
        out_shape=jax.ShapeDtypeStruct((t, GLA_HEADS * GLA_DV), BF16),
        scratch_shapes=[pltpu.VMEM((hs, GLA_DV, GLA_DK), F32)],
        compiler_params=_cparams(("parallel", "parallel")),
        name="gla",
    )(gq, gk, gv, lg, gr, g_head, dmat)


def _swa_kernel(sink_ref, q_ref, k_ref, v_ref, o_ref):
    w = SWA_WINDOW
    seq = q_ref.shape[0]
    hk = pl.program_id(1)
    lane_q = lax.broadcasted_iota(I32, (w, LANES), 1)
    low_q = lane_q < SWA_HEAD_DIM
    lane_b = lax.broadcasted_iota(I32, (2 * w, LANES), 1)
    low_b = lane_b < SWA_HEAD_DIM
    qi = lax.broadcasted_iota(I32, (w, 2 * w), 0)
    kj = lax.broadcasted_iota(I32, (w, 2 * w), 1)
    in_window = (kj > qi) & (kj <= qi + w)
    zero_q = jnp.zeros((w, LANES), BF16)
    zero_b = jnp.zeros((2 * w, LANES), BF16)

    def block(n, carry):
        r0 = pl.multiple_of(n * w, w)
        p0 = pl.multiple_of(jnp.maximum(n - 1, 0) * w, w)
        kb = jnp.concatenate([k_ref[pl.ds(p0, w), :], k_ref[pl.ds(r0, w), :]], axis=0)
        vb = jnp.concatenate([v_ref[pl.ds(p0, w), :], v_ref[pl.ds(r0, w), :]], axis=0)
        valid = in_window & ((kj >= w) | (n > 0))
        v_lo = jnp.where(low_b, vb, zero_b)
        v_hi = jnp.where(low_b, zero_b, vb)
        for m in range(SWA_GROUP // 2):
            qp = q_ref[pl.ds(r0, w), LANES * m:LANES * (m + 1)]
            acc = jnp.zeros((w, LANES), F32)
            for par in range(2):
                qm = jnp.where(low_q, qp, zero_q) if par == 0 else jnp.where(low_q, zero_q, qp)
                s = jnp.where(valid, _dot_nt(qm, kb), NEG)
                sink = sink_ref[hk * SWA_GROUP + 2 * m + par]
                mx = jnp.maximum(jnp.max(s, axis=-1, keepdims=True), sink)
                p = jnp.exp(s - mx)
                den = jnp.sum(p, axis=-1, keepdims=True) + jnp.exp(sink - mx)
                pv = jnp.dot(p.astype(BF16), v_lo if par == 0 else v_hi, preferred_element_type=F32)
                acc = acc + pv / den
            o_ref[pl.ds(r0, w), LANES * m:LANES * (m + 1)] = acc.astype(BF16)
        return carry

    lax.fori_loop(0, seq // w, block, 0, unroll=2)


def _swa(sinks, sq, sk, sv, bsz, seq):
    t = sq.shape[0]
    gw = SWA_GROUP * SWA_HEAD_DIM
    return pl.pallas_call(
        _swa_kernel,
        grid_spec=pltpu.PrefetchScalarGridSpec(
            num_scalar_prefetch=1,
            grid=(bsz, SWA_KV_HEADS),
            in_specs=[
                pl.BlockSpec((seq, gw), lambda b, h, s: (b, h)),
                pl.BlockSpec((seq, LANES), lambda b, h, s: (b, h)),
                pl.BlockSpec((seq, LANES), lambda b, h, s: (b, h)),
            ],
            out_specs=pl.BlockSpec((seq, gw), lambda b, h, s: (b, h)),
        ),
        out_shape=jax.ShapeDtypeStruct((t, SWA_Q_HEADS * SWA_HEAD_DIM), BF16),
        compiler_params=_cparams(("parallel", "parallel")),
        name="swa",
    )(sinks, sq, sk, sv)


def _merge_kernel(x_ref, oa_ref, ob_ref, ga_ref, gb_ref, wa_ref, wb_ref, wo_ref, gf_ref, wr_ref, br_ref,
                  x1_ref, hn_ref, lt_ref, mixed_ref):
    rows_g = x_ref.shape[0] // MERGE_ROW_GROUPS
    for h in range(MERGE_ROW_GROUPS):
        rows = slice(rows_g * h, rows_g * (h + 1))
        oa = oa_ref[rows, :]
        ob = ob_ref[rows, :]
        for n in range(_D // MERGE_TN):
            cols = slice(MERGE_TN * n, MERGE_TN * (n + 1))
            ya = ga_ref[rows, cols].astype(F32) * jnp.dot(oa, wa_ref[:, cols], preferred_element_type=F32)
            yb = gb_ref[rows, cols].astype(F32) * jnp.dot(ob, wb_ref[:, cols], preferred_element_type=F32)
            mixed_ref[rows, cols] = (ya + yb).astype(BF16)
        mixed = mixed_ref[rows, :]
        for n in range(_D // MERGE_TN):
            cols = slice(MERGE_TN * n, MERGE_TN * (n + 1))
            x1_ref[rows, cols] = x_ref[rows, cols] + jnp.dot(mixed, wo_ref[:, cols], preferred_element_type=F32)
        hn = _rms(x1_ref[rows, :], gf_ref[...])
        hn_ref[rows, :] = hn
        lt_ref[:, rows] = lax.dot_general(wr_ref[...], hn, (((1,), (1,)), ((), ())),
                                          precision=lax.Precision.HIGHEST,
                                          preferred_element_type=F32) + br_ref[...]


def _merge(x2, oa, ob, ga, gb, wa, wb, wo, g_ffn, wr_t, br_col):
    t = x2.shape[0]
    tm = MERGE_TM
    row = lambda i: (i, 0)
    const = lambda i: (0, 0)
    return pl.pallas_call(
        _merge_kernel,
        grid=(t // tm,),
        in_specs=[pl.BlockSpec((tm, _D), row)] * 5 + [pl.BlockSpec((_D, _D), const)] * 3 + [
            pl.BlockSpec((1, _D), const),
            pl.BlockSpec((N_EXPERTS, _D), const),
            pl.BlockSpec((N_EXPERTS, 1), const),
        ],
        out_specs=[pl.BlockSpec((tm, _D), row), pl.BlockSpec((tm, _D), row),
                   pl.BlockSpec((N_EXPERTS, tm), lambda i: (0, i))],
        out_shape=[jax.ShapeDtypeStruct((t, _D), F32), jax.ShapeDtypeStruct((t, _D), F32),
                   jax.ShapeDtypeStruct((N_EXPERTS, t), F32)],
        scratch_shapes=[pltpu.VMEM((tm, _D), BF16)],
        compiler_params=_cparams(("parallel",)),
        name="merge",
    )(x2, oa, ob, ga, gb, wa, wb, wo, g_ffn, wr_t, br_col)


def _route_kernel(lt_ref, e_ref, g_ref, r_ref, cnt_ref, carry_ref):
    tt = lt_ref.shape[1]

    @pl.when(pl.program_id(0) == 0)
    def _():
        carry_ref[...] = jnp.zeros_like(carry_ref)

    eid = lax.broadcasted_iota(I32, (N_EXPERTS, tt), 0)
    work = lt_ref[...]
    vals, idxs = [], []
    chosen = jnp.zeros((N_EXPERTS, tt), F32)
    for _ in range(TOP_K):
        m = jnp.max(work, axis=0, keepdims=True)
        idx = jnp.min(jnp.where(work == m, eid, N_EXPERTS), axis=0, keepdims=True)
        hit = eid == idx
        work = jnp.where(hit, -jnp.inf, work)
        chosen = jnp.where(hit, 1.0, chosen)
        vals.append(m)
        idxs.append(idx)
    ex = [jnp.exp(v - vals[0]) for v in vals]
    den = ex[0] + ex[1] + ex[2] + ex[3]

    t_r = lax.broadcasted_iota(I32, (tt, tt), 0)
    t_c = lax.broadcasted_iota(I32, (tt, tt), 1)
    before = (t_r < t_c).astype(BF16)
    pref = jnp.dot(chosen.astype(BF16), before, preferred_element_type=F32) + carry_ref[:, 0:1]
    for k in range(TOP_K):
        e_ref[k:k + 1, :] = idxs[k]
        g_ref[k:k + 1, :] = ex[k] / den
        r_ref[k:k + 1, :] = jnp.sum(jnp.where(eid == idxs[k], pref, 0.0), axis=0, keepdims=True).astype(I32)
    total = pref[:, tt - 1:tt] + chosen[:, tt - 1:tt]
    carry_ref[...] = jnp.broadcast_to(total, carry_ref.shape)
    cnt_ref[...] = jnp.broadcast_to(total, cnt_ref.shape)


def _route(logits_t):
    t = logits_t.shape[1]
    tt = ROUTE_TT
    blk = lambda i: (0, i)
    return pl.pallas_call(
        _route_kernel,
        grid=(t // tt,),
        in_specs=[pl.BlockSpec((N_EXPERTS, tt), blk)],
        out_specs=[pl.BlockSpec((TOP_K, tt), blk), pl.BlockSpec((TOP_K, tt), blk), pl.BlockSpec((TOP_K, tt), blk),
                   pl.BlockSpec((N_EXPERTS, LANES), lambda i: (0, 0))],
        out_shape=[jax.ShapeDtypeStruct((TOP_K, t), I32), jax.ShapeDtypeStruct((TOP_K, t), F32),
                   jax.ShapeDtypeStruct((TOP_K, t), I32), jax.ShapeDtypeStruct((N_EXPERTS, LANES), F32)],
        scratch_shapes=[pltpu.VMEM((N_EXPERTS, LANES), F32)],
        compiler_params=_cparams(("arbitrary",)),
        name="route",
    )(logits_t)


def _dispatch_kernel(pend_ref, vend_ref, nblk_ref, nv_ref, dest_ref, hn_ref, w1_ref, w2_ref, perm_ref,
                     xs_hbm, w1o_ref, w2o_ref, inv_ref, zero_ref, sem, zsem):
    n_tok = hn_ref.shape[0]
    n_rows = w1_ref.shape[1]
    gw = perm_ref.shape[0]
    n_col_groups = w1_ref.shape[2] // gw
    slices = DISPATCH_GROUPS * n_col_groups
    tok_s = n_tok // slices
    rows_g = n_rows // DISPATCH_GROUPS
    w2_cols = w2_ref.shape[2] // n_col_groups
    n_blocks = xs_hbm.shape[0] // MOE_BLOCK
    n_tokens = n_tok * N_EXPERTS
    n_slots = TOP_K * n_tokens
    tok0 = pl.program_id(0) * n_tok

    def zero_block(row0):
        return pltpu.make_async_copy(zero_ref, xs_hbm.at[pl.ds(pl.multiple_of(row0, MOE_BLOCK), MOE_BLOCK)], zsem)

    @pl.when(pl.program_id(0) == 0)
    def _():
        zero_ref[...] = jnp.zeros_like(zero_ref)

        def spill_rows(e, carry):
            def one(r, c):
                inv_ref[r] = n_slots + r % MOE_BLOCK
                return c
            return lax.fori_loop(vend_ref[e], pend_ref[e], one, carry)

        lax.fori_loop(0, N_EXPERTS, spill_rows, 0)

        def expert_tail(e, carry, start):
            @pl.when(nblk_ref[e] > 0)
            def _():
                cp = zero_block(pend_ref[e] - MOE_BLOCK)
                cp.start() if start else cp.wait()
            return carry

        def unused_block(b, carry, start):
            cp = zero_block(b * MOE_BLOCK)
            cp.start() if start else cp.wait()
            return carry

        for start in (True, False):
            lax.fori_loop(0, N_EXPERTS, lambda e, c: expert_tail(e, c, start), 0)
            lax.fori_loop(nv_ref[0], n_blocks, lambda b, c: unused_block(b, c, start), 0)

    def group(g, carry):
        r0 = pl.multiple_of(g * rows_g, rows_g)
        for gi in range(n_col_groups):
            t0 = (g * n_col_groups + gi) * tok_s
            for i in range(tok_s):
                for k in range(TOP_K):
                    d = dest_ref[0, 0, (t0 + i) * TOP_K + k]
                    pltpu.make_async_copy(hn_ref.at[pl.ds(t0 + i, 1)], xs_hbm.at[pl.ds(d, 1)], sem).start()
                    inv_ref[d] = k * n_tokens + tok0 + t0 + i
            wb = w1_ref[0, pl.ds(r0, rows_g), gw * gi:gw * (gi + 1)].astype(BF16)
            w1o_ref[0, pl.ds(r0, rows_g), gw * gi:gw * (gi + 1)] = jnp.dot(
                wb, perm_ref[...], preferred_element_type=F32).astype(BF16)
            w2o_ref[0, pl.ds(r0, rows_g), w2_cols * gi:w2_cols * (gi + 1)] = w2_ref[
                0, pl.ds(r0, rows_g), w2_cols * gi:w2_cols * (gi + 1)].astype(BF16)
        return carry

    lax.fori_loop(0, DISPATCH_GROUPS, group, 0)
    for k in range(TOP_K):
        pltpu.make_async_copy(hn_ref, xs_hbm.at[pl.ds(0, n_tok)], sem).wait()


def _dispatch(pend_rows, vend_rows, blocks_e, n_valid, dest_flat, hn, w1, w2, perm, rows):
    t = hn.shape[0]
    n_e, d, n1 = w1.shape
    dff = w2.shape[1]
    tm = t // n_e
    assert dff == d and n_e == N_EXPERTS and tm % (DISPATCH_GROUPS * (n1 // perm.shape[0])) == 0
    emap = lambda e, *_: (e, 0, 0)
    return pl.pallas_call(
        _dispatch_kernel,
        grid_spec=pltpu.PrefetchScalarGridSpec(
            num_scalar_prefetch=4,
            grid=(n_e,),
            in_specs=[pl.BlockSpec((1, 1, tm * TOP_K), emap, memory_space=pltpu.SMEM),
                      pl.BlockSpec((tm, _D), lambda e, *_: (e, 0)),
                      pl.BlockSpec((1, d, n1), emap),
                      pl.BlockSpec((1, dff, _D), emap),
                      pl.BlockSpec(perm.shape, lambda e, *_: (0, 0))],
            out_specs=[pl.BlockSpec(memory_space=pl.ANY),
                       pl.BlockSpec((1, d, n1), emap),
                       pl.BlockSpec((1, dff, _D), emap),
                       pl.BlockSpec(memory_space=pltpu.SMEM)],
            scratch_shapes=[pltpu.VMEM((MOE_BLOCK, _D), F32), pltpu.SemaphoreType.DMA, pltpu.SemaphoreType.DMA],
        ),
        out_shape=[jax.ShapeDtypeStruct((rows, _D), F32),
                   jax.ShapeDtypeStruct((n_e, d, n1), BF16),
                   jax.ShapeDtypeStruct((n_e, dff, _D), BF16),
                   jax.ShapeDtypeStruct((rows,), I32)],
        compiler_params=_cparams(("arbitrary",)),
        name="dispatch",
    )(pend_rows, vend_rows, blocks_e, n_valid, dest_flat.reshape(n_e, 1, tm * TOP_K), hn, w1, w2, perm)


def _experts_kernel(be_ref, nv_ref, invp_ref, xs_ref, w1_ref, b1_ref, w2_ref, b2_ref,
                    y4_hbm, buf_ref, xb_ref, act_ref, sems):
    del be_ref
    i = pl.program_id(0)
    n_valid = nv_ref[0]
    par = i % 2
    cur = buf_ref.at[par]
    prev = buf_ref.at[1 - par]
    cw = 2 * LANES
    n_c1 = w1_ref.shape[2] // cw
    n_c2 = w2_ref.shape[2] // cw
    rows_c = MOE_BLOCK // n_c1

    def row_copy(src_ref, r, sem):
        return pltpu.make_async_copy(src_ref.at[pl.ds(r, 1)], y4_hbm.at[pl.ds(invp_ref[0, 0, r], 1)], sem)

    def drain(src_ref, sem):
        pltpu.make_async_copy(src_ref, y4_hbm.at[pl.ds(0, MOE_BLOCK)], sem).wait()

    @pl.when(i < n_valid)
    def _():
        @pl.when(i == 0)
        def _():
            prev[...] = jnp.zeros(prev.shape, prev.dtype)

        @pl.when(i > 0)
        def _():
            drain(cur, sems.at[par])

        xb_ref[...] = xs_ref[...].astype(BF16)
        for c in range(n_c1):
            u = jnp.dot(xb_ref[...], w1_ref[0, :, cw * c:cw * (c + 1)],
                        preferred_element_type=F32) + b1_ref[0, :, cw * c:cw * (c + 1)]
            g = jnp.minimum(u[:, :LANES], SWIGLU_LIMIT)
            lin = jnp.clip(u[:, LANES:], -SWIGLU_LIMIT, SWIGLU_LIMIT)
            act_ref[:, LANES * c:LANES * (c + 1)] = (g * _sigmoid(SWIGLU_ALPHA * g) * (lin + 1.0)).astype(BF16)
            for r in range(rows_c * c, rows_c * (c + 1)):
                row_copy(prev, r, sems.at[1 - par]).start()
        for c in range(n_c2):
            cur[:, cw * c:cw * (c + 1)] = jnp.dot(act_ref[...], w2_ref[0, :, cw * c:cw * (c + 1)],
                                                  preferred_element_type=F32) + b2_ref[0, :, cw * c:cw * (c + 1)]

    @pl.when(i == n_valid)
    def _():
        drain(cur, sems.at[par])

        def issue(r, carry):
            row_copy(prev, r, sems.at[1 - par]).start()
            return carry

        lax.fori_loop(0, MOE_BLOCK, issue, 0, unroll=8)
        drain(prev, sems.at[1 - par])


def _experts(block_e, n_valid, inv_prev, xs, w1, b1, w2, b2, n_slots):
    rows = xs.shape[0]
    n_blocks = rows // MOE_BLOCK
    dff = w2.shape[1]
    wmap = lambda i, be, nv: (be[i], 0, 0)
    return pl.pallas_call(
        _experts_kernel,
        grid_spec=pltpu.PrefetchScalarGridSpec(
            num_scalar_prefetch=2,
            grid=(n_blocks,),
            in_specs=[
                pl.BlockSpec((1, 1, MOE_BLOCK), lambda i, be, nv: (i, 0, 0), memory_space=pltpu.SMEM),
                pl.BlockSpec((MOE_BLOCK, _D), lambda i, be, nv: (i, 0)),
                pl.BlockSpec((1, _D, 2 * dff), wmap),
                pl.BlockSpec((1, 1, 2 * dff), wmap),
                pl.BlockSpec((1, dff, _D), wmap),
                pl.BlockSpec((1, 1, _D), wmap),
            ],
            out_specs=pl.BlockSpec(memory_space=pl.ANY),
            scratch_shapes=[pltpu.VMEM((2, MOE_BLOCK, _D), F32),
                            pltpu.VMEM((MOE_BLOCK, _D), BF16), pltpu.VMEM((MOE_BLOCK, dff), BF16),
                            pltpu.SemaphoreType.DMA((2,))],
        ),
        out_shape=jax.ShapeDtypeStruct((n_slots + MOE_BLOCK, _D), F32),
        compiler_params=_cparams(("arbitrary",)),
        name="experts",
    )(block_e, n_valid, inv_prev, xs, w1, b1, w2, b2)


def _final_kernel(x1_ref, gt_ref, gf_ref, y0_ref, y1_ref, y2_ref, y3_ref, o_ref):
    gates = gt_ref[...]
    y = x1_ref[...]
    for k, yk_ref in enumerate((y0_ref, y1_ref, y2_ref, y3_ref)):
        y = y + gates[:, k:k + 1] * yk_ref[...]
    o_ref[...] = _rms(y, gf_ref[...])


def _final(x1, gates_tk, g_final, y4):
    t = x1.shape[0]
    tm = FINAL_TM
    steps = t // tm
    row = lambda i: (i, 0)
    assert TOP_K == 4
    return pl.pallas_call(
        _final_kernel,
        grid=(steps,),
        in_specs=[pl.BlockSpec((tm, _D), row), pl.BlockSpec((tm, TOP_K), row),
                  pl.BlockSpec((1, _D), lambda i: (0, 0))]
                 + [pl.BlockSpec((tm, _D), lambda i, k=k: (k * steps + i, 0)) for k in range(TOP_K)],
        out_specs=pl.BlockSpec((tm, _D), row),
        out_shape=jax.ShapeDtypeStruct((t, _D), F32),
        compiler_params=_cparams(("parallel",)),
        name="final",
    )(x1, gates_tk, g_final, y4, y4, y4, y4)


def _prep_in_proj(w_in, b_in, w_gate, b_gate):
    sizes = (GLA_HEADS * GLA_DK, GLA_HEADS * GLA_DK, GLA_HEADS * GLA_DV, GLA_HEADS * GLA_DV, GLA_GATE_RANK,
             SWA_Q_HEADS * SWA_HEAD_DIM, SWA_KV_HEADS * SWA_HEAD_DIM, SWA_KV_HEADS * SWA_HEAD_DIM, _D, _D)
    offs = [0]
    for s in sizes:
        offs.append(offs[-1] + s)

    def rearrange(m, dtype):
        p = [m[..., offs[i]:offs[i + 1]].astype(dtype) for i in range(len(sizes))]
        gq, gk, gv, gr, lr, sq, sk, sv, ga, gb = p

        def dup_heads(a):
            hs = [a[..., SWA_HEAD_DIM * h:SWA_HEAD_DIM * (h + 1)] for h in range(SWA_KV_HEADS)]
            return jnp.concatenate([hh for h in hs for hh in (h, h)], axis=-1)

        lr_pad = jnp.pad(lr, [(0, 0)] * (lr.ndim - 1) + [(0, LANES - GLA_GATE_RANK)])
        return jnp.concatenate([gq, gk, gv, gr, sq, ga, gb, dup_heads(sk), dup_heads(sv), lr_pad], axis=-1)

    w_all = rearrange(w_in, BF16)
    b_all = rearrange(b_in[None, :], F32)
    wg = jnp.pad(w_gate, ((0, LANES - GLA_GATE_RANK), (0, 0)))
    return w_all, b_all, wg, b_gate[None, :]


def _rope_tables(seq):
    half = SWA_HEAD_DIM // 2
    inv_freq = ROPE_THETA ** (-jnp.arange(half, dtype=F32) / half)
    ang = jnp.arange(seq, dtype=F32)[:, None] * inv_freq[None, :]
    cos, sin = jnp.cos(ang), jnp.sin(ang)
    cos_t = jnp.concatenate([cos, cos] * (LANES // SWA_HEAD_DIM), axis=1)
    sin_t = jnp.concatenate([-sin, sin] * (LANES // SWA_HEAD_DIM), axis=1)
    return cos_t, sin_t


def _pair_split_perm():
    src = jnp.arange(2 * LANES, dtype=I32)
    dst = jnp.where(src % 2 == 0, src // 2, LANES + src // 2)
    return (dst[:, None] == jnp.arange(2 * LANES, dtype=I32)[None, :]).astype(BF16)


def kernel(x, g_mix, w_in, b_in, w_gla_gate, b_gla_gate, g_gla_head, w_gla_out, sinks, w_swa_out, w_out,
           g_ffn, w_router, b_router, w_e1, b_e1, w_e2, b_e2, g_final):
    bsz, seq, d = x.shape
    assert d == _D and w_in.shape[0] == 1, "single-layer, d_model=1024 only"
    assert seq % max(PROJ_TM, SWA_WINDOW, GLA_CHUNK) == 0
    t = bsz * seq
    assert t % max(MERGE_TM, ROUTE_TT, FINAL_TM) == 0 and t % N_EXPERTS == 0
    x2 = x.reshape(t, d)

    w_all, b_all, wg, bg = _prep_in_proj(w_in[0], b_in[0], w_gla_gate[0], b_gla_gate[0])
    cos_t, sin_t = _rope_tables(seq)
    gq, gk, gv, gr, sq, ga, gb, sk, sv, lg = _in_proj(x2, g_mix, w_all, b_all, cos_t, sin_t, wg, bg, seq)
    oa = _gla(gq, gk, gv, lg, gr, g_gla_head, bsz, seq)
    ob = _swa(sinks[0], sq, sk, sv, bsz, seq)
    x1, hn, logits_t = _merge(x2, oa, ob, ga, gb, w_gla_out[0].astype(BF16), w_swa_out[0].astype(BF16),
                              w_out[0].astype(BF16), g_ffn, w_router[0].T, b_router[0][:, None])

    e_kt, g_kt, r_kt, cnt = _route(logits_t)
    counts = cnt[:, 0].astype(I32)
    blocks_e = (counts + MOE_BLOCK - 1) // MOE_BLOCK
    bend = jnp.cumsum(blocks_e)
    pstart = (bend - blocks_e) * MOE_BLOCK
    dest_kt = r_kt
    for e in range(N_EXPERTS):
        dest_kt = dest_kt + jnp.where(e_kt == e, pstart[e], 0)
    dest_flat = dest_kt.T.reshape(t * TOP_K)
    n_blocks = (t * TOP_K) // MOE_BLOCK + N_EXPERTS
    block_e = jnp.minimum(jnp.sum(bend[None, :] <= jnp.arange(n_blocks, dtype=I32)[:, None], axis=1),
                          N_EXPERTS - 1).astype(I32)
    n_valid = bend[-1:].astype(I32)

    b1 = b_e1[0].reshape(N_EXPERTS, -1, LANES, 2).transpose(0, 1, 3, 2).reshape(N_EXPERTS, 1, -1)
    xs, w1, w2, inv = _dispatch(bend * MOE_BLOCK, pstart + counts, blocks_e, n_valid, dest_flat, hn,
                                w_e1[0], w_e2[0], _pair_split_perm(), n_blocks * MOE_BLOCK)
    n_slots = t * TOP_K
    inv_prev = jnp.concatenate([n_slots + jnp.arange(MOE_BLOCK, dtype=I32), inv[:-MOE_BLOCK]])
    y4 = _experts(block_e, n_valid, inv_prev.reshape(n_blocks, 1, MOE_BLOCK), xs, w1, b1, w2,
                  b_e2[0][:, None, :], n_slots)
    out = _final(x1, g_kt.T, g_final[None, :], y4)
    return out.reshape(bsz, seq, d)
```

```python
import jax
import jax.numpy as jnp
import numpy as np
from jax import lax
from jax.experimental import pallas as pl
from jax.experimental.pallas import tpu as pltpu

F32 = jnp.float32
BF16 = jnp.bfloat16
I32 = jnp.int32

NORM_EPS = 1e-5
GLA_HEADS = 4
GLA_DK = 128
GLA_DV = 256
GLA_GATE_RANK = 16
GLA_TAU = 16.0
SWA_Q_HEADS = 16
SWA_KV_HEADS = 2
SWA_GROUP = SWA_Q_HEADS // SWA_KV_HEADS
SWA_HEAD_DIM = 64
SWA_WINDOW = 128
ROPE_THETA = 10000.0
N_EXPERTS = 32
TOP_K = 4
SWIGLU_LIMIT = 7.0
SWIGLU_ALPHA = 1.702

LANES = 128
NEG = -1e30
VMEM_LIMIT = 56 * 1024 * 1024

PROJ_TM = 256
GLA_CHUNK = 128
GLA_HEADS_PER_STEP = 4
MERGE_TM = 512
MERGE_TN = 256
MERGE_ROW_GROUPS = 1
ROUTE_TT = 512
MOE_BLOCK = 512
DISPATCH_GROUPS = 16
FINAL_TM = 512
FINAL_SLICE = 32

_D = 1024
_C_GQ = 0
_C_GK = _C_GQ + GLA_HEADS * GLA_DK
_C_GV = _C_GK + GLA_HEADS * GLA_DK
_C_GR = _C_GV + GLA_HEADS * GLA_DV
_C_SQ = _C_GR + GLA_HEADS * GLA_DV
_C_GA = _C_SQ + SWA_Q_HEADS * SWA_HEAD_DIM
_C_GB = _C_GA + _D
_C_SK = _C_GB + _D
_C_SV = _C_SK + SWA_KV_HEADS * LANES
_C_LR = _C_SV + SWA_KV_HEADS * LANES
_C_END = _C_LR + LANES


def _cparams(sem):
    return pltpu.CompilerParams(dimension_semantics=sem, vmem_limit_bytes=VMEM_LIMIT)


def _rms(x, g):
    return x * lax.rsqrt(jnp.mean(x * x, axis=-1, keepdims=True) + NORM_EPS) * g


def _sigmoid(x):
    return 1.0 / (1.0 + jnp.exp(-x))


def _dot_nt(a, b):
    return lax.dot_general(a, b, (((1,), (1,)), ((), ())), preferred_element_type=F32)


def _rope_slabs(acc, cos, sin, first_half):
    outs = []
    for i in range(acc.shape[1] // LANES):
        xs = acc[:, LANES * i:LANES * (i + 1)]
        partner = jnp.where(first_half, pltpu.roll(xs, LANES - 32, 1), pltpu.roll(xs, 32, 1))
        outs.append(xs * cos + partner * sin)
    return jnp.concatenate(outs, axis=1)


def _in_proj_kernel(x_ref, g_ref, w_ref, b_ref, cos_ref, sin_ref, wg_ref, bg_ref,
                    gq_ref, gk_ref, gv_ref, gr_ref, sq_ref, ga_ref, gb_ref, sk_ref, sv_ref, lg_ref):
    h = _rms(x_ref[...], g_ref[...]).astype(BF16)

    def proj(lo, hi):
        return jnp.dot(h, w_ref[:, lo:hi], preferred_element_type=F32) + b_ref[:, lo:hi]

    cos = cos_ref[...]
    sin = sin_ref[...]
    lane = lax.broadcasted_iota(I32, cos.shape, 1)
    first_half = (lane % SWA_HEAD_DIM) < (SWA_HEAD_DIM // 2)

    gq_ref[...] = (proj(_C_GQ, _C_GK) * (GLA_DK ** -0.5)).astype(BF16)
    gk_ref[...] = proj(_C_GK, _C_GV).astype(BF16)
    gv_ref[...] = proj(_C_GV, _C_GR).astype(BF16)
    gr = proj(_C_GR, _C_SQ)
    gr_ref[...] = (gr * _sigmoid(gr)).astype(BF16)
    sq = proj(_C_SQ, _C_GA) * (SWA_HEAD_DIM ** -0.5)
    sq_ref[...] = _rope_slabs(sq, cos, sin, first_half).astype(BF16)
    ga_ref[...] = _sigmoid(proj(_C_GA, _C_GB)).astype(BF16)
    gb_ref[...] = _sigmoid(proj(_C_GB, _C_SK)).astype(BF16)
    sk_ref[...] = _rope_slabs(proj(_C_SK, _C_SV), cos, sin, first_half).astype(BF16)
    sv_ref[...] = proj(_C_SV, _C_LR).astype(BF16)
    z = jnp.dot(proj(_C_LR, _C_END), wg_ref[...], precision=lax.Precision.HIGHEST,
                preferred_element_type=F32) + bg_ref[...]
    log_sig = jnp.minimum(z, 0.0) - jnp.log(1.0 + jnp.exp(-jnp.abs(z)))
    lg_ref[...] = log_sig * (1.0 / GLA_TAU)


def _in_proj(x2, g_mix, w_all, b_all, cos_t, sin_t, wg, bg, seq):
    t = x2.shape[0]
    tm = PROJ_TM
    pos_blocks = seq // tm
    const = lambda i: (0, 0)
    row = lambda i: (i, 0)
    widths = [(_C_GK - _C_GQ, BF16), (_C_GV - _C_GK, BF16), (_C_GR - _C_GV, BF16), (_C_SQ - _C_GR, BF16),
              (_C_GA - _C_SQ, BF16), (_D, BF16), (_D, BF16), (_C_SV - _C_SK, BF16), (_C_LR - _C_SV, BF16),
              (GLA_HEADS * GLA_DK, F32)]
    return pl.pallas_call(
        _in_proj_kernel,
        grid=(t // tm,),
        in_specs=[
            pl.BlockSpec((tm, _D), row),
            pl.BlockSpec((1, _D), const),
            pl.BlockSpec((_D, _C_END), const, pipeline_mode=pl.Buffered(1)),
            pl.BlockSpec((1, _C_END), const),
            pl.BlockSpec((tm, LANES), lambda i: (i % pos_blocks, 0)),
            pl.BlockSpec((tm, LANES), lambda i: (i % pos_blocks, 0)),
            pl.BlockSpec((LANES, GLA_HEADS * GLA_DK), const),
            pl.BlockSpec((1, GLA_HEADS * GLA_DK), const),
        ],
        out_specs=[pl.BlockSpec((tm, w), row) for w, _ in widths],
        out_shape=[jax.ShapeDtypeStruct((t, w), dt) for w, dt in widths],
        compiler_params=_cparams(("parallel",)),
        name="in_proj",
    )(x2, g_mix, w_all, b_all, cos_t, sin_t, wg, bg)


def _gla_cumsum_operator(c_len):
    t = np.arange(c_len)[:, None]
    r = np.arange(c_len)[None, :]
    return np.tile((r <= t).astype(np.float32), (1, 3))


def _gla_kernel(q_ref, k_ref, v_ref, lg_ref, gr_ref, gh_ref, dm_ref, o_ref, st_ref):
    c_len = GLA_CHUNK
    n_lev = c_len.bit_length() - 1
    seq = q_ref.shape[0]
    st_ref[...] = jnp.zeros_like(st_ref)

    t_i = lax.broadcasted_iota(I32, (c_len, c_len), 0)
    j_i = lax.broadcasted_iota(I32, (c_len, c_len), 1)
    row = lax.broadcasted_iota(I32, (c_len, 1), 0)
    diag = t_i == j_i
    upper, pair = [], []
    for lev in range(n_lev):
        s = c_len >> (lev + 1)
        upper.append((row & s) != 0)
        pair.append(((t_i // (2 * s)) == (j_i // (2 * s))) & ((t_i & s) != 0) & ((j_i & s) == 0))

    sub8 = lax.broadcasted_iota(I32, (c_len // 8, 8, GLA_DK), 1)

    def boundary_rows(b, s):
        if s >= 4:
            b3 = b.reshape(c_len // (2 * s), 2 * s, GLA_DK)
            return jnp.broadcast_to(b3[:, s - 1:s, :], b3.shape).reshape(c_len, GLA_DK)
        b3 = b.reshape(c_len // 8, 8, GLA_DK)
        lo = jnp.broadcast_to(b3[:, 1:2, :], b3.shape)
        hi = jnp.broadcast_to(b3[:, 5:6, :], b3.shape)
        return jnp.where(sub8 < 4, lo, hi).reshape(c_len, GLA_DK)

    def head_chunk(r0, hh):
        kcols = slice(GLA_DK * hh, GLA_DK * (hh + 1))
        vcols = slice(GLA_DV * hh, GLA_DV * (hh + 1))
        q_bf = q_ref[pl.ds(r0, c_len), kcols]
        k_bf = k_ref[pl.ds(r0, c_len), kcols]
        q = q_bf.astype(F32)
        k = k_bf.astype(F32)
        v = v_ref[pl.ds(r0, c_len), vcols]

        lg = lg_ref[pl.ds(r0, c_len), kcols]
        lg_hi = lg.astype(BF16)
        rem = lg - lg_hi.astype(F32)
        lg_mid = rem.astype(BF16)
        lg_lo = (rem - lg_mid.astype(F32)).astype(BF16)
        b = jnp.dot(dm_ref[...], jnp.concatenate([lg_hi, lg_mid, lg_lo], axis=0),
                    preferred_element_type=F32)
        w_cum = jnp.exp(b)

        st = st_ref[hh]
        o = _dot_nt((q * w_cum).astype(BF16), st.astype(BF16))

        a = jnp.where(diag, _dot_nt(q_bf, k_bf), 0.0)
        for lev in range(n_lev):
            s = c_len >> (lev + 1)
            if s == 1:
                w = jnp.where(upper[lev], jnp.exp(lg), 1.0)
            else:
                w = jnp.exp(-jnp.abs(b - boundary_rows(b, s)))
            z = (jnp.where(upper[lev], q, k) * w).astype(BF16)
            a = jnp.where(pair[lev], _dot_nt(z, z), a)
        o = o + jnp.dot(a.astype(BF16), v, preferred_element_type=F32)

        b_last = b[c_len - 1:c_len, :]
        upd = lax.dot_general(v, (k * jnp.exp(b_last - b)).astype(BF16), (((0,), (0,)), ((), ())),
                              preferred_element_type=F32)
        st_ref[hh] = st * w_cum[c_len - 1:c_len, :] + upd

        on = _rms(o, gh_ref[...])
        o_ref[pl.ds(r0, c_len), vcols] = (on * gr_ref[pl.ds(r0, c_len), vcols].astype(F32)).astype(BF16)

    def chunk(c, carry):
        r0 = pl.multiple_of(c * c_len, c_len)
        for hh in range(GLA_HEADS_PER_STEP):
            head_chunk(r0, hh)
        return carry

    lax.fori_loop(0, seq // c_len, chunk, 0, unroll=2)


def _gla(gq, gk, gv, lg, gr, g_head, bsz, seq):
    t = gq.shape[0]
    hs = GLA_HEADS_PER_STEP
    dmat = jnp.asarray(_gla_cumsum_operator(GLA_CHUNK), dtype=BF16)
    return pl.pallas_call(
        _gla_kernel,
        grid=(bsz, GLA_HEADS // hs),
        in_specs=[
            pl.BlockSpec((seq, hs * GLA_DK), lambda b, h: (b, h)),
            pl.BlockSpec((seq, hs * GLA_DK), lambda b, h: (b, h)),
            pl.BlockSpec((seq, hs * GLA_DV), lambda b, h: (b, h)),
            pl.BlockSpec((seq, hs * GLA_DK), lambda b, h: (b, h)),
            pl.BlockSpec((seq, hs * GLA_DV), lambda b, h: (b, h)),
            pl.BlockSpec((1, GLA_DV), lambda b, h: (0, 0)),
            pl.BlockSpec(dmat.shape, lambda b, h: (0, 0)),
        ],
        out_specs=pl.BlockSpec((seq, hs * GLA_DV), lambda b, h: (b, h)),
        out_shape=jax.ShapeDtypeStruct((t, GLA_HEADS * GLA_DV), BF16),
        scratch_shapes=[pltpu.VMEM((hs, GLA_DV, GLA_DK), F32)],
        compiler_params=_cparams(("parallel", "parallel")),
        name="gla",
    )(gq, gk, gv, lg, gr, g_head, dmat)


def _swa_kernel(sink_ref, q_ref, k_ref, v_ref, o_ref):
    w = SWA_WINDOW
    seq = q_ref.shape[0]
    hk = pl.program_id(1)
    lane_q = lax.broadcasted_iota(I32, (w, LANES), 1)
    low_q = lane_q < SWA_HEAD_DIM
    lane_b = lax.broadcasted_iota(I32, (2 * w, LANES), 1)
    low_b = lane_b < SWA_HEAD_DIM
    qi = lax.broadcasted_iota(I32, (w, 2 * w), 0)
    kj = lax.broadcasted_iota(I32, (w, 2 * w), 1)
    in_window = (kj > qi) & (kj <= qi + w)
    zero_q = jnp.zeros((w, LANES), BF16)
    zero_b = jnp.zeros((2 * w, LANES), BF16)

    def block(n, carry):
        r0 = pl.multiple_of(n * w, w)
        p0 = pl.multiple_of(jnp.maximum(n - 1, 0) * w, w)
        kb = jnp.concatenate([k_ref[pl.ds(p0, w), :], k_ref[pl.ds(r0, w), :]], axis=0)
        vb = jnp.concatenate([v_ref[pl.ds(p0, w), :], v_ref[pl.ds(r0, w), :]], axis=0)
        valid = in_window & ((kj >= w) | (n > 0))
        v_lo = jnp.where(low_b, vb, zero_b)
        v_hi = jnp.where(low_b, zero_b, vb)
        for m in range(SWA_GROUP // 2):
            qp = q_ref[pl.ds(r0, w), LANES * m:LANES * (m + 1)]
            acc = jnp.zeros((w, LANES), F32)
            for par in range(2):
                qm = jnp.where(low_q, qp, zero_q) if par == 0 else jnp.where(low_q, zero_q, qp)
                s = jnp.where(valid, _dot_nt(qm, kb), NEG)
                sink = sink_ref[hk * SWA_GROUP + 2 * m + par]
                mx = jnp.maximum(jnp.max(s, axis=-1, keepdims=True), sink)
                p = jnp.exp(s - mx)
                den = jnp.sum(p, axis=-1, keepdims=True) + jnp.exp(sink - mx)
                pv = jnp.dot(p.astype(BF16), v_lo if par == 0 else v_hi, preferred_element_type=F32)
                acc = acc + pv / den
            o_ref[pl.ds(r0, w), LANES * m:LANES * (m + 1)] = acc.astype(BF16)
        return carry

    lax.fori_loop(0, seq // w, block, 0, unroll=2)


def _swa(sinks, sq, sk, sv, bsz, seq):
    t = sq.shape[0]
    gw = SWA_GROUP * SWA_HEAD_DIM
    return pl.pallas_call(
        _swa_kernel,
        grid_spec=pltpu.PrefetchScalarGridSpec(
            num_scalar_prefetch=1,
            grid=(bsz, SWA_KV_HEADS),
            in_specs=[
                pl.BlockSpec((seq, gw), lambda b, h, s: (b, h)),
                pl.BlockSpec((seq, LANES), lambda b, h, s: (b, h)),
                pl.BlockSpec((seq, LANES), lambda b, h, s: (b, h)),
            ],
            out_specs=pl.BlockSpec((seq, gw), lambda b, h, s: (b, h)),
        ),
        out_shape=jax.ShapeDtypeStruct((t, SWA_Q_HEADS * SWA_HEAD_DIM), BF16),
        compiler_params=_cparams(("parallel", "parallel")),
        name="swa",
    )(sinks, sq, sk, sv)


def _merge_kernel(x_ref, oa_ref, ob_ref, ga_ref, gb_ref, wa_ref, wb_ref, wo_ref, gf_ref, wr_ref, br_ref,
                  x1_ref, hn_ref, lt_ref, mixed_ref):
    rows_g = x_ref.shape[0] // MERGE_ROW_GROUPS
    for h in range(MERGE_ROW_GROUPS):
        rows = slice(rows_g * h, rows_g * (h + 1))
        oa = oa_ref[rows, :]
        ob = ob_ref[rows, :]
        for n in range(_D // MERGE_TN):
            cols = slice(MERGE_TN * n, MERGE_TN * (n + 1))
            ya = ga_ref[rows, cols].astype(F32) * jnp.dot(oa, wa_ref[:, cols], preferred_element_type=F32)
            yb = gb_ref[rows, cols].astype(F32) * jnp.dot(ob, wb_ref[:, cols], preferred_element_type=F32)
            mixed_ref[rows, cols] = (ya + yb).astype(BF16)
        mixed = mixed_ref[rows, :]
        for n in range(_D // MERGE_TN):
            cols = slice(MERGE_TN * n, MERGE_TN * (n + 1))
            x1_ref[rows, cols] = x_ref[rows, cols] + jnp.dot(mixed, wo_ref[:, cols], preferred_element_type=F32)
        hn = _rms(x1_ref[rows, :], gf_ref[...])
        hn_ref[rows, :] = hn
        lt_ref[:, rows] = lax.dot_general(wr_ref[...], hn, (((1,), (1,)), ((), ())),
                                          precision=lax.Precision.HIGHEST,
                                          preferred_element_type=F32) + br_ref[...]


def _merge(x2, oa, ob, ga, gb, wa, wb, wo, g_ffn, wr_t, br_col):
    t = x2.shape[0]
    tm = MERGE_TM
    row = lambda i: (i, 0)
    const = lambda i: (0, 0)
    return pl.pallas_call(
        _merge_kernel,
        grid=(t // tm,),
        in_specs=[pl.BlockSpec((tm, _D), row)] * 5 + [pl.BlockSpec((_D, _D), const)] * 3 + [
            pl.BlockSpec((1, _D), const),
            pl.BlockSpec((N_EXPERTS, _D), const),
            pl.BlockSpec((N_EXPERTS, 1), const),
        ],
        out_specs=[pl.BlockSpec((tm, _D), row), pl.BlockSpec((tm, _D), row),
                   pl.BlockSpec((N_EXPERTS, tm), lambda i: (0, i))],
        out_shape=[jax.ShapeDtypeStruct((t, _D), F32), jax.ShapeDtypeStruct((t, _D), F32),
                   jax.ShapeDtypeStruct((N_EXPERTS, t), F32)],
        scratch_shapes=[pltpu.VMEM((tm, _D), BF16)],
        compiler_params=_cparams(("parallel",)),
        name="merge",
    )(x2, oa, ob, ga, gb, wa, wb, wo, g_ffn, wr_t, br_col)


def _route_kernel(lt_ref, e_ref, g_ref, r_ref, cnt_ref, carry_ref):
    tt = lt_ref.shape[1]

    @pl.when(pl.program_id(0) == 0)
    def _():
        carry_ref[...] = jnp.zeros_like(carry_ref)

    eid = lax.broadcasted_iota(I32, (N_EXPERTS, tt), 0)
    work = lt_ref[...]
    vals, idxs = [], []
    chosen = jnp.zeros((N_EXPERTS, tt), F32)
    for _ in range(TOP_K):
        m = jnp.max(work, axis=0, keepdims=True)
        idx = jnp.min(jnp.where(work == m, eid, N_EXPERTS), axis=0, keepdims=True)
        hit = eid == idx
        work = jnp.where(hit, -jnp.inf, work)
        chosen = jnp.where(hit, 1.0, chosen)
        vals.append(m)
        idxs.append(idx)
    ex = [jnp.exp(v - vals[0]) for v in vals]
    den = ex[0] + ex[1] + ex[2] + ex[3]

    t_r = lax.broadcasted_iota(I32, (tt, tt), 0)
    t_c = lax.broadcasted_iota(I32, (tt, tt), 1)
    before = (t_r < t_c).astype(BF16)
    pref = jnp.dot(chosen.astype(BF16), before, preferred_element_type=F32) + carry_ref[:, 0:1]
    for k in range(TOP_K):
        e_ref[k:k + 1, :] = idxs[k]
        g_ref[k:k + 1, :] = ex[k] / den
        r_ref[k:k + 1, :] = jnp.sum(jnp.where(eid == idxs[k], pref, 0.0), axis=0, keepdims=True).astype(I32)
    total = pref[:, tt - 1:tt] + chosen[:, tt - 1:tt]
    carry_ref[...] = jnp.broadcast_to(total, carry_ref.shape)
    cnt_ref[...] = jnp.broadcast_to(total, cnt_ref.shape)


def _route(logits_t):
    t = logits_t.shape[1]
    tt = ROUTE_TT
    blk = lambda i: (0, i)
    return pl.pallas_call(
        _route_kernel,
        grid=(t // tt,),
        in_specs=[pl.BlockSpec((N_EXPERTS, tt), blk)],
        out_specs=[pl.BlockSpec((TOP_K, tt), blk), pl.BlockSpec((TOP_K, tt), blk), pl.BlockSpec((TOP_K, tt), blk),
                   pl.BlockSpec((N_EXPERTS, LANES), lambda i: (0, 0))],
        out_shape=[jax.ShapeDtypeStruct((TOP_K, t), I32), jax.ShapeDtypeStruct((TOP_K, t), F32),
                   jax.ShapeDtypeStruct((TOP_K, t), I32), jax.ShapeDtypeStruct((N_EXPERTS, LANES), F32)],
        scratch_shapes=[pltpu.VMEM((N_EXPERTS, LANES), F32)],
        compiler_params=_cparams(("arbitrary",)),
        name="route",
    )(logits_t)


def _dispatch_kernel(pend_ref, nblk_ref, nv_ref, dest_ref, hn_ref, w1_ref, w2_ref, perm_ref,
                     xs_hbm, w1o_ref, w2o_ref, zero_ref, sem, zsem):
    n_tok = hn_ref.shape[0]
    n_rows = w1_ref.shape[1]
    gw = perm_ref.shape[0]
    n_col_groups = w1_ref.shape[2] // gw
    slices = DISPATCH_GROUPS * n_col_groups
    tok_s = n_tok // slices
    rows_g = n_rows // DISPATCH_GROUPS
    w2_cols = w2_ref.shape[2] // n_col_groups
    n_blocks = xs_hbm.shape[0] // MOE_BLOCK

    def zero_block(row0):
        return pltpu.make_async_copy(zero_ref, xs_hbm.at[pl.ds(pl.multiple_of(row0, MOE_BLOCK), MOE_BLOCK)], zsem)

    @pl.when(pl.program_id(0) == 0)
    def _():
        zero_ref[...] = jnp.zeros_like(zero_ref)

        def expert_tail(e, carry, start):
            @pl.when(nblk_ref[e] > 0)
            def _():
                cp = zero_block(pend_ref[e] - MOE_BLOCK)
                cp.start() if start else cp.wait()
            return carry

        def unused_block(b, carry, start):
            cp = zero_block(b * MOE_BLOCK)
            cp.start() if start else cp.wait()
            return carry

        for start in (True, False):
            lax.fori_loop(0, N_EXPERTS, lambda e, c: expert_tail(e, c, start), 0)
            lax.fori_loop(nv_ref[0], n_blocks, lambda b, c: unused_block(b, c, start), 0)

    def group(g, carry):
        r0 = pl.multiple_of(g * rows_g, rows_g)
        for gi in range(n_col_groups):
            t0 = (g * n_col_groups + gi) * tok_s
            for i in range(tok_s):
                for k in range(TOP_K):
                    d = dest_ref[0, 0, (t0 + i) * TOP_K + k]
                    pltpu.make_async_copy(hn_ref.at[pl.ds(t0 + i, 1)], xs_hbm.at[pl.ds(d, 1)], sem).start()
            wb = w1_ref[0, pl.ds(r0, rows_g), gw * gi:gw * (gi + 1)].astype(BF16)
            w1o_ref[0, pl.ds(r0, rows_g), gw * gi:gw * (gi + 1)] = jnp.dot(
                wb, perm_ref[...], preferred_element_type=F32).astype(BF16)
            w2o_ref[0, pl.ds(r0, rows_g), w2_cols * gi:w2_cols * (gi + 1)] = w2_ref[
                0, pl.ds(r0, rows_g), w2_cols * gi:w2_cols * (gi + 1)].astype(BF16)
        return carry

    lax.fori_loop(0, DISPATCH_GROUPS, group, 0)
    for k in range(TOP_K):
        pltpu.make_async_copy(hn_ref, xs_hbm.at[pl.ds(0, n_tok)], sem).wait()


def _dispatch(pend_rows, blocks_e, n_valid, dest_flat, hn, w1, w2, perm, rows):
    t = hn.shape[0]
    n_e, d, n1 = w1.shape
    dff = w2.shape[1]
    tm = t // n_e
    assert dff == d and tm % (DISPATCH_GROUPS * (n1 // perm.shape[0])) == 0
    emap = lambda e, *_: (e, 0, 0)
    return pl.pallas_call(
        _dispatch_kernel,
        grid_spec=pltpu.PrefetchScalarGridSpec(
            num_scalar_prefetch=3,
            grid=(n_e,),
            in_specs=[pl.BlockSpec((1, 1, tm * TOP_K), emap, memory_space=pltpu.SMEM),
                      pl.BlockSpec((tm, _D), lambda e, *_: (e, 0)),
                      pl.BlockSpec((1, d, n1), emap),
                      pl.BlockSpec((1, dff, _D), emap),
                      pl.BlockSpec(perm.shape, lambda e, *_: (0, 0))],
            out_specs=[pl.BlockSpec(memory_space=pl.ANY),
                       pl.BlockSpec((1, d, n1), emap),
                       pl.BlockSpec((1, dff, _D), emap)],
            scratch_shapes=[pltpu.VMEM((MOE_BLOCK, _D), F32), pltpu.SemaphoreType.DMA, pltpu.SemaphoreType.DMA],
        ),
        out_shape=[jax.ShapeDtypeStruct((rows, _D), F32),
                   jax.ShapeDtypeStruct((n_e, d, n1), BF16),
                   jax.ShapeDtypeStruct((n_e, dff, _D), BF16)],
        compiler_params=_cparams(("arbitrary",)),
        name="dispatch",
    )(pend_rows, blocks_e, n_valid, dest_flat.reshape(n_e, 1, tm * TOP_K), hn, w1, w2, perm)


def _experts_kernel(be_ref, nv_ref, xs_ref, w1_ref, b1_ref, w2_ref, b2_ref, ys_ref):
    del be_ref

    @pl.when(pl.program_id(0) < nv_ref[0])
    def _():
        x = xs_ref[...].astype(BF16)
        u = jnp.dot(x, w1_ref[0], preferred_element_type=F32) + b1_ref[0]
        acts = []
        for gi in range(u.shape[1] // (2 * LANES)):
            g = jnp.minimum(u[:, 2 * LANES * gi:2 * LANES * gi + LANES], SWIGLU_LIMIT)
            lin = jnp.clip(u[:, 2 * LANES * gi + LANES:2 * LANES * (gi + 1)], -SWIGLU_LIMIT, SWIGLU_LIMIT)
            acts.append((g * _sigmoid(SWIGLU_ALPHA * g) * (lin + 1.0)).astype(BF16))
        act = jnp.concatenate(acts, axis=1)
        ys_ref[...] = jnp.dot(act, w2_ref[0], preferred_element_type=F32) + b2_ref[0]

    @pl.when(pl.program_id(0) >= nv_ref[0])
    def _():
        ys_ref[...] = jnp.zeros_like(ys_ref)


def _experts(block_e, n_valid, xs, w1, b1, w2, b2):
    rows = xs.shape[0]
    n_blocks = rows // MOE_BLOCK
    dff = w2.shape[1]
    wmap = lambda i, be, nv: (be[i], 0, 0)
    rmap = lambda i, be, nv: (i, 0)
    return pl.pallas_call(
        _experts_kernel,
        grid_spec=pltpu.PrefetchScalarGridSpec(
            num_scalar_prefetch=2,
            grid=(n_blocks,),
            in_specs=[
                pl.BlockSpec((MOE_BLOCK, _D), rmap),
                pl.BlockSpec((1, _D, 2 * dff), wmap),
                pl.BlockSpec((1, 1, 2 * dff), wmap),
                pl.BlockSpec((1, dff, _D), wmap),
                pl.BlockSpec((1, 1, _D), wmap),
            ],
            out_specs=pl.BlockSpec((MOE_BLOCK, _D), rmap),
        ),
        out_shape=jax.ShapeDtypeStruct((rows, _D), F32),
        compiler_params=_cparams(("arbitrary",)),
        name="experts",
    )(block_e, n_valid, xs, w1, b1, w2, b2)


def _final_kernel(dcur_ref, dnext_ref, x1_ref, gt_ref, gf_ref, ys_hbm, o_ref, buf0_ref, buf1_ref, sems):
    n_tok = x1_ref.shape[0]
    rs = FINAL_SLICE
    step = pl.program_id(0)
    last = pl.num_programs(0) - 1
    bufs = (buf0_ref, buf1_ref)

    def issue_rows(d_ref, slot, t0):
        for i in range(rs):
            for k in range(TOP_K):
                d = d_ref[0, 0, (t0 + i) * TOP_K + k]
                pltpu.make_async_copy(ys_hbm.at[pl.ds(d, 1)], bufs[slot].at[k, pl.ds(t0 + i, 1)],
                                      sems.at[slot]).start()

    def compute_rows(slot, r0):
        rows = pl.ds(r0, rs)
        gates = gt_ref[rows, :]
        y = x1_ref[rows, :]
        for k in range(TOP_K):
            y = y + gates[:, k:k + 1] * bufs[slot][k, rows, :]
        o_ref[rows, :] = _rms(y, gf_ref[...])

    def prologue(s, carry):
        issue_rows(dcur_ref, 0, pl.multiple_of(s * rs, rs))
        return carry

    @pl.when(step == 0)
    def _():
        lax.fori_loop(0, n_tok // rs, prologue, 0)

    for slot in (0, 1):
        def overlapped(s, carry, slot=slot):
            r0 = pl.multiple_of(s * rs, rs)
            issue_rows(dnext_ref, 1 - slot, r0)
            compute_rows(slot, r0)
            return carry

        def epilogue(s, carry, slot=slot):
            compute_rows(slot, pl.multiple_of(s * rs, rs))
            return carry

        @pl.when(step % 2 == slot)
        def _(slot=slot, overlapped=overlapped, epilogue=epilogue):
            for k in range(TOP_K):
                pltpu.make_async_copy(ys_hbm.at[pl.ds(0, n_tok)], bufs[slot].at[k], sems.at[slot]).wait()

            @pl.when(step < last)
            def _():
                lax.fori_loop(0, n_tok // rs, overlapped, 0)

            @pl.when(step == last)
            def _():
                lax.fori_loop(0, n_tok // rs, epilogue, 0)


def _final(dest_flat, x1, gates_tk, g_final, ys):
    t = x1.shape[0]
    tm = FINAL_TM
    steps = t // tm
    row = lambda i: (i, 0)
    dest3 = dest_flat.reshape(steps, 1, tm * TOP_K)
    return pl.pallas_call(
        _final_kernel,
        grid=(steps,),
        in_specs=[pl.BlockSpec((1, 1, tm * TOP_K), lambda i: (i, 0, 0), memory_space=pltpu.SMEM),
                  pl.BlockSpec((1, 1, tm * TOP_K), lambda i: (jnp.minimum(i + 1, steps - 1), 0, 0),
                               memory_space=pltpu.SMEM),
                  pl.BlockSpec((tm, _D), row), pl.BlockSpec((tm, TOP_K), row),
                  pl.BlockSpec((1, _D), lambda i: (0, 0)), pl.BlockSpec(memory_space=pl.ANY)],
        out_specs=pl.BlockSpec((tm, _D), row),
        out_shape=jax.ShapeDtypeStruct((t, _D), F32),
        scratch_shapes=[pltpu.VMEM((TOP_K, tm, _D), F32), pltpu.VMEM((TOP_K, tm, _D), F32),
                        pltpu.SemaphoreType.DMA((2,))],
        compiler_params=_cparams(("arbitrary",)),
        name="final",
    )(dest3, dest3, x1, gates_tk, g_final, ys)


def _prep_in_proj(w_in, b_in, w_gate, b_gate):
    sizes = (GLA_HEADS * GLA_DK, GLA_HEADS * GLA_DK, GLA_HEADS * GLA_DV, GLA_HEADS * GLA_DV, GLA_GATE_RANK,
             SWA_Q_HEADS * SWA_HEAD_DIM, SWA_KV_HEADS * SWA_HEAD_DIM, SWA_KV_HEADS * SWA_HEAD_DIM, _D, _D)
    offs = [0]
    for s in sizes:
        offs.append(offs[-1] + s)

    def rearrange(m, dtype):
        p = [m[..., offs[i]:offs[i + 1]].astype(dtype) for i in range(len(sizes))]
        gq, gk, gv, gr, lr, sq, sk, sv, ga, gb = p

        def dup_heads(a):
            hs = [a[..., SWA_HEAD_DIM * h:SWA_HEAD_DIM * (h + 1)] for h in range(SWA_KV_HEADS)]
            return jnp.concatenate([hh for h in hs for hh in (h, h)], axis=-1)

        lr_pad = jnp.pad(lr, [(0, 0)] * (lr.ndim - 1) + [(0, LANES - GLA_GATE_RANK)])
        return jnp.concatenate([gq, gk, gv, gr, sq, ga, gb, dup_heads(sk), dup_heads(sv), lr_pad], axis=-1)

    w_all = rearrange(w_in, BF16)
    b_all = rearrange(b_in[None, :], F32)
    wg = jnp.pad(w_gate, ((0, LANES - GLA_GATE_RANK), (0, 0)))
    return w_all, b_all, wg, b_gate[None, :]


def _rope_tables(seq):
    half = SWA_HEAD_DIM // 2
    inv_freq = ROPE_THETA ** (-jnp.arange(half, dtype=F32) / half)
    ang = jnp.arange(seq, dtype=F32)[:, None] * inv_freq[None, :]
    cos, sin = jnp.cos(ang), jnp.sin(ang)
    cos_t = jnp.concatenate([cos, cos] * (LANES // SWA_HEAD_DIM), axis=1)
    sin_t = jnp.concatenate([-sin, sin] * (LANES // SWA_HEAD_DIM), axis=1)
    return cos_t, sin_t


def _pair_split_perm():
    src = jnp.arange(2 * LANES, dtype=I32)
    dst = jnp.where(src % 2 == 0, src // 2, LANES + src // 2)
    return (dst[:, None] == jnp.arange(2 * LANES, dtype=I32)[None, :]).astype(BF16)


def kernel(x, g_mix, w_in, b_in, w_gla_gate, b_gla_gate, g_gla_head, w_gla_out, sinks, w_swa_out, w_out,
           g_ffn, w_router, b_router, w_e1, b_e1, w_e2, b_e2, g_final):
    bsz, seq, d = x.shape
    assert d == _D and w_in.shape[0] == 1, "single-layer, d_model=1024 only"
    assert seq % max(PROJ_TM, SWA_WINDOW, GLA_CHUNK) == 0
    t = bsz * seq
    assert t % max(MERGE_TM, ROUTE_TT, FINAL_TM) == 0 and t % N_EXPERTS == 0
    x2 = x.reshape(t, d)

    w_all, b_all, wg, bg = _prep_in_proj(w_in[0], b_in[0], w_gla_gate[0], b_gla_gate[0])
    cos_t, sin_t = _rope_tables(seq)
    gq, gk, gv, gr, sq, ga, gb, sk, sv, lg = _in_proj(x2, g_mix, w_all, b_all, cos_t, sin_t, wg, bg, seq)
    oa = _gla(gq, gk, gv, lg, gr, g_gla_head, bsz, seq)
    ob = _swa(sinks[0], sq, sk, sv, bsz, seq)
    x1, hn, logits_t = _merge(x2, oa, ob, ga, gb, w_gla_out[0].astype(BF16), w_swa_out[0].astype(BF16),
                              w_out[0].astype(BF16), g_ffn, w_router[0].T, b_router[0][:, None])

    e_kt, g_kt, r_kt, cnt = _route(logits_t)
    counts = cnt[:, 0].astype(I32)
    blocks_e = (counts + MOE_BLOCK - 1) // MOE_BLOCK
    bend = jnp.cumsum(blocks_e)
    pstart = (bend - blocks_e) * MOE_BLOCK
    dest_kt = r_kt
    for e in range(N_EXPERTS):
        dest_kt = dest_kt + jnp.where(e_kt == e, pstart[e], 0)
    dest_flat = dest_kt.T.reshape(t * TOP_K)
    n_blocks = (t * TOP_K) // MOE_BLOCK + N_EXPERTS
    block_e = jnp.minimum(jnp.sum(bend[None, :] <= jnp.arange(n_blocks, dtype=I32)[:, None], axis=1),
                          N_EXPERTS - 1).astype(I32)
    n_valid = bend[-1:].astype(I32)

    b1 = b_e1[0].reshape(N_EXPERTS, -1, LANES, 2).transpose(0, 1, 3, 2).reshape(N_EXPERTS, 1, -1)
    xs, w1, w2 = _dispatch(bend * MOE_BLOCK, blocks_e, n_valid, dest_flat, hn, w_e1[0], w_e2[0],
                           _pair_split_perm(), n_blocks * MOE_BLOCK)
    ys = _experts(block_e, n_valid, xs, w1, b1, w2, b_e2[0][:, None, :])
    out = _final(dest_flat, x1, g_kt.T, g_final[None, :], ys)
    return out.reshape(bsz, seq, d)
```

```python
import jax
import jax.numpy as jnp
import numpy as np
from jax import lax
from jax.experimental import pallas as pl
from jax.experimental.pallas import tpu as pltpu

F32 = jnp.float32
BF16 = jnp.bfloat16
I32 = jnp.int32

NORM_EPS = 1e-5
GLA_HEADS = 4
GLA_DK = 128
GLA_DV = 256
GLA_GATE_RANK = 16
GLA_TAU = 16.0
SWA_Q_HEADS = 16
SWA_KV_HEADS = 2
SWA_GROUP = SWA_Q_HEADS // SWA_KV_HEADS
SWA_HEAD_DIM = 64
SWA_WINDOW = 128
ROPE_THETA = 10000.0
N_EXPERTS = 32
TOP_K = 4
SWIGLU_LIMIT = 7.0
SWIGLU_ALPHA = 1.702

LANES = 128
SUBLANES = 8
NEG = -1e30
VMEM_LIMIT = 56 * 1024 * 1024

PROJ_TM = 256
GLA_CHUNK = 128
GLA_HEADS_PER_STEP = 4
MERGE_TM = 512
MERGE_TN = 256
MERGE_ROW_GROUPS = 1
ROUTE_TT = 512
MOE_BLOCK = 512
DISPATCH_GROUPS = 16
FINAL_TM = 512
FINAL_KC = 512

_D = 1024
_C_GQ = 0
_C_GK = _C_GQ + GLA_HEADS * GLA_DK
_C_GV = _C_GK + GLA_HEADS * GLA_DK
_C_GR = _C_GV + GLA_HEADS * GLA_DV
_C_SQ = _C_GR + GLA_HEADS * GLA_DV
_C_GA = _C_SQ + SWA_Q_HEADS * SWA_HEAD_DIM
_C_GB = _C_GA + _D
_C_SK = _C_GB + _D
_C_SV = _C_SK + SWA_KV_HEADS * LANES
_C_LR = _C_SV + SWA_KV_HEADS * LANES
_C_END = _C_LR + LANES


def _cparams(sem):
    return pltpu.CompilerParams(dimension_semantics=sem, vmem_limit_bytes=VMEM_LIMIT)


def _rms(x, g):
    return x * lax.rsqrt(jnp.mean(x * x, axis=-1, keepdims=True) + NORM_EPS) * g


def _sigmoid(x):
    return 1.0 / (1.0 + jnp.exp(-x))


def _dot_nt(a, b):
    return lax.dot_general(a, b, (((1,), (1,)), ((), ())), preferred_element_type=F32)


def _rope_slabs(acc, cos, sin, first_half):
    outs = []
    for i in range(acc.shape[1] // LANES):
        xs = acc[:, LANES * i:LANES * (i + 1)]
        partner = jnp.where(first_half, pltpu.roll(xs, LANES - 32, 1), pltpu.roll(xs, 32, 1))
        outs.append(xs * cos + partner * sin)
    return jnp.concatenate(outs, axis=1)


def _in_proj_kernel(x_ref, g_ref, w_ref, b_ref, cos_ref, sin_ref, wg_ref, bg_ref,
                    gq_ref, gk_ref, gv_ref, gr_ref, sq_ref, ga_ref, gb_ref, sk_ref, sv_ref, lg_ref):
    h = _rms(x_ref[...], g_ref[...]).astype(BF16)

    def proj(lo, hi):
        return jnp.dot(h, w_ref[:, lo:hi], preferred_element_type=F32) + b_ref[:, lo:hi]

    cos = cos_ref[...]
    sin = sin_ref[...]
    lane = lax.broadcasted_iota(I32, cos.shape, 1)
    first_half = (lane % SWA_HEAD_DIM) < (SWA_HEAD_DIM // 2)

    gq_ref[...] = (proj(_C_GQ, _C_GK) * (GLA_DK ** -0.5)).astype(BF16)
    gk_ref[...] = proj(_C_GK, _C_GV).astype(BF16)
    gv_ref[...] = proj(_C_GV, _C_GR).astype(BF16)
    gr = proj(_C_GR, _C_SQ)
    gr_ref[...] = (gr * _sigmoid(gr)).astype(BF16)
    sq = proj(_C_SQ, _C_GA) * (SWA_HEAD_DIM ** -0.5)
    sq_ref[...] = _rope_slabs(sq, cos, sin, first_half).astype(BF16)
    ga_ref[...] = _sigmoid(proj(_C_GA, _C_GB)).astype(BF16)
    gb_ref[...] = _sigmoid(proj(_C_GB, _C_SK)).astype(BF16)
    sk_ref[...] = _rope_slabs(proj(_C_SK, _C_SV), cos, sin, first_half).astype(BF16)
    sv_ref[...] = proj(_C_SV, _C_LR).astype(BF16)
    z = jnp.dot(proj(_C_LR, _C_END), wg_ref[...], precision=lax.Precision.HIGHEST,
                preferred_element_type=F32) + bg_ref[...]
    log_sig = jnp.minimum(z, 0.0) - jnp.log(1.0 + jnp.exp(-jnp.abs(z)))
    lg_ref[...] = log_sig * (1.0 / GLA_TAU)


def _in_proj(x2, g_mix, w_all, b_all, cos_t, sin_t, wg, bg, seq):
    t = x2.shape[0]
    tm = PROJ_TM
    pos_blocks = seq // tm
    const = lambda i: (0, 0)
    row = lambda i: (i, 0)
    widths = [(_C_GK - _C_GQ, BF16), (_C_GV - _C_GK, BF16), (_C_GR - _C_GV, BF16), (_C_SQ - _C_GR, BF16),
              (_C_GA - _C_SQ, BF16), (_D, BF16), (_D, BF16), (_C_SV - _C_SK, BF16), (_C_LR - _C_SV, BF16),
              (GLA_HEADS * GLA_DK, F32)]
    return pl.pallas_call(
        _in_proj_kernel,
        grid=(t // tm,),
        in_specs=[
            pl.BlockSpec((tm, _D), row),
            pl.BlockSpec((1, _D), const),
            pl.BlockSpec((_D, _C_END), const, pipeline_mode=pl.Buffered(1)),
            pl.BlockSpec((1, _C_END), const),
            pl.BlockSpec((tm, LANES), lambda i: (i % pos_blocks, 0)),
            pl.BlockSpec((tm, LANES), lambda i: (i % pos_blocks, 0)),
            pl.BlockSpec((LANES, GLA_HEADS * GLA_DK), const),
            pl.BlockSpec((1, GLA_HEADS * GLA_DK), const),
        ],
        out_specs=[pl.BlockSpec((tm, w), row) for w, _ in widths],
        out_shape=[jax.ShapeDtypeStruct((t, w), dt) for w, dt in widths],
        compiler_params=_cparams(("parallel",)),
        name="in_proj",
    )(x2, g_mix, w_all, b_all, cos_t, sin_t, wg, bg)


def _gla_cumsum_operator(c_len):
    t = np.arange(c_len)[:, None]
    r = np.arange(c_len)[None, :]
    return np.tile((r <= t).astype(np.float32), (1, 3))


def _gla_kernel(q_ref, k_ref, v_ref, lg_ref, gr_ref, gh_ref, dm_ref, o_ref, st_ref):
    c_len = GLA_CHUNK
    n_lev = c_len.bit_length() - 1
    seq = q_ref.shape[0]
    st_ref[...] = jnp.zeros_like(st_ref)

    t_i = lax.broadcasted_iota(I32, (c_len, c_len), 0)
    j_i = lax.broadcasted_iota(I32, (c_len, c_len), 1)
    row = lax.broadcasted_iota(I32, (c_len, 1), 0)
    diag = t_i == j_i
    upper, pair = [], []
    for lev in range(n_lev):
        s = c_len >> (lev + 1)
        upper.append((row & s) != 0)
        pair.append(((t_i // (2 * s)) == (j_i // (2 * s))) & ((t_i & s) != 0) & ((j_i & s) == 0))

    sub8 = lax.broadcasted_iota(I32, (c_len // 8, 8, GLA_DK), 1)

    def boundary_rows(b, s):
        if s >= 4:
            b3 = b.reshape(c_len // (2 * s), 2 * s, GLA_DK)
            return jnp.broadcast_to(b3[:, s - 1:s, :], b3.shape).reshape(c_len, GLA_DK)
        b3 = b.reshape(c_len // 8, 8, GLA_DK)
        lo = jnp.broadcast_to(b3[:, 1:2, :], b3.shape)
        hi = jnp.broadcast_to(b3[:, 5:6, :], b3.shape)
        return jnp.where(sub8 < 4, lo, hi).reshape(c_len, GLA_DK)

    def head_chunk(r0, hh):
        kcols = slice(GLA_DK * hh, GLA_DK * (hh + 1))
        vcols = slice(GLA_DV * hh, GLA_DV * (hh + 1))
        q_bf = q_ref[pl.ds(r0, c_len), kcols]
        k_bf = k_ref[pl.ds(r0, c_len), kcols]
        q = q_bf.astype(F32)
        k = k_bf.astype(F32)
        v = v_ref[pl.ds(r0, c_len), vcols]

        lg = lg_ref[pl.ds(r0, c_len), kcols]
        lg_hi = lg.astype(BF16)
        rem = lg - lg_hi.astype(F32)
        lg_mid = rem.astype(BF16)
        lg_lo = (rem - lg_mid.astype(F32)).astype(BF16)
        b = jnp.dot(dm_ref[...], jnp.concatenate([lg_hi, lg_mid, lg_lo], axis=0),
                    preferred_element_type=F32)
        w_cum = jnp.exp(b)

        st = st_ref[hh]
        o = _dot_nt((q * w_cum).astype(BF16), st.astype(BF16))

        a = jnp.where(diag, _dot_nt(q_bf, k_bf), 0.0)
        for lev in range(n_lev):
            s = c_len >> (lev + 1)
            if s == 1:
                w = jnp.where(upper[lev], jnp.exp(lg), 1.0)
            else:
                w = jnp.exp(-jnp.abs(b - boundary_rows(b, s)))
            z = (jnp.where(upper[lev], q, k) * w).astype(BF16)
            a = jnp.where(pair[lev], _dot_nt(z, z), a)
        o = o + jnp.dot(a.astype(BF16), v, preferred_element_type=F32)

        b_last = b[c_len - 1:c_len, :]
        upd = lax.dot_general(v, (k * jnp.exp(b_last - b)).astype(BF16), (((0,), (0,)), ((), ())),
                              preferred_element_type=F32)
        st_ref[hh] = st * w_cum[c_len - 1:c_len, :] + upd

        on = _rms(o, gh_ref[...])
        o_ref[pl.ds(r0, c_len), vcols] = (on * gr_ref[pl.ds(r0, c_len), vcols].astype(F32)).astype(BF16)

    def chunk(c, carry):
        r0 = pl.multiple_of(c * c_len, c_len)
        for hh in range(GLA_HEADS_PER_STEP):
            head_chunk(r0, hh)
        return carry

    lax.fori_loop(0, seq // c_len, chunk, 0, unroll=2)


def _gla(gq, gk, gv, lg, gr, g_head, bsz, seq):
    t = gq.shape[0]
    hs = GLA_HEADS_PER_STEP
    dmat = jnp.asarray(_gla_cumsum_operator(GLA_CHUNK), dtype=BF16)
    return pl.pallas_call(
        _gla_kernel,
        grid=(bsz, GLA_HEADS // hs),
        in_specs=[
            pl.BlockSpec((seq, hs * GLA_DK), lambda b, h: (b, h)),
            pl.BlockSpec((seq, hs * GLA_DK), lambda b, h: (b, h)),
            pl.BlockSpec((seq, hs * GLA_DV), lambda b, h: (b, h)),
            pl.BlockSpec((seq, hs * GLA_DK), lambda b, h: (b, h)),
            pl.BlockSpec((seq, hs * GLA_DV), lambda b, h: (b, h)),
            pl.BlockSpec((1, GLA_DV), lambda b, h: (0, 0)),
            pl.BlockSpec(dmat.shape, lambda b, h: (0, 0)),
        ],
        out_specs=pl.BlockSpec((seq, hs * GLA_DV), lambda b, h: (b, h)),
        out_shape=jax.ShapeDtypeStruct((t, GLA_HEADS * GLA_DV), BF16),
        scratch_shapes=[pltpu.VMEM((hs, GLA_DV, GLA_DK), F32)],
        compiler_params=_cparams(("parallel", "parallel")),
        name="gla",
    )(gq, gk, gv, lg, gr, g_head, dmat)


def _swa_kernel(sink_ref, q_ref, k_ref, v_ref, o_ref):
    w = SWA_WINDOW
    seq = q_ref.shape[0]
    hk = pl.program_id(1)
    lane_q = lax.broadcasted_iota(I32, (w, LANES), 1)
    low_q = lane_q < SWA_HEAD_DIM
    lane_b = lax.broadcasted_iota(I32, (2 * w, LANES), 1)
    low_b = lane_b < SWA_HEAD_DIM
    qi = lax.broadcasted_iota(I32, (w, 2 * w), 0)
    kj = lax.broadcasted_iota(I32, (w, 2 * w), 1)
    in_window = (kj > qi) & (kj <= qi + w)
    zero_q = jnp.zeros((w, LANES), BF16)
    zero_b = jnp.zeros((2 * w, LANES), BF16)

    def block(n, carry):
        r0 = pl.multiple_of(n * w, w)
        p0 = pl.multiple_of(jnp.maximum(n - 1, 0) * w, w)
        kb = jnp.concatenate([k_ref[pl.ds(p0, w), :], k_ref[pl.ds(r0, w), :]], axis=0)
        vb = jnp.concatenate([v_ref[pl.ds(p0, w), :], v_ref[pl.ds(r0, w), :]], axis=0)
        valid = in_window & ((kj >= w) | (n > 0))
        v_lo = jnp.where(low_b, vb, zero_b)
        v_hi = jnp.where(low_b, zero_b, vb)
        for m in range(SWA_GROUP // 2):
            qp = q_ref[pl.ds(r0, w), LANES * m:LANES * (m + 1)]
            acc = jnp.zeros((w, LANES), F32)
            for par in range(2):
                qm = jnp.where(low_q, qp, zero_q) if par == 0 else jnp.where(low_q, zero_q, qp)
                s = jnp.where(valid, _dot_nt(qm, kb), NEG)
                sink = sink_ref[hk * SWA_GROUP + 2 * m + par]
                mx = jnp.maximum(jnp.max(s, axis=-1, keepdims=True), sink)
                p = jnp.exp(s - mx)
                den = jnp.sum(p, axis=-1, keepdims=True) + jnp.exp(sink - mx)
                pv = jnp.dot(p.astype(BF16), v_lo if par == 0 else v_hi, preferred_element_type=F32)
                acc = acc + pv / den
            o_ref[pl.ds(r0, w), LANES * m:LANES * (m + 1)] = acc.astype(BF16)
        return carry

    lax.fori_loop(0, seq // w, block, 0, unroll=2)


def _swa(sinks, sq, sk, sv, bsz, seq):
    t = sq.shape[0]
    gw = SWA_GROUP * SWA_HEAD_DIM
    return pl.pallas_call(
        _swa_kernel,
        grid_spec=pltpu.PrefetchScalarGridSpec(
            num_scalar_prefetch=1,
            grid=(bsz, SWA_KV_HEADS),
            in_specs=[
                pl.BlockSpec((seq, gw), lambda b, h, s: (b, h)),
                pl.BlockSpec((seq, LANES), lambda b, h, s: (b, h)),
                pl.BlockSpec((seq, LANES), lambda b, h, s: (b, h)),
            ],
            out_specs=pl.BlockSpec((seq, gw), lambda b, h, s: (b, h)),
        ),
        out_shape=jax.ShapeDtypeStruct((t, SWA_Q_HEADS * SWA_HEAD_DIM), BF16),
        compiler_params=_cparams(("parallel", "parallel")),
        name="swa",
    )(sinks, sq, sk, sv)


def _merge_kernel(x_ref, oa_ref, ob_ref, ga_ref, gb_ref, wa_ref, wb_ref, wo_ref, gf_ref, wr_ref, br_ref,
                  x1_ref, hn_ref, lt_ref, mixed_ref):
    rows_g = x_ref.shape[0] // MERGE_ROW_GROUPS
    for h in range(MERGE_ROW_GROUPS):
        rows = slice(rows_g * h, rows_g * (h + 1))
        oa = oa_ref[rows, :]
        ob = ob_ref[rows, :]
        for n in range(_D // MERGE_TN):
            cols = slice(MERGE_TN * n, MERGE_TN * (n + 1))
            ya = ga_ref[rows, cols].astype(F32) * jnp.dot(oa, wa_ref[:, cols], preferred_element_type=F32)
            yb = gb_ref[rows, cols].astype(F32) * jnp.dot(ob, wb_ref[:, cols], preferred_element_type=F32)
            mixed_ref[rows, cols] = (ya + yb).astype(BF16)
        mixed = mixed_ref[rows, :]
        for n in range(_D // MERGE_TN):
            cols = slice(MERGE_TN * n, MERGE_TN * (n + 1))
            x1_ref[rows, cols] = x_ref[rows, cols] + jnp.dot(mixed, wo_ref[:, cols], preferred_element_type=F32)
        hn = _rms(x1_ref[rows, :], gf_ref[...])
        hn_ref[rows, :] = hn
        lt_ref[:, rows] = lax.dot_general(wr_ref[...], hn, (((1,), (1,)), ((), ())),
                                          precision=lax.Precision.HIGHEST,
                                          preferred_element_type=F32) + br_ref[...]


def _merge(x2, oa, ob, ga, gb, wa, wb, wo, g_ffn, wr_t, br_col):
    t = x2.shape[0]
    tm = MERGE_TM
    row = lambda i: (i, 0)
    const = lambda i: (0, 0)
    return pl.pallas_call(
        _merge_kernel,
        grid=(t // tm,),
        in_specs=[pl.BlockSpec((tm, _D), row)] * 5 + [pl.BlockSpec((_D, _D), const)] * 3 + [
            pl.BlockSpec((1, _D), const),
            pl.BlockSpec((N_EXPERTS, _D), const),
            pl.BlockSpec((N_EXPERTS, 1), const),
        ],
        out_specs=[pl.BlockSpec((tm, _D), row), pl.BlockSpec((tm, _D), row),
                   pl.BlockSpec((N_EXPERTS, tm), lambda i: (0, i))],
        out_shape=[jax.ShapeDtypeStruct((t, _D), F32), jax.ShapeDtypeStruct((t, _D), F32),
                   jax.ShapeDtypeStruct((N_EXPERTS, t), F32)],
        scratch_shapes=[pltpu.VMEM((tm, _D), BF16)],
        compiler_params=_cparams(("parallel",)),
        name="merge",
    )(x2, oa, ob, ga, gb, wa, wb, wo, g_ffn, wr_t, br_col)


def _route_kernel(lt_ref, e_ref, g_ref, r_ref, cum_ref, carry_ref):
    tt = lt_ref.shape[1]

    @pl.when(pl.program_id(0) == 0)
    def _():
        carry_ref[...] = jnp.zeros_like(carry_ref)
        cum_ref[...] = jnp.zeros_like(cum_ref)

    eid = lax.broadcasted_iota(I32, (N_EXPERTS, tt), 0)
    work = lt_ref[...]
    vals, idxs = [], []
    chosen = jnp.zeros((N_EXPERTS, tt), F32)
    for _ in range(TOP_K):
        m = jnp.max(work, axis=0, keepdims=True)
        idx = jnp.min(jnp.where(work == m, eid, N_EXPERTS), axis=0, keepdims=True)
        hit = eid == idx
        work = jnp.where(hit, -jnp.inf, work)
        chosen = jnp.where(hit, 1.0, chosen)
        vals.append(m)
        idxs.append(idx)
    ex = [jnp.exp(v - vals[0]) for v in vals]
    den = ex[0] + ex[1] + ex[2] + ex[3]

    t_r = lax.broadcasted_iota(I32, (tt, tt), 0)
    t_c = lax.broadcasted_iota(I32, (tt, tt), 1)
    before = (t_r < t_c).astype(BF16)
    pref = jnp.dot(chosen.astype(BF16), before, preferred_element_type=F32) + carry_ref[:, 0:1]
    for k in range(TOP_K):
        e_ref[k:k + 1, :] = idxs[k]
        g_ref[k:k + 1, :] = ex[k] / den
        r_ref[k:k + 1, :] = jnp.sum(jnp.where(eid == idxs[k], pref, 0.0), axis=0, keepdims=True).astype(I32)
    total = pref[:, tt - 1:tt] + chosen[:, tt - 1:tt]
    carry_ref[...] = jnp.broadcast_to(total, carry_ref.shape)
    tile_lane = lax.broadcasted_iota(I32, cum_ref.shape, 1) == pl.program_id(0)
    cum_ref[...] = jnp.where(tile_lane, total, cum_ref[...])


def _route(logits_t):
    t = logits_t.shape[1]
    tt = ROUTE_TT
    blk = lambda i: (0, i)
    return pl.pallas_call(
        _route_kernel,
        grid=(t // tt,),
        in_specs=[pl.BlockSpec((N_EXPERTS, tt), blk)],
        out_specs=[pl.BlockSpec((TOP_K, tt), blk), pl.BlockSpec((TOP_K, tt), blk), pl.BlockSpec((TOP_K, tt), blk),
                   pl.BlockSpec((N_EXPERTS, LANES), lambda i: (0, 0))],
        out_shape=[jax.ShapeDtypeStruct((TOP_K, t), I32), jax.ShapeDtypeStruct((TOP_K, t), F32),
                   jax.ShapeDtypeStruct((TOP_K, t), I32), jax.ShapeDtypeStruct((N_EXPERTS, LANES), F32)],
        scratch_shapes=[pltpu.VMEM((N_EXPERTS, LANES), F32)],
        compiler_params=_cparams(("arbitrary",)),
        name="route",
    )(logits_t)


def _dispatch_kernel(pend_ref, nblk_ref, nv_ref, dest_ref, hn_ref, w1_ref, w2_ref, perm_ref,
                     xs_hbm, w1o_ref, w2o_ref, zero_ref, sem, zsem):
    n_tok = hn_ref.shape[0]
    n_rows = w1_ref.shape[1]
    gw = perm_ref.shape[0]
    n_col_groups = w1_ref.shape[2] // gw
    slices = DISPATCH_GROUPS * n_col_groups
    tok_s = n_tok // slices
    rows_g = n_rows // DISPATCH_GROUPS
    w2_cols = w2_ref.shape[2] // n_col_groups
    n_blocks = xs_hbm.shape[0] // MOE_BLOCK

    def zero_block(row0):
        return pltpu.make_async_copy(zero_ref, xs_hbm.at[pl.ds(pl.multiple_of(row0, MOE_BLOCK), MOE_BLOCK)], zsem)

    @pl.when(pl.program_id(0) == 0)
    def _():
        zero_ref[...] = jnp.zeros_like(zero_ref)

        def expert_tail(e, carry, start):
            @pl.when(nblk_ref[e] > 0)
            def _():
                cp = zero_block(pend_ref[e] - MOE_BLOCK)
                cp.start() if start else cp.wait()
            return carry

        def unused_block(b, carry, start):
            cp = zero_block(b * MOE_BLOCK)
            cp.start() if start else cp.wait()
            return carry

        for start in (True, False):
            lax.fori_loop(0, N_EXPERTS, lambda e, c: expert_tail(e, c, start), 0)
            lax.fori_loop(nv_ref[0], n_blocks, lambda b, c: unused_block(b, c, start), 0)

    def group(g, carry):
        r0 = pl.multiple_of(g * rows_g, rows_g)
        for gi in range(n_col_groups):
            t0 = (g * n_col_groups + gi) * tok_s
            for i in range(tok_s):
                for k in range(TOP_K):
                    d = dest_ref[0, 0, (t0 + i) * TOP_K + k]
                    pltpu.make_async_copy(hn_ref.at[pl.ds(t0 + i, 1)], xs_hbm.at[pl.ds(d, 1)], sem).start()
            wb = w1_ref[0, pl.ds(r0, rows_g), gw * gi:gw * (gi + 1)].astype(BF16)
            w1o_ref[0, pl.ds(r0, rows_g), gw * gi:gw * (gi + 1)] = jnp.dot(
                wb, perm_ref[...], preferred_element_type=F32).astype(BF16)
            w2o_ref[0, pl.ds(r0, rows_g), w2_cols * gi:w2_cols * (gi + 1)] = w2_ref[
                0, pl.ds(r0, rows_g), w2_cols * gi:w2_cols * (gi + 1)].astype(BF16)
        return carry

    lax.fori_loop(0, DISPATCH_GROUPS, group, 0)
    for k in range(TOP_K):
        pltpu.make_async_copy(hn_ref, xs_hbm.at[pl.ds(0, n_tok)], sem).wait()


def _dispatch(pend_rows, blocks_e, n_valid, dest_flat, hn, w1, w2, perm, rows):
    t = hn.shape[0]
    n_e, d, n1 = w1.shape
    dff = w2.shape[1]
    tm = t // n_e
    assert dff == d and tm % (DISPATCH_GROUPS * (n1 // perm.shape[0])) == 0
    emap = lambda e, *_: (e, 0, 0)
    return pl.pallas_call(
        _dispatch_kernel,
        grid_spec=pltpu.PrefetchScalarGridSpec(
            num_scalar_prefetch=3,
            grid=(n_e,),
            in_specs=[pl.BlockSpec((1, 1, tm * TOP_K), emap, memory_space=pltpu.SMEM),
                      pl.BlockSpec((tm, _D), lambda e, *_: (e, 0)),
                      pl.BlockSpec((1, d, n1), emap),
                      pl.BlockSpec((1, dff, _D), emap),
                      pl.BlockSpec(perm.shape, lambda e, *_: (0, 0))],
            out_specs=[pl.BlockSpec(memory_space=pl.ANY),
                       pl.BlockSpec((1, d, n1), emap),
                       pl.BlockSpec((1, dff, _D), emap)],
            scratch_shapes=[pltpu.VMEM((MOE_BLOCK, _D), F32), pltpu.SemaphoreType.DMA, pltpu.SemaphoreType.DMA],
        ),
        out_shape=[jax.ShapeDtypeStruct((rows, _D), F32),
                   jax.ShapeDtypeStruct((n_e, d, n1), BF16),
                   jax.ShapeDtypeStruct((n_e, dff, _D), BF16)],
        compiler_params=_cparams(("arbitrary",)),
        name="dispatch",
    )(pend_rows, blocks_e, n_valid, dest_flat.reshape(n_e, 1, tm * TOP_K), hn, w1, w2, perm)


def _experts_kernel(be_ref, nv_ref, xs_ref, w1_ref, b1_ref, w2_ref, b2_ref, ys_ref):
    del be_ref

    @pl.when(pl.program_id(0) < nv_ref[0])
    def _():
        x = xs_ref[...].astype(BF16)
        u = jnp.dot(x, w1_ref[0], preferred_element_type=F32) + b1_ref[0]
        acts = []
        for gi in range(u.shape[1] // (2 * LANES)):
            g = jnp.minimum(u[:, 2 * LANES * gi:2 * LANES * gi + LANES], SWIGLU_LIMIT)
            lin = jnp.clip(u[:, 2 * LANES * gi + LANES:2 * LANES * (gi + 1)], -SWIGLU_LIMIT, SWIGLU_LIMIT)
            acts.append((g * _sigmoid(SWIGLU_ALPHA * g) * (lin + 1.0)).astype(BF16))
        act = jnp.concatenate(acts, axis=1)
        ys_ref[...] = jnp.dot(act, w2_ref[0], preferred_element_type=F32) + b2_ref[0]

    @pl.when(pl.program_id(0) >= nv_ref[0])
    def _():
        ys_ref[...] = jnp.zeros_like(ys_ref)


def _experts(block_e, n_valid, xs, w1, b1, w2, b2):
    rows = xs.shape[0]
    n_blocks = rows // MOE_BLOCK
    dff = w2.shape[1]
    wmap = lambda i, be, nv: (be[i], 0, 0)
    rmap = lambda i, be, nv: (i, 0)
    return pl.pallas_call(
        _experts_kernel,
        grid_spec=pltpu.PrefetchScalarGridSpec(
            num_scalar_prefetch=2,
            grid=(n_blocks,),
            in_specs=[
                pl.BlockSpec((MOE_BLOCK, _D), rmap),
                pl.BlockSpec((1, _D, 2 * dff), wmap),
                pl.BlockSpec((1, 1, 2 * dff), wmap),
                pl.BlockSpec((1, dff, _D), wmap),
                pl.BlockSpec((1, 1, _D), wmap),
            ],
            out_specs=pl.BlockSpec((MOE_BLOCK, _D), rmap),
        ),
        out_shape=jax.ShapeDtypeStruct((rows, _D), F32),
        compiler_params=_cparams(("arbitrary",)),
        name="experts",
    )(block_e, n_valid, xs, w1, b1, w2, b2)


def _final_kernel(tcur_ref, tnext_ref, x1_ref, gt_ref, pos_ref, gf_ref, ys_hbm, o_ref, st0_ref, st1_ref, sems):
    n_tok = x1_ref.shape[0]
    n_stage = st0_ref.shape[0]
    step = pl.program_id(0)
    last = pl.num_programs(0) - 1
    stages = (st0_ref, st1_ref)
    max_units = n_tok // SUBLANES + 1
    pieces = [1 << b for b in reversed(range(max_units.bit_length()))]

    def for_each_piece(tab_ref, slot, wait):
        for e in range(N_EXPERTS):
            src = tab_ref[0, 0, e]
            units = tab_ref[0, 0, N_EXPERTS + e]
            dst = tab_ref[0, 0, 2 * N_EXPERTS + e]
            for p in pieces:
                @pl.when((units & p) != 0)
                def _(p=p):
                    done = (units & ~(2 * p - 1)) * SUBLANES
                    cp = pltpu.make_async_copy(
                        ys_hbm.at[pl.ds(pl.multiple_of(src + done, SUBLANES), p * SUBLANES)],
                        stages[slot].at[pl.ds(pl.multiple_of(dst + done, SUBLANES), p * SUBLANES)],
                        sems.at[slot])
                    cp.wait() if wait else cp.start()

    def combine(slot):
        gates = gt_ref[...]
        pos = pos_ref[...]
        y = x1_ref[...]
        for c in range(n_stage // FINAL_KC):
            col = lax.broadcasted_iota(I32, (n_tok, FINAL_KC), 1) + c * FINAL_KC
            sel = jnp.zeros((n_tok, FINAL_KC), F32)
            for k in range(TOP_K):
                sel = jnp.where(col == pos[:, k:k + 1], gates[:, k:k + 1], sel)
            rows = stages[slot][FINAL_KC * c:FINAL_KC * (c + 1), :].astype(BF16)
            y = y + jnp.dot(sel.astype(BF16), rows, preferred_element_type=F32)
        o_ref[...] = _rms(y, gf_ref[...])

    @pl.when(step == 0)
    def _():
        st0_ref[...] = jnp.zeros_like(st0_ref)
        st1_ref[...] = jnp.zeros_like(st1_ref)
        for_each_piece(tcur_ref, 0, wait=False)

    for slot in (0, 1):
        @pl.when(step % 2 == slot)
        def _(slot=slot):
            for_each_piece(tcur_ref, slot, wait=True)

            @pl.when(step < last)
            def _():
                for_each_piece(tnext_ref, 1 - slot, wait=False)

            combine(slot)


def _final(run_table, x1, gates_tk, pos_tk, g_final, ys):
    t = x1.shape[0]
    tm = FINAL_TM
    steps = t // tm
    row = lambda i: (i, 0)
    n_stage = -(-(TOP_K * tm + 2 * (SUBLANES - 1) * N_EXPERTS) // FINAL_KC) * FINAL_KC
    return pl.pallas_call(
        _final_kernel,
        grid=(steps,),
        in_specs=[pl.BlockSpec((1, 1, LANES), lambda i: (i, 0, 0), memory_space=pltpu.SMEM),
                  pl.BlockSpec((1, 1, LANES), lambda i: (jnp.minimum(i + 1, steps - 1), 0, 0),
                               memory_space=pltpu.SMEM),
                  pl.BlockSpec((tm, _D), row), pl.BlockSpec((tm, TOP_K), row), pl.BlockSpec((tm, TOP_K), row),
                  pl.BlockSpec((1, _D), lambda i: (0, 0)), pl.BlockSpec(memory_space=pl.ANY)],
        out_specs=pl.BlockSpec((tm, _D), row),
        out_shape=jax.ShapeDtypeStruct((t, _D), F32),
        scratch_shapes=[pltpu.VMEM((n_stage, _D), F32), pltpu.VMEM((n_stage, _D), F32),
                        pltpu.SemaphoreType.DMA((2,))],
        compiler_params=_cparams(("arbitrary",)),
        name="final",
    )(run_table, run_table, x1, gates_tk, pos_tk, g_final, ys)


def _prep_in_proj(w_in, b_in, w_gate, b_gate):
    sizes = (GLA_HEADS * GLA_DK, GLA_HEADS * GLA_DK, GLA_HEADS * GLA_DV, GLA_HEADS * GLA_DV, GLA_GATE_RANK,
             SWA_Q_HEADS * SWA_HEAD_DIM, SWA_KV_HEADS * SWA_HEAD_DIM, SWA_KV_HEADS * SWA_HEAD_DIM, _D, _D)
    offs = [0]
    for s in sizes:
        offs.append(offs[-1] + s)

    def rearrange(m, dtype):
        p = [m[..., offs[i]:offs[i + 1]].astype(dtype) for i in range(len(sizes))]
        gq, gk, gv, gr, lr, sq, sk, sv, ga, gb = p

        def dup_heads(a):
            hs = [a[..., SWA_HEAD_DIM * h:SWA_HEAD_DIM * (h + 1)] for h in range(SWA_KV_HEADS)]
            return jnp.concatenate([hh for h in hs for hh in (h, h)], axis=-1)

        lr_pad = jnp.pad(lr, [(0, 0)] * (lr.ndim - 1) + [(0, LANES - GLA_GATE_RANK)])
        return jnp.concatenate([gq, gk, gv, gr, sq, ga, gb, dup_heads(sk), dup_heads(sv), lr_pad], axis=-1)

    w_all = rearrange(w_in, BF16)
    b_all = rearrange(b_in[None, :], F32)
    wg = jnp.pad(w_gate, ((0, LANES - GLA_GATE_RANK), (0, 0)))
    return w_all, b_all, wg, b_gate[None, :]


def _rope_tables(seq):
    half = SWA_HEAD_DIM // 2
    inv_freq = ROPE_THETA ** (-jnp.arange(half, dtype=F32) / half)
    ang = jnp.arange(seq, dtype=F32)[:, None] * inv_freq[None, :]
    cos, sin = jnp.cos(ang), jnp.sin(ang)
    cos_t = jnp.concatenate([cos, cos] * (LANES // SWA_HEAD_DIM), axis=1)
    sin_t = jnp.concatenate([-sin, sin] * (LANES // SWA_HEAD_DIM), axis=1)
    return cos_t, sin_t


def _pair_split_perm():
    src = jnp.arange(2 * LANES, dtype=I32)
    dst = jnp.where(src % 2 == 0, src // 2, LANES + src // 2)
    return (dst[:, None] == jnp.arange(2 * LANES, dtype=I32)[None, :]).astype(BF16)


def kernel(x, g_mix, w_in, b_in, w_gla_gate, b_gla_gate, g_gla_head, w_gla_out, sinks, w_swa_out, w_out,
           g_ffn, w_router, b_router, w_e1, b_e1, w_e2, b_e2, g_final):
    bsz, seq, d = x.shape
    assert d == _D and w_in.shape[0] == 1, "single-layer, d_model=1024 only"
    assert seq % max(PROJ_TM, SWA_WINDOW, GLA_CHUNK) == 0
    t = bsz * seq
    assert t % max(MERGE_TM, ROUTE_TT, FINAL_TM) == 0 and t % N_EXPERTS == 0
    assert FINAL_TM == ROUTE_TT and t // ROUTE_TT <= LANES and 3 * N_EXPERTS <= LANES
    x2 = x.reshape(t, d)

    w_all, b_all, wg, bg = _prep_in_proj(w_in[0], b_in[0], w_gla_gate[0], b_gla_gate[0])
    cos_t, sin_t = _rope_tables(seq)
    gq, gk, gv, gr, sq, ga, gb, sk, sv, lg = _in_proj(x2, g_mix, w_all, b_all, cos_t, sin_t, wg, bg, seq)
    oa = _gla(gq, gk, gv, lg, gr, g_gla_head, bsz, seq)
    ob = _swa(sinks[0], sq, sk, sv, bsz, seq)
    x1, hn, logits_t = _merge(x2, oa, ob, ga, gb, w_gla_out[0].astype(BF16), w_swa_out[0].astype(BF16),
                              w_out[0].astype(BF16), g_ffn, w_router[0].T, b_router[0][:, None])

    e_kt, g_kt, r_kt, cum = _route(logits_t)
    n_tiles = t // ROUTE_TT
    cum = cum[:, :n_tiles].astype(I32)
    counts = cum[:, -1]
    blocks_e = (counts + MOE_BLOCK - 1) // MOE_BLOCK
    bend = jnp.cumsum(blocks_e)
    pstart = (bend - blocks_e) * MOE_BLOCK
    dest_kt = r_kt
    for e in range(N_EXPERTS):
        dest_kt = dest_kt + jnp.where(e_kt == e, pstart[e], 0)
    dest_flat = dest_kt.T.reshape(t * TOP_K)
    n_blocks = (t * TOP_K) // MOE_BLOCK + N_EXPERTS
    block_e = jnp.minimum(jnp.sum(bend[None, :] <= jnp.arange(n_blocks, dtype=I32)[:, None], axis=1),
                          N_EXPERTS - 1).astype(I32)
    n_valid = bend[-1:].astype(I32)

    before = jnp.concatenate([jnp.zeros((N_EXPERTS, 1), I32), cum[:, :-1]], axis=1)
    run_len = cum - before
    run_src = pstart[:, None] + before
    cov_src = run_src // SUBLANES * SUBLANES
    cov_len = jnp.where(run_len > 0, (run_src + run_len + SUBLANES - 1) // SUBLANES * SUBLANES - cov_src, 0)
    cov_dst = jnp.cumsum(cov_len, axis=0) - cov_len
    run_table = jnp.concatenate([cov_src.T, cov_len.T // SUBLANES, cov_dst.T,
                                 jnp.zeros((n_tiles, LANES - 3 * N_EXPERTS), I32)], axis=1)
    shift = jnp.repeat(cov_dst + run_src - cov_src - before, ROUTE_TT, axis=1)
    pos_kt = r_kt
    for e in range(N_EXPERTS):
        pos_kt = pos_kt + jnp.where(e_kt == e, shift[e][None, :], 0)

    b1 = b_e1[0].reshape(N_EXPERTS, -1, LANES, 2).transpose(0, 1, 3, 2).reshape(N_EXPERTS, 1, -1)
    xs, w1, w2 = _dispatch(bend * MOE_BLOCK, blocks_e, n_valid, dest_flat, hn, w_e1[0], w_e2[0],
                           _pair_split_perm(), n_blocks * MOE_BLOCK)
    ys = _experts(block_e, n_valid, xs, w1, b1, w2, b_e2[0][:, None, :])
    out = _final(run_table.reshape(n_tiles, 1, LANES), x1, g_kt.T, pos_kt.T, g_final[None, :], ys)
    return out.reshape(bsz, seq, d)
```

```python
import jax
import jax.numpy as jnp
import numpy as np
from jax import lax
from jax.experimental import pallas as pl
from jax.experimental.pallas import tpu as pltpu

F32 = jnp.float32
BF16 = jnp.bfloat16
I32 = jnp.int32

NORM_EPS = 1e-5
GLA_HEADS = 4
GLA_DK = 128
GLA_DV = 256
GLA_GATE_RANK = 16
GLA_TAU = 16.0
SWA_Q_HEADS = 16
SWA_KV_HEADS = 2
SWA_GROUP = SWA_Q_HEADS // SWA_KV_HEADS
SWA_HEAD_DIM = 64
SWA_WINDOW = 128
ROPE_THETA = 10000.0
N_EXPERTS = 32
TOP_K = 4
SWIGLU_LIMIT = 7.0
SWIGLU_ALPHA = 1.702

LANES = 128
SUBLANES = 8
NEG = -1e30
VMEM_LIMIT = 56 * 1024 * 1024

PROJ_TM = 256
GLA_CHUNK = 128
GLA_HEADS_PER_STEP = 4
MERGE_TM = 512
MERGE_TN = 256
MERGE_ROW_GROUPS = 1
ROUTE_TT = 512
MOE_BLOCK = 512
DISPATCH_TM = 512
FINAL_TM = 512
FINAL_KC = 512

_D = 1024
_C_GQ = 0
_C_GK = _C_GQ + GLA_HEADS * GLA_DK
_C_GV = _C_GK + GLA_HEADS * GLA_DK
_C_GR = _C_GV + GLA_HEADS * GLA_DV
_C_SQ = _C_GR + GLA_HEADS * GLA_DV
_C_GA = _C_SQ + SWA_Q_HEADS * SWA_HEAD_DIM
_C_GB = _C_GA + _D
_C_SK = _C_GB + _D
_C_SV = _C_SK + SWA_KV_HEADS * LANES
_C_LR = _C_SV + SWA_KV_HEADS * LANES
_C_END = _C_LR + LANES


def _cparams(sem):
    return pltpu.CompilerParams(dimension_semantics=sem, vmem_limit_bytes=VMEM_LIMIT)


def _rms(x, g):
    return x * lax.rsqrt(jnp.mean(x * x, axis=-1, keepdims=True) + NORM_EPS) * g


def _sigmoid(x):
    return 1.0 / (1.0 + jnp.exp(-x))


def _dot_nt(a, b):
    return lax.dot_general(a, b, (((1,), (1,)), ((), ())), preferred_element_type=F32)


def _rope_slabs(acc, cos, sin, first_half):
    outs = []
    for i in range(acc.shape[1] // LANES):
        xs = acc[:, LANES * i:LANES * (i + 1)]
        partner = jnp.where(first_half, pltpu.roll(xs, LANES - 32, 1), pltpu.roll(xs, 32, 1))
        outs.append(xs * cos + partner * sin)
    return jnp.concatenate(outs, axis=1)


def _in_proj_kernel(x_ref, g_ref, w_ref, b_ref, cos_ref, sin_ref, wg_ref, bg_ref,
                    gq_ref, gk_ref, gv_ref, gr_ref, sq_ref, ga_ref, gb_ref, sk_ref, sv_ref, lg_ref):
    h = _rms(x_ref[...], g_ref[...]).astype(BF16)

    def proj(lo, hi):
        return jnp.dot(h, w_ref[:, lo:hi], preferred_element_type=F32) + b_ref[:, lo:hi]

    cos = cos_ref[...]
    sin = sin_ref[...]
    lane = lax.broadcasted_iota(I32, cos.shape, 1)
    first_half = (lane % SWA_HEAD_DIM) < (SWA_HEAD_DIM // 2)

    gq_ref[...] = (proj(_C_GQ, _C_GK) * (GLA_DK ** -0.5)).astype(BF16)
    gk_ref[...] = proj(_C_GK, _C_GV).astype(BF16)
    gv_ref[...] = proj(_C_GV, _C_GR).astype(BF16)
    gr = proj(_C_GR, _C_SQ)
    gr_ref[...] = (gr * _sigmoid(gr)).astype(BF16)
    sq = proj(_C_SQ, _C_GA) * (SWA_HEAD_DIM ** -0.5)
    sq_ref[...] = _rope_slabs(sq, cos, sin, first_half).astype(BF16)
    ga_ref[...] = _sigmoid(proj(_C_GA, _C_GB)).astype(BF16)
    gb_ref[...] = _sigmoid(proj(_C_GB, _C_SK)).astype(BF16)
    sk_ref[...] = _rope_slabs(proj(_C_SK, _C_SV), cos, sin, first_half).astype(BF16)
    sv_ref[...] = proj(_C_SV, _C_LR).astype(BF16)
    z = jnp.dot(proj(_C_LR, _C_END), wg_ref[...], precision=lax.Precision.HIGHEST,
                preferred_element_type=F32) + bg_ref[...]
    log_sig = jnp.minimum(z, 0.0) - jnp.log(1.0 + jnp.exp(-jnp.abs(z)))
    lg_ref[...] = log_sig * (1.0 / GLA_TAU)


def _in_proj(x2, g_mix, w_all, b_all, cos_t, sin_t, wg, bg, seq):
    t = x2.shape[0]
    tm = PROJ_TM
    pos_blocks = seq // tm
    const = lambda i: (0, 0)
    row = lambda i: (i, 0)
    widths = [(_C_GK - _C_GQ, BF16), (_C_GV - _C_GK, BF16), (_C_GR - _C_GV, BF16), (_C_SQ - _C_GR, BF16),
              (_C_GA - _C_SQ, BF16), (_D, BF16), (_D, BF16), (_C_SV - _C_SK, BF16), (_C_LR - _C_SV, BF16),
              (GLA_HEADS * GLA_DK, F32)]
    return pl.pallas_call(
        _in_proj_kernel,
        grid=(t // tm,),
        in_specs=[
            pl.BlockSpec((tm, _D), row),
            pl.BlockSpec((1, _D), const),
            pl.BlockSpec((_D, _C_END), const, pipeline_mode=pl.Buffered(1)),
            pl.BlockSpec((1, _C_END), const),
            pl.BlockSpec((tm, LANES), lambda i: (i % pos_blocks, 0)),
            pl.BlockSpec((tm, LANES), lambda i: (i % pos_blocks, 0)),
            pl.BlockSpec((LANES, GLA_HEADS * GLA_DK), const),
            pl.BlockSpec((1, GLA_HEADS * GLA_DK), const),
        ],
        out_specs=[pl.BlockSpec((tm, w), row) for w, _ in widths],
        out_shape=[jax.ShapeDtypeStruct((t, w), dt) for w, dt in widths],
        compiler_params=_cparams(("parallel",)),
        name="in_proj",
    )(x2, g_mix, w_all, b_all, cos_t, sin_t, wg, bg)


def _gla_cumsum_operator(c_len):
    t = np.arange(c_len)[:, None]
    r = np.arange(c_len)[None, :]
    return np.tile((r <= t).astype(np.float32), (1, 3))


def _gla_kernel(q_ref, k_ref, v_ref, lg_ref, gr_ref, gh_ref, dm_ref, o_ref, st_ref):
    c_len = GLA_CHUNK
    n_lev = c_len.bit_length() - 1
    seq = q_ref.shape[0]
    st_ref[...] = jnp.zeros_like(st_ref)

    t_i = lax.broadcasted_iota(I32, (c_len, c_len), 0)
    j_i = lax.broadcasted_iota(I32, (c_len, c_len), 1)
    row = lax.broadcasted_iota(I32, (c_len, 1), 0)
    diag = t_i == j_i
    upper, pair = [], []
    for lev in range(n_lev):
        s = c_len >> (lev + 1)
        upper.append((row & s) != 0)
        pair.append(((t_i // (2 * s)) == (j_i // (2 * s))) & ((t_i & s) != 0) & ((j_i & s) == 0))

    sub8 = lax.broadcasted_iota(I32, (c_len // 8, 8, GLA_DK), 1)

    def boundary_rows(b, s):
        if s >= 4:
            b3 = b.reshape(c_len // (2 * s), 2 * s, GLA_DK)
            return jnp.broadcast_to(b3[:, s - 1:s, :], b3.shape).reshape(c_len, GLA_DK)
        b3 = b.reshape(c_len // 8, 8, GLA_DK)
        lo = jnp.broadcast_to(b3[:, 1:2, :], b3.shape)
        hi = jnp.broadcast_to(b3[:, 5:6, :], b3.shape)
        return jnp.where(sub8 < 4, lo, hi).reshape(c_len, GLA_DK)

    def head_chunk(r0, hh):
        kcols = slice(GLA_DK * hh, GLA_DK * (hh + 1))
        vcols = slice(GLA_DV * hh, GLA_DV * (hh + 1))
        q_bf = q_ref[pl.ds(r0, c_len), kcols]
        k_bf = k_ref[pl.ds(r0, c_len), kcols]
        q = q_bf.astype(F32)
        k = k_bf.astype(F32)
        v = v_ref[pl.ds(r0, c_len), vcols]

        lg = lg_ref[pl.ds(r0, c_len), kcols]
        lg_hi = lg.astype(BF16)
        rem = lg - lg_hi.astype(F32)
        lg_mid = rem.astype(BF16)
        lg_lo = (rem - lg_mid.astype(F32)).astype(BF16)
        b = jnp.dot(dm_ref[...], jnp.concatenate([lg_hi, lg_mid, lg_lo], axis=0),
                    preferred_element_type=F32)
        w_cum = jnp.exp(b)

        st = st_ref[hh]
        o = _dot_nt((q * w_cum).astype(BF16), st.astype(BF16))

        a = jnp.where(diag, _dot_nt(q_bf, k_bf), 0.0)
        for lev in range(n_lev):
            s = c_len >> (lev + 1)
            if s == 1:
                w = jnp.where(upper[lev], jnp.exp(lg), 1.0)
            else:
                w = jnp.exp(-jnp.abs(b - boundary_rows(b, s)))
            z = (jnp.where(upper[lev], q, k) * w).astype(BF16)
            a = jnp.where(pair[lev], _dot_nt(z, z), a)
        o = o + jnp.dot(a.astype(BF16), v, preferred_element_type=F32)

        b_last = b[c_len - 1:c_len, :]
        upd = lax.dot_general(v, (k * jnp.exp(b_last - b)).astype(BF16), (((0,), (0,)), ((), ())),
                              preferred_element_type=F32)
        st_ref[hh] = st * w_cum[c_len - 1:c_len, :] + upd

        on = _rms(o, gh_ref[...])
        o_ref[pl.ds(r0, c_len), vcols] = (on * gr_ref[pl.ds(r0, c_len), vcols].astype(F32)).astype(BF16)

    def chunk(c, carry):
        r0 = pl.multiple_of(c * c_len, c_len)
        for hh in range(GLA_HEADS_PER_STEP):
            head_chunk(r0, hh)
        return carry

    lax.fori_loop(0, seq // c_len, chunk, 0, unroll=2)


def _gla(gq, gk, gv, lg, gr, g_head, bsz, seq):
    t = gq.shape[0]
    hs = GLA_HEADS_PER_STEP
    dmat = jnp.asarray(_gla_cumsum_operator(GLA_CHUNK), dtype=BF16)
    return pl.pallas_call(
        _gla_kernel,
        grid=(bsz, GLA_HEADS // hs),
        in_specs=[
            pl.BlockSpec((seq, hs * GLA_DK), lambda b, h: (b, h)),
            pl.BlockSpec((seq, hs * GLA_DK), lambda b, h: (b, h)),
            pl.BlockSpec((seq, hs * GLA_DV), lambda b, h: (b, h)),
            pl.BlockSpec((seq, hs * GLA_DK), lambda b, h: (b, h)),
            pl.BlockSpec((seq, hs * GLA_DV), lambda b, h: (b, h)),
            pl.BlockSpec((1, GLA_DV), lambda b, h: (0, 0)),
            pl.BlockSpec(dmat.shape, lambda b, h: (0, 0)),
        ],
        out_specs=pl.BlockSpec((seq, hs * GLA_DV), lambda b, h: (b, h)),
        out_shape=jax.ShapeDtypeStruct((t, GLA_HEADS * GLA_DV), BF16),
        scratch_shapes=[pltpu.VMEM((hs, GLA_DV, GLA_DK), F32)],
        compiler_params=_cparams(("parallel", "parallel")),
        name="gla",
    )(gq, gk, gv, lg, gr, g_head, dmat)


def _swa_kernel(sink_ref, q_ref, k_ref, v_ref, o_ref):
    w = SWA_WINDOW
    seq = q_ref.shape[0]
    hk = pl.program_id(1)
    lane_q = lax.broadcasted_iota(I32, (w, LANES), 1)
    low_q = lane_q < SWA_HEAD_DIM
    lane_b = lax.broadcasted_iota(I32, (2 * w, LANES), 1)
    low_b = lane_b < SWA_HEAD_DIM
    qi = lax.broadcasted_iota(I32, (w, 2 * w), 0)
    kj = lax.broadcasted_iota(I32, (w, 2 * w), 1)
    in_window = (kj > qi) & (kj <= qi + w)
    zero_q = jnp.zeros((w, LANES), BF16)
    zero_b = jnp.zeros((2 * w, LANES), BF16)

    def block(n, carry):
        r0 = pl.multiple_of(n * w, w)
        p0 = pl.multiple_of(jnp.maximum(n - 1, 0) * w, w)
        kb = jnp.concatenate([k_ref[pl.ds(p0, w), :], k_ref[pl.ds(r0, w), :]], axis=0)
        vb = jnp.concatenate([v_ref[pl.ds(p0, w), :], v_ref[pl.ds(r0, w), :]], axis=0)
        valid = in_window & ((kj >= w) | (n > 0))
        v_lo = jnp.where(low_b, vb, zero_b)
        v_hi = jnp.where(low_b, zero_b, vb)
        for m in range(SWA_GROUP // 2):
            qp = q_ref[pl.ds(r0, w), LANES * m:LANES * (m + 1)]
            acc = jnp.zeros((w, LANES), F32)
            for par in range(2):
                qm = jnp.where(low_q, qp, zero_q) if par == 0 else jnp.where(low_q, zero_q, qp)
                s = jnp.where(valid, _dot_nt(qm, kb), NEG)
                sink = sink_ref[hk * SWA_GROUP + 2 * m + par]
                mx = jnp.maximum(jnp.max(s, axis=-1, keepdims=True), sink)
                p = jnp.exp(s - mx)
                den = jnp.sum(p, axis=-1, keepdims=True) + jnp.exp(sink - mx)
                pv = jnp.dot(p.astype(BF16), v_lo if par == 0 else v_hi, preferred_element_type=F32)
                acc = acc + pv / den
            o_ref[pl.ds(r0, w), LANES * m:LANES * (m + 1)] = acc.astype(BF16)
        return carry

    lax.fori_loop(0, seq // w, block, 0, unroll=2)


def _swa(sinks, sq, sk, sv, bsz, seq):
    t = sq.shape[0]
    gw = SWA_GROUP * SWA_HEAD_DIM
    return pl.pallas_call(
        _swa_kernel,
        grid_spec=pltpu.PrefetchScalarGridSpec(
            num_scalar_prefetch=1,
            grid=(bsz, SWA_KV_HEADS),
            in_specs=[
                pl.BlockSpec((seq, gw), lambda b, h, s: (b, h)),
                pl.BlockSpec((seq, LANES), lambda b, h, s: (b, h)),
                pl.BlockSpec((seq, LANES), lambda b, h, s: (b, h)),
            ],
            out_specs=pl.BlockSpec((seq, gw), lambda b, h, s: (b, h)),
        ),
        out_shape=jax.ShapeDtypeStruct((t, SWA_Q_HEADS * SWA_HEAD_DIM), BF16),
        compiler_params=_cparams(("parallel", "parallel")),
        name="swa",
    )(sinks, sq, sk, sv)


def _merge_kernel(x_ref, oa_ref, ob_ref, ga_ref, gb_ref, wa_ref, wb_ref, wo_ref, gf_ref, wr_ref, br_ref,
                  x1_ref, hn_ref, lt_ref, mixed_ref):
    rows_g = x_ref.shape[0] // MERGE_ROW_GROUPS
    for h in range(MERGE_ROW_GROUPS):
        rows = slice(rows_g * h, rows_g * (h + 1))
        oa = oa_ref[rows, :]
        ob = ob_ref[rows, :]
        for n in range(_D // MERGE_TN):
            cols = slice(MERGE_TN * n, MERGE_TN * (n + 1))
            ya = ga_ref[rows, cols].astype(F32) * jnp.dot(oa, wa_ref[:, cols], preferred_element_type=F32)
            yb = gb_ref[rows, cols].astype(F32) * jnp.dot(ob, wb_ref[:, cols], preferred_element_type=F32)
            mixed_ref[rows, cols] = (ya + yb).astype(BF16)
        mixed = mixed_ref[rows, :]
        for n in range(_D // MERGE_TN):
            cols = slice(MERGE_TN * n, MERGE_TN * (n + 1))
            x1_ref[rows, cols] = x_ref[rows, cols] + jnp.dot(mixed, wo_ref[:, cols], preferred_element_type=F32)
        hn = _rms(x1_ref[rows, :], gf_ref[...])
        hn_ref[rows, :] = hn
        lt_ref[:, rows] = lax.dot_general(wr_ref[...], hn, (((1,), (1,)), ((), ())),
                                          precision=lax.Precision.HIGHEST,
                                          preferred_element_type=F32) + br_ref[...]


def _merge(x2, oa, ob, ga, gb, wa, wb, wo, g_ffn, wr_t, br_col):
    t = x2.shape[0]
    tm = MERGE_TM
    row = lambda i: (i, 0)
    const = lambda i: (0, 0)
    return pl.pallas_call(
        _merge_kernel,
        grid=(t // tm,),
        in_specs=[pl.BlockSpec((tm, _D), row)] * 5 + [pl.BlockSpec((_D, _D), const)] * 3 + [
            pl.BlockSpec((1, _D), const),
            pl.BlockSpec((N_EXPERTS, _D), const),
            pl.BlockSpec((N_EXPERTS, 1), const),
        ],
        out_specs=[pl.BlockSpec((tm, _D), row), pl.BlockSpec((tm, _D), row),
                   pl.BlockSpec((N_EXPERTS, tm), lambda i: (0, i))],
        out_shape=[jax.ShapeDtypeStruct((t, _D), F32), jax.ShapeDtypeStruct((t, _D), F32),
                   jax.ShapeDtypeStruct((N_EXPERTS, t), F32)],
        scratch_shapes=[pltpu.VMEM((tm, _D), BF16)],
        compiler_params=_cparams(("parallel",)),
        name="merge",
    )(x2, oa, ob, ga, gb, wa, wb, wo, g_ffn, wr_t, br_col)


def _route_kernel(lt_ref, e_ref, g_ref, r_ref, cum_ref, carry_ref):
    tt = lt_ref.shape[1]

    @pl.when(pl.program_id(0) == 0)
    def _():
        carry_ref[...] = jnp.zeros_like(carry_ref)
        cum_ref[...] = jnp.zeros_like(cum_ref)

    eid = lax.broadcasted_iota(I32, (N_EXPERTS, tt), 0)
    work = lt_ref[...]
    vals, idxs = [], []
    chosen = jnp.zeros((N_EXPERTS, tt), F32)
    for _ in range(TOP_K):
        m = jnp.max(work, axis=0, keepdims=True)
        idx = jnp.min(jnp.where(work == m, eid, N_EXPERTS), axis=0, keepdims=True)
        hit = eid == idx
        work = jnp.where(hit, -jnp.inf, work)
        chosen = jnp.where(hit, 1.0, chosen)
        vals.append(m)
        idxs.append(idx)
    ex = [jnp.exp(v - vals[0]) for v in vals]
    den = ex[0] + ex[1] + ex[2] + ex[3]

    t_r = lax.broadcasted_iota(I32, (tt, tt), 0)
    t_c = lax.broadcasted_iota(I32, (tt, tt), 1)
    before = (t_r < t_c).astype(BF16)
    pref = jnp.dot(chosen.astype(BF16), before, preferred_element_type=F32) + carry_ref[:, 0:1]
    for k in range(TOP_K):
        e_ref[k:k + 1, :] = idxs[k]
        g_ref[k:k + 1, :] = ex[k] / den
        r_ref[k:k + 1, :] = jnp.sum(jnp.where(eid == idxs[k], pref, 0.0), axis=0, keepdims=True).astype(I32)
    total = pref[:, tt - 1:tt] + chosen[:, tt - 1:tt]
    carry_ref[...] = jnp.broadcast_to(total, carry_ref.shape)
    tile_lane = lax.broadcasted_iota(I32, cum_ref.shape, 1) == pl.program_id(0)
    cum_ref[...] = jnp.where(tile_lane, total, cum_ref[...])


def _route(logits_t):
    t = logits_t.shape[1]
    tt = ROUTE_TT
    blk = lambda i: (0, i)
    return pl.pallas_call(
        _route_kernel,
        grid=(t // tt,),
        in_specs=[pl.BlockSpec((N_EXPERTS, tt), blk)],
        out_specs=[pl.BlockSpec((TOP_K, tt), blk), pl.BlockSpec((TOP_K, tt), blk), pl.BlockSpec((TOP_K, tt), blk),
                   pl.BlockSpec((N_EXPERTS, LANES), lambda i: (0, 0))],
        out_shape=[jax.ShapeDtypeStruct((TOP_K, t), I32), jax.ShapeDtypeStruct((TOP_K, t), F32),
                   jax.ShapeDtypeStruct((TOP_K, t), I32), jax.ShapeDtypeStruct((N_EXPERTS, LANES), F32)],
        scratch_shapes=[pltpu.VMEM((N_EXPERTS, LANES), F32)],
        compiler_params=_cparams(("arbitrary",)),
        name="route",
    )(logits_t)


def _dispatch_kernel(pend_ref, nblk_ref, nv_ref, dest_ref, hn_ref, xs_hbm, zero_ref, sem, zsem):
    n_tok = hn_ref.shape[0]
    n_blocks = xs_hbm.shape[0] // MOE_BLOCK

    def zero_block(row0):
        return pltpu.make_async_copy(zero_ref, xs_hbm.at[pl.ds(pl.multiple_of(row0, MOE_BLOCK), MOE_BLOCK)], zsem)

    @pl.when(pl.program_id(0) == 0)
    def _():
        zero_ref[...] = jnp.zeros_like(zero_ref)

        def expert_tail(e, carry, start):
            @pl.when(nblk_ref[e] > 0)
            def _():
                cp = zero_block(pend_ref[e] - MOE_BLOCK)
                cp.start() if start else cp.wait()
            return carry

        def unused_block(b, carry, start):
            cp = zero_block(b * MOE_BLOCK)
            cp.start() if start else cp.wait()
            return carry

        for start in (True, False):
            lax.fori_loop(0, N_EXPERTS, lambda e, c: expert_tail(e, c, start), 0)
            lax.fori_loop(nv_ref[0], n_blocks, lambda b, c: unused_block(b, c, start), 0)

    def issue(i, carry):
        for k in range(TOP_K):
            d = dest_ref[0, 0, i * TOP_K + k]
            pltpu.make_async_copy(hn_ref.at[pl.ds(i, 1)], xs_hbm.at[pl.ds(d, 1)], sem).start()
        return carry

    lax.fori_loop(0, n_tok, issue, 0, unroll=4)
    for k in range(TOP_K):
        pltpu.make_async_copy(hn_ref, xs_hbm.at[pl.ds(0, n_tok)], sem).wait()


def _dispatch(pend_rows, blocks_e, n_valid, dest_flat, hn, rows):
    t = hn.shape[0]
    tm = DISPATCH_TM
    steps = t // tm
    return pl.pallas_call(
        _dispatch_kernel,
        grid_spec=pltpu.PrefetchScalarGridSpec(
            num_scalar_prefetch=3,
            grid=(steps,),
            in_specs=[pl.BlockSpec((1, 1, tm * TOP_K), lambda i, *_: (i, 0, 0), memory_space=pltpu.SMEM),
                      pl.BlockSpec((tm, _D), lambda i, *_: (i, 0))],
            out_specs=pl.BlockSpec(memory_space=pl.ANY),
            scratch_shapes=[pltpu.VMEM((MOE_BLOCK, _D), F32), pltpu.SemaphoreType.DMA, pltpu.SemaphoreType.DMA],
        ),
        out_shape=jax.ShapeDtypeStruct((rows, _D), F32),
        compiler_params=_cparams(("arbitrary",)),
        name="dispatch",
    )(pend_rows, blocks_e, n_valid, dest_flat.reshape(steps, 1, tm * TOP_K), hn)


def _experts_kernel(be_ref, nv_ref, first_ref, xs_ref, w1_ref, b1_ref, w2_ref, b2_ref, perm_ref, ys_ref,
                    w1b_ref, w2b_ref):
    del be_ref
    i = pl.program_id(0)
    gw = perm_ref.shape[0]

    @pl.when((i < nv_ref[0]) & (first_ref[i] != 0))
    def _():
        for gi in range(w1_ref.shape[2] // gw):
            wb = w1_ref[0, :, gw * gi:gw * (gi + 1)].astype(BF16)
            w1b_ref[:, gw * gi:gw * (gi + 1)] = jnp.dot(wb, perm_ref[...], preferred_element_type=F32).astype(BF16)
        w2b_ref[...] = w2_ref[0].astype(BF16)

    @pl.when(i < nv_ref[0])
    def _():
        x = xs_ref[...].astype(BF16)
        u = jnp.dot(x, w1b_ref[...], preferred_element_type=F32) + b1_ref[0]
        acts = []
        for gi in range(u.shape[1] // (2 * LANES)):
            g = jnp.minimum(u[:, 2 * LANES * gi:2 * LANES * gi + LANES], SWIGLU_LIMIT)
            lin = jnp.clip(u[:, 2 * LANES * gi + LANES:2 * LANES * (gi + 1)], -SWIGLU_LIMIT, SWIGLU_LIMIT)
            acts.append((g * _sigmoid(SWIGLU_ALPHA * g) * (lin + 1.0)).astype(BF16))
        act = jnp.concatenate(acts, axis=1)
        ys_ref[...] = jnp.dot(act, w2b_ref[...], preferred_element_type=F32) + b2_ref[0]

    @pl.when(i >= nv_ref[0])
    def _():
        ys_ref[...] = jnp.zeros_like(ys_ref)


def _experts(block_e, n_valid, first, xs, w1, b1, w2, b2, perm):
    rows = xs.shape[0]
    n_blocks = rows // MOE_BLOCK
    dff = w2.shape[1]
    wmap = lambda i, be, nv, fi: (be[i], 0, 0)
    rmap = lambda i, be, nv, fi: (i, 0)
    return pl.pallas_call(
        _experts_kernel,
        grid_spec=pltpu.PrefetchScalarGridSpec(
            num_scalar_prefetch=3,
            grid=(n_blocks,),
            in_specs=[
                pl.BlockSpec((MOE_BLOCK, _D), rmap),
                pl.BlockSpec((1, _D, 2 * dff), wmap),
                pl.BlockSpec((1, 1, 2 * dff), wmap),
                pl.BlockSpec((1, dff, _D), wmap),
                pl.BlockSpec((1, 1, _D), wmap),
                pl.BlockSpec(perm.shape, lambda i, be, nv, fi: (0, 0)),
            ],
            out_specs=pl.BlockSpec((MOE_BLOCK, _D), rmap),
            scratch_shapes=[pltpu.VMEM((_D, 2 * dff), BF16), pltpu.VMEM((dff, _D), BF16)],
        ),
        out_shape=jax.ShapeDtypeStruct((rows, _D), F32),
        compiler_params=_cparams(("arbitrary",)),
        name="experts",
    )(block_e, n_valid, first, xs, w1, b1, w2, b2, perm)


def _final_kernel(tcur_ref, tnext_ref, x1_ref, gt_ref, pos_ref, gf_ref, ys_hbm, o_ref, st0_ref, st1_ref, sems):
    n_tok = x1_ref.shape[0]
    n_stage = st0_ref.shape[0]
    step = pl.program_id(0)
    last = pl.num_programs(0) - 1
    stages = (st0_ref, st1_ref)
    max_units = n_tok // SUBLANES + 1
    pieces = [1 << b for b in reversed(range(max_units.bit_length()))]

    def for_each_piece(tab_ref, slot, wait):
        for e in range(N_EXPERTS):
            src = tab_ref[0, 0, e]
            units = tab_ref[0, 0, N_EXPERTS + e]
            dst = tab_ref[0, 0, 2 * N_EXPERTS + e]
            for p in pieces:
                @pl.when((units & p) != 0)
                def _(p=p):
                    done = (units & ~(2 * p - 1)) * SUBLANES
                    cp = pltpu.make_async_copy(
                        ys_hbm.at[pl.ds(pl.multiple_of(src + done, SUBLANES), p * SUBLANES)],
                        stages[slot].at[pl.ds(pl.multiple_of(dst + done, SUBLANES), p * SUBLANES)],
                        sems.at[slot])
                    cp.wait() if wait else cp.start()

    def combine(slot):
        gates = gt_ref[...]
        pos = pos_ref[...]
        y = x1_ref[...]
        for c in range(n_stage // FINAL_KC):
            col = lax.broadcasted_iota(I32, (n_tok, FINAL_KC), 1) + c * FINAL_KC
            sel = jnp.zeros((n_tok, FINAL_KC), F32)
            for k in range(TOP_K):
                sel = jnp.where(col == pos[:, k:k + 1], gates[:, k:k + 1], sel)
            rows = stages[slot][FINAL_KC * c:FINAL_KC * (c + 1), :].astype(BF16)
            y = y + jnp.dot(sel.astype(BF16), rows, preferred_element_type=F32)
        o_ref[...] = _rms(y, gf_ref[...])

    @pl.when(step == 0)
    def _():
        st0_ref[...] = jnp.zeros_like(st0_ref)
        st1_ref[...] = jnp.zeros_like(st1_ref)
        for_each_piece(tcur_ref, 0, wait=False)

    for slot in (0, 1):
        @pl.when(step % 2 == slot)
        def _(slot=slot):
            for_each_piece(tcur_ref, slot, wait=True)

            @pl.when(step < last)
            def _():
                for_each_piece(tnext_ref, 1 - slot, wait=False)

            combine(slot)


def _final(run_table, x1, gates_tk, pos_tk, g_final, ys):
    t = x1.shape[0]
    tm = FINAL_TM
    steps = t // tm
    row = lambda i: (i, 0)
    n_stage = -(-(TOP_K * tm + 2 * (SUBLANES - 1) * N_EXPERTS) // FINAL_KC) * FINAL_KC
    return pl.pallas_call(
        _final_kernel,
        grid=(steps,),
        in_specs=[pl.BlockSpec((1, 1, LANES), lambda i: (i, 0, 0), memory_space=pltpu.SMEM),
                  pl.BlockSpec((1, 1, LANES), lambda i: (jnp.minimum(i + 1, steps - 1), 0, 0),
                               memory_space=pltpu.SMEM),
                  pl.BlockSpec((tm, _D), row), pl.BlockSpec((tm, TOP_K), row), pl.BlockSpec((tm, TOP_K), row),
                  pl.BlockSpec((1, _D), lambda i: (0, 0)), pl.BlockSpec(memory_space=pl.ANY)],
        out_specs=pl.BlockSpec((tm, _D), row),
        out_shape=jax.ShapeDtypeStruct((t, _D), F32),
        scratch_shapes=[pltpu.VMEM((n_stage, _D), F32), pltpu.VMEM((n_stage, _D), F32),
                        pltpu.SemaphoreType.DMA((2,))],
        compiler_params=_cparams(("arbitrary",)),
        name="final",
    )(run_table, run_table, x1, gates_tk, pos_tk, g_final, ys)


def _prep_in_proj(w_in, b_in, w_gate, b_gate):
    sizes = (GLA_HEADS * GLA_DK, GLA_HEADS * GLA_DK, GLA_HEADS * GLA_DV, GLA_HEADS * GLA_DV, GLA_GATE_RANK,
             SWA_Q_HEADS * SWA_HEAD_DIM, SWA_KV_HEADS * SWA_HEAD_DIM, SWA_KV_HEADS * SWA_HEAD_DIM, _D, _D)
    offs = [0]
    for s in sizes:
        offs.append(offs[-1] + s)

    def rearrange(m, dtype):
        p = [m[..., offs[i]:offs[i + 1]].astype(dtype) for i in range(len(sizes))]
        gq, gk, gv, gr, lr, sq, sk, sv, ga, gb = p

        def dup_heads(a):
            hs = [a[..., SWA_HEAD_DIM * h:SWA_HEAD_DIM * (h + 1)] for h in range(SWA_KV_HEADS)]
            return jnp.concatenate([hh for h in hs for hh in (h, h)], axis=-1)

        lr_pad = jnp.pad(lr, [(0, 0)] * (lr.ndim - 1) + [(0, LANES - GLA_GATE_RANK)])
        return jnp.concatenate([gq, gk, gv, gr, sq, ga, gb, dup_heads(sk), dup_heads(sv), lr_pad], axis=-1)

    w_all = rearrange(w_in, BF16)
    b_all = rearrange(b_in[None, :], F32)
    wg = jnp.pad(w_gate, ((0, LANES - GLA_GATE_RANK), (0, 0)))
    return w_all, b_all, wg, b_gate[None, :]


def _rope_tables(seq):
    half = SWA_HEAD_DIM // 2
    inv_freq = ROPE_THETA ** (-jnp.arange(half, dtype=F32) / half)
    ang = jnp.arange(seq, dtype=F32)[:, None] * inv_freq[None, :]
    cos, sin = jnp.cos(ang), jnp.sin(ang)
    cos_t = jnp.concatenate([cos, cos] * (LANES // SWA_HEAD_DIM), axis=1)
    sin_t = jnp.concatenate([-sin, sin] * (LANES // SWA_HEAD_DIM), axis=1)
    return cos_t, sin_t


def _pair_split_perm():
    src = jnp.arange(2 * LANES, dtype=I32)
    dst = jnp.where(src % 2 == 0, src // 2, LANES + src // 2)
    return (dst[:, None] == jnp.arange(2 * LANES, dtype=I32)[None, :]).astype(BF16)


def kernel(x, g_mix, w_in, b_in, w_gla_gate, b_gla_gate, g_gla_head, w_gla_out, sinks, w_swa_out, w_out,
           g_ffn, w_router, b_router, w_e1, b_e1, w_e2, b_e2, g_final):
    bsz, seq, d = x.shape
    assert d == _D and w_in.shape[0] == 1, "single-layer, d_model=1024 only"
    assert seq % max(PROJ_TM, SWA_WINDOW, GLA_CHUNK) == 0
    t = bsz * seq
    assert t % max(MERGE_TM, ROUTE_TT, FINAL_TM, DISPATCH_TM) == 0
    assert FINAL_TM == ROUTE_TT and t // ROUTE_TT <= LANES and 3 * N_EXPERTS <= LANES
    x2 = x.reshape(t, d)

    w_all, b_all, wg, bg = _prep_in_proj(w_in[0], b_in[0], w_gla_gate[0], b_gla_gate[0])
    cos_t, sin_t = _rope_tables(seq)
    gq, gk, gv, gr, sq, ga, gb, sk, sv, lg = _in_proj(x2, g_mix, w_all, b_all, cos_t, sin_t, wg, bg, seq)
    oa = _gla(gq, gk, gv, lg, gr, g_gla_head, bsz, seq)
    ob = _swa(sinks[0], sq, sk, sv, bsz, seq)
    x1, hn, logits_t = _merge(x2, oa, ob, ga, gb, w_gla_out[0].astype(BF16), w_swa_out[0].astype(BF16),
                              w_out[0].astype(BF16), g_ffn, w_router[0].T, b_router[0][:, None])

    e_kt, g_kt, r_kt, cum = _route(logits_t)
    n_tiles = t // ROUTE_TT
    cum = cum[:, :n_tiles].astype(I32)
    counts = cum[:, -1]
    blocks_e = (counts + MOE_BLOCK - 1) // MOE_BLOCK
    bend = jnp.cumsum(blocks_e)
    pstart = (bend - blocks_e) * MOE_BLOCK
    dest_kt = r_kt
    for e in range(N_EXPERTS):
        dest_kt = dest_kt + jnp.where(e_kt == e, pstart[e], 0)
    dest_flat = dest_kt.T.reshape(t * TOP_K)
    n_blocks = (t * TOP_K) // MOE_BLOCK + N_EXPERTS
    block_e = jnp.minimum(jnp.sum(bend[None, :] <= jnp.arange(n_blocks, dtype=I32)[:, None], axis=1),
                          N_EXPERTS - 1).astype(I32)
    n_valid = bend[-1:].astype(I32)

    before = jnp.concatenate([jnp.zeros((N_EXPERTS, 1), I32), cum[:, :-1]], axis=1)
    run_len = cum - before
    run_src = pstart[:, None] + before
    cov_src = run_src // SUBLANES * SUBLANES
    cov_len = jnp.where(run_len > 0, (run_src + run_len + SUBLANES - 1) // SUBLANES * SUBLANES - cov_src, 0)
    cov_dst = jnp.cumsum(cov_len, axis=0) - cov_len
    run_table = jnp.concatenate([cov_src.T, cov_len.T // SUBLANES, cov_dst.T,
                                 jnp.zeros((n_tiles, LANES - 3 * N_EXPERTS), I32)], axis=1)
    shift = jnp.repeat(cov_dst + run_src - cov_src - before, ROUTE_TT, axis=1)
    pos_kt = r_kt
    for e in range(N_EXPERTS):
        pos_kt = pos_kt + jnp.where(e_kt == e, shift[e][None, :], 0)

    b1 = b_e1[0].reshape(N_EXPERTS, -1, LANES, 2).transpose(0, 1, 3, 2).reshape(N_EXPERTS, 1, -1)
    xs = _dispatch(bend * MOE_BLOCK, blocks_e, n_valid, dest_flat, hn, n_blocks * MOE_BLOCK)
    first = jnp.concatenate([jnp.ones((1,), I32), (block_e[1:] != block_e[:-1]).astype(I32)])
    ys = _experts(block_e, n_valid, first, xs, w_e1[0], b1, w_e2[0], b_e2[0][:, None, :], _pair_split_perm())
    out = _final(run_table.reshape(n_tiles, 1, LANES), x1, g_kt.T, pos_kt.T, g_final[None, :], ys)
    return out.reshape(bsz, seq, d)
```

```python
import jax
import jax.numpy as jnp
import numpy as np
from jax import lax
from jax.experimental import pallas as pl
from jax.experimental.pallas import tpu as pltpu

F32 = jnp.float32
BF16 = jnp.bfloat16
I32 = jnp.int32

NORM_EPS = 1e-5
GLA_HEADS = 4
GLA_DK = 128
GLA_DV = 256
GLA_GATE_RANK = 16
GLA_TAU = 16.0
SWA_Q_HEADS = 16
SWA_KV_HEADS = 2
SWA_GROUP = SWA_Q_HEADS // SWA_KV_HEADS
SWA_HEAD_DIM = 64
SWA_WINDOW = 128
ROPE_THETA = 10000.0
N_EXPERTS = 32
TOP_K = 4
SWIGLU_LIMIT = 7.0
SWIGLU_ALPHA = 1.702

LANES = 128
SUBLANES = 8
NEG = -1e30
VMEM_LIMIT = 56 * 1024 * 1024

PROJ_TM = 256
GLA_CHUNK = 128
GLA_HEADS_PER_STEP = 4
MERGE_TM = 512
MERGE_TN = 256
MERGE_ROW_GROUPS = 1
ROUTE_TT = 512
MOE_BLOCK = 512
DISPATCH_TM = 512
FINAL_TM = 512
FINAL_KC = 512

_D = 1024
_C_GQ = 0
_C_GK = _C_GQ + GLA_HEADS * GLA_DK
_C_GV = _C_GK + GLA_HEADS * GLA_DK
_C_GR = _C_GV + GLA_HEADS * GLA_DV
_C_SQ = _C_GR + GLA_HEADS * GLA_DV
_C_GA = _C_SQ + SWA_Q_HEADS * SWA_HEAD_DIM
_C_GB = _C_GA + _D
_C_SK = _C_GB + _D
_C_SV = _C_SK + SWA_KV_HEADS * LANES
_C_LR = _C_SV + SWA_KV_HEADS * LANES
_C_END = _C_LR + LANES


def _cparams(sem):
    return pltpu.CompilerParams(dimension_semantics=sem, vmem_limit_bytes=VMEM_LIMIT)


def _rms(x, g):
    return x * lax.rsqrt(jnp.mean(x * x, axis=-1, keepdims=True) + NORM_EPS) * g


def _sigmoid(x):
    return 1.0 / (1.0 + jnp.exp(-x))


def _dot_nt(a, b):
    return lax.dot_general(a, b, (((1,), (1,)), ((), ())), preferred_element_type=F32)


def _rope_slabs(acc, cos, sin, first_half):
    outs = []
    for i in range(acc.shape[1] // LANES):
        xs = acc[:, LANES * i:LANES * (i + 1)]
        partner = jnp.where(first_half, pltpu.roll(xs, LANES - 32, 1), pltpu.roll(xs, 32, 1))
        outs.append(xs * cos + partner * sin)
    return jnp.concatenate(outs, axis=1)


def _in_proj_kernel(x_ref, g_ref, w_ref, b_ref, cos_ref, sin_ref, wg_ref, bg_ref,
                    gq_ref, gk_ref, gv_ref, gr_ref, sq_ref, ga_ref, gb_ref, sk_ref, sv_ref, lg_ref):
    h = _rms(x_ref[...], g_ref[...]).astype(BF16)

    def proj(lo, hi):
        return jnp.dot(h, w_ref[:, lo:hi], preferred_element_type=F32) + b_ref[:, lo:hi]

    cos = cos_ref[...]
    sin = sin_ref[...]
    lane = lax.broadcasted_iota(I32, cos.shape, 1)
    first_half = (lane % SWA_HEAD_DIM) < (SWA_HEAD_DIM // 2)

    gq_ref[...] = (proj(_C_GQ, _C_GK) * (GLA_DK ** -0.5)).astype(BF16)
    gk_ref[...] = proj(_C_GK, _C_GV).astype(BF16)
    gv_ref[...] = proj(_C_GV, _C_GR).astype(BF16)
    gr = proj(_C_GR, _C_SQ)
    gr_ref[...] = (gr * _sigmoid(gr)).astype(BF16)
    sq = proj(_C_SQ, _C_GA) * (SWA_HEAD_DIM ** -0.5)
    sq_ref[...] = _rope_slabs(sq, cos, sin, first_half).astype(BF16)
    ga_ref[...] = _sigmoid(proj(_C_GA, _C_GB)).astype(BF16)
    gb_ref[...] = _sigmoid(proj(_C_GB, _C_SK)).astype(BF16)
    sk_ref[...] = _rope_slabs(proj(_C_SK, _C_SV), cos, sin, first_half).astype(BF16)
    sv_ref[...] = proj(_C_SV, _C_LR).astype(BF16)
    z = jnp.dot(proj(_C_LR, _C_END), wg_ref[...], precision=lax.Precision.HIGHEST,
                preferred_element_type=F32) + bg_ref[...]
    log_sig = jnp.minimum(z, 0.0) - jnp.log(1.0 + jnp.exp(-jnp.abs(z)))
    lg_ref[...] = log_sig * (1.0 / GLA_TAU)


def _in_proj(x2, g_mix, w_all, b_all, cos_t, sin_t, wg, bg, seq):
    t = x2.shape[0]
    tm = PROJ_TM
    pos_blocks = seq // tm
    const = lambda i: (0, 0)
    row = lambda i: (i, 0)
    widths = [(_C_GK - _C_GQ, BF16), (_C_GV - _C_GK, BF16), (_C_GR - _C_GV, BF16), (_C_SQ - _C_GR, BF16),
              (_C_GA - _C_SQ, BF16), (_D, BF16), (_D, BF16), (_C_SV - _C_SK, BF16), (_C_LR - _C_SV, BF16),
              (GLA_HEADS * GLA_DK, F32)]
    return pl.pallas_call(
        _in_proj_kernel,
        grid=(t // tm,),
        in_specs=[
            pl.BlockSpec((tm, _D), row),
            pl.BlockSpec((1, _D), const),
            pl.BlockSpec((_D, _C_END), const, pipeline_mode=pl.Buffered(1)),
            pl.BlockSpec((1, _C_END), const),
            pl.BlockSpec((tm, LANES), lambda i: (i % pos_blocks, 0)),
            pl.BlockSpec((tm, LANES), lambda i: (i % pos_blocks, 0)),
            pl.BlockSpec((LANES, GLA_HEADS * GLA_DK), const),
            pl.BlockSpec((1, GLA_HEADS * GLA_DK), const),
        ],
        out_specs=[pl.BlockSpec((tm, w), row) for w, _ in widths],
        out_shape=[jax.ShapeDtypeStruct((t, w), dt) for w, dt in widths],
        compiler_params=_cparams(("parallel",)),
        name="in_proj",
    )(x2, g_mix, w_all, b_all, cos_t, sin_t, wg, bg)


def _gla_cumsum_operator(c_len):
    t = np.arange(c_len)[:, None]
    r = np.arange(c_len)[None, :]
    return np.tile((r <= t).astype(np.float32), (1, 3))


def _gla_kernel(q_ref, k_ref, v_ref, lg_ref, gr_ref, gh_ref, dm_ref, o_ref, st_ref):
    c_len = GLA_CHUNK
    n_lev = c_len.bit_length() - 1
    seq = q_ref.shape[0]
    st_ref[...] = jnp.zeros_like(st_ref)

    t_i = lax.broadcasted_iota(I32, (c_len, c_len), 0)
    j_i = lax.broadcasted_iota(I32, (c_len, c_len), 1)
    row = lax.broadcasted_iota(I32, (c_len, 1), 0)
    diag = t_i == j_i
    upper, pair = [], []
    for lev in range(n_lev):
        s = c_len >> (lev + 1)
        upper.append((row & s) != 0)
        pair.append(((t_i // (2 * s)) == (j_i // (2 * s))) & ((t_i & s) != 0) & ((j_i & s) == 0))

    sub8 = lax.broadcasted_iota(I32, (c_len // 8, 8, GLA_DK), 1)

    def boundary_rows(b, s):
        if s >= 4:
            b3 = b.reshape(c_len // (2 * s), 2 * s, GLA_DK)
            return jnp.broadcast_to(b3[:, s - 1:s, :], b3.shape).reshape(c_len, GLA_DK)
        b3 = b.reshape(c_len // 8, 8, GLA_DK)
        lo = jnp.broadcast_to(b3[:, 1:2, :], b3.shape)
        hi = jnp.broadcast_to(b3[:, 5:6, :], b3.shape)
        return jnp.where(sub8 < 4, lo, hi).reshape(c_len, GLA_DK)

    def head_chunk(r0, hh):
        kcols = slice(GLA_DK * hh, GLA_DK * (hh + 1))
        vcols = slice(GLA_DV * hh, GLA_DV * (hh + 1))
        q_bf = q_ref[pl.ds(r0, c_len), kcols]
        k_bf = k_ref[pl.ds(r0, c_len), kcols]
        q = q_bf.astype(F32)
        k = k_bf.astype(F32)
        v = v_ref[pl.ds(r0, c_len), vcols]

        lg = lg_ref[pl.ds(r0, c_len), kcols]
        lg_hi = lg.astype(BF16)
        rem = lg - lg_hi.astype(F32)
        lg_mid = rem.astype(BF16)
        lg_lo = (rem - lg_mid.astype(F32)).astype(BF16)
        b = jnp.dot(dm_ref[...], jnp.concatenate([lg_hi, lg_mid, lg_lo], axis=0),
                    preferred_element_type=F32)
        w_cum = jnp.exp(b)

        st = st_ref[hh]
        o = _dot_nt((q * w_cum).astype(BF16), st.astype(BF16))

        a = jnp.where(diag, _dot_nt(q_bf, k_bf), 0.0)
        for lev in range(n_lev):
            s = c_len >> (lev + 1)
            if s == 1:
                w = jnp.where(upper[lev], jnp.exp(lg), 1.0)
            else:
                w = jnp.exp(-jnp.abs(b - boundary_rows(b, s)))
            z = (jnp.where(upper[lev], q, k) * w).astype(BF16)
            a = jnp.where(pair[lev], _dot_nt(z, z), a)
        o = o + jnp.dot(a.astype(BF16), v, preferred_element_type=F32)

        b_last = b[c_len - 1:c_len, :]
        upd = lax.dot_general(v, (k * jnp.exp(b_last - b)).astype(BF16), (((0,), (0,)), ((), ())),
                              preferred_element_type=F32)
        st_ref[hh] = st * w_cum[c_len - 1:c_len, :] + upd

        on = _rms(o, gh_ref[...])
        o_ref[pl.ds(r0, c_len), vcols] = (on * gr_ref[pl.ds(r0, c_len), vcols].astype(F32)).astype(BF16)

    def chunk(c, carry):
        r0 = pl.multiple_of(c * c_len, c_len)
        for hh in range(GLA_HEADS_PER_STEP):
            head_chunk(r0, hh)
        return carry

    lax.fori_loop(0, seq // c_len, chunk, 0, unroll=2)


def _gla(gq, gk, gv, lg, gr, g_head, bsz, seq):
    t = gq.shape[0]
    hs = GLA_HEADS_PER_STEP
    dmat = jnp.asarray(_gla_cumsum_operator(GLA_CHUNK), dtype=BF16)
    return pl.pallas_call(
        _gla_kernel,
        grid=(bsz, GLA_HEADS // hs),
        in_specs=[
            pl.BlockSpec((seq, hs * GLA_DK), lambda b, h: (b, h)),
            pl.BlockSpec((seq, hs * GLA_DK), lambda b, h: (b, h)),
            pl.BlockSpec((seq, hs * GLA_DV), lambda b, h: (b, h)),
            pl.BlockSpec((seq, hs * GLA_DK), lambda b, h: (b, h)),
            pl.BlockSpec((seq, hs * GLA_DV), lambda b, h: (b, h)),
            pl.BlockSpec((1, GLA_DV), lambda b, h: (0, 0)),
            pl.BlockSpec(dmat.shape, lambda b, h: (0, 0)),
        ],
        out_specs=pl.BlockSpec((seq, hs * GLA_DV), lambda b, h: (b, h)),
        out_shape=jax.ShapeDtypeStruct((t, GLA_HEADS * GLA_DV), BF16),
        scratch_shapes=[pltpu.VMEM((hs, GLA_DV, GLA_DK), F32)],
        compiler_params=_cparams(("parallel", "parallel")),
        name="gla",
    )(gq, gk, gv, lg, gr, g_head, dmat)


def _swa_kernel(sink_ref, q_ref, k_ref, v_ref, o_ref):
    w = SWA_WINDOW
    seq = q_ref.shape[0]
    hk = pl.program_id(1)
    lane_q = lax.broadcasted_iota(I32, (w, LANES), 1)
    low_q = lane_q < SWA_HEAD_DIM
    lane_b = lax.broadcasted_iota(I32, (2 * w, LANES), 1)
    low_b = lane_b < SWA_HEAD_DIM
    qi = lax.broadcasted_iota(I32, (w, 2 * w), 0)
    kj = lax.broadcasted_iota(I32, (w, 2 * w), 1)
    in_window = (kj > qi) & (kj <= qi + w)
    zero_q = jnp.zeros((w, LANES), BF16)
    zero_b = jnp.zeros((2 * w, LANES), BF16)

    def block(n, carry):
        r0 = pl.multiple_of(n * w, w)
        p0 = pl.multiple_of(jnp.maximum(n - 1, 0) * w, w)
        kb = jnp.concatenate([k_ref[pl.ds(p0, w), :], k_ref[pl.ds(r0, w), :]], axis=0)
        vb = jnp.concatenate([v_ref[pl.ds(p0, w), :], v_ref[pl.ds(r0, w), :]], axis=0)
        valid = in_window & ((kj >= w) | (n > 0))
        v_lo = jnp.where(low_b, vb, zero_b)
        v_hi = jnp.where(low_b, zero_b, vb)
        for m in range(SWA_GROUP // 2):
            qp = q_ref[pl.ds(r0, w), LANES * m:LANES * (m + 1)]
            acc = jnp.zeros((w, LANES), F32)
            for par in range(2):
                qm = jnp.where(low_q, qp, zero_q) if par == 0 else jnp.where(low_q, zero_q, qp)
                s = jnp.where(valid, _dot_nt(qm, kb), NEG)
                sink = sink_ref[hk * SWA_GROUP + 2 * m + par]
                mx = jnp.maximum(jnp.max(s, axis=-1, keepdims=True), sink)
                p = jnp.exp(s - mx)
                den = jnp.sum(p, axis=-1, keepdims=True) + jnp.exp(sink - mx)
                pv = jnp.dot(p.astype(BF16), v_lo if par == 0 else v_hi, preferred_element_type=F32)
                acc = acc + pv / den
            o_ref[pl.ds(r0, w), LANES * m:LANES * (m + 1)] = acc.astype(BF16)
        return carry

    lax.fori_loop(0, seq // w, block, 0, unroll=2)


def _swa(sinks, sq, sk, sv, bsz, seq):
    t = sq.shape[0]
    gw = SWA_GROUP * SWA_HEAD_DIM
    return pl.pallas_call(
        _swa_kernel,
        grid_spec=pltpu.PrefetchScalarGridSpec(
            num_scalar_prefetch=1,
            grid=(bsz, SWA_KV_HEADS),
            in_specs=[
                pl.BlockSpec((seq, gw), lambda b, h, s: (b, h)),
                pl.BlockSpec((seq, LANES), lambda b, h, s: (b, h)),
                pl.BlockSpec((seq, LANES), lambda b, h, s: (b, h)),
            ],
            out_specs=pl.BlockSpec((seq, gw), lambda b, h, s: (b, h)),
        ),
        out_shape=jax.ShapeDtypeStruct((t, SWA_Q_HEADS * SWA_HEAD_DIM), BF16),
        compiler_params=_cparams(("parallel", "parallel")),
        name="swa",
    )(sinks, sq, sk, sv)


def _merge_kernel(x_ref, oa_ref, ob_ref, ga_ref, gb_ref, wa_ref, wb_ref, wo_ref, gf_ref, wr_ref, br_ref,
                  x1_ref, hn_ref, lt_ref, mixed_ref):
    rows_g = x_ref.shape[0] // MERGE_ROW_GROUPS
    for h in range(MERGE_ROW_GROUPS):
        rows = slice(rows_g * h, rows_g * (h + 1))
        oa = oa_ref[rows, :]
        ob = ob_ref[rows, :]
        for n in range(_D // MERGE_TN):
            cols = slice(MERGE_TN * n, MERGE_TN * (n + 1))
            ya = ga_ref[rows, cols].astype(F32) * jnp.dot(oa, wa_ref[:, cols], preferred_element_type=F32)
            yb = gb_ref[rows, cols].astype(F32) * jnp.dot(ob, wb_ref[:, cols], preferred_element_type=F32)
            mixed_ref[rows, cols] = (ya + yb).astype(BF16)
        mixed = mixed_ref[rows, :]
        for n in range(_D // MERGE_TN):
            cols = slice(MERGE_TN * n, MERGE_TN * (n + 1))
            x1_ref[rows, cols] = x_ref[rows, cols] + jnp.dot(mixed, wo_ref[:, cols], preferred_element_type=F32)
        hn = _rms(x1_ref[rows, :], gf_ref[...])
        hn_ref[rows, :] = hn
        lt_ref[:, rows] = lax.dot_general(wr_ref[...], hn, (((1,), (1,)), ((), ())),
                                          precision=lax.Precision.HIGHEST,
                                          preferred_element_type=F32) + br_ref[...]


def _merge(x2, oa, ob, ga, gb, wa, wb, wo, g_ffn, wr_t, br_col):
    t = x2.shape[0]
    tm = MERGE_TM
    row = lambda i: (i, 0)
    const = lambda i: (0, 0)
    return pl.pallas_call(
        _merge_kernel,
        grid=(t // tm,),
        in_specs=[pl.BlockSpec((tm, _D), row)] * 5 + [pl.BlockSpec((_D, _D), const)] * 3 + [
            pl.BlockSpec((1, _D), const),
            pl.BlockSpec((N_EXPERTS, _D), const),
            pl.BlockSpec((N_EXPERTS, 1), const),
        ],
        out_specs=[pl.BlockSpec((tm, _D), row), pl.BlockSpec((tm, _D), row),
                   pl.BlockSpec((N_EXPERTS, tm), lambda i: (0, i))],
        out_shape=[jax.ShapeDtypeStruct((t, _D), F32), jax.ShapeDtypeStruct((t, _D), F32),
                   jax.ShapeDtypeStruct((N_EXPERTS, t), F32)],
        scratch_shapes=[pltpu.VMEM((tm, _D), BF16)],
        compiler_params=_cparams(("parallel",)),
        name="merge",
    )(x2, oa, ob, ga, gb, wa, wb, wo, g_ffn, wr_t, br_col)


def _route_kernel(lt_ref, e_ref, g_ref, r_ref, cum_ref, carry_ref):
    tt = lt_ref.shape[1]

    @pl.when(pl.program_id(0) == 0)
    def _():
        carry_ref[...] = jnp.zeros_like(carry_ref)
        cum_ref[...] = jnp.zeros_like(cum_ref)

    eid = lax.broadcasted_iota(I32, (N_EXPERTS, tt), 0)
    work = lt_ref[...]
    vals, idxs = [], []
    chosen = jnp.zeros((N_EXPERTS, tt), F32)
    for _ in range(TOP_K):
        m = jnp.max(work, axis=0, keepdims=True)
        idx = jnp.min(jnp.where(work == m, eid, N_EXPERTS), axis=0, keepdims=True)
        hit = eid == idx
        work = jnp.where(hit, -jnp.inf, work)
        chosen = jnp.where(hit, 1.0, chosen)
        vals.append(m)
        idxs.append(idx)
    ex = [jnp.exp(v - vals[0]) for v in vals]
    den = ex[0] + ex[1] + ex[2] + ex[3]

    t_r = lax.broadcasted_iota(I32, (tt, tt), 0)
    t_c = lax.broadcasted_iota(I32, (tt, tt), 1)
    before = (t_r < t_c).astype(BF16)
    pref = jnp.dot(chosen.astype(BF16), before, preferred_element_type=F32) + carry_ref[:, 0:1]
    for k in range(TOP_K):
        e_ref[k:k + 1, :] = idxs[k]
        g_ref[k:k + 1, :] = ex[k] / den
        r_ref[k:k + 1, :] = jnp.sum(jnp.where(eid == idxs[k], pref, 0.0), axis=0, keepdims=True).astype(I32)
    total = pref[:, tt - 1:tt] + chosen[:, tt - 1:tt]
    carry_ref[...] = jnp.broadcast_to(total, carry_ref.shape)
    tile_lane = lax.broadcasted_iota(I32, cum_ref.shape, 1) == pl.program_id(0)
    cum_ref[...] = jnp.where(tile_lane, total, cum_ref[...])


def _route(logits_t):
    t = logits_t.shape[1]
    tt = ROUTE_TT
    blk = lambda i: (0, i)
    return pl.pallas_call(
        _route_kernel,
        grid=(t // tt,),
        in_specs=[pl.BlockSpec((N_EXPERTS, tt), blk)],
        out_specs=[pl.BlockSpec((TOP_K, tt), blk), pl.BlockSpec((TOP_K, tt), blk), pl.BlockSpec((TOP_K, tt), blk),
                   pl.BlockSpec((N_EXPERTS, LANES), lambda i: (0, 0))],
        out_shape=[jax.ShapeDtypeStruct((TOP_K, t), I32), jax.ShapeDtypeStruct((TOP_K, t), F32),
                   jax.ShapeDtypeStruct((TOP_K, t), I32), jax.ShapeDtypeStruct((N_EXPERTS, LANES), F32)],
        scratch_shapes=[pltpu.VMEM((N_EXPERTS, LANES), F32)],
        compiler_params=_cparams(("arbitrary",)),
        name="route",
    )(logits_t)


def _dispatch_kernel(pend_ref, nblk_ref, nv_ref, dest_ref, hn_ref, xs_hbm, zero_ref, sem, zsem):
    n_tok = hn_ref.shape[0]
    n_blocks = xs_hbm.shape[0] // MOE_BLOCK

    def zero_block(row0):
        return pltpu.make_async_copy(zero_ref, xs_hbm.at[pl.ds(pl.multiple_of(row0, MOE_BLOCK), MOE_BLOCK)], zsem)

    @pl.when(pl.program_id(0) == 0)
    def _():
        zero_ref[...] = jnp.zeros_like(zero_ref)

        def expert_tail(e, carry, start):
            @pl.when(nblk_ref[e] > 0)
            def _():
                cp = zero_block(pend_ref[e] - MOE_BLOCK)
                cp.start() if start else cp.wait()
            return carry

        def unused_block(b, carry, start):
            cp = zero_block(b * MOE_BLOCK)
            cp.start() if start else cp.wait()
            return carry

        for start in (True, False):
            lax.fori_loop(0, N_EXPERTS, lambda e, c: expert_tail(e, c, start), 0)
            lax.fori_loop(nv_ref[0], n_blocks, lambda b, c: unused_block(b, c, start), 0)

    def issue(i, carry):
        for k in range(TOP_K):
            d = dest_ref[0, 0, i * TOP_K + k]
            pltpu.make_async_copy(hn_ref.at[pl.ds(i, 1)], xs_hbm.at[pl.ds(d, 1)], sem).start()
        return carry

    lax.fori_loop(0, n_tok, issue, 0, unroll=4)
    for k in range(TOP_K):
        pltpu.make_async_copy(hn_ref, xs_hbm.at[pl.ds(0, n_tok)], sem).wait()


def _dispatch(pend_rows, blocks_e, n_valid, dest_flat, hn, rows):
    t = hn.shape[0]
    tm = DISPATCH_TM
    steps = t // tm
    return pl.pallas_call(
        _dispatch_kernel,
        grid_spec=pltpu.PrefetchScalarGridSpec(
            num_scalar_prefetch=3,
            grid=(steps,),
            in_specs=[pl.BlockSpec((1, 1, tm * TOP_K), lambda i, *_: (i, 0, 0), memory_space=pltpu.SMEM),
                      pl.BlockSpec((tm, _D), lambda i, *_: (i, 0))],
            out_specs=pl.BlockSpec(memory_space=pl.ANY),
            scratch_shapes=[pltpu.VMEM((MOE_BLOCK, _D), F32), pltpu.SemaphoreType.DMA, pltpu.SemaphoreType.DMA],
        ),
        out_shape=jax.ShapeDtypeStruct((rows, _D), F32),
        compiler_params=_cparams(("arbitrary",)),
        name="dispatch",
    )(pend_rows, blocks_e, n_valid, dest_flat.reshape(steps, 1, tm * TOP_K), hn)


def _experts_kernel(be_ref, nv_ref, first_ref, slot_ref, next_ref, xs_ref, b1_ref, b2_ref, perm_ref,
                    w1_hbm, w2_hbm, ys_ref, wf1_ref, wf2_ref, w1b_ref, w2b_ref, sems):
    i = pl.program_id(0)
    gw = perm_ref.shape[0]

    def weight_copies(e, slot):
        return (pltpu.make_async_copy(w1_hbm.at[e], wf1_ref.at[slot], sems.at[0, slot]),
                pltpu.make_async_copy(w2_hbm.at[e], wf2_ref.at[slot], sems.at[1, slot]))

    @pl.when(i == 0)
    def _():
        for cp in weight_copies(be_ref[0], 0):
            cp.start()

    @pl.when((i < nv_ref[0]) & (first_ref[i] != 0))
    def _():
        slot = slot_ref[i]
        for cp in weight_copies(be_ref[i], slot):
            cp.wait()

        @pl.when(next_ref[i] >= 0)
        def _():
            for cp in weight_copies(next_ref[i], 1 - slot):
                cp.start()

        for gi in range(wf1_ref.shape[2] // gw):
            wb = wf1_ref[slot, :, gw * gi:gw * (gi + 1)].astype(BF16)
            w1b_ref[:, gw * gi:gw * (gi + 1)] = jnp.dot(wb, perm_ref[...], preferred_element_type=F32).astype(BF16)
        w2b_ref[...] = wf2_ref[slot].astype(BF16)

    @pl.when(i < nv_ref[0])
    def _():
        x = xs_ref[...].astype(BF16)
        u = jnp.dot(x, w1b_ref[...], preferred_element_type=F32) + b1_ref[0]
        acts = []
        for gi in range(u.shape[1] // (2 * LANES)):
            g = jnp.minimum(u[:, 2 * LANES * gi:2 * LANES * gi + LANES], SWIGLU_LIMIT)
            lin = jnp.clip(u[:, 2 * LANES * gi + LANES:2 * LANES * (gi + 1)], -SWIGLU_LIMIT, SWIGLU_LIMIT)
            acts.append((g * _sigmoid(SWIGLU_ALPHA * g) * (lin + 1.0)).astype(BF16))
        act = jnp.concatenate(acts, axis=1)
        ys_ref[...] = jnp.dot(act, w2b_ref[...], preferred_element_type=F32) + b2_ref[0]

    @pl.when(i >= nv_ref[0])
    def _():
        ys_ref[...] = jnp.zeros_like(ys_ref)


def _experts(block_e, n_valid, first, slot, next_e, xs, w1, b1, w2, b2, perm):
    rows = xs.shape[0]
    n_blocks = rows // MOE_BLOCK
    n_e, d, n1 = w1.shape
    dff = w2.shape[1]
    wmap = lambda i, be, *_: (be[i], 0, 0)
    rmap = lambda i, *_: (i, 0)
    return pl.pallas_call(
        _experts_kernel,
        grid_spec=pltpu.PrefetchScalarGridSpec(
            num_scalar_prefetch=5,
            grid=(n_blocks,),
            in_specs=[
                pl.BlockSpec((MOE_BLOCK, _D), rmap),
                pl.BlockSpec((1, 1, n1), wmap),
                pl.BlockSpec((1, 1, _D), wmap),
                pl.BlockSpec(perm.shape, lambda i, *_: (0, 0)),
                pl.BlockSpec(memory_space=pl.ANY),
                pl.BlockSpec(memory_space=pl.ANY),
            ],
            out_specs=pl.BlockSpec((MOE_BLOCK, _D), rmap),
            scratch_shapes=[pltpu.VMEM((2, d, n1), F32), pltpu.VMEM((2, dff, _D), F32),
                            pltpu.VMEM((d, n1), BF16), pltpu.VMEM((dff, _D), BF16),
                            pltpu.SemaphoreType.DMA((2, 2))],
        ),
        out_shape=jax.ShapeDtypeStruct((rows, _D), F32),
        compiler_params=_cparams(("arbitrary",)),
        name="experts",
    )(block_e, n_valid, first, slot, next_e, xs, b1, b2, perm, w1, w2)


def _final_kernel(tcur_ref, tnext_ref, x1_ref, gt_ref, pos_ref, gf_ref, ys_hbm, o_ref, st0_ref, st1_ref, sems):
    n_tok = x1_ref.shape[0]
    n_stage = st0_ref.shape[0]
    step = pl.program_id(0)
    last = pl.num_programs(0) - 1
    stages = (st0_ref, st1_ref)
    max_units = n_tok // SUBLANES + 1
    pieces = [1 << b for b in reversed(range(max_units.bit_length()))]

    def for_each_piece(tab_ref, slot, wait):
        for e in range(N_EXPERTS):
            src = tab_ref[0, 0, e]
            units = tab_ref[0, 0, N_EXPERTS + e]
            dst = tab_ref[0, 0, 2 * N_EXPERTS + e]
            for p in pieces:
                @pl.when((units & p) != 0)
                def _(p=p):
                    done = (units & ~(2 * p - 1)) * SUBLANES
                    cp = pltpu.make_async_copy(
                        ys_hbm.at[pl.ds(pl.multiple_of(src + done, SUBLANES), p * SUBLANES)],
                        stages[slot].at[pl.ds(pl.multiple_of(dst + done, SUBLANES), p * SUBLANES)],
                        sems.at[slot])
                    cp.wait() if wait else cp.start()

    def combine(slot):
        gates = gt_ref[...]
        pos = pos_ref[...]
        y = x1_ref[...]
        for c in range(n_stage // FINAL_KC):
            col = lax.broadcasted_iota(I32, (n_tok, FINAL_KC), 1) + c * FINAL_KC
            sel = jnp.zeros((n_tok, FINAL_KC), F32)
            for k in range(TOP_K):
                sel = jnp.where(col == pos[:, k:k + 1], gates[:, k:k + 1], sel)
            rows = stages[slot][FINAL_KC * c:FINAL_KC * (c + 1), :].astype(BF16)
            y = y + jnp.dot(sel.astype(BF16), rows, preferred_element_type=F32)
        o_ref[...] = _rms(y, gf_ref[...])

    @pl.when(step == 0)
    def _():
        st0_ref[...] = jnp.zeros_like(st0_ref)
        st1_ref[...] = jnp.zeros_like(st1_ref)
        for_each_piece(tcur_ref, 0, wait=False)

    for slot in (0, 1):
        @pl.when(step % 2 == slot)
        def _(slot=slot):
            for_each_piece(tcur_ref, slot, wait=True)

            @pl.when(step < last)
            def _():
                for_each_piece(tnext_ref, 1 - slot, wait=False)

            combine(slot)


def _final(run_table, x1, gates_tk, pos_tk, g_final, ys):
    t = x1.shape[0]
    tm = FINAL_TM
    steps = t // tm
    row = lambda i: (i, 0)
    n_stage = -(-(TOP_K * tm + 2 * (SUBLANES - 1) * N_EXPERTS) // FINAL_KC) * FINAL_KC
    return pl.pallas_call(
        _final_kernel,
        grid=(steps,),
        in_specs=[pl.BlockSpec((1, 1, LANES), lambda i: (i, 0, 0), memory_space=pltpu.SMEM),
                  pl.BlockSpec((1, 1, LANES), lambda i: (jnp.minimum(i + 1, steps - 1), 0, 0),
                               memory_space=pltpu.SMEM),
                  pl.BlockSpec((tm, _D), row), pl.BlockSpec((tm, TOP_K), row), pl.BlockSpec((tm, TOP_K), row),
                  pl.BlockSpec((1, _D), lambda i: (0, 0)), pl.BlockSpec(memory_space=pl.ANY)],
        out_specs=pl.BlockSpec((tm, _D), row),
        out_shape=jax.ShapeDtypeStruct((t, _D), F32),
        scratch_shapes=[pltpu.VMEM((n_stage, _D), F32), pltpu.VMEM((n_stage, _D), F32),
                        pltpu.SemaphoreType.DMA((2,))],
        compiler_params=_cparams(("arbitrary",)),
        name="final",
    )(run_table, run_table, x1, gates_tk, pos_tk, g_final, ys)


def _prep_in_proj(w_in, b_in, w_gate, b_gate):
    sizes = (GLA_HEADS * GLA_DK, GLA_HEADS * GLA_DK, GLA_HEADS * GLA_DV, GLA_HEADS * GLA_DV, GLA_GATE_RANK,
             SWA_Q_HEADS * SWA_HEAD_DIM, SWA_KV_HEADS * SWA_HEAD_DIM, SWA_KV_HEADS * SWA_HEAD_DIM, _D, _D)
    offs = [0]
    for s in sizes:
        offs.append(offs[-1] + s)

    def rearrange(m, dtype):
        p = [m[..., offs[i]:offs[i + 1]].astype(dtype) for i in range(len(sizes))]
        gq, gk, gv, gr, lr, sq, sk, sv, ga, gb = p

        def dup_heads(a):
            hs = [a[..., SWA_HEAD_DIM * h:SWA_HEAD_DIM * (h + 1)] for h in range(SWA_KV_HEADS)]
            return jnp.concatenate([hh for h in hs for hh in (h, h)], axis=-1)

        lr_pad = jnp.pad(lr, [(0, 0)] * (lr.ndim - 1) + [(0, LANES - GLA_GATE_RANK)])
        return jnp.concatenate([gq, gk, gv, gr, sq, ga, gb, dup_heads(sk), dup_heads(sv), lr_pad], axis=-1)

    w_all = rearrange(w_in, BF16)
    b_all = rearrange(b_in[None, :], F32)
    wg = jnp.pad(w_gate, ((0, LANES - GLA_GATE_RANK), (0, 0)))
    return w_all, b_all, wg, b_gate[None, :]


def _rope_tables(seq):
    half = SWA_HEAD_DIM // 2
    inv_freq = ROPE_THETA ** (-jnp.arange(half, dtype=F32) / half)
    ang = jnp.arange(seq, dtype=F32)[:, None] * inv_freq[None, :]
    cos, sin = jnp.cos(ang), jnp.sin(ang)
    cos_t = jnp.concatenate([cos, cos] * (LANES // SWA_HEAD_DIM), axis=1)
    sin_t = jnp.concatenate([-sin, sin] * (LANES // SWA_HEAD_DIM), axis=1)
    return cos_t, sin_t


def _pair_split_perm():
    src = jnp.arange(2 * LANES, dtype=I32)
    dst = jnp.where(src % 2 == 0, src // 2, LANES + src // 2)
    return (dst[:, None] == jnp.arange(2 * LANES, dtype=I32)[None, :]).astype(BF16)


def kernel(x, g_mix, w_in, b_in, w_gla_gate, b_gla_gate, g_gla_head, w_gla_out, sinks, w_swa_out, w_out,
           g_ffn, w_router, b_router, w_e1, b_e1, w_e2, b_e2, g_final):
    bsz, seq, d = x.shape
    assert d == _D and w_in.shape[0] == 1, "single-layer, d_model=1024 only"
    assert seq % max(PROJ_TM, SWA_WINDOW, GLA_CHUNK) == 0
    t = bsz * seq
    assert t % max(MERGE_TM, ROUTE_TT, FINAL_TM, DISPATCH_TM) == 0
    assert FINAL_TM == ROUTE_TT and t // ROUTE_TT <= LANES and 3 * N_EXPERTS <= LANES
    x2 = x.reshape(t, d)

    w_all, b_all, wg, bg = _prep_in_proj(w_in[0], b_in[0], w_gla_gate[0], b_gla_gate[0])
    cos_t, sin_t = _rope_tables(seq)
    gq, gk, gv, gr, sq, ga, gb, sk, sv, lg = _in_proj(x2, g_mix, w_all, b_all, cos_t, sin_t, wg, bg, seq)
    oa = _gla(gq, gk, gv, lg, gr, g_gla_head, bsz, seq)
    ob = _swa(sinks[0], sq, sk, sv, bsz, seq)
    x1, hn, logits_t = _merge(x2, oa, ob, ga, gb, w_gla_out[0].astype(BF16), w_swa_out[0].astype(BF16),
                              w_out[0].astype(BF16), g_ffn, w_router[0].T, b_router[0][:, None])

    e_kt, g_kt, r_kt, cum = _route(logits_t)
    n_tiles = t // ROUTE_TT
    cum = cum[:, :n_tiles].astype(I32)
    counts = cum[:, -1]
    blocks_e = (counts + MOE_BLOCK - 1) // MOE_BLOCK
    bend = jnp.cumsum(blocks_e)
    pstart = (bend - blocks_e) * MOE_BLOCK
    dest_kt = r_kt
    for e in range(N_EXPERTS):
        dest_kt = dest_kt + jnp.where(e_kt == e, pstart[e], 0)
    dest_flat = dest_kt.T.reshape(t * TOP_K)
    n_blocks = (t * TOP_K) // MOE_BLOCK + N_EXPERTS
    block_e = jnp.minimum(jnp.sum(bend[None, :] <= jnp.arange(n_blocks, dtype=I32)[:, None], axis=1),
                          N_EXPERTS - 1).astype(I32)
    n_valid = bend[-1:].astype(I32)

    before = jnp.concatenate([jnp.zeros((N_EXPERTS, 1), I32), cum[:, :-1]], axis=1)
    run_len = cum - before
    run_src = pstart[:, None] + before
    cov_src = run_src // SUBLANES * SUBLANES
    cov_len = jnp.where(run_len > 0, (run_src + run_len + SUBLANES - 1) // SUBLANES * SUBLANES - cov_src, 0)
    cov_dst = jnp.cumsum(cov_len, axis=0) - cov_len
    run_table = jnp.concatenate([cov_src.T, cov_len.T // SUBLANES, cov_dst.T,
                                 jnp.zeros((n_tiles, LANES - 3 * N_EXPERTS), I32)], axis=1)
    shift = jnp.repeat(cov_dst + run_src - cov_src - before, ROUTE_TT, axis=1)
    pos_kt = r_kt
    for e in range(N_EXPERTS):
        pos_kt = pos_kt + jnp.where(e_kt == e, shift[e][None, :], 0)

    b1 = b_e1[0].reshape(N_EXPERTS, -1, LANES, 2).transpose(0, 1, 3, 2).reshape(N_EXPERTS, 1, -1)
    xs = _dispatch(bend * MOE_BLOCK, blocks_e, n_valid, dest_flat, hn, n_blocks * MOE_BLOCK)
    first = jnp.concatenate([jnp.ones((1,), I32), (block_e[1:] != block_e[:-1]).astype(I32)])
    slot = (jnp.cumsum(first) - 1) % 2
    ids = jnp.arange(N_EXPERTS, dtype=I32)
    later = (ids[None, :] > ids[:, None]) & (blocks_e[None, :] > 0)
    next_of = jnp.min(jnp.where(later, ids[None, :], N_EXPERTS), axis=1)
    next_of = jnp.where(next_of < N_EXPERTS, next_of, -1)
    next_e = jnp.sum(jnp.where(block_e[:, None] == ids[None, :], next_of[None, :], 0), axis=1).astype(I32)
    ys = _experts(block_e, n_valid, first, slot.astype(I32), next_e, xs, w_e1[0], b1, w_e2[0],
                  b_e2[0][:, None, :], _pair_split_perm())
    out = _final(run_table.reshape(n_tiles, 1, LANES), x1, g_kt.T, pos_kt.T, g_final[None, :], ys)
    return out.reshape(bsz, seq, d)
```

```python
import jax
import jax.numpy as jnp
import numpy as np
from jax import lax
from jax.experimental import pallas as pl
from jax.experimental.pallas import tpu as pltpu

F32 = jnp.float32
BF16 = jnp.bfloat16
I32 = jnp.int32

NORM_EPS = 1e-5
GLA_HEADS = 4
GLA_DK = 128
GLA_DV = 256
GLA_GATE_RANK = 16
GLA_TAU = 16.0
SWA_Q_HEADS = 16
SWA_KV_HEADS = 2
SWA_GROUP = SWA_Q_HEADS // SWA_KV_HEADS
SWA_HEAD_DIM = 64
SWA_WINDOW = 128
ROPE_THETA = 10000.0
N_EXPERTS = 32
TOP_K = 4
SWIGLU_LIMIT = 7.0
SWIGLU_ALPHA = 1.702

LANES = 128
SUBLANES = 8
NEG = -1e30
VMEM_LIMIT = 56 * 1024 * 1024

PROJ_TM = 256
GLA_CHUNK = 128
GLA_HEADS_PER_STEP = 4
MERGE_TM = 512
MERGE_TN = 256
MERGE_ROW_GROUPS = 1
ROUTE_TT = 512
MOE_BLOCK = 512
DISPATCH_TM = 512
DISPATCH_KC = 512
FINAL_TM = 512
FINAL_KC = 512

_D = 1024
_C_GQ = 0
_C_GK = _C_GQ + GLA_HEADS * GLA_DK
_C_GV = _C_GK + GLA_HEADS * GLA_DK
_C_GR = _C_GV + GLA_HEADS * GLA_DV
_C_SQ = _C_GR + GLA_HEADS * GLA_DV
_C_GA = _C_SQ + SWA_Q_HEADS * SWA_HEAD_DIM
_C_GB = _C_GA + _D
_C_SK = _C_GB + _D
_C_SV = _C_SK + SWA_KV_HEADS * LANES
_C_LR = _C_SV + SWA_KV_HEADS * LANES
_C_END = _C_LR + LANES


def _cparams(sem):
    return pltpu.CompilerParams(dimension_semantics=sem, vmem_limit_bytes=VMEM_LIMIT)


def _rms(x, g):
    return x * lax.rsqrt(jnp.mean(x * x, axis=-1, keepdims=True) + NORM_EPS) * g


def _sigmoid(x):
    return 1.0 / (1.0 + jnp.exp(-x))


def _dot_nt(a, b):
    return lax.dot_general(a, b, (((1,), (1,)), ((), ())), preferred_element_type=F32)


def _rope_slabs(acc, cos, sin, first_half):
    outs = []
    for i in range(acc.shape[1] // LANES):
        xs = acc[:, LANES * i:LANES * (i + 1)]
        partner = jnp.where(first_half, pltpu.roll(xs, LANES - 32, 1), pltpu.roll(xs, 32, 1))
        outs.append(xs * cos + partner * sin)
    return jnp.concatenate(outs, axis=1)


def _in_proj_kernel(x_ref, g_ref, w_ref, b_ref, cos_ref, sin_ref, wg_ref, bg_ref,
                    gq_ref, gk_ref, gv_ref, gr_ref, sq_ref, ga_ref, gb_ref, sk_ref, sv_ref, lg_ref):
    h = _rms(x_ref[...], g_ref[...]).astype(BF16)

    def proj(lo, hi):
        return jnp.dot(h, w_ref[:, lo:hi], preferred_element_type=F32) + b_ref[:, lo:hi]

    cos = cos_ref[...]
    sin = sin_ref[...]
    lane = lax.broadcasted_iota(I32, cos.shape, 1)
    first_half = (lane % SWA_HEAD_DIM) < (SWA_HEAD_DIM // 2)

    gq_ref[...] = (proj(_C_GQ, _C_GK) * (GLA_DK ** -0.5)).astype(BF16)
    gk_ref[...] = proj(_C_GK, _C_GV).astype(BF16)
    gv_ref[...] = proj(_C_GV, _C_GR).astype(BF16)
    gr = proj(_C_GR, _C_SQ)
    gr_ref[...] = (gr * _sigmoid(gr)).astype(BF16)
    sq = proj(_C_SQ, _C_GA) * (SWA_HEAD_DIM ** -0.5)
    sq_ref[...] = _rope_slabs(sq, cos, sin, first_half).astype(BF16)
    ga_ref[...] = _sigmoid(proj(_C_GA, _C_GB)).astype(BF16)
    gb_ref[...] = _sigmoid(proj(_C_GB, _C_SK)).astype(BF16)
    sk_ref[...] = _rope_slabs(proj(_C_SK, _C_SV), cos, sin, first_half).astype(BF16)
    sv_ref[...] = proj(_C_SV, _C_LR).astype(BF16)
    z = jnp.dot(proj(_C_LR, _C_END), wg_ref[...], precision=lax.Precision.HIGHEST,
                preferred_element_type=F32) + bg_ref[...]
    log_sig = jnp.minimum(z, 0.0) - jnp.log(1.0 + jnp.exp(-jnp.abs(z)))
    lg_ref[...] = log_sig * (1.0 / GLA_TAU)


def _in_proj(x2, g_mix, w_all, b_all, cos_t, sin_t, wg, bg, seq):
    t = x2.shape[0]
    tm = PROJ_TM
    pos_blocks = seq // tm
    const = lambda i: (0, 0)
    row = lambda i: (i, 0)
    widths = [(_C_GK - _C_GQ, BF16), (_C_GV - _C_GK, BF16), (_C_GR - _C_GV, BF16), (_C_SQ - _C_GR, BF16),
              (_C_GA - _C_SQ, BF16), (_D, BF16), (_D, BF16), (_C_SV - _C_SK, BF16), (_C_LR - _C_SV, BF16),
              (GLA_HEADS * GLA_DK, F32)]
    return pl.pallas_call(
        _in_proj_kernel,
        grid=(t // tm,),
        in_specs=[
            pl.BlockSpec((tm, _D), row),
            pl.BlockSpec((1, _D), const),
            pl.BlockSpec((_D, _C_END), const, pipeline_mode=pl.Buffered(1)),
            pl.BlockSpec((1, _C_END), const),
            pl.BlockSpec((tm, LANES), lambda i: (i % pos_blocks, 0)),
            pl.BlockSpec((tm, LANES), lambda i: (i % pos_blocks, 0)),
            pl.BlockSpec((LANES, GLA_HEADS * GLA_DK), const),
            pl.BlockSpec((1, GLA_HEADS * GLA_DK), const),
        ],
        out_specs=[pl.BlockSpec((tm, w), row) for w, _ in widths],
        out_shape=[jax.ShapeDtypeStruct((t, w), dt) for w, dt in widths],
        compiler_params=_cparams(("parallel",)),
        name="in_proj",
    )(x2, g_mix, w_all, b_all, cos_t, sin_t, wg, bg)


def _gla_cumsum_operator(c_len):
    t = np.arange(c_len)[:, None]
    r = np.arange(c_len)[None, :]
    return np.tile((r <= t).astype(np.float32), (1, 3))


def _gla_kernel(q_ref, k_ref, v_ref, lg_ref, gr_ref, gh_ref, dm_ref, o_ref, st_ref):
    c_len = GLA_CHUNK
    n_lev = c_len.bit_length() - 1
    seq = q_ref.shape[0]
    st_ref[...] = jnp.zeros_like(st_ref)

    t_i = lax.broadcasted_iota(I32, (c_len, c_len), 0)
    j_i = lax.broadcasted_iota(I32, (c_len, c_len), 1)
    row = lax.broadcasted_iota(I32, (c_len, 1), 0)
    diag = t_i == j_i
    upper, pair = [], []
    for lev in range(n_lev):
        s = c_len >> (lev + 1)
        upper.append((row & s) != 0)
        pair.append(((t_i // (2 * s)) == (j_i // (2 * s))) & ((t_i & s) != 0) & ((j_i & s) == 0))

    sub8 = lax.broadcasted_iota(I32, (c_len // 8, 8, GLA_DK), 1)

    def boundary_rows(b, s):
        if s >= 4:
            b3 = b.reshape(c_len // (2 * s), 2 * s, GLA_DK)
            return jnp.broadcast_to(b3[:, s - 1:s, :], b3.shape).reshape(c_len, GLA_DK)
        b3 = b.reshape(c_len // 8, 8, GLA_DK)
        lo = jnp.broadcast_to(b3[:, 1:2, :], b3.shape)
        hi = jnp.broadcast_to(b3[:, 5:6, :], b3.shape)
        return jnp.where(sub8 < 4, lo, hi).reshape(c_len, GLA_DK)

    def head_chunk(r0, hh):
        kcols = slice(GLA_DK * hh, GLA_DK * (hh + 1))
        vcols = slice(GLA_DV * hh, GLA_DV * (hh + 1))
        q_bf = q_ref[pl.ds(r0, c_len), kcols]
        k_bf = k_ref[pl.ds(r0, c_len), kcols]
        q = q_bf.astype(F32)
        k = k_bf.astype(F32)
        v = v_ref[pl.ds(r0, c_len), vcols]

        lg = lg_ref[pl.ds(r0, c_len), kcols]
        lg_hi = lg.astype(BF16)
        rem = lg - lg_hi.astype(F32)
        lg_mid = rem.astype(BF16)
        lg_lo = (rem - lg_mid.astype(F32)).astype(BF16)
        b = jnp.dot(dm_ref[...], jnp.concatenate([lg_hi, lg_mid, lg_lo], axis=0),
                    preferred_element_type=F32)
        w_cum = jnp.exp(b)

        st = st_ref[hh]
        o = _dot_nt((q * w_cum).astype(BF16), st.astype(BF16))

        a = jnp.where(diag, _dot_nt(q_bf, k_bf), 0.0)
        for lev in range(n_lev):
            s = c_len >> (lev + 1)
            if s == 1:
                w = jnp.where(upper[lev], jnp.exp(lg), 1.0)
            else:
                w = jnp.exp(-jnp.abs(b - boundary_rows(b, s)))
            z = (jnp.where(upper[lev], q, k) * w).astype(BF16)
            a = jnp.where(pair[lev], _dot_nt(z, z), a)
        o = o + jnp.dot(a.astype(BF16), v, preferred_element_type=F32)

        b_last = b[c_len - 1:c_len, :]
        upd = lax.dot_general(v, (k * jnp.exp(b_last - b)).astype(BF16), (((0,), (0,)), ((), ())),
                              preferred_element_type=F32)
        st_ref[hh] = st * w_cum[c_len - 1:c_len, :] + upd

        on = _rms(o, gh_ref[...])
        o_ref[pl.ds(r0, c_len), vcols] = (on * gr_ref[pl.ds(r0, c_len), vcols].astype(F32)).astype(BF16)

    def chunk(c, carry):
        r0 = pl.multiple_of(c * c_len, c_len)
        for hh in range(GLA_HEADS_PER_STEP):
            head_chunk(r0, hh)
        return carry

    lax.fori_loop(0, seq // c_len, chunk, 0, unroll=2)


def _gla(gq, gk, gv, lg, gr, g_head, bsz, seq):
    t = gq.shape[0]
    hs = GLA_HEADS_PER_STEP
    dmat = jnp.asarray(_gla_cumsum_operator(GLA_CHUNK), dtype=BF16)
    return pl.pallas_call(
        _gla_kernel,
        grid=(bsz, GLA_HEADS // hs),
        in_specs=[
            pl.BlockSpec((seq, hs * GLA_DK), lambda b, h: (b, h)),
            pl.BlockSpec((seq, hs * GLA_DK), lambda b, h: (b, h)),
            pl.BlockSpec((seq, hs * GLA_DV), lambda b, h: (b, h)),
            pl.BlockSpec((seq, hs * GLA_DK), lambda b, h: (b, h)),
            pl.BlockSpec((seq, hs * GLA_DV), lambda b, h: (b, h)),
            pl.BlockSpec((1, GLA_DV), lambda b, h: (0, 0)),
            pl.BlockSpec(dmat.shape, lambda b, h: (0, 0)),
        ],
        out_specs=pl.BlockSpec((seq, hs * GLA_DV), lambda b, h: (b, h)),
        out_shape=jax.ShapeDtypeStruct((t, GLA_HEADS * GLA_DV), BF16),
        scratch_shapes=[pltpu.VMEM((hs, GLA_DV, GLA_DK), F32)],
        compiler_params=_cparams(("parallel", "parallel")),
        name="gla",
    )(gq, gk, gv, lg, gr, g_head, dmat)


def _swa_kernel(sink_ref, q_ref, k_ref, v_ref, o_ref):
    w = SWA_WINDOW
    seq = q_ref.shape[0]
    hk = pl.program_id(1)
    lane_q = lax.broadcasted_iota(I32, (w, LANES), 1)
    low_q = lane_q < SWA_HEAD_DIM
    lane_b = lax.broadcasted_iota(I32, (2 * w, LANES), 1)
    low_b = lane_b < SWA_HEAD_DIM
    qi = lax.broadcasted_iota(I32, (w, 2 * w), 0)
    kj = lax.broadcasted_iota(I32, (w, 2 * w), 1)
    in_window = (kj > qi) & (kj <= qi + w)
    zero_q = jnp.zeros((w, LANES), BF16)
    zero_b = jnp.zeros((2 * w, LANES), BF16)

    def block(n, carry):
        r0 = pl.multiple_of(n * w, w)
        p0 = pl.multiple_of(jnp.maximum(n - 1, 0) * w, w)
        kb = jnp.concatenate([k_ref[pl.ds(p0, w), :], k_ref[pl.ds(r0, w), :]], axis=0)
        vb = jnp.concatenate([v_ref[pl.ds(p0, w), :], v_ref[pl.ds(r0, w), :]], axis=0)
        valid = in_window & ((kj >= w) | (n > 0))
        v_lo = jnp.where(low_b, vb, zero_b)
        v_hi = jnp.where(low_b, zero_b, vb)
        for m in range(SWA_GROUP // 2):
            qp = q_ref[pl.ds(r0, w), LANES * m:LANES * (m + 1)]
            acc = jnp.zeros((w, LANES), F32)
            for par in range(2):
                qm = jnp.where(low_q, qp, zero_q) if par == 0 else jnp.where(low_q, zero_q, qp)
                s = jnp.where(valid, _dot_nt(qm, kb), NEG)
                sink = sink_ref[hk * SWA_GROUP + 2 * m + par]
                mx = jnp.maximum(jnp.max(s, axis=-1, keepdims=True), sink)
                p = jnp.exp(s - mx)
                den = jnp.sum(p, axis=-1, keepdims=True) + jnp.exp(sink - mx)
                pv = jnp.dot(p.astype(BF16), v_lo if par == 0 else v_hi, preferred_element_type=F32)
                acc = acc + pv / den
            o_ref[pl.ds(r0, w), LANES * m:LANES * (m + 1)] = acc.astype(BF16)
        return carry

    lax.fori_loop(0, seq // w, block, 0, unroll=2)


def _swa(sinks, sq, sk, sv, bsz, seq):
    t = sq.shape[0]
    gw = SWA_GROUP * SWA_HEAD_DIM
    return pl.pallas_call(
        _swa_kernel,
        grid_spec=pltpu.PrefetchScalarGridSpec(
            num_scalar_prefetch=1,
            grid=(bsz, SWA_KV_HEADS),
            in_specs=[
                pl.BlockSpec((seq, gw), lambda b, h, s: (b, h)),
                pl.BlockSpec((seq, LANES), lambda b, h, s: (b, h)),
                pl.BlockSpec((seq, LANES), lambda b, h, s: (b, h)),
            ],
            out_specs=pl.BlockSpec((seq, gw), lambda b, h, s: (b, h)),
        ),
        out_shape=jax.ShapeDtypeStruct((t, SWA_Q_HEADS * SWA_HEAD_DIM), BF16),
        compiler_params=_cparams(("parallel", "parallel")),
        name="swa",
    )(sinks, sq, sk, sv)


def _merge_kernel(x_ref, oa_ref, ob_ref, ga_ref, gb_ref, wa_ref, wb_ref, wo_ref, gf_ref, wr_ref, br_ref,
                  x1_ref, hn_ref, lt_ref, mixed_ref):
    rows_g = x_ref.shape[0] // MERGE_ROW_GROUPS
    for h in range(MERGE_ROW_GROUPS):
        rows = slice(rows_g * h, rows_g * (h + 1))
        oa = oa_ref[rows, :]
        ob = ob_ref[rows, :]
        for n in range(_D // MERGE_TN):
            cols = slice(MERGE_TN * n, MERGE_TN * (n + 1))
            ya = ga_ref[rows, cols].astype(F32) * jnp.dot(oa, wa_ref[:, cols], preferred_element_type=F32)
            yb = gb_ref[rows, cols].astype(F32) * jnp.dot(ob, wb_ref[:, cols], preferred_element_type=F32)
            mixed_ref[rows, cols] = (ya + yb).astype(BF16)
        mixed = mixed_ref[rows, :]
        for n in range(_D // MERGE_TN):
            cols = slice(MERGE_TN * n, MERGE_TN * (n + 1))
            x1_ref[rows, cols] = x_ref[rows, cols] + jnp.dot(mixed, wo_ref[:, cols], preferred_element_type=F32)
        hn = _rms(x1_ref[rows, :], gf_ref[...])
        hn_ref[rows, :] = hn
        lt_ref[:, rows] = lax.dot_general(wr_ref[...], hn, (((1,), (1,)), ((), ())),
                                          precision=lax.Precision.HIGHEST,
                                          preferred_element_type=F32) + br_ref[...]


def _merge(x2, oa, ob, ga, gb, wa, wb, wo, g_ffn, wr_t, br_col):
    t = x2.shape[0]
    tm = MERGE_TM
    row = lambda i: (i, 0)
    const = lambda i: (0, 0)
    return pl.pallas_call(
        _merge_kernel,
        grid=(t // tm,),
        in_specs=[pl.BlockSpec((tm, _D), row)] * 5 + [pl.BlockSpec((_D, _D), const)] * 3 + [
            pl.BlockSpec((1, _D), const),
            pl.BlockSpec((N_EXPERTS, _D), const),
            pl.BlockSpec((N_EXPERTS, 1), const),
        ],
        out_specs=[pl.BlockSpec((tm, _D), row), pl.BlockSpec((tm, _D), row),
                   pl.BlockSpec((N_EXPERTS, tm), lambda i: (0, i))],
        out_shape=[jax.ShapeDtypeStruct((t, _D), F32), jax.ShapeDtypeStruct((t, _D), F32),
                   jax.ShapeDtypeStruct((N_EXPERTS, t), F32)],
        scratch_shapes=[pltpu.VMEM((tm, _D), BF16)],
        compiler_params=_cparams(("parallel",)),
        name="merge",
    )(x2, oa, ob, ga, gb, wa, wb, wo, g_ffn, wr_t, br_col)


def _route_kernel(lt_ref, e_ref, g_ref, r_ref, cum_ref, carry_ref):
    tt = lt_ref.shape[1]

    @pl.when(pl.program_id(0) == 0)
    def _():
        carry_ref[...] = jnp.zeros_like(carry_ref)
        cum_ref[...] = jnp.zeros_like(cum_ref)

    eid = lax.broadcasted_iota(I32, (N_EXPERTS, tt), 0)
    work = lt_ref[...]
    vals, idxs = [], []
    chosen = jnp.zeros((N_EXPERTS, tt), F32)
    for _ in range(TOP_K):
        m = jnp.max(work, axis=0, keepdims=True)
        idx = jnp.min(jnp.where(work == m, eid, N_EXPERTS), axis=0, keepdims=True)
        hit = eid == idx
        work = jnp.where(hit, -jnp.inf, work)
        chosen = jnp.where(hit, 1.0, chosen)
        vals.append(m)
        idxs.append(idx)
    ex = [jnp.exp(v - vals[0]) for v in vals]
    den = ex[0] + ex[1] + ex[2] + ex[3]

    t_r = lax.broadcasted_iota(I32, (tt, tt), 0)
    t_c = lax.broadcasted_iota(I32, (tt, tt), 1)
    before = (t_r < t_c).astype(BF16)
    pref = jnp.dot(chosen.astype(BF16), before, preferred_element_type=F32) + carry_ref[:, 0:1]
    for k in range(TOP_K):
        e_ref[k:k + 1, :] = idxs[k]
        g_ref[k:k + 1, :] = ex[k] / den
        r_ref[k:k + 1, :] = jnp.sum(jnp.where(eid == idxs[k], pref, 0.0), axis=0, keepdims=True).astype(I32)
    total = pref[:, tt - 1:tt] + chosen[:, tt - 1:tt]
    carry_ref[...] = jnp.broadcast_to(total, carry_ref.shape)
    tile_lane = lax.broadcasted_iota(I32, cum_ref.shape, 1) == pl.program_id(0)
    cum_ref[...] = jnp.where(tile_lane, total, cum_ref[...])


def _route(logits_t):
    t = logits_t.shape[1]
    tt = ROUTE_TT
    blk = lambda i: (0, i)
    return pl.pallas_call(
        _route_kernel,
        grid=(t // tt,),
        in_specs=[pl.BlockSpec((N_EXPERTS, tt), blk)],
        out_specs=[pl.BlockSpec((TOP_K, tt), blk), pl.BlockSpec((TOP_K, tt), blk), pl.BlockSpec((TOP_K, tt), blk),
                   pl.BlockSpec((N_EXPERTS, LANES), lambda i: (0, 0))],
        out_shape=[jax.ShapeDtypeStruct((TOP_K, t), I32), jax.ShapeDtypeStruct((TOP_K, t), F32),
                   jax.ShapeDtypeStruct((TOP_K, t), I32), jax.ShapeDtypeStruct((N_EXPERTS, LANES), F32)],
        scratch_shapes=[pltpu.VMEM((N_EXPERTS, LANES), F32)],
        compiler_params=_cparams(("arbitrary",)),
        name="route",
    )(logits_t)


def _dispatch_kernel(pend_ref, nblk_ref, nv_ref, tab_ref, tprev_ref, hn_ref, pos_ref, xs_hbm,
                     st0_ref, st1_ref, carry_ref, zero_ref, sems, zsem):
    n_tok = hn_ref.shape[0]
    n_stage = st0_ref.shape[0]
    n_blocks = xs_hbm.shape[0] // MOE_BLOCK
    step = pl.program_id(0)
    last = pl.num_programs(0) - 1
    stages = (st0_ref, st1_ref)
    max_units = n_tok // SUBLANES + 1
    pieces = [1 << b for b in reversed(range(max_units.bit_length()))]

    def zero_block(row0):
        return pltpu.make_async_copy(zero_ref, xs_hbm.at[pl.ds(pl.multiple_of(row0, MOE_BLOCK), MOE_BLOCK)], zsem)

    @pl.when(step == 0)
    def _():
        zero_ref[...] = jnp.zeros_like(zero_ref)
        carry_ref[...] = jnp.zeros_like(carry_ref)

        def expert_tail(e, carry, start):
            @pl.when(nblk_ref[e] > 0)
            def _():
                cp = zero_block(pend_ref[e] - MOE_BLOCK)
                cp.start() if start else cp.wait()
            return carry

        def unused_block(b, carry, start):
            cp = zero_block(b * MOE_BLOCK)
            cp.start() if start else cp.wait()
            return carry

        for start in (True, False):
            lax.fori_loop(0, N_EXPERTS, lambda e, c: expert_tail(e, c, start), 0)
            lax.fori_loop(nv_ref[0], n_blocks, lambda b, c: unused_block(b, c, start), 0)

    def permute(slot):
        hb = hn_ref[...].astype(BF16)
        pos = pos_ref[...]
        for c in range(n_stage // DISPATCH_KC):
            row = lax.broadcasted_iota(I32, (DISPATCH_KC, n_tok), 0) + c * DISPATCH_KC
            sel = jnp.zeros((DISPATCH_KC, n_tok), F32)
            for k in range(TOP_K):
                sel = jnp.where(row == pos[k:k + 1, :], 1.0, sel)
            stages[slot][DISPATCH_KC * c:DISPATCH_KC * (c + 1), :] = jnp.dot(
                sel.astype(BF16), hb, preferred_element_type=F32)
        sub = lax.broadcasted_iota(I32, (SUBLANES, hn_ref.shape[1]), 0)
        for e in range(N_EXPERTS):
            units = tab_ref[0, 0, N_EXPERTS + e]

            @pl.when(units > 0)
            def _(e=e, units=units):
                first = pl.multiple_of(tab_ref[0, 0, 2 * N_EXPERTS + e], SUBLANES)
                shared = tab_ref[0, 0, 3 * N_EXPERTS + e]
                head = stages[slot][pl.ds(first, SUBLANES), :]
                stages[slot][pl.ds(first, SUBLANES), :] = jnp.where(sub < shared, carry_ref[e], head)
                final = pl.multiple_of(first + (units - 1) * SUBLANES, SUBLANES)
                carry_ref[e] = stages[slot][pl.ds(final, SUBLANES), :]

    def for_each_piece(table_ref, slot, wait):
        for e in range(N_EXPERTS):
            dst = table_ref[0, 0, e]
            units = table_ref[0, 0, N_EXPERTS + e]
            src = table_ref[0, 0, 2 * N_EXPERTS + e]
            for p in pieces:
                @pl.when((units & p) != 0)
                def _(p=p):
                    done = (units & ~(2 * p - 1)) * SUBLANES
                    cp = pltpu.make_async_copy(
                        stages[slot].at[pl.ds(pl.multiple_of(src + done, SUBLANES), p * SUBLANES)],
                        xs_hbm.at[pl.ds(pl.multiple_of(dst + done, SUBLANES), p * SUBLANES)],
                        sems.at[slot])
                    cp.wait() if wait else cp.start()

    for slot in (0, 1):
        @pl.when(step % 2 == slot)
        def _(slot=slot):
            permute(slot)

            @pl.when(step > 0)
            def _():
                for_each_piece(tprev_ref, 1 - slot, wait=True)

            for_each_piece(tab_ref, slot, wait=False)

            @pl.when(step == last)
            def _():
                for_each_piece(tab_ref, slot, wait=True)


def _dispatch(pend_rows, blocks_e, n_valid, run_table, pos_kt, hn, rows):
    t = hn.shape[0]
    tm = DISPATCH_TM
    steps = t // tm
    n_stage = -(-(TOP_K * tm + 2 * (SUBLANES - 1) * N_EXPERTS) // DISPATCH_KC) * DISPATCH_KC
    return pl.pallas_call(
        _dispatch_kernel,
        grid_spec=pltpu.PrefetchScalarGridSpec(
            num_scalar_prefetch=3,
            grid=(steps,),
            in_specs=[pl.BlockSpec((1, 1, LANES), lambda i, *_: (i, 0, 0), memory_space=pltpu.SMEM),
                      pl.BlockSpec((1, 1, LANES), lambda i, *_: (jnp.maximum(i - 1, 0), 0, 0),
                                   memory_space=pltpu.SMEM),
                      pl.BlockSpec((tm, _D), lambda i, *_: (i, 0)),
                      pl.BlockSpec((TOP_K, tm), lambda i, *_: (0, i))],
            out_specs=pl.BlockSpec(memory_space=pl.ANY),
            scratch_shapes=[pltpu.VMEM((n_stage, _D), F32), pltpu.VMEM((n_stage, _D), F32),
                            pltpu.VMEM((N_EXPERTS, SUBLANES, _D), F32), pltpu.VMEM((MOE_BLOCK, _D), F32),
                            pltpu.SemaphoreType.DMA((2,)), pltpu.SemaphoreType.DMA],
        ),
        out_shape=jax.ShapeDtypeStruct((rows, _D), F32),
        compiler_params=_cparams(("arbitrary",)),
        name="dispatch",
    )(pend_rows, blocks_e, n_valid, run_table, run_table, hn, pos_kt)


def _experts_kernel(be_ref, nv_ref, first_ref, slot_ref, next_ref, xs_ref, b1_ref, b2_ref, perm_ref,
                    w1_hbm, w2_hbm, ys_ref, wf1_ref, wf2_ref, w1b_ref, w2b_ref, sems):
    i = pl.program_id(0)
    gw = perm_ref.shape[0]

    def weight_copies(e, slot):
        return (pltpu.make_async_copy(w1_hbm.at[e], wf1_ref.at[slot], sems.at[0, slot]),
                pltpu.make_async_copy(w2_hbm.at[e], wf2_ref.at[slot], sems.at[1, slot]))

    @pl.when(i == 0)
    def _():
        for cp in weight_copies(be_ref[0], 0):
            cp.start()

    @pl.when((i < nv_ref[0]) & (first_ref[i] != 0))
    def _():
        slot = slot_ref[i]
        for cp in weight_copies(be_ref[i], slot):
            cp.wait()

        @pl.when(next_ref[i] >= 0)
        def _():
            for cp in weight_copies(next_ref[i], 1 - slot):
                cp.start()

        for gi in range(wf1_ref.shape[2] // gw):
            wb = wf1_ref[slot, :, gw * gi:gw * (gi + 1)].astype(BF16)
            w1b_ref[:, gw * gi:gw * (gi + 1)] = jnp.dot(wb, perm_ref[...], preferred_element_type=F32).astype(BF16)
        w2b_ref[...] = wf2_ref[slot].astype(BF16)

    @pl.when(i < nv_ref[0])
    def _():
        x = xs_ref[...].astype(BF16)
        u = jnp.dot(x, w1b_ref[...], preferred_element_type=F32) + b1_ref[0]
        acts = []
        for gi in range(u.shape[1] // (2 * LANES)):
            g = jnp.minimum(u[:, 2 * LANES * gi:2 * LANES * gi + LANES], SWIGLU_LIMIT)
            lin = jnp.clip(u[:, 2 * LANES * gi + LANES:2 * LANES * (gi + 1)], -SWIGLU_LIMIT, SWIGLU_LIMIT)
            acts.append((g * _sigmoid(SWIGLU_ALPHA * g) * (lin + 1.0)).astype(BF16))
        act = jnp.concatenate(acts, axis=1)
        ys_ref[...] = jnp.dot(act, w2b_ref[...], preferred_element_type=F32) + b2_ref[0]

    @pl.when(i >= nv_ref[0])
    def _():
        ys_ref[...] = jnp.zeros_like(ys_ref)


def _experts(block_e, n_valid, first, slot, next_e, xs, w1, b1, w2, b2, perm):
    rows = xs.shape[0]
    n_blocks = rows // MOE_BLOCK
    n_e, d, n1 = w1.shape
    dff = w2.shape[1]
    wmap = lambda i, be, *_: (be[i], 0, 0)
    rmap = lambda i, *_: (i, 0)
    return pl.pallas_call(
        _experts_kernel,
        grid_spec=pltpu.PrefetchScalarGridSpec(
            num_scalar_prefetch=5,
            grid=(n_blocks,),
            in_specs=[
                pl.BlockSpec((MOE_BLOCK, _D), rmap),
                pl.BlockSpec((1, 1, n1), wmap),
                pl.BlockSpec((1, 1, _D), wmap),
                pl.BlockSpec(perm.shape, lambda i, *_: (0, 0)),
                pl.BlockSpec(memory_space=pl.ANY),
                pl.BlockSpec(memory_space=pl.ANY),
            ],
            out_specs=pl.BlockSpec((MOE_BLOCK, _D), rmap),
            scratch_shapes=[pltpu.VMEM((2, d, n1), F32), pltpu.VMEM((2, dff, _D), F32),
                            pltpu.VMEM((d, n1), BF16), pltpu.VMEM((dff, _D), BF16),
                            pltpu.SemaphoreType.DMA((2, 2))],
        ),
        out_shape=jax.ShapeDtypeStruct((rows, _D), F32),
        compiler_params=_cparams(("arbitrary",)),
        name="experts",
    )(block_e, n_valid, first, slot, next_e, xs, b1, b2, perm, w1, w2)


def _final_kernel(tcur_ref, tnext_ref, x1_ref, gt_ref, pos_ref, gf_ref, ys_hbm, o_ref, st0_ref, st1_ref, sems):
    n_tok = x1_ref.shape[0]
    n_stage = st0_ref.shape[0]
    step = pl.program_id(0)
    last = pl.num_programs(0) - 1
    stages = (st0_ref, st1_ref)
    max_units = n_tok // SUBLANES + 1
    pieces = [1 << b for b in reversed(range(max_units.bit_length()))]

    def for_each_piece(tab_ref, slot, wait):
        for e in range(N_EXPERTS):
            src = tab_ref[0, 0, e]
            units = tab_ref[0, 0, N_EXPERTS + e]
            dst = tab_ref[0, 0, 2 * N_EXPERTS + e]
            for p in pieces:
                @pl.when((units & p) != 0)
                def _(p=p):
                    done = (units & ~(2 * p - 1)) * SUBLANES
                    cp = pltpu.make_async_copy(
                        ys_hbm.at[pl.ds(pl.multiple_of(src + done, SUBLANES), p * SUBLANES)],
                        stages[slot].at[pl.ds(pl.multiple_of(dst + done, SUBLANES), p * SUBLANES)],
                        sems.at[slot])
                    cp.wait() if wait else cp.start()

    def combine(slot):
        gates = gt_ref[...]
        pos = pos_ref[...]
        y = x1_ref[...]
        for c in range(n_stage // FINAL_KC):
            col = lax.broadcasted_iota(I32, (n_tok, FINAL_KC), 1) + c * FINAL_KC
            sel = jnp.zeros((n_tok, FINAL_KC), F32)
            for k in range(TOP_K):
                sel = jnp.where(col == pos[:, k:k + 1], gates[:, k:k + 1], sel)
            rows = stages[slot][FINAL_KC * c:FINAL_KC * (c + 1), :].astype(BF16)
            y = y + jnp.dot(sel.astype(BF16), rows, preferred_element_type=F32)
        o_ref[...] = _rms(y, gf_ref[...])

    @pl.when(step == 0)
    def _():
        st0_ref[...] = jnp.zeros_like(st0_ref)
        st1_ref[...] = jnp.zeros_like(st1_ref)
        for_each_piece(tcur_ref, 0, wait=False)

    for slot in (0, 1):
        @pl.when(step % 2 == slot)
        def _(slot=slot):
            for_each_piece(tcur_ref, slot, wait=True)

            @pl.when(step < last)
            def _():
                for_each_piece(tnext_ref, 1 - slot, wait=False)

            combine(slot)


def _final(run_table, x1, gates_tk, pos_tk, g_final, ys):
    t = x1.shape[0]
    tm = FINAL_TM
    steps = t // tm
    row = lambda i: (i, 0)
    n_stage = -(-(TOP_K * tm + 2 * (SUBLANES - 1) * N_EXPERTS) // FINAL_KC) * FINAL_KC
    return pl.pallas_call(
        _final_kernel,
        grid=(steps,),
        in_specs=[pl.BlockSpec((1, 1, LANES), lambda i: (i, 0, 0), memory_space=pltpu.SMEM),
                  pl.BlockSpec((1, 1, LANES), lambda i: (jnp.minimum(i + 1, steps - 1), 0, 0),
                               memory_space=pltpu.SMEM),
                  pl.BlockSpec((tm, _D), row), pl.BlockSpec((tm, TOP_K), row), pl.BlockSpec((tm, TOP_K), row),
                  pl.BlockSpec((1, _D), lambda i: (0, 0)), pl.BlockSpec(memory_space=pl.ANY)],
        out_specs=pl.BlockSpec((tm, _D), row),
        out_shape=jax.ShapeDtypeStruct((t, _D), F32),
        scratch_shapes=[pltpu.VMEM((n_stage, _D), F32), pltpu.VMEM((n_stage, _D), F32),
                        pltpu.SemaphoreType.DMA((2,))],
        compiler_params=_cparams(("arbitrary",)),
        name="final",
    )(run_table, run_table, x1, gates_tk, pos_tk, g_final, ys)


def _prep_in_proj(w_in, b_in, w_gate, b_gate):
    sizes = (GLA_HEADS * GLA_DK, GLA_HEADS * GLA_DK, GLA_HEADS * GLA_DV, GLA_HEADS * GLA_DV, GLA_GATE_RANK,
             SWA_Q_HEADS * SWA_HEAD_DIM, SWA_KV_HEADS * SWA_HEAD_DIM, SWA_KV_HEADS * SWA_HEAD_DIM, _D, _D)
    offs = [0]
    for s in sizes:
        offs.append(offs[-1] + s)

    def rearrange(m, dtype):
        p = [m[..., offs[i]:offs[i + 1]].astype(dtype) for i in range(len(sizes))]
        gq, gk, gv, gr, lr, sq, sk, sv, ga, gb = p

        def dup_heads(a):
            hs = [a[..., SWA_HEAD_DIM * h:SWA_HEAD_DIM * (h + 1)] for h in range(SWA_KV_HEADS)]
            return jnp.concatenate([hh for h in hs for hh in (h, h)], axis=-1)

        lr_pad = jnp.pad(lr, [(0, 0)] * (lr.ndim - 1) + [(0, LANES - GLA_GATE_RANK)])
        return jnp.concatenate([gq, gk, gv, gr, sq, ga, gb, dup_heads(sk), dup_heads(sv), lr_pad], axis=-1)

    w_all = rearrange(w_in, BF16)
    b_all = rearrange(b_in[None, :], F32)
    wg = jnp.pad(w_gate, ((0, LANES - GLA_GATE_RANK), (0, 0)))
    return w_all, b_all, wg, b_gate[None, :]


def _rope_tables(seq):
    half = SWA_HEAD_DIM // 2
    inv_freq = ROPE_THETA ** (-jnp.arange(half, dtype=F32) / half)
    ang = jnp.arange(seq, dtype=F32)[:, None] * inv_freq[None, :]
    cos, sin = jnp.cos(ang), jnp.sin(ang)
    cos_t = jnp.concatenate([cos, cos] * (LANES // SWA_HEAD_DIM), axis=1)
    sin_t = jnp.concatenate([-sin, sin] * (LANES // SWA_HEAD_DIM), axis=1)
    return cos_t, sin_t


def _pair_split_perm():
    src = jnp.arange(2 * LANES, dtype=I32)
    dst = jnp.where(src % 2 == 0, src // 2, LANES + src // 2)
    return (dst[:, None] == jnp.arange(2 * LANES, dtype=I32)[None, :]).astype(BF16)


def kernel(x, g_mix, w_in, b_in, w_gla_gate, b_gla_gate, g_gla_head, w_gla_out, sinks, w_swa_out, w_out,
           g_ffn, w_router, b_router, w_e1, b_e1, w_e2, b_e2, g_final):
    bsz, seq, d = x.shape
    assert d == _D and w_in.shape[0] == 1, "single-layer, d_model=1024 only"
    assert seq % max(PROJ_TM, SWA_WINDOW, GLA_CHUNK) == 0
    t = bsz * seq
    assert t % max(MERGE_TM, ROUTE_TT, FINAL_TM, DISPATCH_TM) == 0
    assert FINAL_TM == ROUTE_TT == DISPATCH_TM and t // ROUTE_TT <= LANES and 4 * N_EXPERTS == LANES
    x2 = x.reshape(t, d)

    w_all, b_all, wg, bg = _prep_in_proj(w_in[0], b_in[0], w_gla_gate[0], b_gla_gate[0])
    cos_t, sin_t = _rope_tables(seq)
    gq, gk, gv, gr, sq, ga, gb, sk, sv, lg = _in_proj(x2, g_mix, w_all, b_all, cos_t, sin_t, wg, bg, seq)
    oa = _gla(gq, gk, gv, lg, gr, g_gla_head, bsz, seq)
    ob = _swa(sinks[0], sq, sk, sv, bsz, seq)
    x1, hn, logits_t = _merge(x2, oa, ob, ga, gb, w_gla_out[0].astype(BF16), w_swa_out[0].astype(BF16),
                              w_out[0].astype(BF16), g_ffn, w_router[0].T, b_router[0][:, None])

    e_kt, g_kt, r_kt, cum = _route(logits_t)
    n_tiles = t // ROUTE_TT
    cum = cum[:, :n_tiles].astype(I32)
    counts = cum[:, -1]
    blocks_e = (counts + MOE_BLOCK - 1) // MOE_BLOCK
    bend = jnp.cumsum(blocks_e)
    pstart = (bend - blocks_e) * MOE_BLOCK
    n_blocks = (t * TOP_K) // MOE_BLOCK + N_EXPERTS
    block_e = jnp.minimum(jnp.sum(bend[None, :] <= jnp.arange(n_blocks, dtype=I32)[:, None], axis=1),
                          N_EXPERTS - 1).astype(I32)
    n_valid = bend[-1:].astype(I32)

    before = jnp.concatenate([jnp.zeros((N_EXPERTS, 1), I32), cum[:, :-1]], axis=1)
    run_len = cum - before
    run_src = pstart[:, None] + before
    cov_src = run_src // SUBLANES * SUBLANES
    cov_len = jnp.where(run_len > 0, (run_src + run_len + SUBLANES - 1) // SUBLANES * SUBLANES - cov_src, 0)
    cov_dst = jnp.cumsum(cov_len, axis=0) - cov_len
    run_table = jnp.concatenate([cov_src.T, cov_len.T // SUBLANES, cov_dst.T, (run_src - cov_src).T], axis=1)
    run_table = run_table.reshape(n_tiles, 1, LANES)
    shift = jnp.repeat(cov_dst + run_src - cov_src - before, ROUTE_TT, axis=1)
    pos_kt = r_kt
    for e in range(N_EXPERTS):
        pos_kt = pos_kt + jnp.where(e_kt == e, shift[e][None, :], 0)

    b1 = b_e1[0].reshape(N_EXPERTS, -1, LANES, 2).transpose(0, 1, 3, 2).reshape(N_EXPERTS, 1, -1)
    xs = _dispatch(bend * MOE_BLOCK, blocks_e, n_valid, run_table, pos_kt, hn, n_blocks * MOE_BLOCK)
    first = jnp.concatenate([jnp.ones((1,), I32), (block_e[1:] != block_e[:-1]).astype(I32)])
    slot = (jnp.cumsum(first) - 1) % 2
    ids = jnp.arange(N_EXPERTS, dtype=I32)
    later = (ids[None, :] > ids[:, None]) & (blocks_e[None, :] > 0)
    next_of = jnp.min(jnp.where(later, ids[None, :], N_EXPERTS), axis=1)
    next_of = jnp.where(next_of < N_EXPERTS, next_of, -1)
    next_e = jnp.sum(jnp.where(block_e[:, None] == ids[None, :], next_of[None, :], 0), axis=1).astype(I32)
    ys = _experts(block_e, n_valid, first, slot.astype(I32), next_e, xs, w_e1[0], b1, w_e2[0],
                  b_e2[0][:, None, :], _pair_split_perm())
    out = _final(run_table, x1, g_kt.T, pos_kt.T, g_final[None, :], ys)
    return out.reshape(bsz, seq, d)
```

```python
import jax
import jax.numpy as jnp
import numpy as np
from jax import lax
from jax.experimental import pallas as pl
from jax.experimental.pallas import tpu as pltpu

F32 = jnp.float32
BF16 = jnp.bfloat16
I32 = jnp.int32

NORM_EPS = 1e-5
GLA_HEADS = 4
GLA_DK = 128
GLA_DV = 256
GLA_GATE_RANK = 16
GLA_TAU = 16.0
SWA_Q_HEADS = 16
SWA_KV_HEADS = 2
SWA_GROUP = SWA_Q_HEADS // SWA_KV_HEADS
SWA_HEAD_DIM = 64
SWA_WINDOW = 128
ROPE_THETA = 10000.0
N_EXPERTS = 32
TOP_K = 4
SWIGLU_LIMIT = 7.0
SWIGLU_ALPHA = 1.702

LANES = 128
SUBLANES = 8
NEG = -1e30
VMEM_LIMIT = 56 * 1024 * 1024

PROJ_TM = 256
GLA_CHUNK = 128
GLA_HEADS_PER_STEP = 4
MERGE_TM = 512
MERGE_TN = 256
MERGE_ROW_GROUPS = 1
ROUTE_TT = 512
MOE_BLOCK = 512
DISPATCH_TM = 512
DISPATCH_KC = 512
FINAL_TM = 512
FINAL_KC = 512

_D = 1024
_C_GQ = 0
_C_GK = _C_GQ + GLA_HEADS * GLA_DK
_C_GV = _C_GK + GLA_HEADS * GLA_DK
_C_GR = _C_GV + GLA_HEADS * GLA_DV
_C_SQ = _C_GR + GLA_HEADS * GLA_DV
_C_GA = _C_SQ + SWA_Q_HEADS * SWA_HEAD_DIM
_C_GB = _C_GA + _D
_C_SK = _C_GB + _D
_C_SV = _C_SK + SWA_KV_HEADS * LANES
_C_LR = _C_SV + SWA_KV_HEADS * LANES
_C_END = _C_LR + LANES


def _cparams(sem):
    return pltpu.CompilerParams(dimension_semantics=sem, vmem_limit_bytes=VMEM_LIMIT)


def _rms(x, g):
    return x * lax.rsqrt(jnp.mean(x * x, axis=-1, keepdims=True) + NORM_EPS) * g


def _sigmoid(x):
    return 1.0 / (1.0 + jnp.exp(-x))


def _dot_nt(a, b):
    return lax.dot_general(a, b, (((1,), (1,)), ((), ())), preferred_element_type=F32)


def _rope_slabs(acc, cos, sin, first_half):
    outs = []
    for i in range(acc.shape[1] // LANES):
        xs = acc[:, LANES * i:LANES * (i + 1)]
        partner = jnp.where(first_half, pltpu.roll(xs, LANES - 32, 1), pltpu.roll(xs, 32, 1))
        outs.append(xs * cos + partner * sin)
    return jnp.concatenate(outs, axis=1)


def _in_proj_kernel(x_ref, g_ref, w_ref, b_ref, cos_ref, sin_ref, wg_ref, bg_ref,
                    gq_ref, gk_ref, gv_ref, gr_ref, sq_ref, ga_ref, gb_ref, sk_ref, sv_ref, lg_ref):
    h = _rms(x_ref[...], g_ref[...]).astype(BF16)

    def proj(lo, hi):
        return jnp.dot(h, w_ref[:, lo:hi], preferred_element_type=F32) + b_ref[:, lo:hi]

    cos = cos_ref[...]
    sin = sin_ref[...]
    lane = lax.broadcasted_iota(I32, cos.shape, 1)
    first_half = (lane % SWA_HEAD_DIM) < (SWA_HEAD_DIM // 2)

    gq_ref[...] = (proj(_C_GQ, _C_GK) * (GLA_DK ** -0.5)).astype(BF16)
    gk_ref[...] = proj(_C_GK, _C_GV).astype(BF16)
    gv_ref[...] = proj(_C_GV, _C_GR).astype(BF16)
    gr = proj(_C_GR, _C_SQ)
    gr_ref[...] = (gr * _sigmoid(gr)).astype(BF16)
    sq = proj(_C_SQ, _C_GA) * (SWA_HEAD_DIM ** -0.5)
    sq_ref[...] = _rope_slabs(sq, cos, sin, first_half).astype(BF16)
    ga_ref[...] = _sigmoid(proj(_C_GA, _C_GB)).astype(BF16)
    gb_ref[...] = _sigmoid(proj(_C_GB, _C_SK)).astype(BF16)
    sk_ref[...] = _rope_slabs(proj(_C_SK, _C_SV), cos, sin, first_half).astype(BF16)
    sv_ref[...] = proj(_C_SV, _C_LR).astype(BF16)
    z = jnp.dot(proj(_C_LR, _C_END), wg_ref[...], precision=lax.Precision.HIGHEST,
                preferred_element_type=F32) + bg_ref[...]
    log_sig = jnp.minimum(z, 0.0) - jnp.log(1.0 + jnp.exp(-jnp.abs(z)))
    lg_ref[...] = log_sig * (1.0 / GLA_TAU)


def _in_proj(x2, g_mix, w_all, b_all, cos_t, sin_t, wg, bg, seq):
    t = x2.shape[0]
    tm = PROJ_TM
    pos_blocks = seq // tm
    const = lambda i: (0, 0)
    row = lambda i: (i, 0)
    widths = [(_C_GK - _C_GQ, BF16), (_C_GV - _C_GK, BF16), (_C_GR - _C_GV, BF16), (_C_SQ - _C_GR, BF16),
              (_C_GA - _C_SQ, BF16), (_D, BF16), (_D, BF16), (_C_SV - _C_SK, BF16), (_C_LR - _C_SV, BF16),
              (GLA_HEADS * GLA_DK, F32)]
    return pl.pallas_call(
        _in_proj_kernel,
        grid=(t // tm,),
        in_specs=[
            pl.BlockSpec((tm, _D), row),
            pl.BlockSpec((1, _D), const),
            pl.BlockSpec((_D, _C_END), const, pipeline_mode=pl.Buffered(1)),
            pl.BlockSpec((1, _C_END), const),
            pl.BlockSpec((tm, LANES), lambda i: (i % pos_blocks, 0)),
            pl.BlockSpec((tm, LANES), lambda i: (i % pos_blocks, 0)),
            pl.BlockSpec((LANES, GLA_HEADS * GLA_DK), const),
            pl.BlockSpec((1, GLA_HEADS * GLA_DK), const),
        ],
        out_specs=[pl.BlockSpec((tm, w), row) for w, _ in widths],
        out_shape=[jax.ShapeDtypeStruct((t, w), dt) for w, dt in widths],
        compiler_params=_cparams(("parallel",)),
        name="in_proj",
    )(x2, g_mix, w_all, b_all, cos_t, sin_t, wg, bg)


def _gla_cumsum_operator(c_len):
    t = np.arange(c_len)[:, None]
    r = np.arange(c_len)[None, :]
    return np.tile((r <= t).astype(np.float32), (1, 3))


def _gla_kernel(q_ref, k_ref, v_ref, lg_ref, gr_ref, gh_ref, dm_ref, o_ref, st_ref):
    c_len = GLA_CHUNK
    n_lev = c_len.bit_length() - 1
    seq = q_ref.shape[0]
    st_ref[...] = jnp.zeros_like(st_ref)

    t_i = lax.broadcasted_iota(I32, (c_len, c_len), 0)
    j_i = lax.broadcasted_iota(I32, (c_len, c_len), 1)
    row = lax.broadcasted_iota(I32, (c_len, 1), 0)
    diag = t_i == j_i
    upper, pair = [], []
    for lev in range(n_lev):
        s = c_len >> (lev + 1)
        upper.append((row & s) != 0)
        pair.append(((t_i // (2 * s)) == (j_i // (2 * s))) & ((t_i & s) != 0) & ((j_i & s) == 0))

    sub8 = lax.broadcasted_iota(I32, (c_len // 8, 8, GLA_DK), 1)

    def boundary_rows(b, s):
        if s >= 4:
            b3 = b.reshape(c_len // (2 * s), 2 * s, GLA_DK)
            return jnp.broadcast_to(b3[:, s - 1:s, :], b3.shape).reshape(c_len, GLA_DK)
        b3 = b.reshape(c_len // 8, 8, GLA_DK)
        lo = jnp.broadcast_to(b3[:, 1:2, :], b3.shape)
        hi = jnp.broadcast_to(b3[:, 5:6, :], b3.shape)
        return jnp.where(sub8 < 4, lo, hi).reshape(c_len, GLA_DK)

    def head_chunk(r0, hh):
        kcols = slice(GLA_DK * hh, GLA_DK * (hh + 1))
        vcols = slice(GLA_DV * hh, GLA_DV * (hh + 1))
        q_bf = q_ref[pl.ds(r0, c_len), kcols]
        k_bf = k_ref[pl.ds(r0, c_len), kcols]
        q = q_bf.astype(F32)
        k = k_bf.astype(F32)
        v = v_ref[pl.ds(r0, c_len), vcols]

        lg = lg_ref[pl.ds(r0, c_len), kcols]
        lg_hi = lg.astype(BF16)
        rem = lg - lg_hi.astype(F32)
        lg_mid = rem.astype(BF16)
        lg_lo = (rem - lg_mid.astype(F32)).astype(BF16)
        b = jnp.dot(dm_ref[...], jnp.concatenate([lg_hi, lg_mid, lg_lo], axis=0),
                    preferred_element_type=F32)
        w_cum = jnp.exp(b)

        st = st_ref[hh]
        o = _dot_nt((q * w_cum).astype(BF16), st.astype(BF16))

        a = jnp.where(diag, _dot_nt(q_bf, k_bf), 0.0)
        for lev in range(n_lev):
            s = c_len >> (lev + 1)
            if s == 1:
                w = jnp.where(upper[lev], jnp.exp(lg), 1.0)
            else:
                w = jnp.exp(-jnp.abs(b - boundary_rows(b, s)))
            z = (jnp.where(upper[lev], q, k) * w).astype(BF16)
            a = jnp.where(pair[lev], _dot_nt(z, z), a)
        o = o + jnp.dot(a.astype(BF16), v, preferred_element_type=F32)

        b_last = b[c_len - 1:c_len, :]
        upd = lax.dot_general(v, (k * jnp.exp(b_last - b)).astype(BF16), (((0,), (0,)), ((), ())),
                              preferred_element_type=F32)
        st_ref[hh] = st * w_cum[c_len - 1:c_len, :] + upd

        on = _rms(o, gh_ref[...])
        o_ref[pl.ds(r0, c_len), vcols] = (on * gr_ref[pl.ds(r0, c_len), vcols].astype(F32)).astype(BF16)

    def chunk(c, carry):
        r0 = pl.multiple_of(c * c_len, c_len)
        for hh in range(GLA_HEADS_PER_STEP):
            head_chunk(r0, hh)
        return carry

    lax.fori_loop(0, seq // c_len, chunk, 0, unroll=2)


def _gla(gq, gk, gv, lg, gr, g_head, bsz, seq):
    t = gq.shape[0]
    hs = GLA_HEADS_PER_STEP
    dmat = jnp.asarray(_gla_cumsum_operator(GLA_CHUNK), dtype=BF16)
    return pl.pallas_call(
        _gla_kernel,
        grid=(bsz, GLA_HEADS // hs),
        in_specs=[
            pl.BlockSpec((seq, hs * GLA_DK), lambda b, h: (b, h)),
            pl.BlockSpec((seq, hs * GLA_DK), lambda b, h: (b, h)),
            pl.BlockSpec((seq, hs * GLA_DV), lambda b, h: (b, h)),
            pl.BlockSpec((seq, hs * GLA_DK), lambda b, h: (b, h)),
            pl.BlockSpec((seq, hs * GLA_DV), lambda b, h: (b, h)),
            pl.BlockSpec((1, GLA_DV), lambda b, h: (0, 0)),
            pl.BlockSpec(dmat.shape, lambda b, h: (0, 0)),
        ],
        out_specs=pl.BlockSpec((seq, hs * GLA_DV), lambda b, h: (b, h)),
        out_shape=jax.ShapeDtypeStruct((t, GLA_HEADS * GLA_DV), BF16),
        scratch_shapes=[pltpu.VMEM((hs, GLA_DV, GLA_DK), F32)],
        compiler_params=_cparams(("parallel", "parallel")),
        name="gla",
    )(gq, gk, gv, lg, gr, g_head, dmat)


def _swa_kernel(sink_ref, q_ref, k_ref, v_ref, o_ref):
    w = SWA_WINDOW
    seq = q_ref.shape[0]
    hk = pl.program_id(1)
    lane_q = lax.broadcasted_iota(I32, (w, LANES), 1)
    low_q = lane_q < SWA_HEAD_DIM
    lane_b = lax.broadcasted_iota(I32, (2 * w, LANES), 1)
    low_b = lane_b < SWA_HEAD_DIM
    qi = lax.broadcasted_iota(I32, (w, 2 * w), 0)
    kj = lax.broadcasted_iota(I32, (w, 2 * w), 1)
    in_window = (kj > qi) & (kj <= qi + w)
    zero_q = jnp.zeros((w, LANES), BF16)
    zero_b = jnp.zeros((2 * w, LANES), BF16)

    def block(n, carry):
        r0 = pl.multiple_of(n * w, w)
        p0 = pl.multiple_of(jnp.maximum(n - 1, 0) * w, w)
        kb = jnp.concatenate([k_ref[pl.ds(p0, w), :], k_ref[pl.ds(r0, w), :]], axis=0)
        vb = jnp.concatenate([v_ref[pl.ds(p0, w), :], v_ref[pl.ds(r0, w), :]], axis=0)
        valid = in_window & ((kj >= w) | (n > 0))
        v_lo = jnp.where(low_b, vb, zero_b)
        v_hi = jnp.where(low_b, zero_b, vb)
        for m in range(SWA_GROUP // 2):
            qp = q_ref[pl.ds(r0, w), LANES * m:LANES * (m + 1)]
            acc = jnp.zeros((w, LANES), F32)
            for par in range(2):
                qm = jnp.where(low_q, qp, zero_q) if par == 0 else jnp.where(low_q, zero_q, qp)
                s = jnp.where(valid, _dot_nt(qm, kb), NEG)
                sink = sink_ref[hk * SWA_GROUP + 2 * m + par]
                mx = jnp.maximum(jnp.max(s, axis=-1, keepdims=True), sink)
                p = jnp.exp(s - mx)
                den = jnp.sum(p, axis=-1, keepdims=True) + jnp.exp(sink - mx)
                pv = jnp.dot(p.astype(BF16), v_lo if par == 0 else v_hi, preferred_element_type=F32)
                acc = acc + pv / den
            o_ref[pl.ds(r0, w), LANES * m:LANES * (m + 1)] = acc.astype(BF16)
        return carry

    lax.fori_loop(0, seq // w, block, 0, unroll=2)


def _swa(sinks, sq, sk, sv, bsz, seq):
    t = sq.shape[0]
    gw = SWA_GROUP * SWA_HEAD_DIM
    return pl.pallas_call(
        _swa_kernel,
        grid_spec=pltpu.PrefetchScalarGridSpec(
            num_scalar_prefetch=1,
            grid=(bsz, SWA_KV_HEADS),
            in_specs=[
                pl.BlockSpec((seq, gw), lambda b, h, s: (b, h)),
                pl.BlockSpec((seq, LANES), lambda b, h, s: (b, h)),
                pl.BlockSpec((seq, LANES), lambda b, h, s: (b, h)),
            ],
            out_specs=pl.BlockSpec((seq, gw), lambda b, h, s: (b, h)),
        ),
        out_shape=jax.ShapeDtypeStruct((t, SWA_Q_HEADS * SWA_HEAD_DIM), BF16),
        compiler_params=_cparams(("parallel", "parallel")),
        name="swa",
    )(sinks, sq, sk, sv)


def _merge_kernel(x_ref, oa_ref, ob_ref, ga_ref, gb_ref, wa_ref, wb_ref, wo_ref, gf_ref, wr_ref, br_ref,
                  x1_ref, hn_ref, lt_ref, mixed_ref):
    rows_g = x_ref.shape[0] // MERGE_ROW_GROUPS
    for h in range(MERGE_ROW_GROUPS):
        rows = slice(rows_g * h, rows_g * (h + 1))
        oa = oa_ref[rows, :]
        ob = ob_ref[rows, :]
        for n in range(_D // MERGE_TN):
            cols = slice(MERGE_TN * n, MERGE_TN * (n + 1))
            ya = ga_ref[rows, cols].astype(F32) * jnp.dot(oa, wa_ref[:, cols], preferred_element_type=F32)
            yb = gb_ref[rows, cols].astype(F32) * jnp.dot(ob, wb_ref[:, cols], preferred_element_type=F32)
            mixed_ref[rows, cols] = (ya + yb).astype(BF16)
        mixed = mixed_ref[rows, :]
        for n in range(_D // MERGE_TN):
            cols = slice(MERGE_TN * n, MERGE_TN * (n + 1))
            x1_ref[rows, cols] = x_ref[rows, cols] + jnp.dot(mixed, wo_ref[:, cols], preferred_element_type=F32)
        hn = _rms(x1_ref[rows, :], gf_ref[...])
        hn_ref[rows, :] = hn
        lt_ref[:, rows] = lax.dot_general(wr_ref[...], hn, (((1,), (1,)), ((), ())),
                                          precision=lax.Precision.HIGHEST,
                                          preferred_element_type=F32) + br_ref[...]


def _merge(x2, oa, ob, ga, gb, wa, wb, wo, g_ffn, wr_t, br_col):
    t = x2.shape[0]
    tm = MERGE_TM
    row = lambda i: (i, 0)
    const = lambda i: (0, 0)
    return pl.pallas_call(
        _merge_kernel,
        grid=(t // tm,),
        in_specs=[pl.BlockSpec((tm, _D), row)] * 5 + [pl.BlockSpec((_D, _D), const)] * 3 + [
            pl.BlockSpec((1, _D), const),
            pl.BlockSpec((N_EXPERTS, _D), const),
            pl.BlockSpec((N_EXPERTS, 1), const),
        ],
        out_specs=[pl.BlockSpec((tm, _D), row), pl.BlockSpec((tm, _D), row),
                   pl.BlockSpec((N_EXPERTS, tm), lambda i: (0, i))],
        out_shape=[jax.ShapeDtypeStruct((t, _D), F32), jax.ShapeDtypeStruct((t, _D), F32),
                   jax.ShapeDtypeStruct((N_EXPERTS, t), F32)],
        scratch_shapes=[pltpu.VMEM((tm, _D), BF16)],
        compiler_params=_cparams(("parallel",)),
        name="merge",
    )(x2, oa, ob, ga, gb, wa, wb, wo, g_ffn, wr_t, br_col)


def _route_kernel(lt_ref, e_ref, g_ref, r_ref, cum_ref, carry_ref):
    tt = lt_ref.shape[1]

    @pl.when(pl.program_id(0) == 0)
    def _():
        carry_ref[...] = jnp.zeros_like(carry_ref)
        cum_ref[...] = jnp.zeros_like(cum_ref)

    eid = lax.broadcasted_iota(I32, (N_EXPERTS, tt), 0)
    work = lt_ref[...]
    vals, idxs = [], []
    chosen = jnp.zeros((N_EXPERTS, tt), F32)
    for _ in range(TOP_K):
        m = jnp.max(work, axis=0, keepdims=True)
        idx = jnp.min(jnp.where(work == m, eid, N_EXPERTS), axis=0, keepdims=True)
        hit = eid == idx
        work = jnp.where(hit, -jnp.inf, work)
        chosen = jnp.where(hit, 1.0, chosen)
        vals.append(m)
        idxs.append(idx)
    ex = [jnp.exp(v - vals[0]) for v in vals]
    den = ex[0] + ex[1] + ex[2] + ex[3]

    t_r = lax.broadcasted_iota(I32, (tt, tt), 0)
    t_c = lax.broadcasted_iota(I32, (tt, tt), 1)
    before = (t_r < t_c).astype(BF16)
    pref = jnp.dot(chosen.astype(BF16), before, preferred_element_type=F32) + carry_ref[:, 0:1]
    for k in range(TOP_K):
        e_ref[k:k + 1, :] = idxs[k]
        g_ref[k:k + 1, :] = ex[k] / den
        r_ref[k:k + 1, :] = jnp.sum(jnp.where(eid == idxs[k], pref, 0.0), axis=0, keepdims=True).astype(I32)
    total = pref[:, tt - 1:tt] + chosen[:, tt - 1:tt]
    carry_ref[...] = jnp.broadcast_to(total, carry_ref.shape)
    tile_lane = lax.broadcasted_iota(I32, cum_ref.shape, 1) == pl.program_id(0)
    cum_ref[...] = jnp.where(tile_lane, total, cum_ref[...])


def _route(logits_t):
    t = logits_t.shape[1]
    tt = ROUTE_TT
    blk = lambda i: (0, i)
    return pl.pallas_call(
        _route_kernel,
        grid=(t // tt,),
        in_specs=[pl.BlockSpec((N_EXPERTS, tt), blk)],
        out_specs=[pl.BlockSpec((TOP_K, tt), blk), pl.BlockSpec((TOP_K, tt), blk), pl.BlockSpec((TOP_K, tt), blk),
                   pl.BlockSpec((N_EXPERTS, LANES), lambda i: (0, 0))],
        out_shape=[jax.ShapeDtypeStruct((TOP_K, t), I32), jax.ShapeDtypeStruct((TOP_K, t), F32),
                   jax.ShapeDtypeStruct((TOP_K, t), I32), jax.ShapeDtypeStruct((N_EXPERTS, LANES), F32)],
        scratch_shapes=[pltpu.VMEM((N_EXPERTS, LANES), F32)],
        compiler_params=_cparams(("arbitrary",)),
        name="route",
    )(logits_t)


def _dispatch_kernel(pend_ref, nblk_ref, nv_ref, tab_ref, hn_ref, pos_ref, xs_hbm, spill_hbm,
                     st0_ref, st1_ref, carry_ref, zero_ref, sems, zsem):
    n_tok = hn_ref.shape[0]
    n_stage = st0_ref.shape[0]
    n_blocks = xs_hbm.shape[0] // MOE_BLOCK
    step = pl.program_id(0)
    last = pl.num_programs(0) - 1
    stages = (st0_ref, st1_ref)
    max_units = n_tok // SUBLANES + 1
    pieces = [1 << b for b in reversed(range(max_units.bit_length()))]
    fill_pieces = [1 << b for b in reversed(range((n_stage // SUBLANES).bit_length()))]

    def zero_block(row0):
        return pltpu.make_async_copy(zero_ref, xs_hbm.at[pl.ds(pl.multiple_of(row0, MOE_BLOCK), MOE_BLOCK)], zsem)

    @pl.when(step == 0)
    def _():
        zero_ref[...] = jnp.zeros_like(zero_ref)
        carry_ref[...] = jnp.zeros_like(carry_ref)

        def expert_tail(e, carry, start):
            @pl.when(nblk_ref[e] > 0)
            def _():
                cp = zero_block(pend_ref[e] - MOE_BLOCK)
                cp.start() if start else cp.wait()
            return carry

        def unused_block(b, carry, start):
            cp = zero_block(b * MOE_BLOCK)
            cp.start() if start else cp.wait()
            return carry

        for start in (True, False):
            lax.fori_loop(0, N_EXPERTS, lambda e, c: expert_tail(e, c, start), 0)
            lax.fori_loop(nv_ref[0], n_blocks, lambda b, c: unused_block(b, c, start), 0)

    def permute(slot):
        hb = hn_ref[...].astype(BF16)
        pos = pos_ref[...]
        for c in range(n_stage // DISPATCH_KC):
            row = lax.broadcasted_iota(I32, (DISPATCH_KC, n_tok), 0) + c * DISPATCH_KC
            sel = jnp.zeros((DISPATCH_KC, n_tok), F32)
            for k in range(TOP_K):
                sel = jnp.where(row == pos[k:k + 1, :], 1.0, sel)
            stages[slot][DISPATCH_KC * c:DISPATCH_KC * (c + 1), :] = jnp.dot(
                sel.astype(BF16), hb, preferred_element_type=F32)
        sub = lax.broadcasted_iota(I32, (SUBLANES, hn_ref.shape[1]), 0)
        for e in range(N_EXPERTS):
            units = tab_ref[0, 0, N_EXPERTS + e]

            @pl.when(units > 0)
            def _(e=e, units=units):
                first = pl.multiple_of(tab_ref[0, 0, 2 * N_EXPERTS + e], SUBLANES)
                shared = tab_ref[0, 0, 3 * N_EXPERTS + e]
                head = stages[slot][pl.ds(first, SUBLANES), :]
                stages[slot][pl.ds(first, SUBLANES), :] = jnp.where(sub < shared, carry_ref[e], head)
                final = pl.multiple_of(first + (units - 1) * SUBLANES, SUBLANES)
                carry_ref[e] = stages[slot][pl.ds(final, SUBLANES), :]

    def write_covers(slot):
        total = 0
        for e in range(N_EXPERTS):
            dst = tab_ref[0, 0, e]
            units = tab_ref[0, 0, N_EXPERTS + e]
            src = tab_ref[0, 0, 2 * N_EXPERTS + e]
            total = total + units
            for p in pieces:
                @pl.when((units & p) != 0)
                def _(p=p):
                    done = (units & ~(2 * p - 1)) * SUBLANES
                    pltpu.make_async_copy(
                        stages[slot].at[pl.ds(pl.multiple_of(src + done, SUBLANES), p * SUBLANES)],
                        xs_hbm.at[pl.ds(pl.multiple_of(dst + done, SUBLANES), p * SUBLANES)],
                        sems.at[slot]).start()
        spare = n_stage // SUBLANES - total
        for p in fill_pieces:
            @pl.when((spare & p) != 0)
            def _(p=p):
                done = (spare & ~(2 * p - 1)) * SUBLANES
                pltpu.make_async_copy(
                    stages[slot].at[pl.ds(pl.multiple_of(total * SUBLANES + done, SUBLANES), p * SUBLANES)],
                    spill_hbm.at[pl.ds(pl.multiple_of(done, SUBLANES), p * SUBLANES)],
                    sems.at[slot]).start()

    def drain(slot):
        for c in range(n_stage // DISPATCH_KC):
            pltpu.make_async_copy(stages[slot].at[pl.ds(DISPATCH_KC * c, DISPATCH_KC)],
                                  spill_hbm.at[pl.ds(0, DISPATCH_KC)], sems.at[slot]).wait()

    for slot in (0, 1):
        @pl.when(step % 2 == slot)
        def _(slot=slot):
            permute(slot)

            @pl.when(step > 0)
            def _():
                drain(1 - slot)

            write_covers(slot)

            @pl.when(step == last)
            def _():
                drain(slot)


def _dispatch(pend_rows, blocks_e, n_valid, run_table, pos_kt, hn, rows):
    t = hn.shape[0]
    tm = DISPATCH_TM
    steps = t // tm
    n_stage = -(-(TOP_K * tm + 2 * (SUBLANES - 1) * N_EXPERTS) // DISPATCH_KC) * DISPATCH_KC
    return pl.pallas_call(
        _dispatch_kernel,
        grid_spec=pltpu.PrefetchScalarGridSpec(
            num_scalar_prefetch=3,
            grid=(steps,),
            in_specs=[pl.BlockSpec((1, 1, LANES), lambda i, *_: (i, 0, 0), memory_space=pltpu.SMEM),
                      pl.BlockSpec((tm, _D), lambda i, *_: (i, 0)),
                      pl.BlockSpec((TOP_K, tm), lambda i, *_: (0, i))],
            out_specs=[pl.BlockSpec(memory_space=pl.ANY), pl.BlockSpec(memory_space=pl.ANY)],
            scratch_shapes=[pltpu.VMEM((n_stage, _D), F32), pltpu.VMEM((n_stage, _D), F32),
                            pltpu.VMEM((N_EXPERTS, SUBLANES, _D), F32), pltpu.VMEM((MOE_BLOCK, _D), F32),
                            pltpu.SemaphoreType.DMA((2,)), pltpu.SemaphoreType.DMA],
        ),
        out_shape=[jax.ShapeDtypeStruct((rows, _D), F32), jax.ShapeDtypeStruct((n_stage, _D), F32)],
        compiler_params=_cparams(("arbitrary",)),
        name="dispatch",
    )(pend_rows, blocks_e, n_valid, run_table, hn, pos_kt)[0]


def _experts_kernel(be_ref, nv_ref, first_ref, slot_ref, next_ref, xs_ref, b1_ref, b2_ref, perm_ref,
                    w1_hbm, w2_hbm, ys_ref, wf1_ref, wf2_ref, w1b_ref, w2b_ref, sems):
    i = pl.program_id(0)
    gw = perm_ref.shape[0]

    def weight_copies(e, slot):
        return (pltpu.make_async_copy(w1_hbm.at[e], wf1_ref.at[slot], sems.at[0, slot]),
                pltpu.make_async_copy(w2_hbm.at[e], wf2_ref.at[slot], sems.at[1, slot]))

    @pl.when(i == 0)
    def _():
        for cp in weight_copies(be_ref[0], 0):
            cp.start()

    @pl.when((i < nv_ref[0]) & (first_ref[i] != 0))
    def _():
        slot = slot_ref[i]
        for cp in weight_copies(be_ref[i], slot):
            cp.wait()

        @pl.when(next_ref[i] >= 0)
        def _():
            for cp in weight_copies(next_ref[i], 1 - slot):
                cp.start()

        for gi in range(wf1_ref.shape[2] // gw):
            wb = wf1_ref[slot, :, gw * gi:gw * (gi + 1)].astype(BF16)
            w1b_ref[:, gw * gi:gw * (gi + 1)] = jnp.dot(wb, perm_ref[...], preferred_element_type=F32).astype(BF16)
        w2b_ref[...] = wf2_ref[slot].astype(BF16)

    @pl.when(i < nv_ref[0])
    def _():
        x = xs_ref[...].astype(BF16)
        u = jnp.dot(x, w1b_ref[...], preferred_element_type=F32) + b1_ref[0]
        acts = []
        for gi in range(u.shape[1] // (2 * LANES)):
            g = jnp.minimum(u[:, 2 * LANES * gi:2 * LANES * gi + LANES], SWIGLU_LIMIT)
            lin = jnp.clip(u[:, 2 * LANES * gi + LANES:2 * LANES * (gi + 1)], -SWIGLU_LIMIT, SWIGLU_LIMIT)
            acts.append((g * _sigmoid(SWIGLU_ALPHA * g) * (lin + 1.0)).astype(BF16))
        act = jnp.concatenate(acts, axis=1)
        ys_ref[...] = jnp.dot(act, w2b_ref[...], preferred_element_type=F32) + b2_ref[0]

    @pl.when(i >= nv_ref[0])
    def _():
        ys_ref[...] = jnp.zeros_like(ys_ref)


def _experts(block_e, n_valid, first, slot, next_e, xs, w1, b1, w2, b2, perm):
    rows = xs.shape[0]
    n_blocks = rows // MOE_BLOCK
    n_e, d, n1 = w1.shape
    dff = w2.shape[1]
    wmap = lambda i, be, *_: (be[i], 0, 0)
    rmap = lambda i, *_: (i, 0)
    return pl.pallas_call(
        _experts_kernel,
        grid_spec=pltpu.PrefetchScalarGridSpec(
            num_scalar_prefetch=5,
            grid=(n_blocks,),
            in_specs=[
                pl.BlockSpec((MOE_BLOCK, _D), rmap),
                pl.BlockSpec((1, 1, n1), wmap),
                pl.BlockSpec((1, 1, _D), wmap),
                pl.BlockSpec(perm.shape, lambda i, *_: (0, 0)),
                pl.BlockSpec(memory_space=pl.ANY),
                pl.BlockSpec(memory_space=pl.ANY),
            ],
            out_specs=pl.BlockSpec((MOE_BLOCK, _D), rmap),
            scratch_shapes=[pltpu.VMEM((2, d, n1), F32), pltpu.VMEM((2, dff, _D), F32),
                            pltpu.VMEM((d, n1), BF16), pltpu.VMEM((dff, _D), BF16),
                            pltpu.SemaphoreType.DMA((2, 2))],
        ),
        out_shape=jax.ShapeDtypeStruct((rows, _D), F32),
        compiler_params=_cparams(("arbitrary",)),
        name="experts",
    )(block_e, n_valid, first, slot, next_e, xs, b1, b2, perm, w1, w2)


def _final_kernel(tcur_ref, tnext_ref, x1_ref, gt_ref, pos_ref, gf_ref, ys_hbm, o_ref, st0_ref, st1_ref, sems):
    n_tok = x1_ref.shape[0]
    n_stage = st0_ref.shape[0]
    step = pl.program_id(0)
    last = pl.num_programs(0) - 1
    stages = (st0_ref, st1_ref)
    max_units = n_tok // SUBLANES + 1
    pieces = [1 << b for b in reversed(range(max_units.bit_length()))]
    fill_pieces = [1 << b for b in reversed(range((n_stage // SUBLANES).bit_length()))]

    def fetch_covers(tab_ref, slot):
        total = 0
        for e in range(N_EXPERTS):
            src = tab_ref[0, 0, e]
            units = tab_ref[0, 0, N_EXPERTS + e]
            dst = tab_ref[0, 0, 2 * N_EXPERTS + e]
            total = total + units
            for p in pieces:
                @pl.when((units & p) != 0)
                def _(p=p):
                    done = (units & ~(2 * p - 1)) * SUBLANES
                    pltpu.make_async_copy(
                        ys_hbm.at[pl.ds(pl.multiple_of(src + done, SUBLANES), p * SUBLANES)],
                        stages[slot].at[pl.ds(pl.multiple_of(dst + done, SUBLANES), p * SUBLANES)],
                        sems.at[slot]).start()
        spare = n_stage // SUBLANES - total
        for p in fill_pieces:
            @pl.when((spare & p) != 0)
            def _(p=p):
                done = (spare & ~(2 * p - 1)) * SUBLANES
                pltpu.make_async_copy(
                    ys_hbm.at[pl.ds(pl.multiple_of(done, SUBLANES), p * SUBLANES)],
                    stages[slot].at[pl.ds(pl.multiple_of(total * SUBLANES + done, SUBLANES), p * SUBLANES)],
                    sems.at[slot]).start()

    def drain(slot):
        for c in range(n_stage // FINAL_KC):
            pltpu.make_async_copy(ys_hbm.at[pl.ds(0, FINAL_KC)],
                                  stages[slot].at[pl.ds(FINAL_KC * c, FINAL_KC)], sems.at[slot]).wait()

    def combine(slot):
        gates = gt_ref[...]
        pos = pos_ref[...]
        y = x1_ref[...]
        for c in range(n_stage // FINAL_KC):
            col = lax.broadcasted_iota(I32, (n_tok, FINAL_KC), 1) + c * FINAL_KC
            sel = jnp.zeros((n_tok, FINAL_KC), F32)
            for k in range(TOP_K):
                sel = jnp.where(col == pos[:, k:k + 1], gates[:, k:k + 1], sel)
            rows = stages[slot][FINAL_KC * c:FINAL_KC * (c + 1), :].astype(BF16)
            y = y + jnp.dot(sel.astype(BF16), rows, preferred_element_type=F32)
        o_ref[...] = _rms(y, gf_ref[...])

    @pl.when(step == 0)
    def _():
        st0_ref[...] = jnp.zeros_like(st0_ref)
        st1_ref[...] = jnp.zeros_like(st1_ref)
        fetch_covers(tcur_ref, 0)

    for slot in (0, 1):
        @pl.when(step % 2 == slot)
        def _(slot=slot):
            drain(slot)

            @pl.when(step < last)
            def _():
                fetch_covers(tnext_ref, 1 - slot)

            combine(slot)


def _final(run_table, x1, gates_tk, pos_tk, g_final, ys):
    t = x1.shape[0]
    tm = FINAL_TM
    steps = t // tm
    row = lambda i: (i, 0)
    n_stage = -(-(TOP_K * tm + 2 * (SUBLANES - 1) * N_EXPERTS) // FINAL_KC) * FINAL_KC
    return pl.pallas_call(
        _final_kernel,
        grid=(steps,),
        in_specs=[pl.BlockSpec((1, 1, LANES), lambda i: (i, 0, 0), memory_space=pltpu.SMEM),
                  pl.BlockSpec((1, 1, LANES), lambda i: (jnp.minimum(i + 1, steps - 1), 0, 0),
                               memory_space=pltpu.SMEM),
                  pl.BlockSpec((tm, _D), row), pl.BlockSpec((tm, TOP_K), row), pl.BlockSpec((tm, TOP_K), row),
                  pl.BlockSpec((1, _D), lambda i: (0, 0)), pl.BlockSpec(memory_space=pl.ANY)],
        out_specs=pl.BlockSpec((tm, _D), row),
        out_shape=jax.ShapeDtypeStruct((t, _D), F32),
        scratch_shapes=[pltpu.VMEM((n_stage, _D), F32), pltpu.VMEM((n_stage, _D), F32),
                        pltpu.SemaphoreType.DMA((2,))],
        compiler_params=_cparams(("arbitrary",)),
        name="final",
    )(run_table, run_table, x1, gates_tk, pos_tk, g_final, ys)


def _prep_in_proj(w_in, b_in, w_gate, b_gate):
    sizes = (GLA_HEADS * GLA_DK, GLA_HEADS * GLA_DK, GLA_HEADS * GLA_DV, GLA_HEADS * GLA_DV, GLA_GATE_RANK,
             SWA_Q_HEADS * SWA_HEAD_DIM, SWA_KV_HEADS * SWA_HEAD_DIM, SWA_KV_HEADS * SWA_HEAD_DIM, _D, _D)
    offs = [0]
    for s in sizes:
        offs.append(offs[-1] + s)

    def rearrange(m, dtype):
        p = [m[..., offs[i]:offs[i + 1]].astype(dtype) for i in range(len(sizes))]
        gq, gk, gv, gr, lr, sq, sk, sv, ga, gb = p

        def dup_heads(a):
            hs = [a[..., SWA_HEAD_DIM * h:SWA_HEAD_DIM * (h + 1)] for h in range(SWA_KV_HEADS)]
            return jnp.concatenate([hh for h in hs for hh in (h, h)], axis=-1)

        lr_pad = jnp.pad(lr, [(0, 0)] * (lr.ndim - 1) + [(0, LANES - GLA_GATE_RANK)])
        return jnp.concatenate([gq, gk, gv, gr, sq, ga, gb, dup_heads(sk), dup_heads(sv), lr_pad], axis=-1)

    w_all = rearrange(w_in, BF16)
    b_all = rearrange(b_in[None, :], F32)
    wg = jnp.pad(w_gate, ((0, LANES - GLA_GATE_RANK), (0, 0)))
    return w_all, b_all, wg, b_gate[None, :]


def _rope_tables(seq):
    half = SWA_HEAD_DIM // 2
    inv_freq = ROPE_THETA ** (-jnp.arange(half, dtype=F32) / half)
    ang = jnp.arange(seq, dtype=F32)[:, None] * inv_freq[None, :]
    cos, sin = jnp.cos(ang), jnp.sin(ang)
    cos_t = jnp.concatenate([cos, cos] * (LANES // SWA_HEAD_DIM), axis=1)
    sin_t = jnp.concatenate([-sin, sin] * (LANES // SWA_HEAD_DIM), axis=1)
    return cos_t, sin_t


def _pair_split_perm():
    src = jnp.arange(2 * LANES, dtype=I32)
    dst = jnp.where(src % 2 == 0, src // 2, LANES + src // 2)
    return (dst[:, None] == jnp.arange(2 * LANES, dtype=I32)[None, :]).astype(BF16)


def kernel(x, g_mix, w_in, b_in, w_gla_gate, b_gla_gate, g_gla_head, w_gla_out, sinks, w_swa_out, w_out,
           g_ffn, w_router, b_router, w_e1, b_e1, w_e2, b_e2, g_final):
    bsz, seq, d = x.shape
    assert d == _D and w_in.shape[0] == 1, "single-layer, d_model=1024 only"
    assert seq % max(PROJ_TM, SWA_WINDOW, GLA_CHUNK) == 0
    t = bsz * seq
    assert t % max(MERGE_TM, ROUTE_TT, FINAL_TM, DISPATCH_TM) == 0
    assert FINAL_TM == ROUTE_TT == DISPATCH_TM and t // ROUTE_TT <= LANES and 4 * N_EXPERTS == LANES
    x2 = x.reshape(t, d)

    w_all, b_all, wg, bg = _prep_in_proj(w_in[0], b_in[0], w_gla_gate[0], b_gla_gate[0])
    cos_t, sin_t = _rope_tables(seq)
    gq, gk, gv, gr, sq, ga, gb, sk, sv, lg = _in_proj(x2, g_mix, w_all, b_all, cos_t, sin_t, wg, bg, seq)
    oa = _gla(gq, gk, gv, lg, gr, g_gla_head, bsz, seq)
    ob = _swa(sinks[0], sq, sk, sv, bsz, seq)
    x1, hn, logits_t = _merge(x2, oa, ob, ga, gb, w_gla_out[0].astype(BF16), w_swa_out[0].astype(BF16),
                              w_out[0].astype(BF16), g_ffn, w_router[0].T, b_router[0][:, None])

    e_kt, g_kt, r_kt, cum = _route(logits_t)
    n_tiles = t // ROUTE_TT
    cum = cum[:, :n_tiles].astype(I32)
    counts = cum[:, -1]
    blocks_e = (counts + MOE_BLOCK - 1) // MOE_BLOCK
    bend = jnp.cumsum(blocks_e)
    pstart = (bend - blocks_e) * MOE_BLOCK
    n_blocks = (t * TOP_K) // MOE_BLOCK + N_EXPERTS
    block_e = jnp.minimum(jnp.sum(bend[None, :] <= jnp.arange(n_blocks, dtype=I32)[:, None], axis=1),
                          N_EXPERTS - 1).astype(I32)
    n_valid = bend[-1:].astype(I32)

    before = jnp.concatenate([jnp.zeros((N_EXPERTS, 1), I32), cum[:, :-1]], axis=1)
    run_len = cum - before
    run_src = pstart[:, None] + before
    cov_src = run_src // SUBLANES * SUBLANES
    cov_len = jnp.where(run_len > 0, (run_src + run_len + SUBLANES - 1) // SUBLANES * SUBLANES - cov_src, 0)
    cov_dst = jnp.cumsum(cov_len, axis=0) - cov_len
    run_table = jnp.concatenate([cov_src.T, cov_len.T // SUBLANES, cov_dst.T, (run_src - cov_src).T], axis=1)
    run_table = run_table.reshape(n_tiles, 1, LANES)
    shift = jnp.repeat(cov_dst + run_src - cov_src - before, ROUTE_TT, axis=1)
    pos_kt = r_kt
    for e in range(N_EXPERTS):
        pos_kt = pos_kt + jnp.where(e_kt == e, shift[e][None, :], 0)

    b1 = b_e1[0].reshape(N_EXPERTS, -1, LANES, 2).transpose(0, 1, 3, 2).reshape(N_EXPERTS, 1, -1)
    xs = _dispatch(bend * MOE_BLOCK, blocks_e, n_valid, run_table, pos_kt, hn, n_blocks * MOE_BLOCK)
    first = jnp.concatenate([jnp.ones((1,), I32), (block_e[1:] != block_e[:-1]).astype(I32)])
    slot = (jnp.cumsum(first) - 1) % 2
    ids = jnp.arange(N_EXPERTS, dtype=I32)
    later = (ids[None, :] > ids[:, None]) & (blocks_e[None, :] > 0)
    next_of = jnp.min(jnp.where(later, ids[None, :], N_EXPERTS), axis=1)
    next_of = jnp.where(next_of < N_EXPERTS, next_of, -1)
    next_e = jnp.sum(jnp.where(block_e[:, None] == ids[None, :], next_of[None, :], 0), axis=1).astype(I32)
    ys = _experts(block_e, n_valid, first, slot.astype(I32), next_e, xs, w_e1[0], b1, w_e2[0],
                  b_e2[0][:, None, :], _pair_split_perm())
    out = _final(run_table, x1, g_kt.T, pos_kt.T, g_final[None, :], ys)
    return out.reshape(bsz, seq, d)
```

```python
import jax
import jax.numpy as jnp
import numpy as np
from jax import lax
from jax.experimental import pallas as pl
from jax.experimental.pallas import tpu as pltpu

F32 = jnp.float32
BF16 = jnp.bfloat16
I32 = jnp.int32

NORM_EPS = 1e-5
GLA_HEADS = 4
GLA_DK = 128
GLA_DV = 256
GLA_GATE_RANK = 16
GLA_TAU = 16.0
SWA_Q_HEADS = 16
SWA_KV_HEADS = 2
SWA_GROUP = SWA_Q_HEADS // SWA_KV_HEADS
SWA_HEAD_DIM = 64
SWA_WINDOW = 128
ROPE_THETA = 10000.0
N_EXPERTS = 32
TOP_K = 4
SWIGLU_LIMIT = 7.0
SWIGLU_ALPHA = 1.702

LANES = 128
SUBLANES = 8
NEG = -1e30
VMEM_LIMIT = 56 * 1024 * 1024

PROJ_TM = 256
GLA_CHUNK = 128
GLA_HEADS_PER_STEP = 4
MERGE_TM = 512
MERGE_TN = 256
MERGE_ROW_GROUPS = 1
ROUTE_TT = 512
MOE_BLOCK = 512
DISPATCH_TM = 512
DISPATCH_KC = 512
FINAL_TM = 512
FINAL_KC = 512

_D = 1024
_C_GQ = 0
_C_GK = _C_GQ + GLA_HEADS * GLA_DK
_C_GV = _C_GK + GLA_HEADS * GLA_DK
_C_GR = _C_GV + GLA_HEADS * GLA_DV
_C_SQ = _C_GR + GLA_HEADS * GLA_DV
_C_GA = _C_SQ + SWA_Q_HEADS * SWA_HEAD_DIM
_C_GB = _C_GA + _D
_C_SK = _C_GB + _D
_C_SV = _C_SK + SWA_KV_HEADS * LANES
_C_LR = _C_SV + SWA_KV_HEADS * LANES
_C_END = _C_LR + LANES


def _cparams(sem):
    return pltpu.CompilerParams(dimension_semantics=sem, vmem_limit_bytes=VMEM_LIMIT)


def _rms(x, g):
    return x * lax.rsqrt(jnp.mean(x * x, axis=-1, keepdims=True) + NORM_EPS) * g


def _sigmoid(x):
    return 1.0 / (1.0 + jnp.exp(-x))


def _dot_nt(a, b):
    return lax.dot_general(a, b, (((1,), (1,)), ((), ())), preferred_element_type=F32)


def _rope_slabs(acc, cos, sin, first_half):
    outs = []
    for i in range(acc.shape[1] // LANES):
        xs = acc[:, LANES * i:LANES * (i + 1)]
        partner = jnp.where(first_half, pltpu.roll(xs, LANES - 32, 1), pltpu.roll(xs, 32, 1))
        outs.append(xs * cos + partner * sin)
    return jnp.concatenate(outs, axis=1)


def _in_proj_kernel(x_ref, g_ref, w_ref, b_ref, cos_ref, sin_ref, wg_ref, bg_ref,
                    gq_ref, gk_ref, gv_ref, gr_ref, sq_ref, ga_ref, gb_ref, sk_ref, sv_ref, lg_ref):
    h = _rms(x_ref[...], g_ref[...]).astype(BF16)

    def proj(lo, hi):
        return jnp.dot(h, w_ref[:, lo:hi], preferred_element_type=F32) + b_ref[:, lo:hi]

    cos = cos_ref[...]
    sin = sin_ref[...]
    lane = lax.broadcasted_iota(I32, cos.shape, 1)
    first_half = (lane % SWA_HEAD_DIM) < (SWA_HEAD_DIM // 2)

    gq_ref[...] = (proj(_C_GQ, _C_GK) * (GLA_DK ** -0.5)).astype(BF16)
    gk_ref[...] = proj(_C_GK, _C_GV).astype(BF16)
    gv_ref[...] = proj(_C_GV, _C_GR).astype(BF16)
    gr = proj(_C_GR, _C_SQ)
    gr_ref[...] = (gr * _sigmoid(gr)).astype(BF16)
    sq = proj(_C_SQ, _C_GA) * (SWA_HEAD_DIM ** -0.5)
    sq_ref[...] = _rope_slabs(sq, cos, sin, first_half).astype(BF16)
    ga_ref[...] = _sigmoid(proj(_C_GA, _C_GB)).astype(BF16)
    gb_ref[...] = _sigmoid(proj(_C_GB, _C_SK)).astype(BF16)
    sk_ref[...] = _rope_slabs(proj(_C_SK, _C_SV), cos, sin, first_half).astype(BF16)
    sv_ref[...] = proj(_C_SV, _C_LR).astype(BF16)
    z = jnp.dot(proj(_C_LR, _C_END), wg_ref[...], precision=lax.Precision.HIGHEST,
                preferred_element_type=F32) + bg_ref[...]
    log_sig = jnp.minimum(z, 0.0) - jnp.log(1.0 + jnp.exp(-jnp.abs(z)))
    lg_ref[...] = log_sig * (1.0 / GLA_TAU)


def _in_proj(x2, g_mix, w_all, b_all, cos_t, sin_t, wg, bg, seq):
    t = x2.shape[0]
    tm = PROJ_TM
    pos_blocks = seq // tm
    const = lambda i: (0, 0)
    row = lambda i: (i, 0)
    widths = [(_C_GK - _C_GQ, BF16), (_C_GV - _C_GK, BF16), (_C_GR - _C_GV, BF16), (_C_SQ - _C_GR, BF16),
              (_C_GA - _C_SQ, BF16), (_D, BF16), (_D, BF16), (_C_SV - _C_SK, BF16), (_C_LR - _C_SV, BF16),
              (GLA_HEADS * GLA_DK, F32)]
    return pl.pallas_call(
        _in_proj_kernel,
        grid=(t // tm,),
        in_specs=[
            pl.BlockSpec((tm, _D), row),
            pl.BlockSpec((1, _D), const),
            pl.BlockSpec((_D, _C_END), const, pipeline_mode=pl.Buffered(1)),
            pl.BlockSpec((1, _C_END), const),
            pl.BlockSpec((tm, LANES), lambda i: (i % pos_blocks, 0)),
            pl.BlockSpec((tm, LANES), lambda i: (i % pos_blocks, 0)),
            pl.BlockSpec((LANES, GLA_HEADS * GLA_DK), const),
            pl.BlockSpec((1, GLA_HEADS * GLA_DK), const),
        ],
        out_specs=[pl.BlockSpec((tm, w), row) for w, _ in widths],
        out_shape=[jax.ShapeDtypeStruct((t, w), dt) for w, dt in widths],
        compiler_params=_cparams(("parallel",)),
        name="in_proj",
    )(x2, g_mix, w_all, b_all, cos_t, sin_t, wg, bg)


def _gla_cumsum_operator(c_len):
    t = np.arange(c_len)[:, None]
    r = np.arange(c_len)[None, :]
    return np.tile((r <= t).astype(np.float32), (1, 3))


def _gla_kernel(q_ref, k_ref, v_ref, lg_ref, gr_ref, gh_ref, dm_ref, o_ref, st_ref):
    c_len = GLA_CHUNK
    n_lev = c_len.bit_length() - 1
    seq = q_ref.shape[0]
    st_ref[...] = jnp.zeros_like(st_ref)

    t_i = lax.broadcasted_iota(I32, (c_len, c_len), 0)
    j_i = lax.broadcasted_iota(I32, (c_len, c_len), 1)
    row = lax.broadcasted_iota(I32, (c_len, 1), 0)
    diag = t_i == j_i
    upper, pair = [], []
    for lev in range(n_lev):
        s = c_len >> (lev + 1)
        upper.append((row & s) != 0)
        pair.append(((t_i // (2 * s)) == (j_i // (2 * s))) & ((t_i & s) != 0) & ((j_i & s) == 0))

    sub8 = lax.broadcasted_iota(I32, (c_len // 8, 8, GLA_DK), 1)

    def boundary_rows(b, s):
        if s >= 4:
            b3 = b.reshape(c_len // (2 * s), 2 * s, GLA_DK)
            return jnp.broadcast_to(b3[:, s - 1:s, :], b3.shape).reshape(c_len, GLA_DK)
        b3 = b.reshape(c_len // 8, 8, GLA_DK)
        lo = jnp.broadcast_to(b3[:, 1:2, :], b3.shape)
        hi = jnp.broadcast_to(b3[:, 5:6, :], b3.shape)
        return jnp.where(sub8 < 4, lo, hi).reshape(c_len, GLA_DK)

    def head_chunk(r0, hh):
        kcols = slice(GLA_DK * hh, GLA_DK * (hh + 1))
        vcols = slice(GLA_DV * hh, GLA_DV * (hh + 1))
        q_bf = q_ref[pl.ds(r0, c_len), kcols]
        k_bf = k_ref[pl.ds(r0, c_len), kcols]
        q = q_bf.astype(F32)
        k = k_bf.astype(F32)
        v = v_ref[pl.ds(r0, c_len), vcols]

        lg = lg_ref[pl.ds(r0, c_len), kcols]
        lg_hi = lg.astype(BF16)
        rem = lg - lg_hi.astype(F32)
        lg_mid = rem.astype(BF16)
        lg_lo = (rem - lg_mid.astype(F32)).astype(BF16)
        b = jnp.dot(dm_ref[...], jnp.concatenate([lg_hi, lg_mid, lg_lo], axis=0),
                    preferred_element_type=F32)
        w_cum = jnp.exp(b)

        st = st_ref[hh]
        o = _dot_nt((q * w_cum).astype(BF16), st.astype(BF16))

        a = jnp.where(diag, _dot_nt(q_bf, k_bf), 0.0)
        for lev in range(n_lev):
            s = c_len >> (lev + 1)
            if s == 1:
                w = jnp.where(upper[lev], jnp.exp(lg), 1.0)
            else:
                w = jnp.exp(-jnp.abs(b - boundary_rows(b, s)))
            z = (jnp.where(upper[lev], q, k) * w).astype(BF16)
            a = jnp.where(pair[lev], _dot_nt(z, z), a)
        o = o + jnp.dot(a.astype(BF16), v, preferred_element_type=F32)

        b_last = b[c_len - 1:c_len, :]
        upd = lax.dot_general(v, (k * jnp.exp(b_last - b)).astype(BF16), (((0,), (0,)), ((), ())),
                              preferred_element_type=F32)
        st_ref[hh] = st * w_cum[c_len - 1:c_len, :] + upd

        on = _rms(o, gh_ref[...])
        o_ref[pl.ds(r0, c_len), vcols] = (on * gr_ref[pl.ds(r0, c_len), vcols].astype(F32)).astype(BF16)

    def chunk(c, carry):
        r0 = pl.multiple_of(c * c_len, c_len)
        for hh in range(GLA_HEADS_PER_STEP):
            head_chunk(r0, hh)
        return carry

    lax.fori_loop(0, seq // c_len, chunk, 0, unroll=4)


def _gla(gq, gk, gv, lg, gr, g_head, bsz, seq):
    t = gq.shape[0]
    hs = GLA_HEADS_PER_STEP
    dmat = jnp.asarray(_gla_cumsum_operator(GLA_CHUNK), dtype=BF16)
    return pl.pallas_call(
        _gla_kernel,
        grid=(bsz, GLA_HEADS // hs),
        in_specs=[
            pl.BlockSpec((seq, hs * GLA_DK), lambda b, h: (b, h)),
            pl.BlockSpec((seq, hs * GLA_DK), lambda b, h: (b, h)),
            pl.BlockSpec((seq, hs * GLA_DV), lambda b, h: (b, h)),
            pl.BlockSpec((seq, hs * GLA_DK), lambda b, h: (b, h)),
            pl.BlockSpec((seq, hs * GLA_DV), lambda b, h: (b, h)),
            pl.BlockSpec((1, GLA_DV), lambda b, h: (0, 0)),
            pl.BlockSpec(dmat.shape, lambda b, h: (0, 0)),
        ],
        out_specs=pl.BlockSpec((seq, hs * GLA_DV), lambda b, h: (b, h)),
        out_shape=jax.ShapeDtypeStruct((t, GLA_HEADS * GLA_DV), BF16),
        scratch_shapes=[pltpu.VMEM((hs, GLA_DV, GLA_DK), F32)],
        compiler_params=_cparams(("parallel", "parallel")),
        name="gla",
    )(gq, gk, gv, lg, gr, g_head, dmat)


def _swa_kernel(sink_ref, q_ref, k_ref, v_ref, o_ref):
    w = SWA_WINDOW
    seq = q_ref.shape[0]
    hk = pl.program_id(1)
    lane_q = lax.broadcasted_iota(I32, (w, LANES), 1)
    low_q = lane_q < SWA_HEAD_DIM
    lane_b = lax.broadcasted_iota(I32, (2 * w, LANES), 1)
    low_b = lane_b < SWA_HEAD_DIM
    qi = lax.broadcasted_iota(I32, (w, 2 * w), 0)
    kj = lax.broadcasted_iota(I32, (w, 2 * w), 1)
    in_window = (kj > qi) & (kj <= qi + w)
    zero_q = jnp.zeros((w, LANES), BF16)
    zero_b = jnp.zeros((2 * w, LANES), BF16)

    def block(n, carry):
        r0 = pl.multiple_of(n * w, w)
        p0 = pl.multiple_of(jnp.maximum(n - 1, 0) * w, w)
        kb = jnp.concatenate([k_ref[pl.ds(p0, w), :], k_ref[pl.ds(r0, w), :]], axis=0)
        vb = jnp.concatenate([v_ref[pl.ds(p0, w), :], v_ref[pl.ds(r0, w), :]], axis=0)
        valid = in_window & ((kj >= w) | (n > 0))
        v_lo = jnp.where(low_b, vb, zero_b)
        v_hi = jnp.where(low_b, zero_b, vb)
        for m in range(SWA_GROUP // 2):
            qp = q_ref[pl.ds(r0, w), LANES * m:LANES * (m + 1)]
            acc = jnp.zeros((w, LANES), F32)
            for par in range(2):
                qm = jnp.where(low_q, qp, zero_q) if par == 0 else jnp.where(low_q, zero_q, qp)
                s = jnp.where(valid, _dot_nt(qm, kb), NEG)
                sink = sink_ref[hk * SWA_GROUP + 2 * m + par]
                mx = jnp.maximum(jnp.max(s, axis=-1, keepdims=True), sink)
                p = jnp.exp(s - mx)
                den = jnp.sum(p, axis=-1, keepdims=True) + jnp.exp(sink - mx)
                pv = jnp.dot(p.astype(BF16), v_lo if par == 0 else v_hi, preferred_element_type=F32)
                acc = acc + pv / den
            o_ref[pl.ds(r0, w), LANES * m:LANES * (m + 1)] = acc.astype(BF16)
        return carry

    lax.fori_loop(0, seq // w, block, 0, unroll=8)


def _swa(sinks, sq, sk, sv, bsz, seq):
    t = sq.shape[0]
    gw = SWA_GROUP * SWA_HEAD_DIM
    return pl.pallas_call(
        _swa_kernel,
        grid_spec=pltpu.PrefetchScalarGridSpec(
            num_scalar_prefetch=1,
            grid=(bsz, SWA_KV_HEADS),
            in_specs=[
                pl.BlockSpec((seq, gw), lambda b, h, s: (b, h)),
                pl.BlockSpec((seq, LANES), lambda b, h, s: (b, h)),
                pl.BlockSpec((seq, LANES), lambda b, h, s: (b, h)),
            ],
            out_specs=pl.BlockSpec((seq, gw), lambda b, h, s: (b, h)),
        ),
        out_shape=jax.ShapeDtypeStruct((t, SWA_Q_HEADS * SWA_HEAD_DIM), BF16),
        compiler_params=_cparams(("parallel", "parallel")),
        name="swa",
    )(sinks, sq, sk, sv)


def _merge_kernel(x_ref, oa_ref, ob_ref, ga_ref, gb_ref, wa_ref, wb_ref, wo_ref, gf_ref, wr_ref, br_ref,
                  x1_ref, hn_ref, lt_ref, mixed_ref):
    rows_g = x_ref.shape[0] // MERGE_ROW_GROUPS
    for h in range(MERGE_ROW_GROUPS):
        rows = slice(rows_g * h, rows_g * (h + 1))
        oa = oa_ref[rows, :]
        ob = ob_ref[rows, :]
        for n in range(_D // MERGE_TN):
            cols = slice(MERGE_TN * n, MERGE_TN * (n + 1))
            ya = ga_ref[rows, cols].astype(F32) * jnp.dot(oa, wa_ref[:, cols], preferred_element_type=F32)
            yb = gb_ref[rows, cols].astype(F32) * jnp.dot(ob, wb_ref[:, cols], preferred_element_type=F32)
            mixed_ref[rows, cols] = (ya + yb).astype(BF16)
        mixed = mixed_ref[rows, :]
        for n in range(_D // MERGE_TN):
            cols = slice(MERGE_TN * n, MERGE_TN * (n + 1))
            x1_ref[rows, cols] = x_ref[rows, cols] + jnp.dot(mixed, wo_ref[:, cols], preferred_element_type=F32)
        hn = _rms(x1_ref[rows, :], gf_ref[...])
        hn_ref[rows, :] = hn
        lt_ref[:, rows] = lax.dot_general(wr_ref[...], hn, (((1,), (1,)), ((), ())),
                                          precision=lax.Precision.HIGHEST,
                                          preferred_element_type=F32) + br_ref[...]


def _merge(x2, oa, ob, ga, gb, wa, wb, wo, g_ffn, wr_t, br_col):
    t = x2.shape[0]
    tm = MERGE_TM
    row = lambda i: (i, 0)
    const = lambda i: (0, 0)
    return pl.pallas_call(
        _merge_kernel,
        grid=(t // tm,),
        in_specs=[pl.BlockSpec((tm, _D), row)] * 5 + [pl.BlockSpec((_D, _D), const)] * 3 + [
            pl.BlockSpec((1, _D), const),
            pl.BlockSpec((N_EXPERTS, _D), const),
            pl.BlockSpec((N_EXPERTS, 1), const),
        ],
        out_specs=[pl.BlockSpec((tm, _D), row), pl.BlockSpec((tm, _D), row),
                   pl.BlockSpec((N_EXPERTS, tm), lambda i: (0, i))],
        out_shape=[jax.ShapeDtypeStruct((t, _D), F32), jax.ShapeDtypeStruct((t, _D), F32),
                   jax.ShapeDtypeStruct((N_EXPERTS, t), F32)],
        scratch_shapes=[pltpu.VMEM((tm, _D), BF16)],
        compiler_params=_cparams(("parallel",)),
        name="merge",
    )(x2, oa, ob, ga, gb, wa, wb, wo, g_ffn, wr_t, br_col)


def _route_kernel(lt_ref, e_ref, g_ref, r_ref, cum_ref, carry_ref):
    tt = lt_ref.shape[1]

    @pl.when(pl.program_id(0) == 0)
    def _():
        carry_ref[...] = jnp.zeros_like(carry_ref)
        cum_ref[...] = jnp.zeros_like(cum_ref)

    eid = lax.broadcasted_iota(I32, (N_EXPERTS, tt), 0)
    work = lt_ref[...]
    vals, idxs = [], []
    chosen = jnp.zeros((N_EXPERTS, tt), F32)
    for _ in range(TOP_K):
        m = jnp.max(work, axis=0, keepdims=True)
        idx = jnp.min(jnp.where(work == m, eid, N_EXPERTS), axis=0, keepdims=True)
        hit = eid == idx
        work = jnp.where(hit, -jnp.inf, work)
        chosen = jnp.where(hit, 1.0, chosen)
        vals.append(m)
        idxs.append(idx)
    ex = [jnp.exp(v - vals[0]) for v in vals]
    den = ex[0] + ex[1] + ex[2] + ex[3]

    t_r = lax.broadcasted_iota(I32, (tt, tt), 0)
    t_c = lax.broadcasted_iota(I32, (tt, tt), 1)
    before = (t_r < t_c).astype(BF16)
    pref = jnp.dot(chosen.astype(BF16), before, preferred_element_type=F32) + carry_ref[:, 0:1]
    for k in range(TOP_K):
        e_ref[k:k + 1, :] = idxs[k]
        g_ref[k:k + 1, :] = ex[k] / den
        r_ref[k:k + 1, :] = jnp.sum(jnp.where(eid == idxs[k], pref, 0.0), axis=0, keepdims=True).astype(I32)
    total = pref[:, tt - 1:tt] + chosen[:, tt - 1:tt]
    carry_ref[...] = jnp.broadcast_to(total, carry_ref.shape)
    tile_lane = lax.broadcasted_iota(I32, cum_ref.shape, 1) == pl.program_id(0)
    cum_ref[...] = jnp.where(tile_lane, total, cum_ref[...])


def _route(logits_t):
    t = logits_t.shape[1]
    tt = ROUTE_TT
    blk = lambda i: (0, i)
    return pl.pallas_call(
        _route_kernel,
        grid=(t // tt,),
        in_specs=[pl.BlockSpec((N_EXPERTS, tt), blk)],
        out_specs=[pl.BlockSpec((TOP_K, tt), blk), pl.BlockSpec((TOP_K, tt), blk), pl.BlockSpec((TOP_K, tt), blk),
                   pl.BlockSpec((N_EXPERTS, LANES), lambda i: (0, 0))],
        out_shape=[jax.ShapeDtypeStruct((TOP_K, t), I32), jax.ShapeDtypeStruct((TOP_K, t), F32),
                   jax.ShapeDtypeStruct((TOP_K, t), I32), jax.ShapeDtypeStruct((N_EXPERTS, LANES), F32)],
        scratch_shapes=[pltpu.VMEM((N_EXPERTS, LANES), F32)],
        compiler_params=_cparams(("arbitrary",)),
        name="route",
    )(logits_t)


def _dispatch_kernel(pend_ref, nblk_ref, nv_ref, tab_ref, hn_ref, pos_ref, xs_hbm, spill_hbm,
                     st0_ref, st1_ref, carry_ref, zero_ref, sems, zsem):
    n_tok = hn_ref.shape[0]
    n_stage = st0_ref.shape[0]
    n_blocks = xs_hbm.shape[0] // MOE_BLOCK
    step = pl.program_id(0)
    last = pl.num_programs(0) - 1
    stages = (st0_ref, st1_ref)
    max_units = n_tok // SUBLANES + 1
    pieces = [1 << b for b in reversed(range(max_units.bit_length()))]
    fill_pieces = [1 << b for b in reversed(range((n_stage // SUBLANES).bit_length()))]

    def zero_block(row0):
        return pltpu.make_async_copy(zero_ref, xs_hbm.at[pl.ds(pl.multiple_of(row0, MOE_BLOCK), MOE_BLOCK)], zsem)

    @pl.when(step == 0)
    def _():
        zero_ref[...] = jnp.zeros_like(zero_ref)
        carry_ref[...] = jnp.zeros_like(carry_ref)

        def expert_tail(e, carry, start):
            @pl.when(nblk_ref[e] > 0)
            def _():
                cp = zero_block(pend_ref[e] - MOE_BLOCK)
                cp.start() if start else cp.wait()
            return carry

        def unused_block(b, carry, start):
            cp = zero_block(b * MOE_BLOCK)
            cp.start() if start else cp.wait()
            return carry

        for start in (True, False):
            lax.fori_loop(0, N_EXPERTS, lambda e, c: expert_tail(e, c, start), 0)
            lax.fori_loop(nv_ref[0], n_blocks, lambda b, c: unused_block(b, c, start), 0)

    def permute(slot):
        hb = hn_ref[...].astype(BF16)
        pos = pos_ref[...]
        for c in range(n_stage // DISPATCH_KC):
            row = lax.broadcasted_iota(I32, (DISPATCH_KC, n_tok), 0) + c * DISPATCH_KC
            sel = jnp.zeros((DISPATCH_KC, n_tok), F32)
            for k in range(TOP_K):
                sel = jnp.where(row == pos[k:k + 1, :], 1.0, sel)
            stages[slot][DISPATCH_KC * c:DISPATCH_KC * (c + 1), :] = jnp.dot(
                sel.astype(BF16), hb, preferred_element_type=F32)
        sub = lax.broadcasted_iota(I32, (SUBLANES, hn_ref.shape[1]), 0)
        for e in range(N_EXPERTS):
            units = tab_ref[0, 0, N_EXPERTS + e]

            @pl.when(units > 0)
            def _(e=e, units=units):
                first = pl.multiple_of(tab_ref[0, 0, 2 * N_EXPERTS + e], SUBLANES)
                shared = tab_ref[0, 0, 3 * N_EXPERTS + e]
                head = stages[slot][pl.ds(first, SUBLANES), :]
                stages[slot][pl.ds(first, SUBLANES), :] = jnp.where(sub < shared, carry_ref[e], head)
                final = pl.multiple_of(first + (units - 1) * SUBLANES, SUBLANES)
                carry_ref[e] = stages[slot][pl.ds(final, SUBLANES), :]

    def write_covers(slot):
        total = 0
        for e in range(N_EXPERTS):
            dst = tab_ref[0, 0, e]
            units = tab_ref[0, 0, N_EXPERTS + e]
            src = tab_ref[0, 0, 2 * N_EXPERTS + e]
            total = total + units
            for p in pieces:
                @pl.when((units & p) != 0)
                def _(p=p):
                    done = (units & ~(2 * p - 1)) * SUBLANES
                    pltpu.make_async_copy(
                        stages[slot].at[pl.ds(pl.multiple_of(src + done, SUBLANES), p * SUBLANES)],
                        xs_hbm.at[pl.ds(pl.multiple_of(dst + done, SUBLANES), p * SUBLANES)],
                        sems.at[slot]).start()
        spare = n_stage // SUBLANES - total
        for p in fill_pieces:
            @pl.when((spare & p) != 0)
            def _(p=p):
                done = (spare & ~(2 * p - 1)) * SUBLANES
                pltpu.make_async_copy(
                    stages[slot].at[pl.ds(pl.multiple_of(total * SUBLANES + done, SUBLANES), p * SUBLANES)],
                    spill_hbm.at[pl.ds(pl.multiple_of(done, SUBLANES), p * SUBLANES)],
                    sems.at[slot]).start()

    def drain(slot):
        for c in range(n_stage // DISPATCH_KC):
            pltpu.make_async_copy(stages[slot].at[pl.ds(DISPATCH_KC * c, DISPATCH_KC)],
                                  spill_hbm.at[pl.ds(0, DISPATCH_KC)], sems.at[slot]).wait()

    for slot in (0, 1):
        @pl.when(step % 2 == slot)
        def _(slot=slot):
            permute(slot)

            @pl.when(step > 0)
            def _():
                drain(1 - slot)

            write_covers(slot)

            @pl.when(step == last)
            def _():
                drain(slot)


def _dispatch(pend_rows, blocks_e, n_valid, run_table, pos_kt, hn, rows):
    t = hn.shape[0]
    tm = DISPATCH_TM
    steps = t // tm
    n_stage = -(-(TOP_K * tm + 2 * (SUBLANES - 1) * N_EXPERTS) // DISPATCH_KC) * DISPATCH_KC
    return pl.pallas_call(
        _dispatch_kernel,
        grid_spec=pltpu.PrefetchScalarGridSpec(
            num_scalar_prefetch=3,
            grid=(steps,),
            in_specs=[pl.BlockSpec((1, 1, LANES), lambda i, *_: (i, 0, 0), memory_space=pltpu.SMEM),
                      pl.BlockSpec((tm, _D), lambda i, *_: (i, 0)),
                      pl.BlockSpec((TOP_K, tm), lambda i, *_: (0, i))],
            out_specs=[pl.BlockSpec(memory_space=pl.ANY), pl.BlockSpec(memory_space=pl.ANY)],
            scratch_shapes=[pltpu.VMEM((n_stage, _D), F32), pltpu.VMEM((n_stage, _D), F32),
                            pltpu.VMEM((N_EXPERTS, SUBLANES, _D), F32), pltpu.VMEM((MOE_BLOCK, _D), F32),
                            pltpu.SemaphoreType.DMA((2,)), pltpu.SemaphoreType.DMA],
        ),
        out_shape=[jax.ShapeDtypeStruct((rows, _D), F32), jax.ShapeDtypeStruct((n_stage, _D), F32)],
        compiler_params=_cparams(("arbitrary",)),
        name="dispatch",
    )(pend_rows, blocks_e, n_valid, run_table, hn, pos_kt)[0]


def _experts_kernel(be_ref, nv_ref, first_ref, slot_ref, next_ref, xs_ref, b1_ref, b2_ref, perm_ref,
                    w1_hbm, w2_hbm, ys_ref, wf1_ref, wf2_ref, w1b_ref, w2b_ref, sems):
    i = pl.program_id(0)
    gw = perm_ref.shape[0]

    def weight_copies(e, slot):
        return (pltpu.make_async_copy(w1_hbm.at[e], wf1_ref.at[slot], sems.at[0, slot]),
                pltpu.make_async_copy(w2_hbm.at[e], wf2_ref.at[slot], sems.at[1, slot]))

    @pl.when(i == 0)
    def _():
        for cp in weight_copies(be_ref[0], 0):
            cp.start()

    @pl.when((i < nv_ref[0]) & (first_ref[i] != 0))
    def _():
        slot = slot_ref[i]
        for cp in weight_copies(be_ref[i], slot):
            cp.wait()

        @pl.when(next_ref[i] >= 0)
        def _():
            for cp in weight_copies(next_ref[i], 1 - slot):
                cp.start()

        for gi in range(wf1_ref.shape[2] // gw):
            wb = wf1_ref[slot, :, gw * gi:gw * (gi + 1)].astype(BF16)
            w1b_ref[:, gw * gi:gw * (gi + 1)] = jnp.dot(wb, perm_ref[...], preferred_element_type=F32).astype(BF16)
        w2b_ref[...] = wf2_ref[slot].astype(BF16)

    @pl.when(i < nv_ref[0])
    def _():
        x = xs_ref[...].astype(BF16)
        u = jnp.dot(x, w1b_ref[...], preferred_element_type=F32) + b1_ref[0]
        acts = []
        for gi in range(u.shape[1] // (2 * LANES)):
            g = jnp.minimum(u[:, 2 * LANES * gi:2 * LANES * gi + LANES], SWIGLU_LIMIT)
            lin = jnp.clip(u[:, 2 * LANES * gi + LANES:2 * LANES * (gi + 1)], -SWIGLU_LIMIT, SWIGLU_LIMIT)
            acts.append((g * _sigmoid(SWIGLU_ALPHA * g) * (lin + 1.0)).astype(BF16))
        act = jnp.concatenate(acts, axis=1)
        ys_ref[...] = jnp.dot(act, w2b_ref[...], preferred_element_type=F32) + b2_ref[0]

    @pl.when(i >= nv_ref[0])
    def _():
        ys_ref[...] = jnp.zeros_like(ys_ref)


def _experts(block_e, n_valid, first, slot, next_e, xs, w1, b1, w2, b2, perm):
    rows = xs.shape[0]
    n_blocks = rows // MOE_BLOCK
    n_e, d, n1 = w1.shape
    dff = w2.shape[1]
    wmap = lambda i, be, *_: (be[i], 0, 0)
    rmap = lambda i, *_: (i, 0)
    return pl.pallas_call(
        _experts_kernel,
        grid_spec=pltpu.PrefetchScalarGridSpec(
            num_scalar_prefetch=5,
            grid=(n_blocks,),
            in_specs=[
                pl.BlockSpec((MOE_BLOCK, _D), rmap),
                pl.BlockSpec((1, 1, n1), wmap),
                pl.BlockSpec((1, 1, _D), wmap),
                pl.BlockSpec(perm.shape, lambda i, *_: (0, 0)),
                pl.BlockSpec(memory_space=pl.ANY),
                pl.BlockSpec(memory_space=pl.ANY),
            ],
            out_specs=pl.BlockSpec((MOE_BLOCK, _D), rmap),
            scratch_shapes=[pltpu.VMEM((2, d, n1), F32), pltpu.VMEM((2, dff, _D), F32),
                            pltpu.VMEM((d, n1), BF16), pltpu.VMEM((dff, _D), BF16),
                            pltpu.SemaphoreType.DMA((2, 2))],
        ),
        out_shape=jax.ShapeDtypeStruct((rows, _D), F32),
        compiler_params=_cparams(("arbitrary",)),
        name="experts",
    )(block_e, n_valid, first, slot, next_e, xs, b1, b2, perm, w1, w2)


def _final_kernel(tcur_ref, tnext_ref, x1_ref, gt_ref, pos_ref, gf_ref, ys_hbm, o_ref, st0_ref, st1_ref, sems):
    n_tok = x1_ref.shape[0]
    n_stage = st0_ref.shape[0]
    step = pl.program_id(0)
    last = pl.num_programs(0) - 1
    stages = (st0_ref, st1_ref)
    max_units = n_tok // SUBLANES + 1
    pieces = [1 << b for b in reversed(range(max_units.bit_length()))]
    fill_pieces = [1 << b for b in reversed(range((n_stage // SUBLANES).bit_length()))]

    def fetch_covers(tab_ref, slot):
        total = 0
        for e in range(N_EXPERTS):
            src = tab_ref[0, 0, e]
            units = tab_ref[0, 0, N_EXPERTS + e]
            dst = tab_ref[0, 0, 2 * N_EXPERTS + e]
            total = total + units
            for p in pieces:
                @pl.when((units & p) != 0)
                def _(p=p):
                    done = (units & ~(2 * p - 1)) * SUBLANES
                    pltpu.make_async_copy(
                        ys_hbm.at[pl.ds(pl.multiple_of(src + done, SUBLANES), p * SUBLANES)],
                        stages[slot].at[pl.ds(pl.multiple_of(dst + done, SUBLANES), p * SUBLANES)],
                        sems.at[slot]).start()
        spare = n_stage // SUBLANES - total
        for p in fill_pieces:
            @pl.when((spare & p) != 0)
            def _(p=p):
                done = (spare & ~(2 * p - 1)) * SUBLANES
                pltpu.make_async_copy(
                    ys_hbm.at[pl.ds(pl.multiple_of(done, SUBLANES), p * SUBLANES)],
                    stages[slot].at[pl.ds(pl.multiple_of(total * SUBLANES + done, SUBLANES), p * SUBLANES)],
                    sems.at[slot]).start()

    def drain(slot):
        for c in range(n_stage // FINAL_KC):
            pltpu.make_async_copy(ys_hbm.at[pl.ds(0, FINAL_KC)],
                                  stages[slot].at[pl.ds(FINAL_KC * c, FINAL_KC)], sems.at[slot]).wait()

    def combine(slot):
        gates = gt_ref[...]
        pos = pos_ref[...]
        y = x1_ref[...]
        for c in range(n_stage // FINAL_KC):
            col = lax.broadcasted_iota(I32, (n_tok, FINAL_KC), 1) + c * FINAL_KC
            sel = jnp.zeros((n_tok, FINAL_KC), F32)
            for k in range(TOP_K):
                sel = jnp.where(col == pos[:, k:k + 1], gates[:, k:k + 1], sel)
            rows = stages[slot][FINAL_KC * c:FINAL_KC * (c + 1), :].astype(BF16)
            y = y + jnp.dot(sel.astype(BF16), rows, preferred_element_type=F32)
        o_ref[...] = _rms(y, gf_ref[...])

    @pl.when(step == 0)
    def _():
        st0_ref[...] = jnp.zeros_like(st0_ref)
        st1_ref[...] = jnp.zeros_like(st1_ref)
        fetch_covers(tcur_ref, 0)

    for slot in (0, 1):
        @pl.when(step % 2 == slot)
        def _(slot=slot):
            drain(slot)

            @pl.when(step < last)
            def _():
                fetch_covers(tnext_ref, 1 - slot)

            combine(slot)


def _final(run_table, x1, gates_tk, pos_tk, g_final, ys):
    t = x1.shape[0]
    tm = FINAL_TM
    steps = t // tm
    row = lambda i: (i, 0)
    n_stage = -(-(TOP_K * tm + 2 * (SUBLANES - 1) * N_EXPERTS) // FINAL_KC) * FINAL_KC
    return pl.pallas_call(
        _final_kernel,
        grid=(steps,),
        in_specs=[pl.BlockSpec((1, 1, LANES), lambda i: (i, 0, 0), memory_space=pltpu.SMEM),
                  pl.BlockSpec((1, 1, LANES), lambda i: (jnp.minimum(i + 1, steps - 1), 0, 0),
                               memory_space=pltpu.SMEM),
                  pl.BlockSpec((tm, _D), row), pl.BlockSpec((tm, TOP_K), row), pl.BlockSpec((tm, TOP_K), row),
                  pl.BlockSpec((1, _D), lambda i: (0, 0)), pl.BlockSpec(memory_space=pl.ANY)],
        out_specs=pl.BlockSpec((tm, _D), row),
        out_shape=jax.ShapeDtypeStruct((t, _D), F32),
        scratch_shapes=[pltpu.VMEM((n_stage, _D), F32), pltpu.VMEM((n_stage, _D), F32),
                        pltpu.SemaphoreType.DMA((2,))],
        compiler_params=_cparams(("arbitrary",)),
        name="final",
    )(run_table, run_table, x1, gates_tk, pos_tk, g_final, ys)


def _prep_in_proj(w_in, b_in, w_gate, b_gate):
    sizes = (GLA_HEADS * GLA_DK, GLA_HEADS * GLA_DK, GLA_HEADS * GLA_DV, GLA_HEADS * GLA_DV, GLA_GATE_RANK,
             SWA_Q_HEADS * SWA_HEAD_DIM, SWA_KV_HEADS * SWA_HEAD_DIM, SWA_KV_HEADS * SWA_HEAD_DIM, _D, _D)
    offs = [0]
    for s in sizes:
        offs.append(offs[-1] + s)

    def rearrange(m, dtype):
        p = [m[..., offs[i]:offs[i + 1]].astype(dtype) for i in range(len(sizes))]
        gq, gk, gv, gr, lr, sq, sk, sv, ga, gb = p

        def dup_heads(a):
            hs = [a[..., SWA_HEAD_DIM * h:SWA_HEAD_DIM * (h + 1)] for h in range(SWA_KV_HEADS)]
            return jnp.concatenate([hh for h in hs for hh in (h, h)], axis=-1)

        lr_pad = jnp.pad(lr, [(0, 0)] * (lr.ndim - 1) + [(0, LANES - GLA_GATE_RANK)])
        return jnp.concatenate([gq, gk, gv, gr, sq, ga, gb, dup_heads(sk), dup_heads(sv), lr_pad], axis=-1)

    w_all = rearrange(w_in, BF16)
    b_all = rearrange(b_in[None, :], F32)
    wg = jnp.pad(w_gate, ((0, LANES - GLA_GATE_RANK), (0, 0)))
    return w_all, b_all, wg, b_gate[None, :]


def _rope_tables(seq):
    half = SWA_HEAD_DIM // 2
    inv_freq = ROPE_THETA ** (-jnp.arange(half, dtype=F32) / half)
    ang = jnp.arange(seq, dtype=F32)[:, None] * inv_freq[None, :]
    cos, sin = jnp.cos(ang), jnp.sin(ang)
    cos_t = jnp.concatenate([cos, cos] * (LANES // SWA_HEAD_DIM), axis=1)
    sin_t = jnp.concatenate([-sin, sin] * (LANES // SWA_HEAD_DIM), axis=1)
    return cos_t, sin_t


def _pair_split_perm():
    src = jnp.arange(2 * LANES, dtype=I32)
    dst = jnp.where(src % 2 == 0, src // 2, LANES + src // 2)
    return (dst[:, None] == jnp.arange(2 * LANES, dtype=I32)[None, :]).astype(BF16)


def kernel(x, g_mix, w_in, b_in, w_gla_gate, b_gla_gate, g_gla_head, w_gla_out, sinks, w_swa_out, w_out,
           g_ffn, w_router, b_router, w_e1, b_e1, w_e2, b_e2, g_final):
    bsz, seq, d = x.shape
    assert d == _D and w_in.shape[0] == 1, "single-layer, d_model=1024 only"
    assert seq % max(PROJ_TM, SWA_WINDOW, GLA_CHUNK) == 0
    t = bsz * seq
    assert t % max(MERGE_TM, ROUTE_TT, FINAL_TM, DISPATCH_TM) == 0
    assert FINAL_TM == ROUTE_TT == DISPATCH_TM and t // ROUTE_TT <= LANES and 4 * N_EXPERTS == LANES
    x2 = x.reshape(t, d)

    w_all, b_all, wg, bg = _prep_in_proj(w_in[0], b_in[0], w_gla_gate[0], b_gla_gate[0])
    cos_t, sin_t = _rope_tables(seq)
    gq, gk, gv, gr, sq, ga, gb, sk, sv, lg = _in_proj(x2, g_mix, w_all, b_all, cos_t, sin_t, wg, bg, seq)
    oa = _gla(gq, gk, gv, lg, gr, g_gla_head, bsz, seq)
    ob = _swa(sinks[0], sq, sk, sv, bsz, seq)
    x1, hn, logits_t = _merge(x2, oa, ob, ga, gb, w_gla_out[0].astype(BF16), w_swa_out[0].astype(BF16),
                              w_out[0].astype(BF16), g_ffn, w_router[0].T, b_router[0][:, None])

    e_kt, g_kt, r_kt, cum = _route(logits_t)
    n_tiles = t // ROUTE_TT
    cum = cum[:, :n_tiles].astype(I32)
    counts = cum[:, -1]
    blocks_e = (counts + MOE_BLOCK - 1) // MOE_BLOCK
    bend = jnp.cumsum(blocks_e)
    pstart = (bend - blocks_e) * MOE_BLOCK
    n_blocks = (t * TOP_K) // MOE_BLOCK + N_EXPERTS
    block_e = jnp.minimum(jnp.sum(bend[None, :] <= jnp.arange(n_blocks, dtype=I32)[:, None], axis=1),
                          N_EXPERTS - 1).astype(I32)
    n_valid = bend[-1:].astype(I32)

    before = jnp.concatenate([jnp.zeros((N_EXPERTS, 1), I32), cum[:, :-1]], axis=1)
    run_len = cum - before
    run_src = pstart[:, None] + before
    cov_src = run_src // SUBLANES * SUBLANES
    cov_len = jnp.where(run_len > 0, (run_src + run_len + SUBLANES - 1) // SUBLANES * SUBLANES - cov_src, 0)
    cov_dst = jnp.cumsum(cov_len, axis=0) - cov_len
    run_table = jnp.concatenate([cov_src.T, cov_len.T // SUBLANES, cov_dst.T, (run_src - cov_src).T], axis=1)
    run_table = run_table.reshape(n_tiles, 1, LANES)
    shift = jnp.repeat(cov_dst + run_src - cov_src - before, ROUTE_TT, axis=1)
    pos_kt = r_kt
    for e in range(N_EXPERTS):
        pos_kt = pos_kt + jnp.where(e_kt == e, shift[e][None, :], 0)

    b1 = b_e1[0].reshape(N_EXPERTS, -1, LANES, 2).transpose(0, 1, 3, 2).reshape(N_EXPERTS, 1, -1)
    xs = _dispatch(bend * MOE_BLOCK, blocks_e, n_valid, run_table, pos_kt, hn, n_blocks * MOE_BLOCK)
    first = jnp.concatenate([jnp.ones((1,), I32), (block_e[1:] != block_e[:-1]).astype(I32)])
    slot = (jnp.cumsum(first) - 1) % 2
    ids = jnp.arange(N_EXPERTS, dtype=I32)
    later = (ids[None, :] > ids[:, None]) & (blocks_e[None, :] > 0)
    next_of = jnp.min(jnp.where(later, ids[None, :], N_EXPERTS), axis=1)
    next_of = jnp.where(next_of < N_EXPERTS, next_of, -1)
    next_e = jnp.sum(jnp.where(block_e[:, None] == ids[None, :], next_of[None, :], 0), axis=1).astype(I32)
    ys = _experts(block_e, n_valid, first, slot.astype(I32), next_e, xs, w_e1[0], b1, w_e2[0],
                  b_e2[0][:, None, :], _pair_split_perm())
    out = _final(run_table, x1, g_kt.T, pos_kt.T, g_final[None, :], ys)
    return out.reshape(bsz, seq, d)
```

```python
import jax
import jax.numpy as jnp
import numpy as np
from jax import lax
from jax.experimental import pallas as pl
from jax.experimental.pallas import tpu as pltpu

F32 = jnp.float32
BF16 = jnp.bfloat16
I32 = jnp.int32

NORM_EPS = 1e-5
GLA_HEADS = 4
GLA_DK = 128
GLA_DV = 256
GLA_GATE_RANK = 16
GLA_TAU = 16.0
SWA_Q_HEADS = 16
SWA_KV_HEADS = 2
SWA_GROUP = SWA_Q_HEADS // SWA_KV_HEADS
SWA_HEAD_DIM = 64
SWA_WINDOW = 128
ROPE_THETA = 10000.0
N_EXPERTS = 32
TOP_K = 4
SWIGLU_LIMIT = 7.0
SWIGLU_ALPHA = 1.702

LANES = 128
SUBLANES = 8
NEG = -1e30
VMEM_LIMIT = 56 * 1024 * 1024

PROJ_TM = 256
GLA_CHUNK = 128
GLA_HEADS_PER_STEP = 4
MERGE_TM = 512
MERGE_TN = 256
MERGE_ROW_GROUPS = 1
ROUTE_TT = 512
MOE_BLOCK = 512
DISPATCH_TM = 512
DISPATCH_KC = 512
FINAL_TM = 512
FINAL_KC = 512

_D = 1024
_C_GQ = 0
_C_GK = _C_GQ + GLA_HEADS * GLA_DK
_C_GV = _C_GK + GLA_HEADS * GLA_DK
_C_GR = _C_GV + GLA_HEADS * GLA_DV
_C_SQ = _C_GR + GLA_HEADS * GLA_DV
_C_GA = _C_SQ + SWA_Q_HEADS * SWA_HEAD_DIM
_C_GB = _C_GA + _D
_C_SK = _C_GB + _D
_C_SV = _C_SK + SWA_KV_HEADS * LANES
_C_LR = _C_SV + SWA_KV_HEADS * LANES
_C_END = _C_LR + LANES


def _cparams(sem):
    return pltpu.CompilerParams(dimension_semantics=sem, vmem_limit_bytes=VMEM_LIMIT)


def _rms(x, g):
    return x * lax.rsqrt(jnp.mean(x * x, axis=-1, keepdims=True) + NORM_EPS) * g


def _sigmoid(x):
    return 1.0 / (1.0 + jnp.exp(-x))


def _dot_nt(a, b):
    return lax.dot_general(a, b, (((1,), (1,)), ((), ())), preferred_element_type=F32)


def _rope_slabs(acc, cos, sin, first_half):
    outs = []
    for i in range(acc.shape[1] // LANES):
        xs = acc[:, LANES * i:LANES * (i + 1)]
        partner = jnp.where(first_half, pltpu.roll(xs, LANES - 32, 1), pltpu.roll(xs, 32, 1))
        outs.append(xs * cos + partner * sin)
    return jnp.concatenate(outs, axis=1)


def _in_proj_kernel(x_ref, g_ref, w_ref, b_ref, cos_ref, sin_ref, wg_ref, bg_ref,
                    gq_ref, gk_ref, gv_ref, gr_ref, sq_ref, ga_ref, gb_ref, sk_ref, sv_ref, lg_ref):
    h = _rms(x_ref[...], g_ref[...]).astype(BF16)

    def proj(lo, hi):
        return jnp.dot(h, w_ref[:, lo:hi], preferred_element_type=F32) + b_ref[:, lo:hi]

    cos = cos_ref[...]
    sin = sin_ref[...]
    lane = lax.broadcasted_iota(I32, cos.shape, 1)
    first_half = (lane % SWA_HEAD_DIM) < (SWA_HEAD_DIM // 2)

    gq_ref[...] = (proj(_C_GQ, _C_GK) * (GLA_DK ** -0.5)).astype(BF16)
    gk_ref[...] = proj(_C_GK, _C_GV).astype(BF16)
    gv_ref[...] = proj(_C_GV, _C_GR).astype(BF16)
    gr = proj(_C_GR, _C_SQ)
    gr_ref[...] = (gr * _sigmoid(gr)).astype(BF16)
    sq = proj(_C_SQ, _C_GA) * (SWA_HEAD_DIM ** -0.5)
    sq_ref[...] = _rope_slabs(sq, cos, sin, first_half).astype(BF16)
    ga_ref[...] = _sigmoid(proj(_C_GA, _C_GB)).astype(BF16)
    gb_ref[...] = _sigmoid(proj(_C_GB, _C_SK)).astype(BF16)
    sk_ref[...] = _rope_slabs(proj(_C_SK, _C_SV), cos, sin, first_half).astype(BF16)
    sv_ref[...] = proj(_C_SV, _C_LR).astype(BF16)
    z = jnp.dot(proj(_C_LR, _C_END), wg_ref[...], precision=lax.Precision.HIGHEST,
                preferred_element_type=F32) + bg_ref[...]
    log_sig = jnp.minimum(z, 0.0) - jnp.log(1.0 + jnp.exp(-jnp.abs(z)))
    lg_ref[...] = log_sig * (1.0 / GLA_TAU)


def _in_proj(x2, g_mix, w_all, b_all, cos_t, sin_t, wg, bg, seq):
    t = x2.shape[0]
    tm = PROJ_TM
    pos_blocks = seq // tm
    const = lambda i: (0, 0)
    row = lambda i: (i, 0)
    widths = [(_C_GK - _C_GQ, BF16), (_C_GV - _C_GK, BF16), (_C_GR - _C_GV, BF16), (_C_SQ - _C_GR, BF16),
              (_C_GA - _C_SQ, BF16), (_D, BF16), (_D, BF16), (_C_SV - _C_SK, BF16), (_C_LR - _C_SV, BF16),
              (GLA_HEADS * GLA_DK, F32)]
    return pl.pallas_call(
        _in_proj_kernel,
        grid=(t // tm,),
        in_specs=[
            pl.BlockSpec((tm, _D), row),
            pl.BlockSpec((1, _D), const),
            pl.BlockSpec((_D, _C_END), const, pipeline_mode=pl.Buffered(1)),
            pl.BlockSpec((1, _C_END), const),
            pl.BlockSpec((tm, LANES), lambda i: (i % pos_blocks, 0)),
            pl.BlockSpec((tm, LANES), lambda i: (i % pos_blocks, 0)),
            pl.BlockSpec((LANES, GLA_HEADS * GLA_DK), const),
            pl.BlockSpec((1, GLA_HEADS * GLA_DK), const),
        ],
        out_specs=[pl.BlockSpec((tm, w), row) for w, _ in widths],
        out_shape=[jax.ShapeDtypeStruct((t, w), dt) for w, dt in widths],
        compiler_params=_cparams(("parallel",)),
        name="in_proj",
    )(x2, g_mix, w_all, b_all, cos_t, sin_t, wg, bg)


def _gla_cumsum_operator(c_len):
    t = np.arange(c_len)[:, None]
    r = np.arange(c_len)[None, :]
    return np.tile((r <= t).astype(np.float32), (1, 3))


def _gla_kernel(q_ref, k_ref, v_ref, lg_ref, gr_ref, gh_ref, dm_ref, o_ref, st_ref):
    c_len = GLA_CHUNK
    n_lev = c_len.bit_length() - 1
    seq = q_ref.shape[0]
    st_ref[...] = jnp.zeros_like(st_ref)

    t_i = lax.broadcasted_iota(I32, (c_len, c_len), 0)
    j_i = lax.broadcasted_iota(I32, (c_len, c_len), 1)
    row = lax.broadcasted_iota(I32, (c_len, 1), 0)
    diag = t_i == j_i
    upper, pair = [], []
    for lev in range(n_lev):
        s = c_len >> (lev + 1)
        upper.append((row & s) != 0)
        pair.append(((t_i // (2 * s)) == (j_i // (2 * s))) & ((t_i & s) != 0) & ((j_i & s) == 0))

    sub8 = lax.broadcasted_iota(I32, (c_len // 8, 8, GLA_DK), 1)

    def boundary_rows(b, s):
        if s >= 4:
            b3 = b.reshape(c_len // (2 * s), 2 * s, GLA_DK)
            return jnp.broadcast_to(b3[:, s - 1:s, :], b3.shape).reshape(c_len, GLA_DK)
        b3 = b.reshape(c_len // 8, 8, GLA_DK)
        lo = jnp.broadcast_to(b3[:, 1:2, :], b3.shape)
        hi = jnp.broadcast_to(b3[:, 5:6, :], b3.shape)
        return jnp.where(sub8 < 4, lo, hi).reshape(c_len, GLA_DK)

    def head_chunk(r0, hh):
        kcols = slice(GLA_DK * hh, GLA_DK * (hh + 1))
        vcols = slice(GLA_DV * hh, GLA_DV * (hh + 1))
        q_bf = q_ref[pl.ds(r0, c_len), kcols]
        k_bf = k_ref[pl.ds(r0, c_len), kcols]
        q = q_bf.astype(F32)
        k = k_bf.astype(F32)
        v = v_ref[pl.ds(r0, c_len), vcols]

        lg = lg_ref[pl.ds(r0, c_len), kcols]
        lg_hi = lg.astype(BF16)
        rem = lg - lg_hi.astype(F32)
        lg_mid = rem.astype(BF16)
        lg_lo = (rem - lg_mid.astype(F32)).astype(BF16)
        b = jnp.dot(dm_ref[...], jnp.concatenate([lg_hi, lg_mid, lg_lo], axis=0),
                    preferred_element_type=F32)
        w_cum = jnp.exp(b)

        st = st_ref[hh]
        o = _dot_nt((q * w_cum).astype(BF16), st.astype(BF16))

        a = jnp.where(diag, _dot_nt(q_bf, k_bf), 0.0)
        for lev in range(n_lev):
            s = c_len >> (lev + 1)
            if s == 1:
                w = jnp.where(upper[lev], jnp.exp(lg), 1.0)
            else:
                w = jnp.exp(-jnp.abs(b - boundary_rows(b, s)))
            z = (jnp.where(upper[lev], q, k) * w).astype(BF16)
            a = jnp.where(pair[lev], _dot_nt(z, z), a)
        o = o + jnp.dot(a.astype(BF16), v, preferred_element_type=F32)

        b_last = b[c_len - 1:c_len, :]
        upd = lax.dot_general(v, (k * jnp.exp(b_last - b)).astype(BF16), (((0,), (0,)), ((), ())),
                              preferred_element_type=F32)
        st_ref[hh] = st * w_cum[c_len - 1:c_len, :] + upd

        on = _rms(o, gh_ref[...])
        o_ref[pl.ds(r0, c_len), vcols] = (on * gr_ref[pl.ds(r0, c_len), vcols].astype(F32)).astype(BF16)

    def chunk(c, carry):
        r0 = pl.multiple_of(c * c_len, c_len)
        for hh in range(GLA_HEADS_PER_STEP):
            head_chunk(r0, hh)
        return carry

    lax.fori_loop(0, seq // c_len, chunk, 0, unroll=8)


def _gla(gq, gk, gv, lg, gr, g_head, bsz, seq):
    t = gq.shape[0]
    hs = GLA_HEADS_PER_STEP
    dmat = jnp.asarray(_gla_cumsum_operator(GLA_CHUNK), dtype=BF16)
    return pl.pallas_call(
        _gla_kernel,
        grid=(bsz, GLA_HEADS // hs),
        in_specs=[
            pl.BlockSpec((seq, hs * GLA_DK), lambda b, h: (b, h)),
            pl.BlockSpec((seq, hs * GLA_DK), lambda b, h: (b, h)),
            pl.BlockSpec((seq, hs * GLA_DV), lambda b, h: (b, h)),
            pl.BlockSpec((seq, hs * GLA_DK), lambda b, h: (b, h)),
            pl.BlockSpec((seq, hs * GLA_DV), lambda b, h: (b, h)),
            pl.BlockSpec((1, GLA_DV), lambda b, h: (0, 0)),
            pl.BlockSpec(dmat.shape, lambda b, h: (0, 0)),
        ],
        out_specs=pl.BlockSpec((seq, hs * GLA_DV), lambda b, h: (b, h)),
        out_shape=jax.ShapeDtypeStruct((t, GLA_HEADS * GLA_DV), BF16),
        scratch_shapes=[pltpu.VMEM((hs, GLA_DV, GLA_DK), F32)],
        compiler_params=_cparams(("parallel", "parallel")),
        name="gla",
    )(gq, gk, gv, lg, gr, g_head, dmat)


def _swa_kernel(sink_ref, q_ref, k_ref, v_ref, o_ref):
    w = SWA_WINDOW
    seq = q_ref.shape[0]
    hk = pl.program_id(1)
    lane_q = lax.broadcasted_iota(I32, (w, LANES), 1)
    low_q = lane_q < SWA_HEAD_DIM
    lane_b = lax.broadcasted_iota(I32, (2 * w, LANES), 1)
    low_b = lane_b < SWA_HEAD_DIM
    qi = lax.broadcasted_iota(I32, (w, 2 * w), 0)
    kj = lax.broadcasted_iota(I32, (w, 2 * w), 1)
    in_window = (kj > qi) & (kj <= qi + w)
    zero_q = jnp.zeros((w, LANES), BF16)
    zero_b = jnp.zeros((2 * w, LANES), BF16)

    def block(n, carry):
        r0 = pl.multiple_of(n * w, w)
        p0 = pl.multiple_of(jnp.maximum(n - 1, 0) * w, w)
        kb = jnp.concatenate([k_ref[pl.ds(p0, w), :], k_ref[pl.ds(r0, w), :]], axis=0)
        vb = jnp.concatenate([v_ref[pl.ds(p0, w), :], v_ref[pl.ds(r0, w), :]], axis=0)
        valid = in_window & ((kj >= w) | (n > 0))
        v_lo = jnp.where(low_b, vb, zero_b)
        v_hi = jnp.where(low_b, zero_b, vb)
        for m in range(SWA_GROUP // 2):
            qp = q_ref[pl.ds(r0, w), LANES * m:LANES * (m + 1)]
            acc = jnp.zeros((w, LANES), F32)
            for par in range(2):
                qm = jnp.where(low_q, qp, zero_q) if par == 0 else jnp.where(low_q, zero_q, qp)
                s = jnp.where(valid, _dot_nt(qm, kb), NEG)
                sink = sink_ref[hk * SWA_GROUP + 2 * m + par]
                mx = jnp.maximum(jnp.max(s, axis=-1, keepdims=True), sink)
                p = jnp.exp(s - mx)
                den = jnp.sum(p, axis=-1, keepdims=True) + jnp.exp(sink - mx)
                pv = jnp.dot(p.astype(BF16), v_lo if par == 0 else v_hi, preferred_element_type=F32)
                acc = acc + pv / den
            o_ref[pl.ds(r0, w), LANES * m:LANES * (m + 1)] = acc.astype(BF16)
        return carry

    lax.fori_loop(0, seq // w, block, 0, unroll=16)


def _swa(sinks, sq, sk, sv, bsz, seq):
    t = sq.shape[0]
    gw = SWA_GROUP * SWA_HEAD_DIM
    return pl.pallas_call(
        _swa_kernel,
        grid_spec=pltpu.PrefetchScalarGridSpec(
            num_scalar_prefetch=1,
            grid=(bsz, SWA_KV_HEADS),
            in_specs=[
                pl.BlockSpec((seq, gw), lambda b, h, s: (b, h)),
                pl.BlockSpec((seq, LANES), lambda b, h, s: (b, h)),
                pl.BlockSpec((seq, LANES), lambda b, h, s: (b, h)),
            ],
            out_specs=pl.BlockSpec((seq, gw), lambda b, h, s: (b, h)),
        ),
        out_shape=jax.ShapeDtypeStruct((t, SWA_Q_HEADS * SWA_HEAD_DIM), BF16),
        compiler_params=_cparams(("parallel", "parallel")),
        name="swa",
    )(sinks, sq, sk, sv)


def _merge_kernel(x_ref, oa_ref, ob_ref, ga_ref, gb_ref, wa_ref, wb_ref, wo_ref, gf_ref, wr_ref, br_ref,
                  x1_ref, hn_ref, lt_ref, mixed_ref):
    rows_g = x_ref.shape[0] // MERGE_ROW_GROUPS
    for h in range(MERGE_ROW_GROUPS):
        rows = slice(rows_g * h, rows_g * (h + 1))
        oa = oa_ref[rows, :]
        ob = ob_ref[rows, :]
        for n in range(_D // MERGE_TN):
            cols = slice(MERGE_TN * n, MERGE_TN * (n + 1))
            ya = ga_ref[rows, cols].astype(F32) * jnp.dot(oa, wa_ref[:, cols], preferred_element_type=F32)
            yb = gb_ref[rows, cols].astype(F32) * jnp.dot(ob, wb_ref[:, cols], preferred_element_type=F32)
            mixed_ref[rows, cols] = (ya + yb).astype(BF16)
        mixed = mixed_ref[rows, :]
        for n in range(_D // MERGE_TN):
            cols = slice(MERGE_TN * n, MERGE_TN * (n + 1))
            x1_ref[rows, cols] = x_ref[rows, cols] + jnp.dot(mixed, wo_ref[:, cols], preferred_element_type=F32)
        hn = _rms(x1_ref[rows, :], gf_ref[...])
        hn_ref[rows, :] = hn
        lt_ref[:, rows] = lax.dot_general(wr_ref[...], hn, (((1,), (1,)), ((), ())),
                                          precision=lax.Precision.HIGHEST,
                                          preferred_element_type=F32) + br_ref[...]


def _merge(x2, oa, ob, ga, gb, wa, wb, wo, g_ffn, wr_t, br_col):
    t = x2.shape[0]
    tm = MERGE_TM
    row = lambda i: (i, 0)
    const = lambda i: (0, 0)
    return pl.pallas_call(
        _merge_kernel,
        grid=(t // tm,),
        in_specs=[pl.BlockSpec((tm, _D), row)] * 5 + [pl.BlockSpec((_D, _D), const)] * 3 + [
            pl.BlockSpec((1, _D), const),
            pl.BlockSpec((N_EXPERTS, _D), const),
            pl.BlockSpec((N_EXPERTS, 1), const),
        ],
        out_specs=[pl.BlockSpec((tm, _D), row), pl.BlockSpec((tm, _D), row),
                   pl.BlockSpec((N_EXPERTS, tm), lambda i: (0, i))],
        out_shape=[jax.ShapeDtypeStruct((t, _D), F32), jax.ShapeDtypeStruct((t, _D), F32),
                   jax.ShapeDtypeStruct((N_EXPERTS, t), F32)],
        scratch_shapes=[pltpu.VMEM((tm, _D), BF16)],
        compiler_params=_cparams(("parallel",)),
        name="merge",
    )(x2, oa, ob, ga, gb, wa, wb, wo, g_ffn, wr_t, br_col)


def _route_kernel(lt_ref, e_ref, g_ref, r_ref, cum_ref, carry_ref):
    tt = lt_ref.shape[1]

    @pl.when(pl.program_id(0) == 0)
    def _():
        carry_ref[...] = jnp.zeros_like(carry_ref)
        cum_ref[...] = jnp.zeros_like(cum_ref)

    eid = lax.broadcasted_iota(I32, (N_EXPERTS, tt), 0)
    work = lt_ref[...]
    vals, idxs = [], []
    chosen = jnp.zeros((N_EXPERTS, tt), F32)
    for _ in range(TOP_K):
        m = jnp.max(work, axis=0, keepdims=True)
        idx = jnp.min(jnp.where(work == m, eid, N_EXPERTS), axis=0, keepdims=True)
        hit = eid == idx
        work = jnp.where(hit, -jnp.inf, work)
        chosen = jnp.where(hit, 1.0, chosen)
        vals.append(m)
        idxs.append(idx)
    ex = [jnp.exp(v - vals[0]) for v in vals]
    den = ex[0] + ex[1] + ex[2] + ex[3]

    t_r = lax.broadcasted_iota(I32, (tt, tt), 0)
    t_c = lax.broadcasted_iota(I32, (tt, tt), 1)
    before = (t_r < t_c).astype(BF16)
    pref = jnp.dot(chosen.astype(BF16), before, preferred_element_type=F32) + carry_ref[:, 0:1]
    for k in range(TOP_K):
        e_ref[k:k + 1, :] = idxs[k]
        g_ref[k:k + 1, :] = ex[k] / den
        r_ref[k:k + 1, :] = jnp.sum(jnp.where(eid == idxs[k], pref, 0.0), axis=0, keepdims=True).astype(I32)
    total = pref[:, tt - 1:tt] + chosen[:, tt - 1:tt]
    carry_ref[...] = jnp.broadcast_to(total, carry_ref.shape)
    tile_lane = lax.broadcasted_iota(I32, cum_ref.shape, 1) == pl.program_id(0)
    cum_ref[...] = jnp.where(tile_lane, total, cum_ref[...])


def _route(logits_t):
    t = logits_t.shape[1]
    tt = ROUTE_TT
    blk = lambda i: (0, i)
    return pl.pallas_call(
        _route_kernel,
        grid=(t // tt,),
        in_specs=[pl.BlockSpec((N_EXPERTS, tt), blk)],
        out_specs=[pl.BlockSpec((TOP_K, tt), blk), pl.BlockSpec((TOP_K, tt), blk), pl.BlockSpec((TOP_K, tt), blk),
                   pl.BlockSpec((N_EXPERTS, LANES), lambda i: (0, 0))],
        out_shape=[jax.ShapeDtypeStruct((TOP_K, t), I32), jax.ShapeDtypeStruct((TOP_K, t), F32),
                   jax.ShapeDtypeStruct((TOP_K, t), I32), jax.ShapeDtypeStruct((N_EXPERTS, LANES), F32)],
        scratch_shapes=[pltpu.VMEM((N_EXPERTS, LANES), F32)],
        compiler_params=_cparams(("arbitrary",)),
        name="route",
    )(logits_t)


def _dispatch_kernel(pend_ref, nblk_ref, nv_ref, tab_ref, hn_ref, pos_ref, xs_hbm, spill_hbm,
                     st0_ref, st1_ref, carry_ref, zero_ref, sems, zsem):
    n_tok = hn_ref.shape[0]
    n_stage = st0_ref.shape[0]
    n_blocks = xs_hbm.shape[0] // MOE_BLOCK
    step = pl.program_id(0)
    last = pl.num_programs(0) - 1
    stages = (st0_ref, st1_ref)
    max_units = n_tok // SUBLANES + 1
    pieces = [1 << b for b in reversed(range(max_units.bit_length()))]
    fill_pieces = [1 << b for b in reversed(range((n_stage // SUBLANES).bit_length()))]

    def zero_block(row0):
        return pltpu.make_async_copy(zero_ref, xs_hbm.at[pl.ds(pl.multiple_of(row0, MOE_BLOCK), MOE_BLOCK)], zsem)

    @pl.when(step == 0)
    def _():
        zero_ref[...] = jnp.zeros_like(zero_ref)
        carry_ref[...] = jnp.zeros_like(carry_ref)

        def expert_tail(e, carry, start):
            @pl.when(nblk_ref[e] > 0)
            def _():
                cp = zero_block(pend_ref[e] - MOE_BLOCK)
                cp.start() if start else cp.wait()
            return carry

        def unused_block(b, carry, start):
            cp = zero_block(b * MOE_BLOCK)
            cp.start() if start else cp.wait()
            return carry

        for start in (True, False):
            lax.fori_loop(0, N_EXPERTS, lambda e, c: expert_tail(e, c, start), 0)
            lax.fori_loop(nv_ref[0], n_blocks, lambda b, c: unused_block(b, c, start), 0)

    def permute(slot):
        hb = hn_ref[...].astype(BF16)
        pos = pos_ref[...]
        for c in range(n_stage // DISPATCH_KC):
            row = lax.broadcasted_iota(I32, (DISPATCH_KC, n_tok), 0) + c * DISPATCH_KC
            sel = jnp.zeros((DISPATCH_KC, n_tok), F32)
            for k in range(TOP_K):
                sel = jnp.where(row == pos[k:k + 1, :], 1.0, sel)
            stages[slot][DISPATCH_KC * c:DISPATCH_KC * (c + 1), :] = jnp.dot(
                sel.astype(BF16), hb, preferred_element_type=F32)
        sub = lax.broadcasted_iota(I32, (SUBLANES, hn_ref.shape[1]), 0)
        for e in range(N_EXPERTS):
            units = tab_ref[0, 0, N_EXPERTS + e]

            @pl.when(units > 0)
            def _(e=e, units=units):
                first = pl.multiple_of(tab_ref[0, 0, 2 * N_EXPERTS + e], SUBLANES)
                shared = tab_ref[0, 0, 3 * N_EXPERTS + e]
                head = stages[slot][pl.ds(first, SUBLANES), :]
                stages[slot][pl.ds(first, SUBLANES), :] = jnp.where(sub < shared, carry_ref[e], head)
                final = pl.multiple_of(first + (units - 1) * SUBLANES, SUBLANES)
                carry_ref[e] = stages[slot][pl.ds(final, SUBLANES), :]

    def write_covers(slot):
        total = 0
        for e in range(N_EXPERTS):
            dst = tab_ref[0, 0, e]
            units = tab_ref[0, 0, N_EXPERTS + e]
            src = tab_ref[0, 0, 2 * N_EXPERTS + e]
            total = total + units
            for p in pieces:
                @pl.when((units & p) != 0)
                def _(p=p):
                    done = (units & ~(2 * p - 1)) * SUBLANES
                    pltpu.make_async_copy(
                        stages[slot].at[pl.ds(pl.multiple_of(src + done, SUBLANES), p * SUBLANES)],
                        xs_hbm.at[pl.ds(pl.multiple_of(dst + done, SUBLANES), p * SUBLANES)],
                        sems.at[slot]).start()
        spare = n_stage // SUBLANES - total
        for p in fill_pieces:
            @pl.when((spare & p) != 0)
            def _(p=p):
                done = (spare & ~(2 * p - 1)) * SUBLANES
                pltpu.make_async_copy(
                    stages[slot].at[pl.ds(pl.multiple_of(total * SUBLANES + done, SUBLANES), p * SUBLANES)],
                    spill_hbm.at[pl.ds(pl.multiple_of(done, SUBLANES), p * SUBLANES)],
                    sems.at[slot]).start()

    def drain(slot):
        for c in range(n_stage // DISPATCH_KC):
            pltpu.make_async_copy(stages[slot].at[pl.ds(DISPATCH_KC * c, DISPATCH_KC)],
                                  spill_hbm.at[pl.ds(0, DISPATCH_KC)], sems.at[slot]).wait()

    for slot in (0, 1):
        @pl.when(step % 2 == slot)
        def _(slot=slot):
            permute(slot)

            @pl.when(step > 0)
            def _():
                drain(1 - slot)

            write_covers(slot)

            @pl.when(step == last)
            def _():
                drain(slot)


def _dispatch(pend_rows, blocks_e, n_valid, run_table, pos_kt, hn, rows):
    t = hn.shape[0]
    tm = DISPATCH_TM
    steps = t // tm
    n_stage = -(-(TOP_K * tm + 2 * (SUBLANES - 1) * N_EXPERTS) // DISPATCH_KC) * DISPATCH_KC
    return pl.pallas_call(
        _dispatch_kernel,
        grid_spec=pltpu.PrefetchScalarGridSpec(
            num_scalar_prefetch=3,
            grid=(steps,),
            in_specs=[pl.BlockSpec((1, 1, LANES), lambda i, *_: (i, 0, 0), memory_space=pltpu.SMEM),
                      pl.BlockSpec((tm, _D), lambda i, *_: (i, 0)),
                      pl.BlockSpec((TOP_K, tm), lambda i, *_: (0, i))],
            out_specs=[pl.BlockSpec(memory_space=pl.ANY), pl.BlockSpec(memory_space=pl.ANY)],
            scratch_shapes=[pltpu.VMEM((n_stage, _D), F32), pltpu.VMEM((n_stage, _D), F32),
                            pltpu.VMEM((N_EXPERTS, SUBLANES, _D), F32), pltpu.VMEM((MOE_BLOCK, _D), F32),
                            pltpu.SemaphoreType.DMA((2,)), pltpu.SemaphoreType.DMA],
        ),
        out_shape=[jax.ShapeDtypeStruct((rows, _D), F32), jax.ShapeDtypeStruct((n_stage, _D), F32)],
        compiler_params=_cparams(("arbitrary",)),
        name="dispatch",
    )(pend_rows, blocks_e, n_valid, run_table, hn, pos_kt)[0]


def _experts_kernel(be_ref, nv_ref, first_ref, slot_ref, next_ref, xs_ref, b1_ref, b2_ref, perm_ref,
                    w1_hbm, w2_hbm, ys_ref, wf1_ref, wf2_ref, w1b_ref, w2b_ref, sems):
    i = pl.program_id(0)
    gw = perm_ref.shape[0]

    def weight_copies(e, slot):
        return (pltpu.make_async_copy(w1_hbm.at[e], wf1_ref.at[slot], sems.at[0, slot]),
                pltpu.make_async_copy(w2_hbm.at[e], wf2_ref.at[slot], sems.at[1, slot]))

    @pl.when(i == 0)
    def _():
        for cp in weight_copies(be_ref[0], 0):
            cp.start()

    @pl.when((i < nv_ref[0]) & (first_ref[i] != 0))
    def _():
        slot = slot_ref[i]
        for cp in weight_copies(be_ref[i], slot):
            cp.wait()

        @pl.when(next_ref[i] >= 0)
        def _():
            for cp in weight_copies(next_ref[i], 1 - slot):
                cp.start()

        for gi in range(wf1_ref.shape[2] // gw):
            wb = wf1_ref[slot, :, gw * gi:gw * (gi + 1)].astype(BF16)
            w1b_ref[:, gw * gi:gw * (gi + 1)] = jnp.dot(wb, perm_ref[...], preferred_element_type=F32).astype(BF16)
        w2b_ref[...] = wf2_ref[slot].astype(BF16)

    @pl.when(i < nv_ref[0])
    def _():
        x = xs_ref[...].astype(BF16)
        u = jnp.dot(x, w1b_ref[...], preferred_element_type=F32) + b1_ref[0]
        acts = []
        for gi in range(u.shape[1] // (2 * LANES)):
            g = jnp.minimum(u[:, 2 * LANES * gi:2 * LANES * gi + LANES], SWIGLU_LIMIT)
            lin = jnp.clip(u[:, 2 * LANES * gi + LANES:2 * LANES * (gi + 1)], -SWIGLU_LIMIT, SWIGLU_LIMIT)
            acts.append((g * _sigmoid(SWIGLU_ALPHA * g) * (lin + 1.0)).astype(BF16))
        act = jnp.concatenate(acts, axis=1)
        ys_ref[...] = jnp.dot(act, w2b_ref[...], preferred_element_type=F32) + b2_ref[0]

    @pl.when(i >= nv_ref[0])
    def _():
        ys_ref[...] = jnp.zeros_like(ys_ref)


def _experts(block_e, n_valid, first, slot, next_e, xs, w1, b1, w2, b2, perm):
    rows = xs.shape[0]
    n_blocks = rows // MOE_BLOCK
    n_e, d, n1 = w1.shape
    dff = w2.shape[1]
    wmap = lambda i, be, *_: (be[i], 0, 0)
    rmap = lambda i, *_: (i, 0)
    return pl.pallas_call(
        _experts_kernel,
        grid_spec=pltpu.PrefetchScalarGridSpec(
            num_scalar_prefetch=5,
            grid=(n_blocks,),
            in_specs=[
                pl.BlockSpec((MOE_BLOCK, _D), rmap),
                pl.BlockSpec((1, 1, n1), wmap),
                pl.BlockSpec((1, 1, _D), wmap),
                pl.BlockSpec(perm.shape, lambda i, *_: (0, 0)),
                pl.BlockSpec(memory_space=pl.ANY),
                pl.BlockSpec(memory_space=pl.ANY),
            ],
            out_specs=pl.BlockSpec((MOE_BLOCK, _D), rmap),
            scratch_shapes=[pltpu.VMEM((2, d, n1), F32), pltpu.VMEM((2, dff, _D), F32),
                            pltpu.VMEM((d, n1), BF16), pltpu.VMEM((dff, _D), BF16),
                            pltpu.SemaphoreType.DMA((2, 2))],
        ),
        out_shape=jax.ShapeDtypeStruct((rows, _D), F32),
        compiler_params=_cparams(("arbitrary",)),
        name="experts",
    )(block_e, n_valid, first, slot, next_e, xs, b1, b2, perm, w1, w2)


def _final_kernel(tcur_ref, tnext_ref, x1_ref, gt_ref, pos_ref, gf_ref, ys_hbm, o_ref, st0_ref, st1_ref, sems):
    n_tok = x1_ref.shape[0]
    n_stage = st0_ref.shape[0]
    step = pl.program_id(0)
    last = pl.num_programs(0) - 1
    stages = (st0_ref, st1_ref)
    max_units = n_tok // SUBLANES + 1
    pieces = [1 << b for b in reversed(range(max_units.bit_length()))]
    fill_pieces = [1 << b for b in reversed(range((n_stage // SUBLANES).bit_length()))]

    def fetch_covers(tab_ref, slot):
        total = 0
        for e in range(N_EXPERTS):
            src = tab_ref[0, 0, e]
            units = tab_ref[0, 0, N_EXPERTS + e]
            dst = tab_ref[0, 0, 2 * N_EXPERTS + e]
            total = total + units
            for p in pieces:
                @pl.when((units & p) != 0)
                def _(p=p):
                    done = (units & ~(2 * p - 1)) * SUBLANES
                    pltpu.make_async_copy(
                        ys_hbm.at[pl.ds(pl.multiple_of(src + done, SUBLANES), p * SUBLANES)],
                        stages[slot].at[pl.ds(pl.multiple_of(dst + done, SUBLANES), p * SUBLANES)],
                        sems.at[slot]).start()
        spare = n_stage // SUBLANES - total
        for p in fill_pieces:
            @pl.when((spare & p) != 0)
            def _(p=p):
                done = (spare & ~(2 * p - 1)) * SUBLANES
                pltpu.make_async_copy(
                    ys_hbm.at[pl.ds(pl.multiple_of(done, SUBLANES), p * SUBLANES)],
                    stages[slot].at[pl.ds(pl.multiple_of(total * SUBLANES + done, SUBLANES), p * SUBLANES)],
                    sems.at[slot]).start()

    def drain(slot):
        for c in range(n_stage // FINAL_KC):
            pltpu.make_async_copy(ys_hbm.at[pl.ds(0, FINAL_KC)],
                                  stages[slot].at[pl.ds(FINAL_KC * c, FINAL_KC)], sems.at[slot]).wait()

    def combine(slot):
        gates = gt_ref[...]
        pos = pos_ref[...]
        y = x1_ref[...]
        for c in range(n_stage // FINAL_KC):
            col = lax.broadcasted_iota(I32, (n_tok, FINAL_KC), 1) + c * FINAL_KC
            sel = jnp.zeros((n_tok, FINAL_KC), F32)
            for k in range(TOP_K):
                sel = jnp.where(col == pos[:, k:k + 1], gates[:, k:k + 1], sel)
            rows = stages[slot][FINAL_KC * c:FINAL_KC * (c + 1), :].astype(BF16)
            y = y + jnp.dot(sel.astype(BF16), rows, preferred_element_type=F32)
        o_ref[...] = _rms(y, gf_ref[...])

    @pl.when(step == 0)
    def _():
        st0_ref[...] = jnp.zeros_like(st0_ref)
        st1_ref[...] = jnp.zeros_like(st1_ref)
        fetch_covers(tcur_ref, 0)

    for slot in (0, 1):
        @pl.when(step % 2 == slot)
        def _(slot=slot):
            drain(slot)

            @pl.when(step < last)
            def _():
                fetch_covers(tnext_ref, 1 - slot)

            combine(slot)


def _final(run_table, x1, gates_tk, pos_tk, g_final, ys):
    t = x1.shape[0]
    tm = FINAL_TM
    steps = t // tm
    row = lambda i: (i, 0)
    n_stage = -(-(TOP_K * tm + 2 * (SUBLANES - 1) * N_EXPERTS) // FINAL_KC) * FINAL_KC
    return pl.pallas_call(
        _final_kernel,
        grid=(steps,),
        in_specs=[pl.BlockSpec((1, 1, LANES), lambda i: (i, 0, 0), memory_space=pltpu.SMEM),
                  pl.BlockSpec((1, 1, LANES), lambda i: (jnp.minimum(i + 1, steps - 1), 0, 0),
                               memory_space=pltpu.SMEM),
                  pl.BlockSpec((tm, _D), row), pl.BlockSpec((tm, TOP_K), row), pl.BlockSpec((tm, TOP_K), row),
                  pl.BlockSpec((1, _D), lambda i: (0, 0)), pl.BlockSpec(memory_space=pl.ANY)],
        out_specs=pl.BlockSpec((tm, _D), row),
        out_shape=jax.ShapeDtypeStruct((t, _D), F32),
        scratch_shapes=[pltpu.VMEM((n_stage, _D), F32), pltpu.VMEM((n_stage, _D), F32),
                        pltpu.SemaphoreType.DMA((2,))],
        compiler_params=_cparams(("arbitrary",)),
        name="final",
    )(run_table, run_table, x1, gates_tk, pos_tk, g_final, ys)


def _prep_in_proj(w_in, b_in, w_gate, b_gate):
    sizes = (GLA_HEADS * GLA_DK, GLA_HEADS * GLA_DK, GLA_HEADS * GLA_DV, GLA_HEADS * GLA_DV, GLA_GATE_RANK,
             SWA_Q_HEADS * SWA_HEAD_DIM, SWA_KV_HEADS * SWA_HEAD_DIM, SWA_KV_HEADS * SWA_HEAD_DIM, _D, _D)
    offs = [0]
    for s in sizes:
        offs.append(offs[-1] + s)

    def rearrange(m, dtype):
        p = [m[..., offs[i]:offs[i + 1]].astype(dtype) for i in range(len(sizes))]
        gq, gk, gv, gr, lr, sq, sk, sv, ga, gb = p

        def dup_heads(a):
            hs = [a[..., SWA_HEAD_DIM * h:SWA_HEAD_DIM * (h + 1)] for h in range(SWA_KV_HEADS)]
            return jnp.concatenate([hh for h in hs for hh in (h, h)], axis=-1)

        lr_pad = jnp.pad(lr, [(0, 0)] * (lr.ndim - 1) + [(0, LANES - GLA_GATE_RANK)])
        return jnp.concatenate([gq, gk, gv, gr, sq, ga, gb, dup_heads(sk), dup_heads(sv), lr_pad], axis=-1)

    w_all = rearrange(w_in, BF16)
    b_all = rearrange(b_in[None, :], F32)
    wg = jnp.pad(w_gate, ((0, LANES - GLA_GATE_RANK), (0, 0)))
    return w_all, b_all, wg, b_gate[None, :]


def _rope_tables(seq):
    half = SWA_HEAD_DIM // 2
    inv_freq = ROPE_THETA ** (-jnp.arange(half, dtype=F32) / half)
    ang = jnp.arange(seq, dtype=F32)[:, None] * inv_freq[None, :]
    cos, sin = jnp.cos(ang), jnp.sin(ang)
    cos_t = jnp.concatenate([cos, cos] * (LANES // SWA_HEAD_DIM), axis=1)
    sin_t = jnp.concatenate([-sin, sin] * (LANES // SWA_HEAD_DIM), axis=1)
    return cos_t, sin_t


def _pair_split_perm():
    src = jnp.arange(2 * LANES, dtype=I32)
    dst = jnp.where(src % 2 == 0, src // 2, LANES + src // 2)
    return (dst[:, None] == jnp.arange(2 * LANES, dtype=I32)[None, :]).astype(BF16)


def kernel(x, g_mix, w_in, b_in, w_gla_gate, b_gla_gate, g_gla_head, w_gla_out, sinks, w_swa_out, w_out,
           g_ffn, w_router, b_router, w_e1, b_e1, w_e2, b_e2, g_final):
    bsz, seq, d = x.shape
    assert d == _D and w_in.shape[0] == 1, "single-layer, d_model=1024 only"
    assert seq % max(PROJ_TM, SWA_WINDOW, GLA_CHUNK) == 0
    t = bsz * seq
    assert t % max(MERGE_TM, ROUTE_TT, FINAL_TM, DISPATCH_TM) == 0
    assert FINAL_TM == ROUTE_TT == DISPATCH_TM and t // ROUTE_TT <= LANES and 4 * N_EXPERTS == LANES
    x2 = x.reshape(t, d)

    w_all, b_all, wg, bg = _prep_in_proj(w_in[0], b_in[0], w_gla_gate[0], b_gla_gate[0])
    cos_t, sin_t = _rope_tables(seq)
    gq, gk, gv, gr, sq, ga, gb, sk, sv, lg = _in_proj(x2, g_mix, w_all, b_all, cos_t, sin_t, wg, bg, seq)
    oa = _gla(gq, gk, gv, lg, gr, g_gla_head, bsz, seq)
    ob = _swa(sinks[0], sq, sk, sv, bsz, seq)
    x1, hn, logits_t = _merge(x2, oa, ob, ga, gb, w_gla_out[0].astype(BF16), w_swa_out[0].astype(BF16),
                              w_out[0].astype(BF16), g_ffn, w_router[0].T, b_router[0][:, None])

    e_kt, g_kt, r_kt, cum = _route(logits_t)
    n_tiles = t // ROUTE_TT
    cum = cum[:, :n_tiles].astype(I32)
    counts = cum[:, -1]
    blocks_e = (counts + MOE_BLOCK - 1) // MOE_BLOCK
    bend = jnp.cumsum(blocks_e)
    pstart = (bend - blocks_e) * MOE_BLOCK
    n_blocks = (t * TOP_K) // MOE_BLOCK + N_EXPERTS
    block_e = jnp.minimum(jnp.sum(bend[None, :] <= jnp.arange(n_blocks, dtype=I32)[:, None], axis=1),
                          N_EXPERTS - 1).astype(I32)
    n_valid = bend[-1:].astype(I32)

    before = jnp.concatenate([jnp.zeros((N_EXPERTS, 1), I32), cum[:, :-1]], axis=1)
    run_len = cum - before
    run_src = pstart[:, None] + before
    cov_src = run_src // SUBLANES * SUBLANES
    cov_len = jnp.where(run_len > 0, (run_src + run_len + SUBLANES - 1) // SUBLANES * SUBLANES - cov_src, 0)
    cov_dst = jnp.cumsum(cov_len, axis=0) - cov_len
    run_table = jnp.concatenate([cov_src.T, cov_len.T // SUBLANES, cov_dst.T, (run_src - cov_src).T], axis=1)
    run_table = run_table.reshape(n_tiles, 1, LANES)
    shift = jnp.repeat(cov_dst + run_src - cov_src - before, ROUTE_TT, axis=1)
    pos_kt = r_kt
    for e in range(N_EXPERTS):
        pos_kt = pos_kt + jnp.where(e_kt == e, shift[e][None, :], 0)

    b1 = b_e1[0].reshape(N_EXPERTS, -1, LANES, 2).transpose(0, 1, 3, 2).reshape(N_EXPERTS, 1, -1)
    xs = _dispatch(bend * MOE_BLOCK, blocks_e, n_valid, run_table, pos_kt, hn, n_blocks * MOE_BLOCK)
    first = jnp.concatenate([jnp.ones((1,), I32), (block_e[1:] != block_e[:-1]).astype(I32)])
    slot = (jnp.cumsum(first) - 1) % 2
    ids = jnp.arange(N_EXPERTS, dtype=I32)
    later = (ids[None, :] > ids[:, None]) & (blocks_e[None, :] > 0)
    next_of = jnp.min(jnp.where(later, ids[None, :], N_EXPERTS), axis=1)
    next_of = jnp.where(next_of < N_EXPERTS, next_of, -1)
    next_e = jnp.sum(jnp.where(block_e[:, None] == ids[None, :], next_of[None, :], 0), axis=1).astype(I32)
    ys = _experts(block_e, n_valid, first, slot.astype(I32), next_e, xs, w_e1[0], b1, w_e2[0],
                  b_e2[0][:, None, :], _pair_split_perm())
    out = _final(run_table, x1, g_kt.T, pos_kt.T, g_final[None, :], ys)
    return out.reshape(bsz, seq, d)
```

```python
import jax
import jax.numpy as jnp
import numpy as np
from jax import lax
from jax.experimental import pallas as pl
from jax.experimental.pallas import tpu as pltpu

F32 = jnp.float32
BF16 = jnp.bfloat16
I32 = jnp.int32

NORM_EPS = 1e-5
GLA_HEADS = 4
GLA_DK = 128
GLA_DV = 256
GLA_GATE_RANK = 16
GLA_TAU = 16.0
SWA_Q_HEADS = 16
SWA_KV_HEADS = 2
SWA_GROUP = SWA_Q_HEADS // SWA_KV_HEADS
SWA_HEAD_DIM = 64
SWA_WINDOW = 128
ROPE_THETA = 10000.0
N_EXPERTS = 32
TOP_K = 4
SWIGLU_LIMIT = 7.0
SWIGLU_ALPHA = 1.702

LANES = 128
SUBLANES = 8
NEG = -1e30
VMEM_LIMIT = 56 * 1024 * 1024

PROJ_TM = 256
GLA_CHUNK = 128
GLA_HEADS_PER_STEP = 4
MERGE_TM = 512
MERGE_TN = 256
MERGE_ROW_GROUPS = 1
ROUTE_TT = 512
MOE_BLOCK = 512
DISPATCH_TM = 512
DISPATCH_KC = 512
FINAL_TM = 512
FINAL_KC = 512

_D = 1024
_C_GQ = 0
_C_GK = _C_GQ + GLA_HEADS * GLA_DK
_C_GV = _C_GK + GLA_HEADS * GLA_DK
_C_GR = _C_GV + GLA_HEADS * GLA_DV
_C_LR = _C_GR + GLA_HEADS * GLA_DV
_W_SQ = SWA_Q_HEADS * SWA_HEAD_DIM
_W_KV = SWA_KV_HEADS * SWA_HEAD_DIM
_N_IN = _C_LR + GLA_GATE_RANK + _W_SQ + 2 * _W_KV + 2 * _D
_C_END = -(-_N_IN // LANES) * LANES


def _cparams(sem):
    return pltpu.CompilerParams(dimension_semantics=sem, vmem_limit_bytes=VMEM_LIMIT)


def _rms(x, g):
    return x * lax.rsqrt(jnp.mean(x * x, axis=-1, keepdims=True) + NORM_EPS) * g


def _sigmoid(x):
    return 1.0 / (1.0 + jnp.exp(-x))


def _dot_nt(a, b):
    return lax.dot_general(a, b, (((1,), (1,)), ((), ())), preferred_element_type=F32)


def _rope_slabs(acc, cos, sin, first_half):
    outs = []
    for i in range(acc.shape[1] // LANES):
        xs = acc[:, LANES * i:LANES * (i + 1)]
        partner = jnp.where(first_half, pltpu.roll(xs, LANES - 32, 1), pltpu.roll(xs, 32, 1))
        outs.append(xs * cos + partner * sin)
    return jnp.concatenate(outs, axis=1)


def _in_proj_kernel(x_ref, g_ref, w_ref, b_ref, cos_ref, sin_ref, wg_ref, bg_ref,
                    gq_ref, gk_ref, gv_ref, gr_ref, sq_ref, ga_ref, gb_ref, sk_ref, sv_ref, lg_ref):
    tm = x_ref.shape[0]
    h = _rms(x_ref[...], g_ref[...]).astype(BF16)

    def proj(lo, hi):
        return jnp.dot(h, w_ref[:, lo:hi], preferred_element_type=F32) + b_ref[:, lo:hi]

    cos = cos_ref[...]
    sin = sin_ref[...]
    lane = lax.broadcasted_iota(I32, (tm, LANES), 1)
    first_half = (lane % SWA_HEAD_DIM) < (SWA_HEAD_DIM // 2)

    gq_ref[...] = (proj(_C_GQ, _C_GK) * (GLA_DK ** -0.5)).astype(BF16)
    gk_ref[...] = proj(_C_GK, _C_GV).astype(BF16)
    gv_ref[...] = proj(_C_GV, _C_GR).astype(BF16)
    gr = proj(_C_GR, _C_LR)
    gr_ref[...] = (gr * _sigmoid(gr)).astype(BF16)

    slab = proj(_C_LR, _C_END)
    n_tiles = slab.shape[1] // LANES
    turned = [pltpu.roll(slab[:, LANES * i:LANES * (i + 1)], LANES - GLA_GATE_RANK, 1) for i in range(n_tiles)]
    keep = lane < LANES - GLA_GATE_RANK
    rest = jnp.concatenate([jnp.where(keep, turned[i], turned[i + 1]) for i in range(n_tiles - 1)], axis=1)

    sq = rest[:, 0:_W_SQ] * (SWA_HEAD_DIM ** -0.5)
    sq_ref[...] = _rope_slabs(sq, cos, sin, first_half).astype(BF16)
    low = lane < SWA_HEAD_DIM
    for part, ref, roped in ((rest[:, _W_SQ:_W_SQ + _W_KV], sk_ref, True),
                             (rest[:, _W_SQ + _W_KV:_W_SQ + 2 * _W_KV], sv_ref, False)):
        if roped:
            part = _rope_slabs(part, cos, sin, first_half)
        other = pltpu.roll(part, SWA_HEAD_DIM, 1)
        ref[...] = jnp.concatenate([jnp.where(low, part, other), jnp.where(low, other, part)], axis=1).astype(BF16)
    c_ga = _W_SQ + 2 * _W_KV
    ga_ref[...] = _sigmoid(rest[:, c_ga:c_ga + _D]).astype(BF16)
    gb_ref[...] = _sigmoid(rest[:, c_ga + _D:c_ga + 2 * _D]).astype(BF16)

    low_rank = jnp.where(lane < GLA_GATE_RANK, slab[:, 0:LANES], 0.0)
    z = jnp.dot(low_rank, wg_ref[...], precision=lax.Precision.HIGHEST,
                preferred_element_type=F32) + bg_ref[...]
    log_sig = jnp.minimum(z, 0.0) - jnp.log(1.0 + jnp.exp(-jnp.abs(z)))
    lg_ref[...] = log_sig * (1.0 / GLA_TAU)


def _in_proj(x2, g_mix, w_all, b_all, cos_t, sin_t, wg, bg, seq):
    t = x2.shape[0]
    tm = PROJ_TM
    pos_blocks = seq // tm
    const = lambda i: (0, 0)
    row = lambda i: (i, 0)
    widths = [(_C_GK - _C_GQ, BF16), (_C_GV - _C_GK, BF16), (_C_GR - _C_GV, BF16), (_C_LR - _C_GR, BF16),
              (_W_SQ, BF16), (_D, BF16), (_D, BF16), (SWA_KV_HEADS * LANES, BF16), (SWA_KV_HEADS * LANES, BF16),
              (GLA_HEADS * GLA_DK, F32)]
    return pl.pallas_call(
        _in_proj_kernel,
        grid=(t // tm,),
        in_specs=[
            pl.BlockSpec((tm, _D), row),
            pl.BlockSpec((1, _D), const),
            pl.BlockSpec((_D, _C_END), const, pipeline_mode=pl.Buffered(1)),
            pl.BlockSpec((1, _C_END), const),
            pl.BlockSpec((tm, LANES), lambda i: (i % pos_blocks, 0)),
            pl.BlockSpec((tm, LANES), lambda i: (i % pos_blocks, 0)),
            pl.BlockSpec((LANES, GLA_HEADS * GLA_DK), const),
            pl.BlockSpec((1, GLA_HEADS * GLA_DK), const),
        ],
        out_specs=[pl.BlockSpec((tm, w), row) for w, _ in widths],
        out_shape=[jax.ShapeDtypeStruct((t, w), dt) for w, dt in widths],
        compiler_params=_cparams(("parallel",)),
        name="in_proj",
    )(x2, g_mix, w_all, b_all, cos_t, sin_t, wg, bg)


def _gla_cumsum_operator(c_len):
    t = np.arange(c_len)[:, None]
    r = np.arange(c_len)[None, :]
    return np.tile((r <= t).astype(np.float32), (1, 3))


def _gla_kernel(q_ref, k_ref, v_ref, lg_ref, gr_ref, gh_ref, dm_ref, o_ref, st_ref):
    c_len = GLA_CHUNK
    n_lev = c_len.bit_length() - 1
    seq = q_ref.shape[0]
    st_ref[...] = jnp.zeros_like(st_ref)

    t_i = lax.broadcasted_iota(I32, (c_len, c_len), 0)
    j_i = lax.broadcasted_iota(I32, (c_len, c_len), 1)
    row = lax.broadcasted_iota(I32, (c_len, 1), 0)
    diag = t_i == j_i
    upper, pair = [], []
    for lev in range(n_lev):
        s = c_len >> (lev + 1)
        upper.append((row & s) != 0)
        pair.append(((t_i // (2 * s)) == (j_i // (2 * s))) & ((t_i & s) != 0) & ((j_i & s) == 0))

    sub8 = lax.broadcasted_iota(I32, (c_len // 8, 8, GLA_DK), 1)

    def boundary_rows(b, s):
        if s >= 4:
            b3 = b.reshape(c_len // (2 * s), 2 * s, GLA_DK)
            return jnp.broadcast_to(b3[:, s - 1:s, :], b3.shape).reshape(c_len, GLA_DK)
        b3 = b.reshape(c_len // 8, 8, GLA_DK)
        lo = jnp.broadcast_to(b3[:, 1:2, :], b3.shape)
        hi = jnp.broadcast_to(b3[:, 5:6, :], b3.shape)
        return jnp.where(sub8 < 4, lo, hi).reshape(c_len, GLA_DK)

    def head_chunk(r0, hh):
        kcols = slice(GLA_DK * hh, GLA_DK * (hh + 1))
        vcols = slice(GLA_DV * hh, GLA_DV * (hh + 1))
        q_bf = q_ref[pl.ds(r0, c_len), kcols]
        k_bf = k_ref[pl.ds(r0, c_len), kcols]
        q = q_bf.astype(F32)
        k = k_bf.astype(F32)
        v = v_ref[pl.ds(r0, c_len), vcols]

        lg = lg_ref[pl.ds(r0, c_len), kcols]
        lg_hi = lg.astype(BF16)
        rem = lg - lg_hi.astype(F32)
        lg_mid = rem.astype(BF16)
        lg_lo = (rem - lg_mid.astype(F32)).astype(BF16)
        b = jnp.dot(dm_ref[...], jnp.concatenate([lg_hi, lg_mid, lg_lo], axis=0),
                    preferred_element_type=F32)
        w_cum = jnp.exp(b)

        st = st_ref[hh]
        o = _dot_nt((q * w_cum).astype(BF16), st.astype(BF16))

        a = jnp.where(diag, _dot_nt(q_bf, k_bf), 0.0)
        for lev in range(n_lev):
            s = c_len >> (lev + 1)
            if s == 1:
                w = jnp.where(upper[lev], jnp.exp(lg), 1.0)
            else:
                w = jnp.exp(-jnp.abs(b - boundary_rows(b, s)))
            z = (jnp.where(upper[lev], q, k) * w).astype(BF16)
            a = jnp.where(pair[lev], _dot_nt(z, z), a)
        o = o + jnp.dot(a.astype(BF16), v, preferred_element_type=F32)

        b_last = b[c_len - 1:c_len, :]
        upd = lax.dot_general(v, (k * jnp.exp(b_last - b)).astype(BF16), (((0,), (0,)), ((), ())),
                              preferred_element_type=F32)
        st_ref[hh] = st * w_cum[c_len - 1:c_len, :] + upd

        on = _rms(o, gh_ref[...])
        o_ref[pl.ds(r0, c_len), vcols] = (on * gr_ref[pl.ds(r0, c_len), vcols].astype(F32)).astype(BF16)

    def chunk(c, carry):
        r0 = pl.multiple_of(c * c_len, c_len)
        for hh in range(GLA_HEADS_PER_STEP):
            head_chunk(r0, hh)
        return carry

    lax.fori_loop(0, seq // c_len, chunk, 0, unroll=8)


def _gla(gq, gk, gv, lg, gr, g_head, bsz, seq):
    t = gq.shape[0]
    hs = GLA_HEADS_PER_STEP
    dmat = jnp.asarray(_gla_cumsum_operator(GLA_CHUNK), dtype=BF16)
    return pl.pallas_call(
        _gla_kernel,
        grid=(bsz, GLA_HEADS // hs),
        in_specs=[
            pl.BlockSpec((seq, hs * GLA_DK), lambda b, h: (b, h)),
            pl.BlockSpec((seq, hs * GLA_DK), lambda b, h: (b, h)),
            pl.BlockSpec((seq, hs * GLA_DV), lambda b, h: (b, h)),
            pl.BlockSpec((seq, hs * GLA_DK), lambda b, h: (b, h)),
            pl.BlockSpec((seq, hs * GLA_DV), lambda b, h: (b, h)),
            pl.BlockSpec((1, GLA_DV), lambda b, h: (0, 0)),
            pl.BlockSpec(dmat.shape, lambda b, h: (0, 0)),
        ],
        out_specs=pl.BlockSpec((seq, hs * GLA_DV), lambda b, h: (b, h)),
        out_shape=jax.ShapeDtypeStruct((t, GLA_HEADS * GLA_DV), BF16),
        scratch_shapes=[pltpu.VMEM((hs, GLA_DV, GLA_DK), F32)],
        compiler_params=_cparams(("parallel", "parallel")),
        name="gla",
    )(gq, gk, gv, lg, gr, g_head, dmat)


def _swa_kernel(sink_ref, q_ref, k_ref, v_ref, o_ref):
    w = SWA_WINDOW
    seq = q_ref.shape[0]
    hk = pl.program_id(1)
    lane_q = lax.broadcasted_iota(I32, (w, LANES), 1)
    low_q = lane_q < SWA_HEAD_DIM
    lane_b = lax.broadcasted_iota(I32, (2 * w, LANES), 1)
    low_b = lane_b < SWA_HEAD_DIM
    qi = lax.broadcasted_iota(I32, (w, 2 * w), 0)
    kj = lax.broadcasted_iota(I32, (w, 2 * w), 1)
    in_window = (kj > qi) & (kj <= qi + w)
    zero_q = jnp.zeros((w, LANES), BF16)
    zero_b = jnp.zeros((2 * w, LANES), BF16)

    def block(n, carry):
        r0 = pl.multiple_of(n * w, w)
        p0 = pl.multiple_of(jnp.maximum(n - 1, 0) * w, w)
        kb = jnp.concatenate([k_ref[pl.ds(p0, w), :], k_ref[pl.ds(r0, w), :]], axis=0)
        vb = jnp.concatenate([v_ref[pl.ds(p0, w), :], v_ref[pl.ds(r0, w), :]], axis=0)
        valid = in_window & ((kj >= w) | (n > 0))
        v_lo = jnp.where(low_b, vb, zero_b)
        v_hi = jnp.where(low_b, zero_b, vb)
        for m in range(SWA_GROUP // 2):
            qp = q_ref[pl.ds(r0, w), LANES * m:LANES * (m + 1)]
            acc = jnp.zeros((w, LANES), F32)
            for par in range(2):
                qm = jnp.where(low_q, qp, zero_q) if par == 0 else jnp.where(low_q, zero_q, qp)
                s = jnp.where(valid, _dot_nt(qm, kb), NEG)
                sink = sink_ref[hk * SWA_GROUP + 2 * m + par]
                mx = jnp.maximum(jnp.max(s, axis=-1, keepdims=True), sink)
                p = jnp.exp(s - mx)
                den = jnp.sum(p, axis=-1, keepdims=True) + jnp.exp(sink - mx)
                pv = jnp.dot(p.astype(BF16), v_lo if par == 0 else v_hi, preferred_element_type=F32)
                acc = acc + pv / den
            o_ref[pl.ds(r0, w), LANES * m:LANES * (m + 1)] = acc.astype(BF16)
        return carry

    lax.fori_loop(0, seq // w, block, 0, unroll=16)


def _swa(sinks, sq, sk, sv, bsz, seq):
    t = sq.shape[0]
    gw = SWA_GROUP * SWA_HEAD_DIM
    return pl.pallas_call(
        _swa_kernel,
        grid_spec=pltpu.PrefetchScalarGridSpec(
            num_scalar_prefetch=1,
            grid=(bsz, SWA_KV_HEADS),
            in_specs=[
                pl.BlockSpec((seq, gw), lambda b, h, s: (b, h)),
                pl.BlockSpec((seq, LANES), lambda b, h, s: (b, h)),
                pl.BlockSpec((seq, LANES), lambda b, h, s: (b, h)),
            ],
            out_specs=pl.BlockSpec((seq, gw), lambda b, h, s: (b, h)),
        ),
        out_shape=jax.ShapeDtypeStruct((t, SWA_Q_HEADS * SWA_HEAD_DIM), BF16),
        compiler_params=_cparams(("parallel", "parallel")),
        name="swa",
    )(sinks, sq, sk, sv)


def _merge_kernel(x_ref, oa_ref, ob_ref, ga_ref, gb_ref, wa_ref, wb_ref, wo_ref, gf_ref, wr_ref, br_ref,
                  x1_ref, hn_ref, lt_ref, mixed_ref):
    rows_g = x_ref.shape[0] // MERGE_ROW_GROUPS
    for h in range(MERGE_ROW_GROUPS):
        rows = slice(rows_g * h, rows_g * (h + 1))
        oa = oa_ref[rows, :]
        ob = ob_ref[rows, :]
        for n in range(_D // MERGE_TN):
            cols = slice(MERGE_TN * n, MERGE_TN * (n + 1))
            ya = ga_ref[rows, cols].astype(F32) * jnp.dot(oa, wa_ref[:, cols], preferred_element_type=F32)
            yb = gb_ref[rows, cols].astype(F32) * jnp.dot(ob, wb_ref[:, cols], preferred_element_type=F32)
            mixed_ref[rows, cols] = (ya + yb).astype(BF16)
        mixed = mixed_ref[rows, :]
        for n in range(_D // MERGE_TN):
            cols = slice(MERGE_TN * n, MERGE_TN * (n + 1))
            x1_ref[rows, cols] = x_ref[rows, cols] + jnp.dot(mixed, wo_ref[:, cols], preferred_element_type=F32)
        hn = _rms(x1_ref[rows, :], gf_ref[...])
        hn_ref[rows, :] = hn
        lt_ref[:, rows] = lax.dot_general(wr_ref[...], hn, (((1,), (1,)), ((), ())),
                                          precision=lax.Precision.HIGHEST,
                                          preferred_element_type=F32) + br_ref[...]


def _merge(x2, oa, ob, ga, gb, wa, wb, wo, g_ffn, wr_t, br_col):
    t = x2.shape[0]
    tm = MERGE_TM
    row = lambda i: (i, 0)
    const = lambda i: (0, 0)
    return pl.pallas_call(
        _merge_kernel,
        grid=(t // tm,),
        in_specs=[pl.BlockSpec((tm, _D), row)] * 5 + [pl.BlockSpec((_D, _D), const)] * 3 + [
            pl.BlockSpec((1, _D), const),
            pl.BlockSpec((N_EXPERTS, _D), const),
            pl.BlockSpec((N_EXPERTS, 1), const),
        ],
        out_specs=[pl.BlockSpec((tm, _D), row), pl.BlockSpec((tm, _D), row),
                   pl.BlockSpec((N_EXPERTS, tm), lambda i: (0, i))],
        out_shape=[jax.ShapeDtypeStruct((t, _D), F32), jax.ShapeDtypeStruct((t, _D), F32),
                   jax.ShapeDtypeStruct((N_EXPERTS, t), F32)],
        scratch_shapes=[pltpu.VMEM((tm, _D), BF16)],
        compiler_params=_cparams(("parallel",)),
        name="merge",
    )(x2, oa, ob, ga, gb, wa, wb, wo, g_ffn, wr_t, br_col)


def _route_kernel(lt_ref, e_ref, g_ref, r_ref, cum_ref, carry_ref):
    tt = lt_ref.shape[1]

    @pl.when(pl.program_id(0) == 0)
    def _():
        carry_ref[...] = jnp.zeros_like(carry_ref)
        cum_ref[...] = jnp.zeros_like(cum_ref)

    eid = lax.broadcasted_iota(I32, (N_EXPERTS, tt), 0)
    work = lt_ref[...]
    vals, idxs = [], []
    chosen = jnp.zeros((N_EXPERTS, tt), F32)
    for _ in range(TOP_K):
        m = jnp.max(work, axis=0, keepdims=True)
        idx = jnp.min(jnp.where(work == m, eid, N_EXPERTS), axis=0, keepdims=True)
        hit = eid == idx
        work = jnp.where(hit, -jnp.inf, work)
        chosen = jnp.where(hit, 1.0, chosen)
        vals.append(m)
        idxs.append(idx)
    ex = [jnp.exp(v - vals[0]) for v in vals]
    den = ex[0] + ex[1] + ex[2] + ex[3]

    t_r = lax.broadcasted_iota(I32, (tt, tt), 0)
    t_c = lax.broadcasted_iota(I32, (tt, tt), 1)
    before = (t_r < t_c).astype(BF16)
    pref = jnp.dot(chosen.astype(BF16), before, preferred_element_type=F32) + carry_ref[:, 0:1]
    for k in range(TOP_K):
        e_ref[k:k + 1, :] = idxs[k]
        g_ref[k:k + 1, :] = ex[k] / den
        r_ref[k:k + 1, :] = jnp.sum(jnp.where(eid == idxs[k], pref, 0.0), axis=0, keepdims=True).astype(I32)
    total = pref[:, tt - 1:tt] + chosen[:, tt - 1:tt]
    carry_ref[...] = jnp.broadcast_to(total, carry_ref.shape)
    tile_lane = lax.broadcasted_iota(I32, cum_ref.shape, 1) == pl.program_id(0)
    cum_ref[...] = jnp.where(tile_lane, total, cum_ref[...])


def _route(logits_t):
    t = logits_t.shape[1]
    tt = ROUTE_TT
    blk = lambda i: (0, i)
    return pl.pallas_call(
        _route_kernel,
        grid=(t // tt,),
        in_specs=[pl.BlockSpec((N_EXPERTS, tt), blk)],
        out_specs=[pl.BlockSpec((TOP_K, tt), blk), pl.BlockSpec((TOP_K, tt), blk), pl.BlockSpec((TOP_K, tt), blk),
                   pl.BlockSpec((N_EXPERTS, LANES), lambda i: (0, 0))],
        out_shape=[jax.ShapeDtypeStruct((TOP_K, t), I32), jax.ShapeDtypeStruct((TOP_K, t), F32),
                   jax.ShapeDtypeStruct((TOP_K, t), I32), jax.ShapeDtypeStruct((N_EXPERTS, LANES), F32)],
        scratch_shapes=[pltpu.VMEM((N_EXPERTS, LANES), F32)],
        compiler_params=_cparams(("arbitrary",)),
        name="route",
    )(logits_t)


def _dispatch_kernel(pend_ref, nblk_ref, nv_ref, tab_ref, hn_ref, pos_ref, xs_hbm, spill_hbm,
                     st0_ref, st1_ref, carry_ref, zero_ref, sems, zsem):
    n_tok = hn_ref.shape[0]
    n_stage = st0_ref.shape[0]
    n_blocks = xs_hbm.shape[0] // MOE_BLOCK
    step = pl.program_id(0)
    last = pl.num_programs(0) - 1
    stages = (st0_ref, st1_ref)
    max_units = n_tok // SUBLANES + 1
    pieces = [1 << b for b in reversed(range(max_units.bit_length()))]
    fill_pieces = [1 << b for b in reversed(range((n_stage // SUBLANES).bit_length()))]

    def zero_block(row0):
        return pltpu.make_async_copy(zero_ref, xs_hbm.at[pl.ds(pl.multiple_of(row0, MOE_BLOCK), MOE_BLOCK)], zsem)

    @pl.when(step == 0)
    def _():
        zero_ref[...] = jnp.zeros_like(zero_ref)
        carry_ref[...] = jnp.zeros_like(carry_ref)

        def expert_tail(e, carry, start):
            @pl.when(nblk_ref[e] > 0)
            def _():
                cp = zero_block(pend_ref[e] - MOE_BLOCK)
                cp.start() if start else cp.wait()
            return carry

        def unused_block(b, carry, start):
            cp = zero_block(b * MOE_BLOCK)
            cp.start() if start else cp.wait()
            return carry

        for start in (True, False):
            lax.fori_loop(0, N_EXPERTS, lambda e, c: expert_tail(e, c, start), 0)
            lax.fori_loop(nv_ref[0], n_blocks, lambda b, c: unused_block(b, c, start), 0)

    def permute(slot):
        hb = hn_ref[...].astype(BF16)
        pos = pos_ref[...]
        for c in range(n_stage // DISPATCH_KC):
            row = lax.broadcasted_iota(I32, (DISPATCH_KC, n_tok), 0) + c * DISPATCH_KC
            sel = jnp.zeros((DISPATCH_KC, n_tok), F32)
            for k in range(TOP_K):
                sel = jnp.where(row == pos[k:k + 1, :], 1.0, sel)
            stages[slot][DISPATCH_KC * c:DISPATCH_KC * (c + 1), :] = jnp.dot(
                sel.astype(BF16), hb, preferred_element_type=F32)
        sub = lax.broadcasted_iota(I32, (SUBLANES, hn_ref.shape[1]), 0)
        for e in range(N_EXPERTS):
            units = tab_ref[0, 0, N_EXPERTS + e]

            @pl.when(units > 0)
            def _(e=e, units=units):
                first = pl.multiple_of(tab_ref[0, 0, 2 * N_EXPERTS + e], SUBLANES)
                shared = tab_ref[0, 0, 3 * N_EXPERTS + e]
                head = stages[slot][pl.ds(first, SUBLANES), :]
                stages[slot][pl.ds(first, SUBLANES), :] = jnp.where(sub < shared, carry_ref[e], head)
                final = pl.multiple_of(first + (units - 1) * SUBLANES, SUBLANES)
                carry_ref[e] = stages[slot][pl.ds(final, SUBLANES), :]

    def write_covers(slot):
        total = 0
        for e in range(N_EXPERTS):
            dst = tab_ref[0, 0, e]
            units = tab_ref[0, 0, N_EXPERTS + e]
            src = tab_ref[0, 0, 2 * N_EXPERTS + e]
            total = total + units
            for p in pieces:
                @pl.when((units & p) != 0)
                def _(p=p):
                    done = (units & ~(2 * p - 1)) * SUBLANES
                    pltpu.make_async_copy(
                        stages[slot].at[pl.ds(pl.multiple_of(src + done, SUBLANES), p * SUBLANES)],
                        xs_hbm.at[pl.ds(pl.multiple_of(dst + done, SUBLANES), p * SUBLANES)],
                        sems.at[slot]).start()
        spare = n_stage // SUBLANES - total
        for p in fill_pieces:
            @pl.when((spare & p) != 0)
            def _(p=p):
                done = (spare & ~(2 * p - 1)) * SUBLANES
                pltpu.make_async_copy(
                    stages[slot].at[pl.ds(pl.multiple_of(total * SUBLANES + done, SUBLANES), p * SUBLANES)],
                    spill_hbm.at[pl.ds(pl.multiple_of(done, SUBLANES), p * SUBLANES)],
                    sems.at[slot]).start()

    def drain(slot):
        for c in range(n_stage // DISPATCH_KC):
            pltpu.make_async_copy(stages[slot].at[pl.ds(DISPATCH_KC * c, DISPATCH_KC)],
                                  spill_hbm.at[pl.ds(0, DISPATCH_KC)], sems.at[slot]).wait()

    for slot in (0, 1):
        @pl.when(step % 2 == slot)
        def _(slot=slot):
            permute(slot)

            @pl.when(step > 0)
            def _():
                drain(1 - slot)

            write_covers(slot)

            @pl.when(step == last)
            def _():
                drain(slot)


def _dispatch(pend_rows, blocks_e, n_valid, run_table, pos_kt, hn, rows):
    t = hn.shape[0]
    tm = DISPATCH_TM
    steps = t // tm
    n_stage = -(-(TOP_K * tm + 2 * (SUBLANES - 1) * N_EXPERTS) // DISPATCH_KC) * DISPATCH_KC
    return pl.pallas_call(
        _dispatch_kernel,
        grid_spec=pltpu.PrefetchScalarGridSpec(
            num_scalar_prefetch=3,
            grid=(steps,),
            in_specs=[pl.BlockSpec((1, 1, LANES), lambda i, *_: (i, 0, 0), memory_space=pltpu.SMEM),
                      pl.BlockSpec((tm, _D), lambda i, *_: (i, 0)),
                      pl.BlockSpec((TOP_K, tm), lambda i, *_: (0, i))],
            out_specs=[pl.BlockSpec(memory_space=pl.ANY), pl.BlockSpec(memory_space=pl.ANY)],
            scratch_shapes=[pltpu.VMEM((n_stage, _D), F32), pltpu.VMEM((n_stage, _D), F32),
                            pltpu.VMEM((N_EXPERTS, SUBLANES, _D), F32), pltpu.VMEM((MOE_BLOCK, _D), F32),
                            pltpu.SemaphoreType.DMA((2,)), pltpu.SemaphoreType.DMA],
        ),
        out_shape=[jax.ShapeDtypeStruct((rows, _D), F32), jax.ShapeDtypeStruct((n_stage, _D), F32)],
        compiler_params=_cparams(("arbitrary",)),
        name="dispatch",
    )(pend_rows, blocks_e, n_valid, run_table, hn, pos_kt)[0]


def _experts_kernel(be_ref, nv_ref, first_ref, slot_ref, next_ref, xs_ref, b1_ref, b2_ref, perm_ref,
                    w1_hbm, w2_hbm, ys_ref, wf1_ref, wf2_ref, w1b_ref, w2b_ref, sems):
    i = pl.program_id(0)
    gw = perm_ref.shape[0]

    def weight_copies(e, slot):
        return (pltpu.make_async_copy(w1_hbm.at[e], wf1_ref.at[slot], sems.at[0, slot]),
                pltpu.make_async_copy(w2_hbm.at[e], wf2_ref.at[slot], sems.at[1, slot]))

    @pl.when(i == 0)
    def _():
        for cp in weight_copies(be_ref[0], 0):
            cp.start()

    @pl.when((i < nv_ref[0]) & (first_ref[i] != 0))
    def _():
        slot = slot_ref[i]
        for cp in weight_copies(be_ref[i], slot):
            cp.wait()

        @pl.when(next_ref[i] >= 0)
        def _():
            for cp in weight_copies(next_ref[i], 1 - slot):
                cp.start()

        for gi in range(wf1_ref.shape[2] // gw):
            wb = wf1_ref[slot, :, gw * gi:gw * (gi + 1)].astype(BF16)
            w1b_ref[:, gw * gi:gw * (gi + 1)] = jnp.dot(wb, perm_ref[...], preferred_element_type=F32).astype(BF16)
        w2b_ref[...] = wf2_ref[slot].astype(BF16)

    @pl.when(i < nv_ref[0])
    def _():
        x = xs_ref[...].astype(BF16)
        u = jnp.dot(x, w1b_ref[...], preferred_element_type=F32) + b1_ref[0]
        acts = []
        for gi in range(u.shape[1] // (2 * LANES)):
            g = jnp.minimum(u[:, 2 * LANES * gi:2 * LANES * gi + LANES], SWIGLU_LIMIT)
            lin = jnp.clip(u[:, 2 * LANES * gi + LANES:2 * LANES * (gi + 1)], -SWIGLU_LIMIT, SWIGLU_LIMIT)
            acts.append((g * _sigmoid(SWIGLU_ALPHA * g) * (lin + 1.0)).astype(BF16))
        act = jnp.concatenate(acts, axis=1)
        ys_ref[...] = jnp.dot(act, w2b_ref[...], preferred_element_type=F32) + b2_ref[0]

    @pl.when(i >= nv_ref[0])
    def _():
        ys_ref[...] = jnp.zeros_like(ys_ref)


def _experts(block_e, n_valid, first, slot, next_e, xs, w1, b1, w2, b2, perm):
    rows = xs.shape[0]
    n_blocks = rows // MOE_BLOCK
    n_e, d, n1 = w1.shape
    dff = w2.shape[1]
    wmap = lambda i, be, *_: (be[i], 0, 0)
    rmap = lambda i, *_: (i, 0)
    return pl.pallas_call(
        _experts_kernel,
        grid_spec=pltpu.PrefetchScalarGridSpec(
            num_scalar_prefetch=5,
            grid=(n_blocks,),
            in_specs=[
                pl.BlockSpec((MOE_BLOCK, _D), rmap),
                pl.BlockSpec((1, 1, n1), wmap),
                pl.BlockSpec((1, 1, _D), wmap),
                pl.BlockSpec(perm.shape, lambda i, *_: (0, 0)),
                pl.BlockSpec(memory_space=pl.ANY),
                pl.BlockSpec(memory_space=pl.ANY),
            ],
            out_specs=pl.BlockSpec((MOE_BLOCK, _D), rmap),
            scratch_shapes=[pltpu.VMEM((2, d, n1), F32), pltpu.VMEM((2, dff, _D), F32),
                            pltpu.VMEM((d, n1), BF16), pltpu.VMEM((dff, _D), BF16),
                            pltpu.SemaphoreType.DMA((2, 2))],
        ),
        out_shape=jax.ShapeDtypeStruct((rows, _D), F32),
        compiler_params=_cparams(("arbitrary",)),
        name="experts",
    )(block_e, n_valid, first, slot, next_e, xs, b1, b2, perm, w1, w2)


def _final_kernel(tcur_ref, tnext_ref, x1_ref, gt_ref, pos_ref, gf_ref, ys_hbm, o_ref, st0_ref, st1_ref, sems):
    n_tok = x1_ref.shape[0]
    n_stage = st0_ref.shape[0]
    step = pl.program_id(0)
    last = pl.num_programs(0) - 1
    stages = (st0_ref, st1_ref)
    max_units = n_tok // SUBLANES + 1
    pieces = [1 << b for b in reversed(range(max_units.bit_length()))]
    fill_pieces = [1 << b for b in reversed(range((n_stage // SUBLANES).bit_length()))]

    def fetch_covers(tab_ref, slot):
        total = 0
        for e in range(N_EXPERTS):
            src = tab_ref[0, 0, e]
            units = tab_ref[0, 0, N_EXPERTS + e]
            dst = tab_ref[0, 0, 2 * N_EXPERTS + e]
            total = total + units
            for p in pieces:
                @pl.when((units & p) != 0)
                def _(p=p):
                    done = (units & ~(2 * p - 1)) * SUBLANES
                    pltpu.make_async_copy(
                        ys_hbm.at[pl.ds(pl.multiple_of(src + done, SUBLANES), p * SUBLANES)],
                        stages[slot].at[pl.ds(pl.multiple_of(dst + done, SUBLANES), p * SUBLANES)],
                        sems.at[slot]).start()
        spare = n_stage // SUBLANES - total
        for p in fill_pieces:
            @pl.when((spare & p) != 0)
            def _(p=p):
                done = (spare & ~(2 * p - 1)) * SUBLANES
                pltpu.make_async_copy(
                    ys_hbm.at[pl.ds(pl.multiple_of(done, SUBLANES), p * SUBLANES)],
                    stages[slot].at[pl.ds(pl.multiple_of(total * SUBLANES + done, SUBLANES), p * SUBLANES)],
                    sems.at[slot]).start()

    def drain(slot):
        for c in range(n_stage // FINAL_KC):
            pltpu.make_async_copy(ys_hbm.at[pl.ds(0, FINAL_KC)],
                                  stages[slot].at[pl.ds(FINAL_KC * c, FINAL_KC)], sems.at[slot]).wait()

    def combine(slot):
        gates = gt_ref[...]
        pos = pos_ref[...]
        y = x1_ref[...]
        for c in range(n_stage // FINAL_KC):
            col = lax.broadcasted_iota(I32, (n_tok, FINAL_KC), 1) + c * FINAL_KC
            sel = jnp.zeros((n_tok, FINAL_KC), F32)
            for k in range(TOP_K):
                sel = jnp.where(col == pos[:, k:k + 1], gates[:, k:k + 1], sel)
            rows = stages[slot][FINAL_KC * c:FINAL_KC * (c + 1), :].astype(BF16)
            y = y + jnp.dot(sel.astype(BF16), rows, preferred_element_type=F32)
        o_ref[...] = _rms(y, gf_ref[...])

    @pl.when(step == 0)
    def _():
        st0_ref[...] = jnp.zeros_like(st0_ref)
        st1_ref[...] = jnp.zeros_like(st1_ref)
        fetch_covers(tcur_ref, 0)

    for slot in (0, 1):
        @pl.when(step % 2 == slot)
        def _(slot=slot):
            drain(slot)

            @pl.when(step < last)
            def _():
                fetch_covers(tnext_ref, 1 - slot)

            combine(slot)


def _final(run_table, x1, gates_tk, pos_tk, g_final, ys):
    t = x1.shape[0]
    tm = FINAL_TM
    steps = t // tm
    row = lambda i: (i, 0)
    n_stage = -(-(TOP_K * tm + 2 * (SUBLANES - 1) * N_EXPERTS) // FINAL_KC) * FINAL_KC
    return pl.pallas_call(
        _final_kernel,
        grid=(steps,),
        in_specs=[pl.BlockSpec((1, 1, LANES), lambda i: (i, 0, 0), memory_space=pltpu.SMEM),
                  pl.BlockSpec((1, 1, LANES), lambda i: (jnp.minimum(i + 1, steps - 1), 0, 0),
                               memory_space=pltpu.SMEM),
                  pl.BlockSpec((tm, _D), row), pl.BlockSpec((tm, TOP_K), row), pl.BlockSpec((tm, TOP_K), row),
                  pl.BlockSpec((1, _D), lambda i: (0, 0)), pl.BlockSpec(memory_space=pl.ANY)],
        out_specs=pl.BlockSpec((tm, _D), row),
        out_shape=jax.ShapeDtypeStruct((t, _D), F32),
        scratch_shapes=[pltpu.VMEM((n_stage, _D), F32), pltpu.VMEM((n_stage, _D), F32),
                        pltpu.SemaphoreType.DMA((2,))],
        compiler_params=_cparams(("arbitrary",)),
        name="final",
    )(run_table, run_table, x1, gates_tk, pos_tk, g_final, ys)


def _prep_in_proj(w_in, b_in, w_gate, b_gate):
    assert w_in.shape[1] == _N_IN and SWA_KV_HEADS * SWA_HEAD_DIM == LANES
    pad = _C_END - _N_IN
    w_all = jnp.pad(w_in, ((0, 0), (0, pad))).astype(BF16)
    b_all = jnp.pad(b_in, (0, pad))[None, :]
    wg = jnp.pad(w_gate, ((0, LANES - GLA_GATE_RANK), (0, 0)))
    return w_all, b_all, wg, b_gate[None, :]


def _rope_tables(seq):
    half = SWA_HEAD_DIM // 2
    inv_freq = ROPE_THETA ** (-jnp.arange(half, dtype=F32) / half)
    ang = jnp.arange(seq, dtype=F32)[:, None] * inv_freq[None, :]
    cos, sin = jnp.cos(ang), jnp.sin(ang)
    cos_t = jnp.concatenate([cos, cos] * (LANES // SWA_HEAD_DIM), axis=1)
    sin_t = jnp.concatenate([-sin, sin] * (LANES // SWA_HEAD_DIM), axis=1)
    return cos_t, sin_t


def _pair_split_perm():
    src = jnp.arange(2 * LANES, dtype=I32)
    dst = jnp.where(src % 2 == 0, src // 2, LANES + src // 2)
    return (dst[:, None] == jnp.arange(2 * LANES, dtype=I32)[None, :]).astype(BF16)


def kernel(x, g_mix, w_in, b_in, w_gla_gate, b_gla_gate, g_gla_head, w_gla_out, sinks, w_swa_out, w_out,
           g_ffn, w_router, b_router, w_e1, b_e1, w_e2, b_e2, g_final):
    bsz, seq, d = x.shape
    assert d == _D and w_in.shape[0] == 1, "single-layer, d_model=1024 only"
    assert seq % max(PROJ_TM, SWA_WINDOW, GLA_CHUNK) == 0
    t = bsz * seq
    assert t % max(MERGE_TM, ROUTE_TT, FINAL_TM, DISPATCH_TM) == 0
    assert FINAL_TM == ROUTE_TT == DISPATCH_TM and t // ROUTE_TT <= LANES and 4 * N_EXPERTS == LANES
    x2 = x.reshape(t, d)

    w_all, b_all, wg, bg = _prep_in_proj(w_in[0], b_in[0], w_gla_gate[0], b_gla_gate[0])
    cos_t, sin_t = _rope_tables(seq)
    gq, gk, gv, gr, sq, ga, gb, sk, sv, lg = _in_proj(x2, g_mix, w_all, b_all, cos_t, sin_t, wg, bg, seq)
    oa = _gla(gq, gk, gv, lg, gr, g_gla_head, bsz, seq)
    ob = _swa(sinks[0], sq, sk, sv, bsz, seq)
    x1, hn, logits_t = _merge(x2, oa, ob, ga, gb, w_gla_out[0].astype(BF16), w_swa_out[0].astype(BF16),
                              w_out[0].astype(BF16), g_ffn, w_router[0].T, b_router[0][:, None])

    e_kt, g_kt, r_kt, cum = _route(logits_t)
    n_tiles = t // ROUTE_TT
    cum = cum[:, :n_tiles].astype(I32)
    counts = cum[:, -1]
    blocks_e = (counts + MOE_BLOCK - 1) // MOE_BLOCK
    bend = jnp.cumsum(blocks_e)
    pstart = (bend - blocks_e) * MOE_BLOCK
    n_blocks = (t * TOP_K) // MOE_BLOCK + N_EXPERTS
    block_e = jnp.minimum(jnp.sum(bend[None, :] <= jnp.arange(n_blocks, dtype=I32)[:, None], axis=1),
                          N_EXPERTS - 1).astype(I32)
    n_valid = bend[-1:].astype(I32)

    before = jnp.concatenate([jnp.zeros((N_EXPERTS, 1), I32), cum[:, :-1]], axis=1)
    run_len = cum - before
    run_src = pstart[:, None] + before
    cov_src = run_src // SUBLANES * SUBLANES
    cov_len = jnp.where(run_len > 0, (run_src + run_len + SUBLANES - 1) // SUBLANES * SUBLANES - cov_src, 0)
    cov_dst = jnp.cumsum(cov_len, axis=0) - cov_len
    run_table = jnp.concatenate([cov_src.T, cov_len.T // SUBLANES, cov_dst.T, (run_src - cov_src).T], axis=1)
    run_table = run_table.reshape(n_tiles, 1, LANES)
    shift = jnp.repeat(cov_dst + run_src - cov_src - before, ROUTE_TT, axis=1)
    pos_kt = r_kt
    for e in range(N_EXPERTS):
        pos_kt = pos_kt + jnp.where(e_kt == e, shift[e][None, :], 0)

    b1 = b_e1[0].reshape(N_EXPERTS, -1, LANES, 2).transpose(0, 1, 3, 2).reshape(N_EXPERTS, 1, -1)
    xs = _dispatch(bend * MOE_BLOCK, blocks_e, n_valid, run_table, pos_kt, hn, n_blocks * MOE_BLOCK)
    first = jnp.concatenate([jnp.ones((1,), I32), (block_e[1:] != block_e[:-1]).astype(I32)])
    slot = (jnp.cumsum(first) - 1) % 2
    ids = jnp.arange(N_EXPERTS, dtype=I32)
    later = (ids[None, :] > ids[:, None]) & (blocks_e[None, :] > 0)
    next_of = jnp.min(jnp.where(later, ids[None, :], N_EXPERTS), axis=1)
    next_of = jnp.where(next_of < N_EXPERTS, next_of, -1)
    next_e = jnp.sum(jnp.where(block_e[:, None] == ids[None, :], next_of[None, :], 0), axis=1).astype(I32)
    ys = _experts(block_e, n_valid, first, slot.astype(I32), next_e, xs, w_e1[0], b1, w_e2[0],
                  b_e2[0][:, None, :], _pair_split_perm())
    out = _final(run_table, x1, g_kt.T, pos_kt.T, g_final[None, :], ys)
    return out.reshape(bsz, seq, d)
```

```python
import jax
import jax.numpy as jnp
import numpy as np
from jax import lax
from jax.experimental import pallas as pl
from jax.experimental.pallas import tpu as pltpu

F32 = jnp.float32
BF16 = jnp.bfloat16
I32 = jnp.int32

NORM_EPS = 1e-5
GLA_HEADS = 4
GLA_DK = 128
GLA_DV = 256
GLA_GATE_RANK = 16
GLA_TAU = 16.0
SWA_Q_HEADS = 16
SWA_KV_HEADS = 2
SWA_GROUP = SWA_Q_HEADS // SWA_KV_HEADS
SWA_HEAD_DIM = 64
SWA_WINDOW = 128
ROPE_THETA = 10000.0
N_EXPERTS = 32
TOP_K = 4
SWIGLU_LIMIT = 7.0
SWIGLU_ALPHA = 1.702

LANES = 128
SUBLANES = 8
NEG = -1e30
VMEM_LIMIT = 56 * 1024 * 1024

PROJ_TM = 256
GLA_CHUNK = 256
GLA_HEADS_PER_STEP = 4
MERGE_TM = 512
MERGE_TN = 256
MERGE_ROW_GROUPS = 1
ROUTE_TT = 512
MOE_BLOCK = 512
DISPATCH_TM = 512
DISPATCH_KC = 512
FINAL_TM = 512
FINAL_KC = 512

_D = 1024
_C_GQ = 0
_C_GK = _C_GQ + GLA_HEADS * GLA_DK
_C_GV = _C_GK + GLA_HEADS * GLA_DK
_C_GR = _C_GV + GLA_HEADS * GLA_DV
_C_LR = _C_GR + GLA_HEADS * GLA_DV
_W_SQ = SWA_Q_HEADS * SWA_HEAD_DIM
_W_KV = SWA_KV_HEADS * SWA_HEAD_DIM
_N_IN = _C_LR + GLA_GATE_RANK + _W_SQ + 2 * _W_KV + 2 * _D
_C_END = -(-_N_IN // LANES) * LANES


def _cparams(sem):
    return pltpu.CompilerParams(dimension_semantics=sem, vmem_limit_bytes=VMEM_LIMIT)


def _rms(x, g):
    return x * lax.rsqrt(jnp.mean(x * x, axis=-1, keepdims=True) + NORM_EPS) * g


def _sigmoid(x):
    return 1.0 / (1.0 + jnp.exp(-x))


def _dot_nt(a, b):
    return lax.dot_general(a, b, (((1,), (1,)), ((), ())), preferred_element_type=F32)


def _rope_slabs(acc, cos, sin, first_half):
    outs = []
    for i in range(acc.shape[1] // LANES):
        xs = acc[:, LANES * i:LANES * (i + 1)]
        partner = jnp.where(first_half, pltpu.roll(xs, LANES - 32, 1), pltpu.roll(xs, 32, 1))
        outs.append(xs * cos + partner * sin)
    return jnp.concatenate(outs, axis=1)


def _in_proj_kernel(x_ref, g_ref, w_ref, b_ref, cos_ref, sin_ref, wg_ref, bg_ref,
                    gq_ref, gk_ref, gv_ref, gr_ref, sq_ref, ga_ref, gb_ref, sk_ref, sv_ref, lg_ref):
    tm = x_ref.shape[0]
    h = _rms(x_ref[...], g_ref[...]).astype(BF16)

    def proj(lo, hi):
        return jnp.dot(h, w_ref[:, lo:hi], preferred_element_type=F32) + b_ref[:, lo:hi]

    cos = cos_ref[...]
    sin = sin_ref[...]
    lane = lax.broadcasted_iota(I32, (tm, LANES), 1)
    first_half = (lane % SWA_HEAD_DIM) < (SWA_HEAD_DIM // 2)

    gq_ref[...] = (proj(_C_GQ, _C_GK) * (GLA_DK ** -0.5)).astype(BF16)
    gk_ref[...] = proj(_C_GK, _C_GV).astype(BF16)
    gv_ref[...] = proj(_C_GV, _C_GR).astype(BF16)
    gr = proj(_C_GR, _C_LR)
    gr_ref[...] = (gr * _sigmoid(gr)).astype(BF16)

    slab = proj(_C_LR, _C_END)
    n_tiles = slab.shape[1] // LANES
    turned = [pltpu.roll(slab[:, LANES * i:LANES * (i + 1)], LANES - GLA_GATE_RANK, 1) for i in range(n_tiles)]
    keep = lane < LANES - GLA_GATE_RANK
    rest = jnp.concatenate([jnp.where(keep, turned[i], turned[i + 1]) for i in range(n_tiles - 1)], axis=1)

    sq = rest[:, 0:_W_SQ] * (SWA_HEAD_DIM ** -0.5)
    sq_ref[...] = _rope_slabs(sq, cos, sin, first_half).astype(BF16)
    low = lane < SWA_HEAD_DIM
    for part, ref, roped in ((rest[:, _W_SQ:_W_SQ + _W_KV], sk_ref, True),
                             (rest[:, _W_SQ + _W_KV:_W_SQ + 2 * _W_KV], sv_ref, False)):
        if roped:
            part = _rope_slabs(part, cos, sin, first_half)
        other = pltpu.roll(part, SWA_HEAD_DIM, 1)
        ref[...] = jnp.concatenate([jnp.where(low, part, other), jnp.where(low, other, part)], axis=1).astype(BF16)
    c_ga = _W_SQ + 2 * _W_KV
    ga_ref[...] = _sigmoid(rest[:, c_ga:c_ga + _D]).astype(BF16)
    gb_ref[...] = _sigmoid(rest[:, c_ga + _D:c_ga + 2 * _D]).astype(BF16)

    low_rank = jnp.where(lane < GLA_GATE_RANK, slab[:, 0:LANES], 0.0)
    z = jnp.dot(low_rank, wg_ref[...], precision=lax.Precision.HIGHEST,
                preferred_element_type=F32) + bg_ref[...]
    log_sig = jnp.minimum(z, 0.0) - jnp.log(1.0 + jnp.exp(-jnp.abs(z)))
    lg_ref[...] = log_sig * (1.0 / GLA_TAU)


def _in_proj(x2, g_mix, w_all, b_all, cos_t, sin_t, wg, bg, seq):
    t = x2.shape[0]
    tm = PROJ_TM
    pos_blocks = seq // tm
    const = lambda i: (0, 0)
    row = lambda i: (i, 0)
    widths = [(_C_GK - _C_GQ, BF16), (_C_GV - _C_GK, BF16), (_C_GR - _C_GV, BF16), (_C_LR - _C_GR, BF16),
              (_W_SQ, BF16), (_D, BF16), (_D, BF16), (SWA_KV_HEADS * LANES, BF16), (SWA_KV_HEADS * LANES, BF16),
              (GLA_HEADS * GLA_DK, F32)]
    return pl.pallas_call(
        _in_proj_kernel,
        grid=(t // tm,),
        in_specs=[
            pl.BlockSpec((tm, _D), row),
            pl.BlockSpec((1, _D), const),
            pl.BlockSpec((_D, _C_END), const, pipeline_mode=pl.Buffered(1)),
            pl.BlockSpec((1, _C_END), const),
            pl.BlockSpec((tm, LANES), lambda i: (i % pos_blocks, 0)),
            pl.BlockSpec((tm, LANES), lambda i: (i % pos_blocks, 0)),
            pl.BlockSpec((LANES, GLA_HEADS * GLA_DK), const),
            pl.BlockSpec((1, GLA_HEADS * GLA_DK), const),
        ],
        out_specs=[pl.BlockSpec((tm, w), row) for w, _ in widths],
        out_shape=[jax.ShapeDtypeStruct((t, w), dt) for w, dt in widths],
        compiler_params=_cparams(("parallel",)),
        name="in_proj",
    )(x2, g_mix, w_all, b_all, cos_t, sin_t, wg, bg)


def _gla_cumsum_operator(c_len):
    t = np.arange(c_len)[:, None]
    r = np.arange(c_len)[None, :]
    return np.tile((r <= t).astype(np.float32), (1, 3))


def _gla_kernel(q_ref, k_ref, v_ref, lg_ref, gr_ref, gh_ref, dm_ref, o_ref, st_ref):
    c_len = GLA_CHUNK
    n_lev = c_len.bit_length() - 1
    seq = q_ref.shape[0]
    st_ref[...] = jnp.zeros_like(st_ref)

    t_i = lax.broadcasted_iota(I32, (c_len, c_len), 0)
    j_i = lax.broadcasted_iota(I32, (c_len, c_len), 1)
    row = lax.broadcasted_iota(I32, (c_len, 1), 0)
    diag = t_i == j_i
    upper, pair = [], []
    for lev in range(n_lev):
        s = c_len >> (lev + 1)
        upper.append((row & s) != 0)
        pair.append(((t_i // (2 * s)) == (j_i // (2 * s))) & ((t_i & s) != 0) & ((j_i & s) == 0))

    sub8 = lax.broadcasted_iota(I32, (c_len // 8, 8, GLA_DK), 1)

    def boundary_rows(b, s):
        if s >= 4:
            b3 = b.reshape(c_len // (2 * s), 2 * s, GLA_DK)
            return jnp.broadcast_to(b3[:, s - 1:s, :], b3.shape).reshape(c_len, GLA_DK)
        b3 = b.reshape(c_len // 8, 8, GLA_DK)
        lo = jnp.broadcast_to(b3[:, 1:2, :], b3.shape)
        hi = jnp.broadcast_to(b3[:, 5:6, :], b3.shape)
        return jnp.where(sub8 < 4, lo, hi).reshape(c_len, GLA_DK)

    def head_chunk(r0, hh):
        kcols = slice(GLA_DK * hh, GLA_DK * (hh + 1))
        vcols = slice(GLA_DV * hh, GLA_DV * (hh + 1))
        q_bf = q_ref[pl.ds(r0, c_len), kcols]
        k_bf = k_ref[pl.ds(r0, c_len), kcols]
        q = q_bf.astype(F32)
        k = k_bf.astype(F32)
        v = v_ref[pl.ds(r0, c_len), vcols]

        lg = lg_ref[pl.ds(r0, c_len), kcols]
        lg_hi = lg.astype(BF16)
        rem = lg - lg_hi.astype(F32)
        lg_mid = rem.astype(BF16)
        lg_lo = (rem - lg_mid.astype(F32)).astype(BF16)
        b = jnp.dot(dm_ref[...], jnp.concatenate([lg_hi, lg_mid, lg_lo], axis=0),
                    preferred_element_type=F32)
        w_cum = jnp.exp(b)

        st = st_ref[hh]
        o = _dot_nt((q * w_cum).astype(BF16), st.astype(BF16))

        a = jnp.where(diag, _dot_nt(q_bf, k_bf), 0.0)
        for lev in range(n_lev):
            s = c_len >> (lev + 1)
            if s == 1:
                w = jnp.where(upper[lev], jnp.exp(lg), 1.0)
            else:
                w = jnp.exp(-jnp.abs(b - boundary_rows(b, s)))
            z = (jnp.where(upper[lev], q, k) * w).astype(BF16)
            a = jnp.where(pair[lev], _dot_nt(z, z), a)
        o = o + jnp.dot(a.astype(BF16), v, preferred_element_type=F32)

        b_last = b[c_len - 1:c_len, :]
        upd = lax.dot_general(v, (k * jnp.exp(b_last - b)).astype(BF16), (((0,), (0,)), ((), ())),
                              preferred_element_type=F32)
        st_ref[hh] = st * w_cum[c_len - 1:c_len, :] + upd

        on = _rms(o, gh_ref[...])
        o_ref[pl.ds(r0, c_len), vcols] = (on * gr_ref[pl.ds(r0, c_len), vcols].astype(F32)).astype(BF16)

    def chunk(c, carry):
        r0 = pl.multiple_of(c * c_len, c_len)
        for hh in range(GLA_HEADS_PER_STEP):
            head_chunk(r0, hh)
        return carry

    lax.fori_loop(0, seq // c_len, chunk, 0, unroll=4)


def _gla(gq, gk, gv, lg, gr, g_head, bsz, seq):
    t = gq.shape[0]
    hs = GLA_HEADS_PER_STEP
    dmat = jnp.asarray(_gla_cumsum_operator(GLA_CHUNK), dtype=BF16)
    return pl.pallas_call(
        _gla_kernel,
        grid=(bsz, GLA_HEADS // hs),
        in_specs=[
            pl.BlockSpec((seq, hs * GLA_DK), lambda b, h: (b, h)),
            pl.BlockSpec((seq, hs * GLA_DK), lambda b, h: (b, h)),
            pl.BlockSpec((seq, hs * GLA_DV), lambda b, h: (b, h)),
            pl.BlockSpec((seq, hs * GLA_DK), lambda b, h: (b, h)),
            pl.BlockSpec((seq, hs * GLA_DV), lambda b, h: (b, h)),
            pl.BlockSpec((1, GLA_DV), lambda b, h: (0, 0)),
            pl.BlockSpec(dmat.shape, lambda b, h: (0, 0)),
        ],
        out_specs=pl.BlockSpec((seq, hs * GLA_DV), lambda b, h: (b, h)),
        out_shape=jax.ShapeDtypeStruct((t, GLA_HEADS * GLA_DV), BF16),
        scratch_shapes=[pltpu.VMEM((hs, GLA_DV, GLA_DK), F32)],
        compiler_params=_cparams(("parallel", "parallel")),
        name="gla",
    )(gq, gk, gv, lg, gr, g_head, dmat)


def _swa_kernel(sink_ref, q_ref, k_ref, v_ref, o_ref):
    w = SWA_WINDOW
    seq = q_ref.shape[0]
    hk = pl.program_id(1)
    lane_q = lax.broadcasted_iota(I32, (w, LANES), 1)
    low_q = lane_q < SWA_HEAD_DIM
    lane_b = lax.broadcasted_iota(I32, (2 * w, LANES), 1)
    low_b = lane_b < SWA_HEAD_DIM
    qi = lax.broadcasted_iota(I32, (w, 2 * w), 0)
    kj = lax.broadcasted_iota(I32, (w, 2 * w), 1)
    in_window = (kj > qi) & (kj <= qi + w)
    zero_q = jnp.zeros((w, LANES), BF16)
    zero_b = jnp.zeros((2 * w, LANES), BF16)

    def block(n, carry):
        r0 = pl.multiple_of(n * w, w)
        p0 = pl.multiple_of(jnp.maximum(n - 1, 0) * w, w)
        kb = jnp.concatenate([k_ref[pl.ds(p0, w), :], k_ref[pl.ds(r0, w), :]], axis=0)
        vb = jnp.concatenate([v_ref[pl.ds(p0, w), :], v_ref[pl.ds(r0, w), :]], axis=0)
        valid = in_window & ((kj >= w) | (n > 0))
        v_lo = jnp.where(low_b, vb, zero_b)
        v_hi = jnp.where(low_b, zero_b, vb)
        for m in range(SWA_GROUP // 2):
            qp = q_ref[pl.ds(r0, w), LANES * m:LANES * (m + 1)]
            acc = jnp.zeros((w, LANES), F32)
            for par in range(2):
                qm = jnp.where(low_q, qp, zero_q) if par == 0 else jnp.where(low_q, zero_q, qp)
                s = jnp.where(valid, _dot_nt(qm, kb), NEG)
                sink = sink_ref[hk * SWA_GROUP + 2 * m + par]
                mx = jnp.maximum(jnp.max(s, axis=-1, keepdims=True), sink)
                p = jnp.exp(s - mx)
                den = jnp.sum(p, axis=-1, keepdims=True) + jnp.exp(sink - mx)
                pv = jnp.dot(p.astype(BF16), v_lo if par == 0 else v_hi, preferred_element_type=F32)
                acc = acc + pv / den
            o_ref[pl.ds(r0, w), LANES * m:LANES * (m + 1)] = acc.astype(BF16)
        return carry

    lax.fori_loop(0, seq // w, block, 0, unroll=16)


def _swa(sinks, sq, sk, sv, bsz, seq):
    t = sq.shape[0]
    gw = SWA_GROUP * SWA_HEAD_DIM
    return pl.pallas_call(
        _swa_kernel,
        grid_spec=pltpu.PrefetchScalarGridSpec(
            num_scalar_prefetch=1,
            grid=(bsz, SWA_KV_HEADS),
            in_specs=[
                pl.BlockSpec((seq, gw), lambda b, h, s: (b, h)),
                pl.BlockSpec((seq, LANES), lambda b, h, s: (b, h)),
                pl.BlockSpec((seq, LANES), lambda b, h, s: (b, h)),
            ],
            out_specs=pl.BlockSpec((seq, gw), lambda b, h, s: (b, h)),
        ),
        out_shape=jax.ShapeDtypeStruct((t, SWA_Q_HEADS * SWA_HEAD_DIM), BF16),
        compiler_params=_cparams(("parallel", "parallel")),
        name="swa",
    )(sinks, sq, sk, sv)


def _merge_kernel(x_ref, oa_ref, ob_ref, ga_ref, gb_ref, wa_ref, wb_ref, wo_ref, gf_ref, wr_ref, br_ref,
                  x1_ref, hn_ref, lt_ref, mixed_ref):
    rows_g = x_ref.shape[0] // MERGE_ROW_GROUPS
    for h in range(MERGE_ROW_GROUPS):
        rows = slice(rows_g * h, rows_g * (h + 1))
        oa = oa_ref[rows, :]
        ob = ob_ref[rows, :]
        for n in range(_D // MERGE_TN):
            cols = slice(MERGE_TN * n, MERGE_TN * (n + 1))
            ya = ga_ref[rows, cols].astype(F32) * jnp.dot(oa, wa_ref[:, cols], preferred_element_type=F32)
            yb = gb_ref[rows, cols].astype(F32) * jnp.dot(ob, wb_ref[:, cols], preferred_element_type=F32)
            mixed_ref[rows, cols] = (ya + yb).astype(BF16)
        mixed = mixed_ref[rows, :]
        for n in range(_D // MERGE_TN):
            cols = slice(MERGE_TN * n, MERGE_TN * (n + 1))
            x1_ref[rows, cols] = x_ref[rows, cols] + jnp.dot(mixed, wo_ref[:, cols], preferred_element_type=F32)
        hn = _rms(x1_ref[rows, :], gf_ref[...])
        hn_ref[rows, :] = hn
        lt_ref[:, rows] = lax.dot_general(wr_ref[...], hn, (((1,), (1,)), ((), ())),
                                          precision=lax.Precision.HIGHEST,
                                          preferred_element_type=F32) + br_ref[...]


def _merge(x2, oa, ob, ga, gb, wa, wb, wo, g_ffn, wr_t, br_col):
    t = x2.shape[0]
    tm = MERGE_TM
    row = lambda i: (i, 0)
    const = lambda i: (0, 0)
    return pl.pallas_call(
        _merge_kernel,
        grid=(t // tm,),
        in_specs=[pl.BlockSpec((tm, _D), row)] * 5 + [pl.BlockSpec((_D, _D), const)] * 3 + [
            pl.BlockSpec((1, _D), const),
            pl.BlockSpec((N_EXPERTS, _D), const),
            pl.BlockSpec((N_EXPERTS, 1), const),
        ],
        out_specs=[pl.BlockSpec((tm, _D), row), pl.BlockSpec((tm, _D), row),
                   pl.BlockSpec((N_EXPERTS, tm), lambda i: (0, i))],
        out_shape=[jax.ShapeDtypeStruct((t, _D), F32), jax.ShapeDtypeStruct((t, _D), F32),
                   jax.ShapeDtypeStruct((N_EXPERTS, t), F32)],
        scratch_shapes=[pltpu.VMEM((tm, _D), BF16)],
        compiler_params=_cparams(("parallel",)),
        name="merge",
    )(x2, oa, ob, ga, gb, wa, wb, wo, g_ffn, wr_t, br_col)


def _route_kernel(lt_ref, e_ref, g_ref, r_ref, cum_ref, carry_ref):
    tt = lt_ref.shape[1]

    @pl.when(pl.program_id(0) == 0)
    def _():
        carry_ref[...] = jnp.zeros_like(carry_ref)
        cum_ref[...] = jnp.zeros_like(cum_ref)

    eid = lax.broadcasted_iota(I32, (N_EXPERTS, tt), 0)
    work = lt_ref[...]
    vals, idxs = [], []
    chosen = jnp.zeros((N_EXPERTS, tt), F32)
    for _ in range(TOP_K):
        m = jnp.max(work, axis=0, keepdims=True)
        idx = jnp.min(jnp.where(work == m, eid, N_EXPERTS), axis=0, keepdims=True)
        hit = eid == idx
        work = jnp.where(hit, -jnp.inf, work)
        chosen = jnp.where(hit, 1.0, chosen)
        vals.append(m)
        idxs.append(idx)
    ex = [jnp.exp(v - vals[0]) for v in vals]
    den = ex[0] + ex[1] + ex[2] + ex[3]

    t_r = lax.broadcasted_iota(I32, (tt, tt), 0)
    t_c = lax.broadcasted_iota(I32, (tt, tt), 1)
    before = (t_r < t_c).astype(BF16)
    pref = jnp.dot(chosen.astype(BF16), before, preferred_element_type=F32) + carry_ref[:, 0:1]
    for k in range(TOP_K):
        e_ref[k:k + 1, :] = idxs[k]
        g_ref[k:k + 1, :] = ex[k] / den
        r_ref[k:k + 1, :] = jnp.sum(jnp.where(eid == idxs[k], pref, 0.0), axis=0, keepdims=True).astype(I32)
    total = pref[:, tt - 1:tt] + chosen[:, tt - 1:tt]
    carry_ref[...] = jnp.broadcast_to(total, carry_ref.shape)
    tile_lane = lax.broadcasted_iota(I32, cum_ref.shape, 1) == pl.program_id(0)
    cum_ref[...] = jnp.where(tile_lane, total, cum_ref[...])


def _route(logits_t):
    t = logits_t.shape[1]
    tt = ROUTE_TT
    blk = lambda i: (0, i)
    return pl.pallas_call(
        _route_kernel,
        grid=(t // tt,),
        in_specs=[pl.BlockSpec((N_EXPERTS, tt), blk)],
        out_specs=[pl.BlockSpec((TOP_K, tt), blk), pl.BlockSpec((TOP_K, tt), blk), pl.BlockSpec((TOP_K, tt), blk),
                   pl.BlockSpec((N_EXPERTS, LANES), lambda i: (0, 0))],
        out_shape=[jax.ShapeDtypeStruct((TOP_K, t), I32), jax.ShapeDtypeStruct((TOP_K, t), F32),
                   jax.ShapeDtypeStruct((TOP_K, t), I32), jax.ShapeDtypeStruct((N_EXPERTS, LANES), F32)],
        scratch_shapes=[pltpu.VMEM((N_EXPERTS, LANES), F32)],
        compiler_params=_cparams(("arbitrary",)),
        name="route",
    )(logits_t)


def _dispatch_kernel(pend_ref, nblk_ref, nv_ref, tab_ref, hn_ref, pos_ref, xs_hbm, spill_hbm,
                     st0_ref, st1_ref, carry_ref, zero_ref, sems, zsem):
    n_tok = hn_ref.shape[0]
    n_stage = st0_ref.shape[0]
    n_blocks = xs_hbm.shape[0] // MOE_BLOCK
    step = pl.program_id(0)
    last = pl.num_programs(0) - 1
    stages = (st0_ref, st1_ref)
    max_units = n_tok // SUBLANES + 1
    pieces = [1 << b for b in reversed(range(max_units.bit_length()))]
    fill_pieces = [1 << b for b in reversed(range((n_stage // SUBLANES).bit_length()))]

    def zero_block(row0):
        return pltpu.make_async_copy(zero_ref, xs_hbm.at[pl.ds(pl.multiple_of(row0, MOE_BLOCK), MOE_BLOCK)], zsem)

    @pl.when(step == 0)
    def _():
        zero_ref[...] = jnp.zeros_like(zero_ref)
        carry_ref[...] = jnp.zeros_like(carry_ref)

        def expert_tail(e, carry, start):
            @pl.when(nblk_ref[e] > 0)
            def _():
                cp = zero_block(pend_ref[e] - MOE_BLOCK)
                cp.start() if start else cp.wait()
            return carry

        def unused_block(b, carry, start):
            cp = zero_block(b * MOE_BLOCK)
            cp.start() if start else cp.wait()
            return carry

        for start in (True, False):
            lax.fori_loop(0, N_EXPERTS, lambda e, c: expert_tail(e, c, start), 0)
            lax.fori_loop(nv_ref[0], n_blocks, lambda b, c: unused_block(b, c, start), 0)

    def permute(slot):
        hb = hn_ref[...].astype(BF16)
        pos = pos_ref[...]
        for c in range(n_stage // DISPATCH_KC):
            row = lax.broadcasted_iota(I32, (DISPATCH_KC, n_tok), 0) + c * DISPATCH_KC
            sel = jnp.zeros((DISPATCH_KC, n_tok), F32)
            for k in range(TOP_K):
                sel = jnp.where(row == pos[k:k + 1, :], 1.0, sel)
            stages[slot][DISPATCH_KC * c:DISPATCH_KC * (c + 1), :] = jnp.dot(
                sel.astype(BF16), hb, preferred_element_type=F32)
        sub = lax.broadcasted_iota(I32, (SUBLANES, hn_ref.shape[1]), 0)
        for e in range(N_EXPERTS):
            units = tab_ref[0, 0, N_EXPERTS + e]

            @pl.when(units > 0)
            def _(e=e, units=units):
                first = pl.multiple_of(tab_ref[0, 0, 2 * N_EXPERTS + e], SUBLANES)
                shared = tab_ref[0, 0, 3 * N_EXPERTS + e]
                head = stages[slot][pl.ds(first, SUBLANES), :]
                stages[slot][pl.ds(first, SUBLANES), :] = jnp.where(sub < shared, carry_ref[e], head)
                final = pl.multiple_of(first + (units - 1) * SUBLANES, SUBLANES)
                carry_ref[e] = stages[slot][pl.ds(final, SUBLANES), :]

    def write_covers(slot):
        total = 0
        for e in range(N_EXPERTS):
            dst = tab_ref[0, 0, e]
            units = tab_ref[0, 0, N_EXPERTS + e]
            src = tab_ref[0, 0, 2 * N_EXPERTS + e]
            total = total + units
            for p in pieces:
                @pl.when((units & p) != 0)
                def _(p=p):
                    done = (units & ~(2 * p - 1)) * SUBLANES
                    pltpu.make_async_copy(
                        stages[slot].at[pl.ds(pl.multiple_of(src + done, SUBLANES), p * SUBLANES)],
                        xs_hbm.at[pl.ds(pl.multiple_of(dst + done, SUBLANES), p * SUBLANES)],
                        sems.at[slot]).start()
        spare = n_stage // SUBLANES - total
        for p in fill_pieces:
            @pl.when((spare & p) != 0)
            def _(p=p):
                done = (spare & ~(2 * p - 1)) * SUBLANES
                pltpu.make_async_copy(
                    stages[slot].at[pl.ds(pl.multiple_of(total * SUBLANES + done, SUBLANES), p * SUBLANES)],
                    spill_hbm.at[pl.ds(pl.multiple_of(done, SUBLANES), p * SUBLANES)],
                    sems.at[slot]).start()

    def drain(slot):
        for c in range(n_stage // DISPATCH_KC):
            pltpu.make_async_copy(stages[slot].at[pl.ds(DISPATCH_KC * c, DISPATCH_KC)],
                                  spill_hbm.at[pl.ds(0, DISPATCH_KC)], sems.at[slot]).wait()

    for slot in (0, 1):
        @pl.when(step % 2 == slot)
        def _(slot=slot):
            permute(slot)

            @pl.when(step > 0)
            def _():
                drain(1 - slot)

            write_covers(slot)

            @pl.when(step == last)
            def _():
                drain(slot)


def _dispatch(pend_rows, blocks_e, n_valid, run_table, pos_kt, hn, rows):
    t = hn.shape[0]
    tm = DISPATCH_TM
    steps = t // tm
    n_stage = -(-(TOP_K * tm + 2 * (SUBLANES - 1) * N_EXPERTS) // DISPATCH_KC) * DISPATCH_KC
    return pl.pallas_call(
        _dispatch_kernel,
        grid_spec=pltpu.PrefetchScalarGridSpec(
            num_scalar_prefetch=3,
            grid=(steps,),
            in_specs=[pl.BlockSpec((1, 1, LANES), lambda i, *_: (i, 0, 0), memory_space=pltpu.SMEM),
                      pl.BlockSpec((tm, _D), lambda i, *_: (i, 0)),
                      pl.BlockSpec((TOP_K, tm), lambda i, *_: (0, i))],
            out_specs=[pl.BlockSpec(memory_space=pl.ANY), pl.BlockSpec(memory_space=pl.ANY)],
            scratch_shapes=[pltpu.VMEM((n_stage, _D), F32), pltpu.VMEM((n_stage, _D), F32),
                            pltpu.VMEM((N_EXPERTS, SUBLANES, _D), F32), pltpu.VMEM((MOE_BLOCK, _D), F32),
                            pltpu.SemaphoreType.DMA((2,)), pltpu.SemaphoreType.DMA],
        ),
        out_shape=[jax.ShapeDtypeStruct((rows, _D), F32), jax.ShapeDtypeStruct((n_stage, _D), F32)],
        compiler_params=_cparams(("arbitrary",)),
        name="dispatch",
    )(pend_rows, blocks_e, n_valid, run_table, hn, pos_kt)[0]


def _experts_kernel(be_ref, nv_ref, first_ref, slot_ref, next_ref, xs_ref, b1_ref, b2_ref, perm_ref,
                    w1_hbm, w2_hbm, ys_ref, wf1_ref, wf2_ref, w1b_ref, w2b_ref, sems):
    i = pl.program_id(0)
    gw = perm_ref.shape[0]

    def weight_copies(e, slot):
        return (pltpu.make_async_copy(w1_hbm.at[e], wf1_ref.at[slot], sems.at[0, slot]),
                pltpu.make_async_copy(w2_hbm.at[e], wf2_ref.at[slot], sems.at[1, slot]))

    @pl.when(i == 0)
    def _():
        for cp in weight_copies(be_ref[0], 0):
            cp.start()

    @pl.when((i < nv_ref[0]) & (first_ref[i] != 0))
    def _():
        slot = slot_ref[i]
        for cp in weight_copies(be_ref[i], slot):
            cp.wait()

        @pl.when(next_ref[i] >= 0)
        def _():
            for cp in weight_copies(next_ref[i], 1 - slot):
                cp.start()

        for gi in range(wf1_ref.shape[2] // gw):
            wb = wf1_ref[slot, :, gw * gi:gw * (gi + 1)].astype(BF16)
            w1b_ref[:, gw * gi:gw * (gi + 1)] = jnp.dot(wb, perm_ref[...], preferred_element_type=F32).astype(BF16)
        w2b_ref[...] = wf2_ref[slot].astype(BF16)

    @pl.when(i < nv_ref[0])
    def _():
        x = xs_ref[...].astype(BF16)
        u = jnp.dot(x, w1b_ref[...], preferred_element_type=F32) + b1_ref[0]
        acts = []
        for gi in range(u.shape[1] // (2 * LANES)):
            g = jnp.minimum(u[:, 2 * LANES * gi:2 * LANES * gi + LANES], SWIGLU_LIMIT)
            lin = jnp.clip(u[:, 2 * LANES * gi + LANES:2 * LANES * (gi + 1)], -SWIGLU_LIMIT, SWIGLU_LIMIT)
            acts.append((g * _sigmoid(SWIGLU_ALPHA * g) * (lin + 1.0)).astype(BF16))
        act = jnp.concatenate(acts, axis=1)
        ys_ref[...] = jnp.dot(act, w2b_ref[...], preferred_element_type=F32) + b2_ref[0]

    @pl.when(i >= nv_ref[0])
    def _():
        ys_ref[...] = jnp.zeros_like(ys_ref)


def _experts(block_e, n_valid, first, slot, next_e, xs, w1, b1, w2, b2, perm):
    rows = xs.shape[0]
    n_blocks = rows // MOE_BLOCK
    n_e, d, n1 = w1.shape
    dff = w2.shape[1]
    wmap = lambda i, be, *_: (be[i], 0, 0)
    rmap = lambda i, *_: (i, 0)
    return pl.pallas_call(
        _experts_kernel,
        grid_spec=pltpu.PrefetchScalarGridSpec(
            num_scalar_prefetch=5,
            grid=(n_blocks,),
            in_specs=[
                pl.BlockSpec((MOE_BLOCK, _D), rmap),
                pl.BlockSpec((1, 1, n1), wmap),
                pl.BlockSpec((1, 1, _D), wmap),
                pl.BlockSpec(perm.shape, lambda i, *_: (0, 0)),
                pl.BlockSpec(memory_space=pl.ANY),
                pl.BlockSpec(memory_space=pl.ANY),
            ],
            out_specs=pl.BlockSpec((MOE_BLOCK, _D), rmap),
            scratch_shapes=[pltpu.VMEM((2, d, n1), F32), pltpu.VMEM((2, dff, _D), F32),
                            pltpu.VMEM((d, n1), BF16), pltpu.VMEM((dff, _D), BF16),
                            pltpu.SemaphoreType.DMA((2, 2))],
        ),
        out_shape=jax.ShapeDtypeStruct((rows, _D), F32),
        compiler_params=_cparams(("arbitrary",)),
        name="experts",
    )(block_e, n_valid, first, slot, next_e, xs, b1, b2, perm, w1, w2)


def _final_kernel(tcur_ref, tnext_ref, x1_ref, gt_ref, pos_ref, gf_ref, ys_hbm, o_ref, st0_ref, st1_ref, sems):
    n_tok = x1_ref.shape[0]
    n_stage = st0_ref.shape[0]
    step = pl.program_id(0)
    last = pl.num_programs(0) - 1
    stages = (st0_ref, st1_ref)
    max_units = n_tok // SUBLANES + 1
    pieces = [1 << b for b in reversed(range(max_units.bit_length()))]
    fill_pieces = [1 << b for b in reversed(range((n_stage // SUBLANES).bit_length()))]

    def fetch_covers(tab_ref, slot):
        total = 0
        for e in range(N_EXPERTS):
            src = tab_ref[0, 0, e]
            units = tab_ref[0, 0, N_EXPERTS + e]
            dst = tab_ref[0, 0, 2 * N_EXPERTS + e]
            total = total + units
            for p in pieces:
                @pl.when((units & p) != 0)
                def _(p=p):
                    done = (units & ~(2 * p - 1)) * SUBLANES
                    pltpu.make_async_copy(
                        ys_hbm.at[pl.ds(pl.multiple_of(src + done, SUBLANES), p * SUBLANES)],
                        stages[slot].at[pl.ds(pl.multiple_of(dst + done, SUBLANES), p * SUBLANES)],
                        sems.at[slot]).start()
        spare = n_stage // SUBLANES - total
        for p in fill_pieces:
            @pl.when((spare & p) != 0)
            def _(p=p):
                done = (spare & ~(2 * p - 1)) * SUBLANES
                pltpu.make_async_copy(
                    ys_hbm.at[pl.ds(pl.multiple_of(done, SUBLANES), p * SUBLANES)],
                    stages[slot].at[pl.ds(pl.multiple_of(total * SUBLANES + done, SUBLANES), p * SUBLANES)],
                    sems.at[slot]).start()

    def drain(slot):
        for c in range(n_stage // FINAL_KC):
            pltpu.make_async_copy(ys_hbm.at[pl.ds(0, FINAL_KC)],
                                  stages[slot].at[pl.ds(FINAL_KC * c, FINAL_KC)], sems.at[slot]).wait()

    def combine(slot):
        gates = gt_ref[...]
        pos = pos_ref[...]
        y = x1_ref[...]
        for c in range(n_stage // FINAL_KC):
            col = lax.broadcasted_iota(I32, (n_tok, FINAL_KC), 1) + c * FINAL_KC
            sel = jnp.zeros((n_tok, FINAL_KC), F32)
            for k in range(TOP_K):
                sel = jnp.where(col == pos[:, k:k + 1], gates[:, k:k + 1], sel)
            rows = stages[slot][FINAL_KC * c:FINAL_KC * (c + 1), :].astype(BF16)
            y = y + jnp.dot(sel.astype(BF16), rows, preferred_element_type=F32)
        o_ref[...] = _rms(y, gf_ref[...])

    @pl.when(step == 0)
    def _():
        st0_ref[...] = jnp.zeros_like(st0_ref)
        st1_ref[...] = jnp.zeros_like(st1_ref)
        fetch_covers(tcur_ref, 0)

    for slot in (0, 1):
        @pl.when(step % 2 == slot)
        def _(slot=slot):
            drain(slot)

            @pl.when(step < last)
            def _():
                fetch_covers(tnext_ref, 1 - slot)

            combine(slot)


def _final(run_table, x1, gates_tk, pos_tk, g_final, ys):
    t = x1.shape[0]
    tm = FINAL_TM
    steps = t // tm
    row = lambda i: (i, 0)
    n_stage = -(-(TOP_K * tm + 2 * (SUBLANES - 1) * N_EXPERTS) // FINAL_KC) * FINAL_KC
    return pl.pallas_call(
        _final_kernel,
        grid=(steps,),
        in_specs=[pl.BlockSpec((1, 1, LANES), lambda i: (i, 0, 0), memory_space=pltpu.SMEM),
                  pl.BlockSpec((1, 1, LANES), lambda i: (jnp.minimum(i + 1, steps - 1), 0, 0),
                               memory_space=pltpu.SMEM),
                  pl.BlockSpec((tm, _D), row), pl.BlockSpec((tm, TOP_K), row), pl.BlockSpec((tm, TOP_K), row),
                  pl.BlockSpec((1, _D), lambda i: (0, 0)), pl.BlockSpec(memory_space=pl.ANY)],
        out_specs=pl.BlockSpec((tm, _D), row),
        out_shape=jax.ShapeDtypeStruct((t, _D), F32),
        scratch_shapes=[pltpu.VMEM((n_stage, _D), F32), pltpu.VMEM((n_stage, _D), F32),
                        pltpu.SemaphoreType.DMA((2,))],
        compiler_params=_cparams(("arbitrary",)),
        name="final",
    )(run_table, run_table, x1, gates_tk, pos_tk, g_final, ys)


def _prep_in_proj(w_in, b_in, w_gate, b_gate):
    assert w_in.shape[1] == _N_IN and SWA_KV_HEADS * SWA_HEAD_DIM == LANES
    pad = _C_END - _N_IN
    w_all = jnp.pad(w_in, ((0, 0), (0, pad))).astype(BF16)
    b_all = jnp.pad(b_in, (0, pad))[None, :]
    wg = jnp.pad(w_gate, ((0, LANES - GLA_GATE_RANK), (0, 0)))
    return w_all, b_all, wg, b_gate[None, :]


def _rope_tables(seq):
    half = SWA_HEAD_DIM // 2
    inv_freq = ROPE_THETA ** (-jnp.arange(half, dtype=F32) / half)
    ang = jnp.arange(seq, dtype=F32)[:, None] * inv_freq[None, :]
    cos, sin = jnp.cos(ang), jnp.sin(ang)
    cos_t = jnp.concatenate([cos, cos] * (LANES // SWA_HEAD_DIM), axis=1)
    sin_t = jnp.concatenate([-sin, sin] * (LANES // SWA_HEAD_DIM), axis=1)
    return cos_t, sin_t


def _pair_split_perm():
    src = jnp.arange(2 * LANES, dtype=I32)
    dst = jnp.where(src % 2 == 0, src // 2, LANES + src // 2)
    return (dst[:, None] == jnp.arange(2 * LANES, dtype=I32)[None, :]).astype(BF16)


def kernel(x, g_mix, w_in, b_in, w_gla_gate, b_gla_gate, g_gla_head, w_gla_out, sinks, w_swa_out, w_out,
           g_ffn, w_router, b_router, w_e1, b_e1, w_e2, b_e2, g_final):
    bsz, seq, d = x.shape
    assert d == _D and w_in.shape[0] == 1, "single-layer, d_model=1024 only"
    assert seq % max(PROJ_TM, SWA_WINDOW, GLA_CHUNK) == 0
    t = bsz * seq
    assert t % max(MERGE_TM, ROUTE_TT, FINAL_TM, DISPATCH_TM) == 0
    assert FINAL_TM == ROUTE_TT == DISPATCH_TM and t // ROUTE_TT <= LANES and 4 * N_EXPERTS == LANES
    x2 = x.reshape(t, d)

    w_all, b_all, wg, bg = _prep_in_proj(w_in[0], b_in[0], w_gla_gate[0], b_gla_gate[0])
    cos_t, sin_t = _rope_tables(seq)
    gq, gk, gv, gr, sq, ga, gb, sk, sv, lg = _in_proj(x2, g_mix, w_all, b_all, cos_t, sin_t, wg, bg, seq)
    oa = _gla(gq, gk, gv, lg, gr, g_gla_head, bsz, seq)
    ob = _swa(sinks[0], sq, sk, sv, bsz, seq)
    x1, hn, logits_t = _merge(x2, oa, ob, ga, gb, w_gla_out[0].astype(BF16), w_swa_out[0].astype(BF16),
                              w_out[0].astype(BF16), g_ffn, w_router[0].T, b_router[0][:, None])

    e_kt, g_kt, r_kt, cum = _route(logits_t)
    n_tiles = t // ROUTE_TT
    cum = cum[:, :n_tiles].astype(I32)
    counts = cum[:, -1]
    blocks_e = (counts + MOE_BLOCK - 1) // MOE_BLOCK
    bend = jnp.cumsum(blocks_e)
    pstart = (bend - blocks_e) * MOE_BLOCK
    n_blocks = (t * TOP_K) // MOE_BLOCK + N_EXPERTS
    block_e = jnp.minimum(jnp.sum(bend[None, :] <= jnp.arange(n_blocks, dtype=I32)[:, None], axis=1),
                          N_EXPERTS - 1).astype(I32)
    n_valid = bend[-1:].astype(I32)

    before = jnp.concatenate([jnp.zeros((N_EXPERTS, 1), I32), cum[:, :-1]], axis=1)
    run_len = cum - before
    run_src = pstart[:, None] + before
    cov_src = run_src // SUBLANES * SUBLANES
    cov_len = jnp.where(run_len > 0, (run_src + run_len + SUBLANES - 1) // SUBLANES * SUBLANES - cov_src, 0)
    cov_dst = jnp.cumsum(cov_len, axis=0) - cov_len
    run_table = jnp.concatenate([cov_src.T, cov_len.T // SUBLANES, cov_dst.T, (run_src - cov_src).T], axis=1)
    run_table = run_table.reshape(n_tiles, 1, LANES)
    shift = jnp.repeat(cov_dst + run_src - cov_src - before, ROUTE_TT, axis=1)
    pos_kt = r_kt
    for e in range(N_EXPERTS):
        pos_kt = pos_kt + jnp.where(e_kt == e, shift[e][None, :], 0)

    b1 = b_e1[0].reshape(N_EXPERTS, -1, LANES, 2).transpose(0, 1, 3, 2).reshape(N_EXPERTS, 1, -1)
    xs = _dispatch(bend * MOE_BLOCK, blocks_e, n_valid, run_table, pos_kt, hn, n_blocks * MOE_BLOCK)
    first = jnp.concatenate([jnp.ones((1,), I32), (block_e[1:] != block_e[:-1]).astype(I32)])
    slot = (jnp.cumsum(first) - 1) % 2
    ids = jnp.arange(N_EXPERTS, dtype=I32)
    later = (ids[None, :] > ids[:, None]) & (blocks_e[None, :] > 0)
    next_of = jnp.min(jnp.where(later, ids[None, :], N_EXPERTS), axis=1)
    next_of = jnp.where(next_of < N_EXPERTS, next_of, -1)
    next_e = jnp.sum(jnp.where(block_e[:, None] == ids[None, :], next_of[None, :], 0), axis=1).astype(I32)
    ys = _experts(block_e, n_valid, first, slot.astype(I32), next_e, xs, w_e1[0], b1, w_e2[0],
                  b_e2[0][:, None, :], _pair_split_perm())
    out = _final(run_table, x1, g_kt.T, pos_kt.T, g_final[None, :], ys)
    return out.reshape(bsz, seq, d)
```

```python
import jax
import jax.numpy as jnp
import numpy as np
from jax import lax
from jax.experimental import pallas as pl
from jax.experimental.pallas import tpu as pltpu

F32 = jnp.float32
BF16 = jnp.bfloat16
I32 = jnp.int32

NORM_EPS = 1e-5
GLA_HEADS = 4
GLA_DK = 128
GLA_DV = 256
GLA_GATE_RANK = 16
GLA_TAU = 16.0
SWA_Q_HEADS = 16
SWA_KV_HEADS = 2
SWA_GROUP = SWA_Q_HEADS // SWA_KV_HEADS
SWA_HEAD_DIM = 64
SWA_WINDOW = 128
ROPE_THETA = 10000.0
N_EXPERTS = 32
TOP_K = 4
SWIGLU_LIMIT = 7.0
SWIGLU_ALPHA = 1.702

LANES = 128
SUBLANES = 8
NEG = -1e30
VMEM_LIMIT = 56 * 1024 * 1024

PROJ_TM = 256
GLA_CHUNK = 256
GLA_HEADS_PER_STEP = 4
MERGE_TM = 512
MERGE_TN = 256
MERGE_ROW_GROUPS = 1
ROUTE_TT = 512
MOE_BLOCK = 512
DISPATCH_TM = 512
DISPATCH_KC = 512
FINAL_TM = 512
FINAL_KC = 512

_D = 1024
_C_GQ = 0
_C_GK = _C_GQ + GLA_HEADS * GLA_DK
_C_GV = _C_GK + GLA_HEADS * GLA_DK
_C_GR = _C_GV + GLA_HEADS * GLA_DV
_C_LR = _C_GR + GLA_HEADS * GLA_DV
_W_SQ = SWA_Q_HEADS * SWA_HEAD_DIM
_W_KV = SWA_KV_HEADS * SWA_HEAD_DIM
_N_IN = _C_LR + GLA_GATE_RANK + _W_SQ + 2 * _W_KV + 2 * _D
_C_END = -(-_N_IN // LANES) * LANES


def _cparams(sem):
    return pltpu.CompilerParams(dimension_semantics=sem, vmem_limit_bytes=VMEM_LIMIT)


def _rms(x, g):
    return x * lax.rsqrt(jnp.mean(x * x, axis=-1, keepdims=True) + NORM_EPS) * g


def _sigmoid(x):
    return 1.0 / (1.0 + jnp.exp(-x))


def _dot_nt(a, b):
    return lax.dot_general(a, b, (((1,), (1,)), ((), ())), preferred_element_type=F32)


def _rope_slabs(acc, cos, sin, first_half):
    outs = []
    for i in range(acc.shape[1] // LANES):
        xs = acc[:, LANES * i:LANES * (i + 1)]
        partner = jnp.where(first_half, pltpu.roll(xs, LANES - 32, 1), pltpu.roll(xs, 32, 1))
        outs.append(xs * cos + partner * sin)
    return jnp.concatenate(outs, axis=1)


def _in_proj_kernel(x_ref, g_ref, w_ref, b_ref, cos_ref, sin_ref, wg_ref, bg_ref,
                    gq_ref, gk_ref, gv_ref, gr_ref, sq_ref, ga_ref, gb_ref, sk_ref, sv_ref, lg_ref):
    tm = x_ref.shape[0]
    h = _rms(x_ref[...], g_ref[...]).astype(BF16)

    def proj(lo, hi):
        return jnp.dot(h, w_ref[:, lo:hi], preferred_element_type=F32) + b_ref[:, lo:hi]

    cos = cos_ref[...]
    sin = sin_ref[...]
    lane = lax.broadcasted_iota(I32, (tm, LANES), 1)
    first_half = (lane % SWA_HEAD_DIM) < (SWA_HEAD_DIM // 2)

    gq_ref[...] = (proj(_C_GQ, _C_GK) * (GLA_DK ** -0.5)).astype(BF16)
    gk_ref[...] = proj(_C_GK, _C_GV).astype(BF16)
    gv_ref[...] = proj(_C_GV, _C_GR).astype(BF16)
    gr = proj(_C_GR, _C_LR)
    gr_ref[...] = (gr * _sigmoid(gr)).astype(BF16)

    slab = proj(_C_LR, _C_END)
    n_tiles = slab.shape[1] // LANES
    turned = [pltpu.roll(slab[:, LANES * i:LANES * (i + 1)], LANES - GLA_GATE_RANK, 1) for i in range(n_tiles)]
    keep = lane < LANES - GLA_GATE_RANK
    rest = jnp.concatenate([jnp.where(keep, turned[i], turned[i + 1]) for i in range(n_tiles - 1)], axis=1)

    sq = rest[:, 0:_W_SQ] * (SWA_HEAD_DIM ** -0.5)
    sq_ref[...] = _rope_slabs(sq, cos, sin, first_half).astype(BF16)
    low = lane < SWA_HEAD_DIM
    for part, ref, roped in ((rest[:, _W_SQ:_W_SQ + _W_KV], sk_ref, True),
                             (rest[:, _W_SQ + _W_KV:_W_SQ + 2 * _W_KV], sv_ref, False)):
        if roped:
            part = _rope_slabs(part, cos, sin, first_half)
        other = pltpu.roll(part, SWA_HEAD_DIM, 1)
        ref[...] = jnp.concatenate([jnp.where(low, part, other), jnp.where(low, other, part)], axis=1).astype(BF16)
    c_ga = _W_SQ + 2 * _W_KV
    ga_ref[...] = _sigmoid(rest[:, c_ga:c_ga + _D]).astype(BF16)
    gb_ref[...] = _sigmoid(rest[:, c_ga + _D:c_ga + 2 * _D]).astype(BF16)

    low_rank = jnp.where(lane < GLA_GATE_RANK, slab[:, 0:LANES], 0.0)
    z = jnp.dot(low_rank, wg_ref[...], precision=lax.Precision.HIGHEST,
                preferred_element_type=F32) + bg_ref[...]
    log_sig = jnp.minimum(z, 0.0) - jnp.log(1.0 + jnp.exp(-jnp.abs(z)))
    lg_ref[...] = log_sig * (1.0 / GLA_TAU)


def _in_proj(x2, g_mix, w_all, b_all, cos_t, sin_t, wg, bg, seq):
    t = x2.shape[0]
    tm = PROJ_TM
    pos_blocks = seq // tm
    const = lambda i: (0, 0)
    row = lambda i: (i, 0)
    widths = [(_C_GK - _C_GQ, BF16), (_C_GV - _C_GK, BF16), (_C_GR - _C_GV, BF16), (_C_LR - _C_GR, BF16),
              (_W_SQ, BF16), (_D, BF16), (_D, BF16), (SWA_KV_HEADS * LANES, BF16), (SWA_KV_HEADS * LANES, BF16),
              (GLA_HEADS * GLA_DK, F32)]
    return pl.pallas_call(
        _in_proj_kernel,
        grid=(t // tm,),
        in_specs=[
            pl.BlockSpec((tm, _D), row),
            pl.BlockSpec((1, _D), const),
            pl.BlockSpec((_D, _C_END), const, pipeline_mode=pl.Buffered(1)),
            pl.BlockSpec((1, _C_END), const),
            pl.BlockSpec((tm, LANES), lambda i: (i % pos_blocks, 0)),
            pl.BlockSpec((tm, LANES), lambda i: (i % pos_blocks, 0)),
            pl.BlockSpec((LANES, GLA_HEADS * GLA_DK), const),
            pl.BlockSpec((1, GLA_HEADS * GLA_DK), const),
        ],
        out_specs=[pl.BlockSpec((tm, w), row) for w, _ in widths],
        out_shape=[jax.ShapeDtypeStruct((t, w), dt) for w, dt in widths],
        compiler_params=_cparams(("parallel",)),
        name="in_proj",
    )(x2, g_mix, w_all, b_all, cos_t, sin_t, wg, bg)


def _gla_cumsum_operator(c_len):
    t = np.arange(c_len)[:, None]
    r = np.arange(c_len)[None, :]
    return np.tile((r <= t).astype(np.float32), (1, 3))


def _gla_kernel(q_ref, k_ref, v_ref, lg_ref, gr_ref, gh_ref, dm_ref, o_ref, st_ref):
    c_len = GLA_CHUNK
    n_lev = c_len.bit_length() - 1
    seq = q_ref.shape[0]
    st_ref[...] = jnp.zeros_like(st_ref)

    t_i = lax.broadcasted_iota(I32, (c_len, c_len), 0)
    j_i = lax.broadcasted_iota(I32, (c_len, c_len), 1)
    row = lax.broadcasted_iota(I32, (c_len, 1), 0)
    diag = t_i == j_i
    upper, pair = [], []
    for lev in range(n_lev):
        s = c_len >> (lev + 1)
        upper.append((row & s) != 0)
        pair.append(((t_i // (2 * s)) == (j_i // (2 * s))) & ((t_i & s) != 0) & ((j_i & s) == 0))

    sub8 = lax.broadcasted_iota(I32, (c_len // 8, 8, GLA_DK), 1)

    def boundary_rows(b, s):
        if s >= 4:
            b3 = b.reshape(c_len // (2 * s), 2 * s, GLA_DK)
            return jnp.broadcast_to(b3[:, s - 1:s, :], b3.shape).reshape(c_len, GLA_DK)
        b3 = b.reshape(c_len // 8, 8, GLA_DK)
        lo = jnp.broadcast_to(b3[:, 1:2, :], b3.shape)
        hi = jnp.broadcast_to(b3[:, 5:6, :], b3.shape)
        return jnp.where(sub8 < 4, lo, hi).reshape(c_len, GLA_DK)

    def head_chunk(r0, hh):
        kcols = slice(GLA_DK * hh, GLA_DK * (hh + 1))
        vcols = slice(GLA_DV * hh, GLA_DV * (hh + 1))
        q_bf = q_ref[pl.ds(r0, c_len), kcols]
        k_bf = k_ref[pl.ds(r0, c_len), kcols]
        q = q_bf.astype(F32)
        k = k_bf.astype(F32)
        v = v_ref[pl.ds(r0, c_len), vcols]

        lg = lg_ref[pl.ds(r0, c_len), kcols]
        lg_hi = lg.astype(BF16)
        rem = lg - lg_hi.astype(F32)
        lg_mid = rem.astype(BF16)
        lg_lo = (rem - lg_mid.astype(F32)).astype(BF16)
        b = jnp.dot(dm_ref[...], jnp.concatenate([lg_hi, lg_mid, lg_lo], axis=0),
                    preferred_element_type=F32)
        w_cum = jnp.exp(b)

        st = st_ref[hh]
        o = _dot_nt((q * w_cum).astype(BF16), st.astype(BF16))

        a = jnp.where(diag, _dot_nt(q_bf, k_bf), 0.0)
        for lev in range(n_lev):
            s = c_len >> (lev + 1)
            if s == 1:
                w = jnp.where(upper[lev], jnp.exp(lg), 1.0)
            else:
                w = jnp.exp(-jnp.abs(b - boundary_rows(b, s)))
            z = (jnp.where(upper[lev], q, k) * w).astype(BF16)
            a = jnp.where(pair[lev], _dot_nt(z, z), a)
        o = o + jnp.dot(a.astype(BF16), v, preferred_element_type=F32)

        b_last = b[c_len - 1:c_len, :]
        upd = lax.dot_general(v, (k * jnp.exp(b_last - b)).astype(BF16), (((0,), (0,)), ((), ())),
                              preferred_element_type=F32)
        st_ref[hh] = st * w_cum[c_len - 1:c_len, :] + upd

        on = _rms(o, gh_ref[...])
        o_ref[pl.ds(r0, c_len), vcols] = (on * gr_ref[pl.ds(r0, c_len), vcols].astype(F32)).astype(BF16)

    def chunk(c, carry):
        r0 = pl.multiple_of(c * c_len, c_len)
        for hh in range(GLA_HEADS_PER_STEP):
            head_chunk(r0, hh)
        return carry

    lax.fori_loop(0, seq // c_len, chunk, 0, unroll=4)


def _gla(gq, gk, gv, lg, gr, g_head, bsz, seq):
    t = gq.shape[0]
    hs = GLA_HEADS_PER_STEP
    dmat = jnp.asarray(_gla_cumsum_operator(GLA_CHUNK), dtype=BF16)
    return pl.pallas_call(
        _gla_kernel,
        grid=(bsz, GLA_HEADS // hs),
        in_specs=[
            pl.BlockSpec((seq, hs * GLA_DK), lambda b, h: (b, h)),
            pl.BlockSpec((seq, hs * GLA_DK), lambda b, h: (b, h)),
            pl.BlockSpec((seq, hs * GLA_DV), lambda b, h: (b, h)),
            pl.BlockSpec((seq, hs * GLA_DK), lambda b, h: (b, h)),
            pl.BlockSpec((seq, hs * GLA_DV), lambda b, h: (b, h)),
            pl.BlockSpec((1, GLA_DV), lambda b, h: (0, 0)),
            pl.BlockSpec(dmat.shape, lambda b, h: (0, 0)),
        ],
        out_specs=pl.BlockSpec((seq, hs * GLA_DV), lambda b, h: (b, h)),
        out_shape=jax.ShapeDtypeStruct((t, GLA_HEADS * GLA_DV), BF16),
        scratch_shapes=[pltpu.VMEM((hs, GLA_DV, GLA_DK), F32)],
        compiler_params=_cparams(("parallel", "parallel")),
        name="gla",
    )(gq, gk, gv, lg, gr, g_head, dmat)


def _swa_kernel(sink_ref, q_ref, k_ref, v_ref, o_ref):
    w = SWA_WINDOW
    seq = q_ref.shape[0]
    hk = pl.program_id(1)
    lane_q = lax.broadcasted_iota(I32, (w, LANES), 1)
    low_q = lane_q < SWA_HEAD_DIM
    lane_b = lax.broadcasted_iota(I32, (2 * w, LANES), 1)
    low_b = lane_b < SWA_HEAD_DIM
    qi = lax.broadcasted_iota(I32, (w, 2 * w), 0)
    kj = lax.broadcasted_iota(I32, (w, 2 * w), 1)
    in_window = (kj > qi) & (kj <= qi + w)
    zero_q = jnp.zeros((w, LANES), BF16)
    zero_b = jnp.zeros((2 * w, LANES), BF16)

    def block(n, carry):
        r0 = pl.multiple_of(n * w, w)
        p0 = pl.multiple_of(jnp.maximum(n - 1, 0) * w, w)
        kb = jnp.concatenate([k_ref[pl.ds(p0, w), :], k_ref[pl.ds(r0, w), :]], axis=0)
        vb = jnp.concatenate([v_ref[pl.ds(p0, w), :], v_ref[pl.ds(r0, w), :]], axis=0)
        valid = in_window & ((kj >= w) | (n > 0))
        v_lo = jnp.where(low_b, vb, zero_b)
        v_hi = jnp.where(low_b, zero_b, vb)
        for m in range(SWA_GROUP // 2):
            qp = q_ref[pl.ds(r0, w), LANES * m:LANES * (m + 1)]
            acc = jnp.zeros((w, LANES), F32)
            for par in range(2):
                qm = jnp.where(low_q, qp, zero_q) if par == 0 else jnp.where(low_q, zero_q, qp)
                s = jnp.where(valid, _dot_nt(qm, kb), NEG)
                sink = sink_ref[hk * SWA_GROUP + 2 * m + par]
                mx = jnp.maximum(jnp.max(s, axis=-1, keepdims=True), sink)
                p = jnp.exp(s - mx)
                den = jnp.sum(p, axis=-1, keepdims=True) + jnp.exp(sink - mx)
                pv = jnp.dot(p.astype(BF16), v_lo if par == 0 else v_hi, preferred_element_type=F32)
                acc = acc + pv / den
            o_ref[pl.ds(r0, w), LANES * m:LANES * (m + 1)] = acc.astype(BF16)
        return carry

    lax.fori_loop(0, seq // w, block, 0, unroll=16)


def _swa(sinks, sq, sk, sv, bsz, seq):
    t = sq.shape[0]
    gw = SWA_GROUP * SWA_HEAD_DIM
    return pl.pallas_call(
        _swa_kernel,
        grid_spec=pltpu.PrefetchScalarGridSpec(
            num_scalar_prefetch=1,
            grid=(bsz, SWA_KV_HEADS),
            in_specs=[
                pl.BlockSpec((seq, gw), lambda b, h, s: (b, h)),
                pl.BlockSpec((seq, LANES), lambda b, h, s: (b, h)),
                pl.BlockSpec((seq, LANES), lambda b, h, s: (b, h)),
            ],
            out_specs=pl.BlockSpec((seq, gw), lambda b, h, s: (b, h)),
        ),
        out_shape=jax.ShapeDtypeStruct((t, SWA_Q_HEADS * SWA_HEAD_DIM), BF16),
        compiler_params=_cparams(("parallel", "parallel")),
        name="swa",
    )(sinks, sq, sk, sv)


def _merge_kernel(x_ref, oa_ref, ob_ref, ga_ref, gb_ref, wa_ref, wb_ref, wo_ref, gf_ref, wr_ref, br_ref,
                  x1_ref, hn_ref, lt_ref, mixed_ref, prev_ref):
    @pl.when(pl.program_id(0) == 0)
    def _():
        prev_ref[...] = jnp.zeros_like(prev_ref)

    hn = _rms(prev_ref[...], gf_ref[...])
    hn_ref[...] = hn
    lt_ref[...] = lax.dot_general(wr_ref[...], hn, (((1,), (1,)), ((), ())),
                                  precision=lax.Precision.HIGHEST,
                                  preferred_element_type=F32) + br_ref[...]

    oa = oa_ref[...]
    ob = ob_ref[...]
    for n in range(_D // MERGE_TN):
        cols = slice(MERGE_TN * n, MERGE_TN * (n + 1))
        ya = ga_ref[:, cols].astype(F32) * jnp.dot(oa, wa_ref[:, cols], preferred_element_type=F32)
        yb = gb_ref[:, cols].astype(F32) * jnp.dot(ob, wb_ref[:, cols], preferred_element_type=F32)
        mixed_ref[:, cols] = (ya + yb).astype(BF16)
    mixed = mixed_ref[...]
    for n in range(_D // MERGE_TN):
        cols = slice(MERGE_TN * n, MERGE_TN * (n + 1))
        x1 = x_ref[:, cols] + jnp.dot(mixed, wo_ref[:, cols], preferred_element_type=F32)
        x1_ref[:, cols] = x1
        prev_ref[:, cols] = x1


def _merge(x2, oa, ob, ga, gb, wa, wb, wo, g_ffn, wr_t, br_col):
    t = x2.shape[0]
    tm = MERGE_TM
    n_tiles = t // tm
    row = lambda i: (jnp.minimum(i, n_tiles - 1), 0)
    lag = lambda i: (jnp.maximum(i - 1, 0), 0)
    const = lambda i: (0, 0)
    return pl.pallas_call(
        _merge_kernel,
        grid=(n_tiles + 1,),
        in_specs=[pl.BlockSpec((tm, _D), row)] * 5 + [pl.BlockSpec((_D, _D), const)] * 3 + [
            pl.BlockSpec((1, _D), const),
            pl.BlockSpec((N_EXPERTS, _D), const),
            pl.BlockSpec((N_EXPERTS, 1), const),
        ],
        out_specs=[pl.BlockSpec((tm, _D), row), pl.BlockSpec((tm, _D), lag),
                   pl.BlockSpec((N_EXPERTS, tm), lambda i: (0, jnp.maximum(i - 1, 0)))],
        out_shape=[jax.ShapeDtypeStruct((t, _D), F32), jax.ShapeDtypeStruct((t, _D), F32),
                   jax.ShapeDtypeStruct((N_EXPERTS, t), F32)],
        scratch_shapes=[pltpu.VMEM((tm, _D), BF16), pltpu.VMEM((tm, _D), F32)],
        compiler_params=_cparams(("arbitrary",)),
        name="merge",
    )(x2, oa, ob, ga, gb, wa, wb, wo, g_ffn, wr_t, br_col)


def _route_kernel(lt_ref, e_ref, g_ref, r_ref, cum_ref, carry_ref):
    tt = lt_ref.shape[1]

    @pl.when(pl.program_id(0) == 0)
    def _():
        carry_ref[...] = jnp.zeros_like(carry_ref)
        cum_ref[...] = jnp.zeros_like(cum_ref)

    eid = lax.broadcasted_iota(I32, (N_EXPERTS, tt), 0)
    work = lt_ref[...]
    vals, idxs = [], []
    chosen = jnp.zeros((N_EXPERTS, tt), F32)
    for _ in range(TOP_K):
        m = jnp.max(work, axis=0, keepdims=True)
        idx = jnp.min(jnp.where(work == m, eid, N_EXPERTS), axis=0, keepdims=True)
        hit = eid == idx
        work = jnp.where(hit, -jnp.inf, work)
        chosen = jnp.where(hit, 1.0, chosen)
        vals.append(m)
        idxs.append(idx)
    ex = [jnp.exp(v - vals[0]) for v in vals]
    den = ex[0] + ex[1] + ex[2] + ex[3]

    t_r = lax.broadcasted_iota(I32, (tt, tt), 0)
    t_c = lax.broadcasted_iota(I32, (tt, tt), 1)
    before = (t_r < t_c).astype(BF16)
    pref = jnp.dot(chosen.astype(BF16), before, preferred_element_type=F32) + carry_ref[:, 0:1]
    for k in range(TOP_K):
        e_ref[k:k + 1, :] = idxs[k]
        g_ref[k:k + 1, :] = ex[k] / den
        r_ref[k:k + 1, :] = jnp.sum(jnp.where(eid == idxs[k], pref, 0.0), axis=0, keepdims=True).astype(I32)
    total = pref[:, tt - 1:tt] + chosen[:, tt - 1:tt]
    carry_ref[...] = jnp.broadcast_to(total, carry_ref.shape)
    tile_lane = lax.broadcasted_iota(I32, cum_ref.shape, 1) == pl.program_id(0)
    cum_ref[...] = jnp.where(tile_lane, total, cum_ref[...])


def _route(logits_t):
    t = logits_t.shape[1]
    tt = ROUTE_TT
    blk = lambda i: (0, i)
    return pl.pallas_call(
        _route_kernel,
        grid=(t // tt,),
        in_specs=[pl.BlockSpec((N_EXPERTS, tt), blk)],
        out_specs=[pl.BlockSpec((TOP_K, tt), blk), pl.BlockSpec((TOP_K, tt), blk), pl.BlockSpec((TOP_K, tt), blk),
                   pl.BlockSpec((N_EXPERTS, LANES), lambda i: (0, 0))],
        out_shape=[jax.ShapeDtypeStruct((TOP_K, t), I32), jax.ShapeDtypeStruct((TOP_K, t), F32),
                   jax.ShapeDtypeStruct((TOP_K, t), I32), jax.ShapeDtypeStruct((N_EXPERTS, LANES), F32)],
        scratch_shapes=[pltpu.VMEM((N_EXPERTS, LANES), F32)],
        compiler_params=_cparams(("arbitrary",)),
        name="route",
    )(logits_t)


def _dispatch_kernel(pend_ref, nblk_ref, nv_ref, tab_ref, hn_ref, pos_ref, xs_hbm, spill_hbm,
                     st0_ref, st1_ref, carry_ref, zero_ref, sems, zsem):
    n_tok = hn_ref.shape[0]
    n_stage = st0_ref.shape[0]
    n_blocks = xs_hbm.shape[0] // MOE_BLOCK
    step = pl.program_id(0)
    last = pl.num_programs(0) - 1
    stages = (st0_ref, st1_ref)
    max_units = n_tok // SUBLANES + 1
    pieces = [1 << b for b in reversed(range(max_units.bit_length()))]
    fill_pieces = [1 << b for b in reversed(range((n_stage // SUBLANES).bit_length()))]

    def zero_block(row0):
        return pltpu.make_async_copy(zero_ref, xs_hbm.at[pl.ds(pl.multiple_of(row0, MOE_BLOCK), MOE_BLOCK)], zsem)

    @pl.when(step == 0)
    def _():
        zero_ref[...] = jnp.zeros_like(zero_ref)
        carry_ref[...] = jnp.zeros_like(carry_ref)

        def expert_tail(e, carry, start):
            @pl.when(nblk_ref[e] > 0)
            def _():
                cp = zero_block(pend_ref[e] - MOE_BLOCK)
                cp.start() if start else cp.wait()
            return carry

        def unused_block(b, carry, start):
            cp = zero_block(b * MOE_BLOCK)
            cp.start() if start else cp.wait()
            return carry

        for start in (True, False):
            lax.fori_loop(0, N_EXPERTS, lambda e, c: expert_tail(e, c, start), 0)
            lax.fori_loop(nv_ref[0], n_blocks, lambda b, c: unused_block(b, c, start), 0)

    def permute(slot):
        hb = hn_ref[...].astype(BF16)
        pos = pos_ref[...]
        for c in range(n_stage // DISPATCH_KC):
            row = lax.broadcasted_iota(I32, (DISPATCH_KC, n_tok), 0) + c * DISPATCH_KC
            sel = jnp.zeros((DISPATCH_KC, n_tok), F32)
            for k in range(TOP_K):
                sel = jnp.where(row == pos[k:k + 1, :], 1.0, sel)
            stages[slot][DISPATCH_KC * c:DISPATCH_KC * (c + 1), :] = jnp.dot(
                sel.astype(BF16), hb, preferred_element_type=F32)
        sub = lax.broadcasted_iota(I32, (SUBLANES, hn_ref.shape[1]), 0)
        for e in range(N_EXPERTS):
            units = tab_ref[0, 0, N_EXPERTS + e]

            @pl.when(units > 0)
            def _(e=e, units=units):
                first = pl.multiple_of(tab_ref[0, 0, 2 * N_EXPERTS + e], SUBLANES)
                shared = tab_ref[0, 0, 3 * N_EXPERTS + e]
                head = stages[slot][pl.ds(first, SUBLANES), :]
                stages[slot][pl.ds(first, SUBLANES), :] = jnp.where(sub < shared, carry_ref[e], head)
                final = pl.multiple_of(first + (units - 1) * SUBLANES, SUBLANES)
                carry_ref[e] = stages[slot][pl.ds(final, SUBLANES), :]

    def write_covers(slot):
        total = 0
        for e in range(N_EXPERTS):
            dst = tab_ref[0, 0, e]
            units = tab_ref[0, 0, N_EXPERTS + e]
            src = tab_ref[0, 0, 2 * N_EXPERTS + e]
            total = total + units
            for p in pieces:
                @pl.when((units & p) != 0)
                def _(p=p):
                    done = (units & ~(2 * p - 1)) * SUBLANES
                    pltpu.make_async_copy(
                        stages[slot].at[pl.ds(pl.multiple_of(src + done, SUBLANES), p * SUBLANES)],
                        xs_hbm.at[pl.ds(pl.multiple_of(dst + done, SUBLANES), p * SUBLANES)],
                        sems.at[slot]).start()
        spare = n_stage // SUBLANES - total
        for p in fill_pieces:
            @pl.when((spare & p) != 0)
            def _(p=p):
                done = (spare & ~(2 * p - 1)) * SUBLANES
                pltpu.make_async_copy(
                    stages[slot].at[pl.ds(pl.multiple_of(total * SUBLANES + done, SUBLANES), p * SUBLANES)],
                    spill_hbm.at[pl.ds(pl.multiple_of(done, SUBLANES), p * SUBLANES)],
                    sems.at[slot]).start()

    def drain(slot):
        for c in range(n_stage // DISPATCH_KC):
            pltpu.make_async_copy(stages[slot].at[pl.ds(DISPATCH_KC * c, DISPATCH_KC)],
                                  spill_hbm.at[pl.ds(0, DISPATCH_KC)], sems.at[slot]).wait()

    for slot in (0, 1):
        @pl.when(step % 2 == slot)
        def _(slot=slot):
            permute(slot)

            @pl.when(step > 0)
            def _():
                drain(1 - slot)

            write_covers(slot)

            @pl.when(step == last)
            def _():
                drain(slot)


def _dispatch(pend_rows, blocks_e, n_valid, run_table, pos_kt, hn, rows):
    t = hn.shape[0]
    tm = DISPATCH_TM
    steps = t // tm
    n_stage = -(-(TOP_K * tm + 2 * (SUBLANES - 1) * N_EXPERTS) // DISPATCH_KC) * DISPATCH_KC
    return pl.pallas_call(
        _dispatch_kernel,
        grid_spec=pltpu.PrefetchScalarGridSpec(
            num_scalar_prefetch=3,
            grid=(steps,),
            in_specs=[pl.BlockSpec((1, 1, LANES), lambda i, *_: (i, 0, 0), memory_space=pltpu.SMEM),
                      pl.BlockSpec((tm, _D), lambda i, *_: (i, 0)),
                      pl.BlockSpec((TOP_K, tm), lambda i, *_: (0, i))],
            out_specs=[pl.BlockSpec(memory_space=pl.ANY), pl.BlockSpec(memory_space=pl.ANY)],
            scratch_shapes=[pltpu.VMEM((n_stage, _D), F32), pltpu.VMEM((n_stage, _D), F32),
                            pltpu.VMEM((N_EXPERTS, SUBLANES, _D), F32), pltpu.VMEM((MOE_BLOCK, _D), F32),
                            pltpu.SemaphoreType.DMA((2,)), pltpu.SemaphoreType.DMA],
        ),
        out_shape=[jax.ShapeDtypeStruct((rows, _D), F32), jax.ShapeDtypeStruct((n_stage, _D), F32)],
        compiler_params=_cparams(("arbitrary",)),
        name="dispatch",
    )(pend_rows, blocks_e, n_valid, run_table, hn, pos_kt)[0]


def _experts_kernel(be_ref, nv_ref, first_ref, slot_ref, next_ref, xs_ref, b1_ref, b2_ref, perm_ref,
                    w1_hbm, w2_hbm, ys_ref, wf1_ref, wf2_ref, w1b_ref, w2b_ref, sems):
    i = pl.program_id(0)
    gw = perm_ref.shape[0]

    def weight_copies(e, slot):
        return (pltpu.make_async_copy(w1_hbm.at[e], wf1_ref.at[slot], sems.at[0, slot]),
                pltpu.make_async_copy(w2_hbm.at[e], wf2_ref.at[slot], sems.at[1, slot]))

    @pl.when(i == 0)
    def _():
        for cp in weight_copies(be_ref[0], 0):
            cp.start()

    @pl.when((i < nv_ref[0]) & (first_ref[i] != 0))
    def _():
        slot = slot_ref[i]
        for cp in weight_copies(be_ref[i], slot):
            cp.wait()

        @pl.when(next_ref[i] >= 0)
        def _():
            for cp in weight_copies(next_ref[i], 1 - slot):
                cp.start()

        for gi in range(wf1_ref.shape[2] // gw):
            wb = wf1_ref[slot, :, gw * gi:gw * (gi + 1)].astype(BF16)
            w1b_ref[:, gw * gi:gw * (gi + 1)] = jnp.dot(wb, perm_ref[...], preferred_element_type=F32).astype(BF16)
        w2b_ref[...] = wf2_ref[slot].astype(BF16)

    @pl.when(i < nv_ref[0])
    def _():
        x = xs_ref[...].astype(BF16)
        u = jnp.dot(x, w1b_ref[...], preferred_element_type=F32) + b1_ref[0]
        acts = []
        for gi in range(u.shape[1] // (2 * LANES)):
            g = jnp.minimum(u[:, 2 * LANES * gi:2 * LANES * gi + LANES], SWIGLU_LIMIT)
            lin = jnp.clip(u[:, 2 * LANES * gi + LANES:2 * LANES * (gi + 1)], -SWIGLU_LIMIT, SWIGLU_LIMIT)
            acts.append((g * _sigmoid(SWIGLU_ALPHA * g) * (lin + 1.0)).astype(BF16))
        act = jnp.concatenate(acts, axis=1)
        ys_ref[...] = jnp.dot(act, w2b_ref[...], preferred_element_type=F32) + b2_ref[0]

    @pl.when(i >= nv_ref[0])
    def _():
        ys_ref[...] = jnp.zeros_like(ys_ref)


def _experts(block_e, n_valid, first, slot, next_e, xs, w1, b1, w2, b2, perm):
    rows = xs.shape[0]
    n_blocks = rows // MOE_BLOCK
    n_e, d, n1 = w1.shape
    dff = w2.shape[1]
    wmap = lambda i, be, *_: (be[i], 0, 0)
    rmap = lambda i, *_: (i, 0)
    return pl.pallas_call(
        _experts_kernel,
        grid_spec=pltpu.PrefetchScalarGridSpec(
            num_scalar_prefetch=5,
            grid=(n_blocks,),
            in_specs=[
                pl.BlockSpec((MOE_BLOCK, _D), rmap),
                pl.BlockSpec((1, 1, n1), wmap),
                pl.BlockSpec((1, 1, _D), wmap),
                pl.BlockSpec(perm.shape, lambda i, *_: (0, 0)),
                pl.BlockSpec(memory_space=pl.ANY),
                pl.BlockSpec(memory_space=pl.ANY),
            ],
            out_specs=pl.BlockSpec((MOE_BLOCK, _D), rmap),
            scratch_shapes=[pltpu.VMEM((2, d, n1), F32), pltpu.VMEM((2, dff, _D), F32),
                            pltpu.VMEM((d, n1), BF16), pltpu.VMEM((dff, _D), BF16),
                            pltpu.SemaphoreType.DMA((2, 2))],
        ),
        out_shape=jax.ShapeDtypeStruct((rows, _D), F32),
        compiler_params=_cparams(("arbitrary",)),
        name="experts",
    )(block_e, n_valid, first, slot, next_e, xs, b1, b2, perm, w1, w2)


def _final_kernel(tcur_ref, tnext_ref, x1_ref, gt_ref, pos_ref, gf_ref, ys_hbm, o_ref, st0_ref, st1_ref, sems):
    n_tok = x1_ref.shape[0]
    n_stage = st0_ref.shape[0]
    step = pl.program_id(0)
    last = pl.num_programs(0) - 1
    stages = (st0_ref, st1_ref)
    max_units = n_tok // SUBLANES + 1
    pieces = [1 << b for b in reversed(range(max_units.bit_length()))]
    fill_pieces = [1 << b for b in reversed(range((n_stage // SUBLANES).bit_length()))]

    def fetch_covers(tab_ref, slot):
        total = 0
        for e in range(N_EXPERTS):
            src = tab_ref[0, 0, e]
            units = tab_ref[0, 0, N_EXPERTS + e]
            dst = tab_ref[0, 0, 2 * N_EXPERTS + e]
            total = total + units
            for p in pieces:
                @pl.when((units & p) != 0)
                def _(p=p):
                    done = (units & ~(2 * p - 1)) * SUBLANES
                    pltpu.make_async_copy(
                        ys_hbm.at[pl.ds(pl.multiple_of(src + done, SUBLANES), p * SUBLANES)],
                        stages[slot].at[pl.ds(pl.multiple_of(dst + done, SUBLANES), p * SUBLANES)],
                        sems.at[slot]).start()
        spare = n_stage // SUBLANES - total
        for p in fill_pieces:
            @pl.when((spare & p) != 0)
            def _(p=p):
                done = (spare & ~(2 * p - 1)) * SUBLANES
                pltpu.make_async_copy(
                    ys_hbm.at[pl.ds(pl.multiple_of(done, SUBLANES), p * SUBLANES)],
                    stages[slot].at[pl.ds(pl.multiple_of(total * SUBLANES + done, SUBLANES), p * SUBLANES)],
                    sems.at[slot]).start()

    def drain(slot):
        for c in range(n_stage // FINAL_KC):
            pltpu.make_async_copy(ys_hbm.at[pl.ds(0, FINAL_KC)],
                                  stages[slot].at[pl.ds(FINAL_KC * c, FINAL_KC)], sems.at[slot]).wait()

    def combine(slot):
        gates = gt_ref[...]
        pos = pos_ref[...]
        y = x1_ref[...]
        for c in range(n_stage // FINAL_KC):
            col = lax.broadcasted_iota(I32, (n_tok, FINAL_KC), 1) + c * FINAL_KC
            sel = jnp.zeros((n_tok, FINAL_KC), F32)
            for k in range(TOP_K):
                sel = jnp.where(col == pos[:, k:k + 1], gates[:, k:k + 1], sel)
            rows = stages[slot][FINAL_KC * c:FINAL_KC * (c + 1), :].astype(BF16)
            y = y + jnp.dot(sel.astype(BF16), rows, preferred_element_type=F32)
        o_ref[...] = _rms(y, gf_ref[...])

    @pl.when(step == 0)
    def _():
        st0_ref[...] = jnp.zeros_like(st0_ref)
        st1_ref[...] = jnp.zeros_like(st1_ref)
        fetch_covers(tcur_ref, 0)

    for slot in (0, 1):
        @pl.when(step % 2 == slot)
        def _(slot=slot):
            drain(slot)

            @pl.when(step < last)
            def _():
                fetch_covers(tnext_ref, 1 - slot)

            combine(slot)


def _final(run_table, x1, gates_tk, pos_tk, g_final, ys):
    t = x1.shape[0]
    tm = FINAL_TM
    steps = t // tm
    row = lambda i: (i, 0)
    n_stage = -(-(TOP_K * tm + 2 * (SUBLANES - 1) * N_EXPERTS) // FINAL_KC) * FINAL_KC
    return pl.pallas_call(
        _final_kernel,
        grid=(steps,),
        in_specs=[pl.BlockSpec((1, 1, LANES), lambda i: (i, 0, 0), memory_space=pltpu.SMEM),
                  pl.BlockSpec((1, 1, LANES), lambda i: (jnp.minimum(i + 1, steps - 1), 0, 0),
                               memory_space=pltpu.SMEM),
                  pl.BlockSpec((tm, _D), row), pl.BlockSpec((tm, TOP_K), row), pl.BlockSpec((tm, TOP_K), row),
                  pl.BlockSpec((1, _D), lambda i: (0, 0)), pl.BlockSpec(memory_space=pl.ANY)],
        out_specs=pl.BlockSpec((tm, _D), row),
        out_shape=jax.ShapeDtypeStruct((t, _D), F32),
        scratch_shapes=[pltpu.VMEM((n_stage, _D), F32), pltpu.VMEM((n_stage, _D), F32),
                        pltpu.SemaphoreType.DMA((2,))],
        compiler_params=_cparams(("arbitrary",)),
        name="final",
    )(run_table, run_table, x1, gates_tk, pos_tk, g_final, ys)


def _prep_in_proj(w_in, b_in, w_gate, b_gate):
    assert w_in.shape[1] == _N_IN and SWA_KV_HEADS * SWA_HEAD_DIM == LANES
    pad = _C_END - _N_IN
    w_all = jnp.pad(w_in, ((0, 0), (0, pad))).astype(BF16)
    b_all = jnp.pad(b_in, (0, pad))[None, :]
    wg = jnp.pad(w_gate, ((0, LANES - GLA_GATE_RANK), (0, 0)))
    return w_all, b_all, wg, b_gate[None, :]


def _rope_tables(seq):
    half = SWA_HEAD_DIM // 2
    inv_freq = ROPE_THETA ** (-jnp.arange(half, dtype=F32) / half)
    ang = jnp.arange(seq, dtype=F32)[:, None] * inv_freq[None, :]
    cos, sin = jnp.cos(ang), jnp.sin(ang)
    cos_t = jnp.concatenate([cos, cos] * (LANES // SWA_HEAD_DIM), axis=1)
    sin_t = jnp.concatenate([-sin, sin] * (LANES // SWA_HEAD_DIM), axis=1)
    return cos_t, sin_t


def _pair_split_perm():
    src = jnp.arange(2 * LANES, dtype=I32)
    dst = jnp.where(src % 2 == 0, src // 2, LANES + src // 2)
    return (dst[:, None] == jnp.arange(2 * LANES, dtype=I32)[None, :]).astype(BF16)


def kernel(x, g_mix, w_in, b_in, w_gla_gate, b_gla_gate, g_gla_head, w_gla_out, sinks, w_swa_out, w_out,
           g_ffn, w_router, b_router, w_e1, b_e1, w_e2, b_e2, g_final):
    bsz, seq, d = x.shape
    assert d == _D and w_in.shape[0] == 1, "single-layer, d_model=1024 only"
    assert seq % max(PROJ_TM, SWA_WINDOW, GLA_CHUNK) == 0
    t = bsz * seq
    assert t % max(MERGE_TM, ROUTE_TT, FINAL_TM, DISPATCH_TM) == 0
    assert FINAL_TM == ROUTE_TT == DISPATCH_TM and t // ROUTE_TT <= LANES and 4 * N_EXPERTS == LANES
    x2 = x.reshape(t, d)

    w_all, b_all, wg, bg = _prep_in_proj(w_in[0], b_in[0], w_gla_gate[0], b_gla_gate[0])
    cos_t, sin_t = _rope_tables(seq)
    gq, gk, gv, gr, sq, ga, gb, sk, sv, lg = _in_proj(x2, g_mix, w_all, b_all, cos_t, sin_t, wg, bg, seq)
    oa = _gla(gq, gk, gv, lg, gr, g_gla_head, bsz, seq)
    ob = _swa(sinks[0], sq, sk, sv, bsz, seq)
    x1, hn, logits_t = _merge(x2, oa, ob, ga, gb, w_gla_out[0].astype(BF16), w_swa_out[0].astype(BF16),
                              w_out[0].astype(BF16), g_ffn, w_router[0].T, b_router[0][:, None])

    e_kt, g_kt, r_kt, cum = _route(logits_t)
    n_tiles = t // ROUTE_TT
    cum = cum[:, :n_tiles].astype(I32)
    counts = cum[:, -1]
    blocks_e = (counts + MOE_BLOCK - 1) // MOE_BLOCK
    bend = jnp.cumsum(blocks_e)
    pstart = (bend - blocks_e) * MOE_BLOCK
    n_blocks = (t * TOP_K) // MOE_BLOCK + N_EXPERTS
    block_e = jnp.minimum(jnp.sum(bend[None, :] <= jnp.arange(n_blocks, dtype=I32)[:, None], axis=1),
                          N_EXPERTS - 1).astype(I32)
    n_valid = bend[-1:].astype(I32)

    before = jnp.concatenate([jnp.zeros((N_EXPERTS, 1), I32), cum[:, :-1]], axis=1)
    run_len = cum - before
    run_src = pstart[:, None] + before
    cov_src = run_src // SUBLANES * SUBLANES
    cov_len = jnp.where(run_len > 0, (run_src + run_len + SUBLANES - 1) // SUBLANES * SUBLANES - cov_src, 0)
    cov_dst = jnp.cumsum(cov_len, axis=0) - cov_len
    run_table = jnp.concatenate([cov_src.T, cov_len.T // SUBLANES, cov_dst.T, (run_src - cov_src).T], axis=1)
    run_table = run_table.reshape(n_tiles, 1, LANES)
    shift = jnp.repeat(cov_dst + run_src - cov_src - before, ROUTE_TT, axis=1)
    pos_kt = r_kt
    for e in range(N_EXPERTS):
        pos_kt = pos_kt + jnp.where(e_kt == e, shift[e][None, :], 0)

    b1 = b_e1[0].reshape(N_EXPERTS, -1, LANES, 2).transpose(0, 1, 3, 2).reshape(N_EXPERTS, 1, -1)
    xs = _dispatch(bend * MOE_BLOCK, blocks_e, n_valid, run_table, pos_kt, hn, n_blocks * MOE_BLOCK)
    first = jnp.concatenate([jnp.ones((1,), I32), (block_e[1:] != block_e[:-1]).astype(I32)])
    slot = (jnp.cumsum(first) - 1) % 2
    ids = jnp.arange(N_EXPERTS, dtype=I32)
    later = (ids[None, :] > ids[:, None]) & (blocks_e[None, :] > 0)
    next_of = jnp.min(jnp.where(later, ids[None, :], N_EXPERTS), axis=1)
    next_of = jnp.where(next_of < N_EXPERTS, next_of, -1)
    next_e = jnp.sum(jnp.where(block_e[:, None] == ids[None, :], next_of[None, :], 0), axis=1).astype(I32)
    ys = _experts(block_e, n_valid, first, slot.astype(I32), next_e, xs, w_e1[0], b1, w_e2[0],
                  b_e2[0][:, None, :], _pair_split_perm())
    out = _final(run_table, x1, g_kt.T, pos_kt.T, g_final[None, :], ys)
    return out.reshape(bsz, seq, d)
```
